```python
import math
import jax, jax.numpy as jnp
from jax import lax
import numpy as np

D_MODEL = 2048
BATCH = 4
SEQ = 2048
DEPTH = 1
DEC_BATCH = 128
DEC_SEQ = 1
PAST_LEN = 16384
PAGE_SIZE = 128

D_MIX = D_MODEL
W_LRU = D_MIX // 2
W_SSD = D_MIX - W_LRU
LRU_HEADS = 16
LRU_BLOCK = W_LRU // LRU_HEADS
LRU_C = 8.0
SSD_HEAD_DIM = 64
SSD_HEADS = W_SSD // SSD_HEAD_DIM
SSD_GROUPS = 2
SSD_HPG = SSD_HEADS // SSD_GROUPS
SSD_STATE = 128
SSD_CHUNK = 128
CONV_W = 4
SSD_CONV_DIM = W_SSD + 2 * SSD_GROUPS * SSD_STATE
IN_COLS = 2 * W_LRU + W_SSD + SSD_CONV_DIM + SSD_HEADS
D_FF = 5632
N_MOD = 9
EPS = 1e-6

kernel_name = "hymba_style_rglru_ssd_macaron_decoder_step"


def rmsnorm(x, g):
    xf = x.astype(jnp.float32)
    y = xf * lax.rsqrt(jnp.mean(xf * xf, axis=-1, keepdims=True) + EPS)
    return (y * g.astype(jnp.float32)).astype(x.dtype)


def modulate(h, shift, scale):
    return h * (1.0 + scale[:, None, :]) + shift[:, None, :]


def swiglu(h, w_up, w_down):
    g, u = jnp.split(h @ w_up, 2, axis=-1)
    return (jax.nn.silu(g) * u) @ w_down


def causal_conv(x, buf, w, b):
    T = x.shape[1]
    xp = jnp.concatenate([buf.astype(x.dtype), x], axis=1)
    y = b
    for k in range(CONV_W):
        y = y + w[k] * xp[:, k:k + T]
    return y, xp[:, xp.shape[1] - (CONV_W - 1):]


def rg_lru(x, h0, w_a, b_a, w_i, b_i, lam):
    Bsz, T, _ = x.shape
    xf = x.astype(jnp.float32)
    xb = xf.reshape(Bsz, T, LRU_HEADS, LRU_BLOCK)
    r = jax.nn.sigmoid(jnp.einsum('bthi,hij->bthj', xb, w_a).reshape(Bsz, T, W_LRU) + b_a)
    i = jax.nn.sigmoid(jnp.einsum('bthi,hij->bthj', xb, w_i).reshape(Bsz, T, W_LRU) + b_i)
    log_a = -LRU_C * r * jax.nn.softplus(-lam.astype(jnp.float32))
    a = jnp.exp(log_a)
    bt = jnp.sqrt(-jnp.expm1(2.0 * log_a)) * (i * xf)

    def combine(e1, e2):
        a1, b1 = e1
        a2, b2 = e2
        return a1 * a2, a2 * b1 + b2

    a_cum, b_cum = lax.associative_scan(combine, (a, bt), axis=1)
    h = a_cum * h0.astype(jnp.float32)[:, None, :] + b_cum
    return h, h[:, -1]


def ssd_scan(x, dt, A, Bm, Cm, h0):
    b, T = x.shape[0], x.shape[1]
    Lc = min(SSD_CHUNK, T)
    pad = (-T) % Lc
    if pad:
        padw = lambda t: jnp.pad(t, [(0, 0), (0, pad)] + [(0, 0)] * (t.ndim - 2))
        x, dt, Bm, Cm = padw(x), padw(dt), padw(Bm), padw(Cm)
    nc = (T + pad) // Lc
    G, E, P, N = SSD_GROUPS, SSD_HPG, SSD_HEAD_DIM, SSD_STATE
    x = x.reshape(b, nc, Lc, G, E, P)
    dt = dt.reshape(b, nc, Lc, G, E)
    Bm = Bm.reshape(b, nc, Lc, G, N)
    Cm = Cm.reshape(b, nc, Lc, G, N)
    dA_cs = jnp.cumsum(dt * A.reshape(G, E), axis=2)
    xdt = x * dt[..., None]
    diff = dA_cs[:, :, :, None] - dA_cs[:, :, None, :]
    causal = jnp.tril(jnp.ones((Lc, Lc), dtype=bool))[:, :, None, None]
    Lmat = jnp.exp(jnp.where(causal, diff, -jnp.inf))
    CB = jnp.einsum('bclgn,bcsgn->bclsg', Cm, Bm)
    M = CB[..., None] * Lmat
    y_diag = jnp.einsum('bclsge,bcsgep->bclgep', M, xdt)
    decay_last = jnp.exp(dA_cs[:, :, -1:] - dA_cs)
    states = jnp.einsum('bclgn,bclge,bclgep->bcgepn', Bm, decay_last, xdt)
    chunk_decay = jnp.exp(dA_cs[:, :, -1])

    def step(h, inp):
        s, d = inp
        return d[..., None, None] * h + s, h

    h_init = h0.astype(jnp.float32).reshape(b, G, E, P, N)
    hT, h_in = lax.scan(step, h_init, (jnp.moveaxis(states, 1, 0), jnp.moveaxis(chunk_decay, 1, 0)))
    h_in = jnp.moveaxis(h_in, 0, 1)
    y_off = jnp.einsum('bclgn,bcgepn,bclge->bclgep', Cm, h_in, jnp.exp(dA_cs))
    y = (y_diag + y_off).reshape(b, nc * Lc, SSD_HEADS, P)[:, :T]
    return y, hT.reshape(b, SSD_HEADS, P, N)


def hybrid_mixer(h, lru_h0, lru_buf0, ssm_h0, ssd_buf0, p):
    Bsz, T, _ = h.shape
    proj = h @ p['w_in']
    o1 = W_LRU
    o2 = o1 + W_LRU
    o3 = o2 + W_SSD
    o4 = o3 + SSD_CONV_DIM
    xl, gl, z, xbc, dt_raw = jnp.split(proj, [o1, o2, o3, o4], axis=-1)
    xl, lru_buf = causal_conv(xl, lru_buf0, p['lru_conv_w'], p['lru_conv_b'])
    hl, lru_hT = rg_lru(xl, lru_h0, p['lru_wa'], p['lru_ba'], p['lru_wi'], p['lru_bi'], p['lru_lambda'])
    out_l = hl * jax.nn.gelu(gl.astype(jnp.float32))
    xbc, ssd_buf = causal_conv(xbc, ssd_buf0, p['ssd_conv_w'], p['ssd_conv_b'])
    xbc = jax.nn.silu(xbc.astype(jnp.float32))
    xs, Bm, Cm = jnp.split(xbc, [W_SSD, W_SSD + SSD_GROUPS * SSD_STATE], axis=-1)
    dt = jax.nn.softplus(dt_raw.astype(jnp.float32) + p['ssd_dt_bias'].astype(jnp.float32))
    A = -jnp.exp(p['ssd_A_log'].astype(jnp.float32))
    xs4 = xs.reshape(Bsz, T, SSD_HEADS, SSD_HEAD_DIM)
    y, ssm_hT = ssd_scan(xs4, dt, A,
                         Bm.reshape(Bsz, T, SSD_GROUPS, SSD_STATE),
                         Cm.reshape(Bsz, T, SSD_GROUPS, SSD_STATE), ssm_h0)
    y = y + p['ssd_D'].astype(jnp.float32)[:, None] * xs4
    y = y.reshape(Bsz, T, W_SSD) * jax.nn.silu(z.astype(jnp.float32))
    y = rmsnorm(y, p['ssd_norm_g'])
    out = jnp.concatenate([out_l, y], axis=-1).astype(h.dtype) @ p['w_out']
    return out, (lru_hT, lru_buf, ssm_hT, ssd_buf)


def decoder_layer(x, c, lru_h0, lru_buf0, ssm_h0, ssd_buf0, p):
    mod = jax.nn.silu(c) @ p['w_ada'] + p['b_ada']
    sh1, sc1, g1, sh2, sc2, g2, sh3, sc3, g3 = jnp.split(mod, N_MOD, axis=-1)
    h = modulate(rmsnorm(x, p['g_ffn1']), sh1, sc1)
    x = x + 0.5 * g1[:, None, :] * swiglu(h, p['w_up1'], p['w_down1'])
    h = modulate(rmsnorm(x, p['g_mix']), sh2, sc2)
    o, new_state = hybrid_mixer(h, lru_h0, lru_buf0, ssm_h0, ssd_buf0, p)
    x = x + g2[:, None, :] * o
    h = modulate(rmsnorm(x, p['g_ffn2']), sh3, sc3)
    x = x + 0.5 * g3[:, None, :] * swiglu(h, p['w_up2'], p['w_down2'])
    return x, new_state


def setup_inputs(seed: int = 0) -> dict:
    key = jax.random.key(seed)
    ks = iter(jax.random.split(key, 48))
    f32 = jnp.float32
    nrm = lambda shape, s: jax.random.normal(next(ks), shape, f32) * s
    L = DEPTH
    u = jax.random.uniform(next(ks), (L, W_LRU), f32, 0.9, 0.999)
    a_base = u ** (1.0 / LRU_C)
    lru_lambda = jnp.log(a_base) - jnp.log1p(-a_base)
    dt0 = jnp.exp(jax.random.uniform(next(ks), (L, SSD_HEADS), f32, math.log(1e-3), math.log(1e-1)))
    ssd_dt_bias = dt0 + jnp.log(-jnp.expm1(-dt0))
    ssd_A_log = jnp.log(jax.random.uniform(next(ks), (L, SSD_HEADS), f32, 1.0, 16.0))
    return {
        'x_prompt': nrm((BATCH, SEQ, D_MODEL), 1.0),
        'x_sample': nrm((DEC_BATCH, DEC_SEQ, D_MODEL), 1.0),
        'c_prompt': nrm((BATCH, D_MODEL), 1.0),
        'c_sample': nrm((DEC_BATCH, D_MODEL), 1.0),
        'state_lru_h': nrm((L, DEC_BATCH, W_LRU), 0.5),
        'state_lru_conv': nrm((L, DEC_BATCH, CONV_W - 1, W_LRU), 1.0),
        'state_ssm': nrm((L, DEC_BATCH, SSD_HEADS, SSD_HEAD_DIM, SSD_STATE), 0.1),
        'state_ssd_conv': nrm((L, DEC_BATCH, CONV_W - 1, SSD_CONV_DIM), 1.0),
        'w_ada': nrm((L, D_MODEL, N_MOD * D_MODEL), 0.5 * D_MODEL ** -0.5),
        'b_ada': nrm((L, N_MOD * D_MODEL), 0.01),
        'g_ffn1': 1.0 + nrm((L, D_MODEL), 0.02),
        'w_up1': nrm((L, D_MODEL, 2 * D_FF), D_MODEL ** -0.5),
        'w_down1': nrm((L, D_FF, D_MODEL), D_FF ** -0.5),
        'g_mix': 1.0 + nrm((L, D_MODEL), 0.02),
        'w_in': nrm((L, D_MODEL, IN_COLS), D_MODEL ** -0.5),
        'lru_conv_w': nrm((L, CONV_W, W_LRU), CONV_W ** -0.5),
        'lru_conv_b': nrm((L, W_LRU), 0.01),
        'lru_wa': nrm((L, LRU_HEADS, LRU_BLOCK, LRU_BLOCK), LRU_BLOCK ** -0.5),
        'lru_ba': nrm((L, W_LRU), 0.01),
        'lru_wi': nrm((L, LRU_HEADS, LRU_BLOCK, LRU_BLOCK), LRU_BLOCK ** -0.5),
        'lru_bi': nrm((L, W_LRU), 0.01),
        'lru_lambda': lru_lambda,
        'ssd_conv_w': nrm((L, CONV_W, SSD_CONV_DIM), CONV_W ** -0.5),
        'ssd_conv_b': nrm((L, SSD_CONV_DIM), 0.01),
        'ssd_dt_bias': ssd_dt_bias,
        'ssd_A_log': ssd_A_log,
        'ssd_D': 1.0 + nrm((L, SSD_HEADS), 0.1),
        'ssd_norm_g': 1.0 + nrm((L, W_SSD), 0.02),
        'w_out': nrm((L, D_MIX, D_MODEL), D_MIX ** -0.5),
        'g_ffn2': 1.0 + nrm((L, D_MODEL), 0.02),
        'w_up2': nrm((L, D_MODEL, 2 * D_FF), D_MODEL ** -0.5),
        'w_down2': nrm((L, D_FF, D_MODEL), D_FF ** -0.5),
        'w_ada_f': nrm((D_MODEL, 2 * D_MODEL), 0.5 * D_MODEL ** -0.5),
        'b_ada_f': nrm((2 * D_MODEL,), 0.01),
        'g_final': 1.0 + nrm((D_MODEL,), 0.02),
    }


def reference(x_prompt, x_sample, c_prompt, c_sample, state_lru_h, state_lru_conv, state_ssm,
              state_ssd_conv, w_ada, b_ada, g_ffn1, w_up1, w_down1, g_mix, w_in, lru_conv_w,
              lru_conv_b, lru_wa, lru_ba, lru_wi, lru_bi, lru_lambda, ssd_conv_w, ssd_conv_b,
              ssd_dt_bias, ssd_A_log, ssd_D, ssd_norm_g, w_out, g_ffn2, w_up2, w_down2,
              w_ada_f, b_ada_f, g_final):
    xp, xs = x_prompt, x_sample
    bp = x_prompt.shape[0]
    dtp = x_prompt.dtype
    new_p = ([], [], [], [])
    new_s = ([], [], [], [])
    for l in range(DEPTH):
        p = {
            'w_ada': w_ada[l], 'b_ada': b_ada[l], 'g_ffn1': g_ffn1[l], 'w_up1': w_up1[l],
            'w_down1': w_down1[l], 'g_mix': g_mix[l], 'w_in': w_in[l],
            'lru_conv_w': lru_conv_w[l], 'lru_conv_b': lru_conv_b[l], 'lru_wa': lru_wa[l],
            'lru_ba': lru_ba[l], 'lru_wi': lru_wi[l], 'lru_bi': lru_bi[l],
            'lru_lambda': lru_lambda[l], 'ssd_conv_w': ssd_conv_w[l], 'ssd_conv_b': ssd_conv_b[l],
            'ssd_dt_bias': ssd_dt_bias[l], 'ssd_A_log': ssd_A_log[l], 'ssd_D': ssd_D[l],
            'ssd_norm_g': ssd_norm_g[l], 'w_out': w_out[l], 'g_ffn2': g_ffn2[l],
            'w_up2': w_up2[l], 'w_down2': w_down2[l],
        }
        xp, sp = decoder_layer(
            xp, c_prompt,
            jnp.zeros((bp, W_LRU), dtp),
            jnp.zeros((bp, CONV_W - 1, W_LRU), dtp),
            jnp.zeros((bp, SSD_HEADS, SSD_HEAD_DIM, SSD_STATE), dtp),
            jnp.zeros((bp, CONV_W - 1, SSD_CONV_DIM), dtp), p)
        xs, ss = decoder_layer(xs, c_sample, state_lru_h[l], state_lru_conv[l], state_ssm[l],
                               state_ssd_conv[l], p)
        for j in range(4):
            new_p[j].append(sp[j])
            new_s[j].append(ss[j])

    def final(x, c):
        shf, scf = jnp.split(jax.nn.silu(c) @ w_ada_f + b_ada_f, 2, axis=-1)
        return modulate(rmsnorm(x, g_final), shf, scf)

    y_prompt = final(xp, c_prompt).astype(x_prompt.dtype)
    y_sample = final(xs, c_sample).astype(x_sample.dtype)
    lru_h_prompt = jnp.stack(new_p[0]).astype(state_lru_h.dtype)
    lru_conv_prompt = jnp.stack(new_p[1]).astype(state_lru_conv.dtype)
    ssm_prompt = jnp.stack(new_p[2]).astype(state_ssm.dtype)
    ssd_conv_prompt = jnp.stack(new_p[3]).astype(state_ssd_conv.dtype)
    lru_h_sample = jnp.stack(new_s[0]).astype(state_lru_h.dtype)
    lru_conv_sample = jnp.stack(new_s[1]).astype(state_lru_conv.dtype)
    ssm_sample = jnp.stack(new_s[2]).astype(state_ssm.dtype)
    ssd_conv_sample = jnp.stack(new_s[3]).astype(state_ssd_conv.dtype)
    return (y_prompt, y_sample, lru_h_prompt, lru_conv_prompt, ssm_prompt, ssd_conv_prompt,
            lru_h_sample, lru_conv_sample, ssm_sample, ssd_conv_sample)
```

```python
import functools

import jax
import jax.numpy as jnp
from jax import lax
from jax.experimental import pallas as pl
from jax.experimental.pallas import tpu as pltpu

F32 = jnp.float32
BF16 = jnp.bfloat16

D_MODEL = 2048
D_FF = 5632
W_LRU = 1024
W_SSD = 1024
LRU_HEADS = 16
LRU_BLOCK = 64
LRU_C = 8.0
SSD_HEADS = 16
SSD_HEAD_DIM = 64
SSD_GROUPS = 2
SSD_HPG = 8
SSD_STATE = 128
SSD_CHUNK = 128
CONV_W = 4
SSD_CONV_DIM = W_SSD + 2 * SSD_GROUPS * SSD_STATE
IN_MAIN = 2 * W_LRU + W_SSD + SSD_CONV_DIM
N_MOD = 9
EPS = 1e-6

LANES = 128
SUBLANES = 8
VMEM_LIMIT_BYTES = 56 * 1024 * 1024

LRU_GATE_GROUP = 256
LRU_TIME_TILE = 256
SCAN_ROWS = 2 * SUBLANES


def _silu(v):
    return v * jax.nn.sigmoid(v)


def _softplus(v):
    return jnp.maximum(v, 0.0) + jnp.log1p(jnp.exp(-jnp.abs(v)))


def _gelu_tanh(v):
    return 0.5 * v * (1.0 + jnp.tanh(0.7978845608028654 * (v + 0.044715 * (v * v * v))))


def _bdot(a, b):
    return jnp.dot(a, b, preferred_element_type=F32)


def _params(sem):
    return pltpu.CompilerParams(dimension_semantics=sem, vmem_limit_bytes=VMEM_LIMIT_BYTES)


def _ada_kernel(c_ref, w_ref, b_ref, o_ref):
    s = _silu(c_ref[...]).astype(BF16)
    o_ref[...] = _bdot(s, w_ref[...].astype(BF16)) + b_ref[...]


def _ada(c, w, b, tn=1024):
    m, k = c.shape
    n = w.shape[1]
    return pl.pallas_call(
        _ada_kernel,
        grid=(n // tn,),
        in_specs=[pl.BlockSpec((m, k), lambda j: (0, 0)),
                  pl.BlockSpec((k, tn), lambda j: (0, j)),
                  pl.BlockSpec((1, tn), lambda j: (0, j))],
        out_specs=pl.BlockSpec((m, tn), lambda j: (0, j)),
        out_shape=jax.ShapeDtypeStruct((m, n), F32),
        compiler_params=_params(("parallel",)),
        name="ada_proj",
    )(c, w, b.reshape(1, n))


def _norm_modulate(x, gain, shift, scale):
    ms = jnp.mean(x * x, axis=-1, keepdims=True)
    y = x * lax.rsqrt(ms + EPS) * gain
    return y * (1.0 + scale) + shift


def _normmod_matmul_kernel(*refs, n_w, swiglu, has_extra, row_chunk):
    x_ref, gain_ref, sh_ref, sc_ref = refs[:4]
    w_refs = refs[4:4 + n_w]
    pos = 4 + n_w
    wx_ref = refs[pos] if has_extra else None
    pos += int(has_extra)
    o_ref = refs[pos]
    ox_ref = refs[pos + 1] if has_extra else None
    h_scr = refs[-1]
    tm = x_ref.shape[0]
    per_row = sh_ref.shape[0] != 1

    @pl.when(pl.program_id(1) == 0)
    def _():
        gain = gain_ref[...]

        def body(r, carry):
            rows = pl.ds(pl.multiple_of(r * row_chunk, row_chunk), row_chunk)
            sh = sh_ref[rows, :] if per_row else sh_ref[...]
            sc = sc_ref[rows, :] if per_row else sc_ref[...]
            h_scr[rows, :] = _norm_modulate(x_ref[rows, :], gain, sh, sc).astype(BF16)
            return carry

        lax.fori_loop(0, tm // row_chunk, body, 0)
        if has_extra:
            ox_ref[...] = _bdot(h_scr[...], wx_ref[...].astype(BF16))

    h = h_scr[...]
    if swiglu:
        g = _bdot(h, w_refs[0][...].astype(BF16))
        u = _bdot(h, w_refs[1][...].astype(BF16))
        o_ref[...] = (_silu(g) * u).astype(o_ref.dtype)
    else:
        o_ref[...] = _bdot(h, w_refs[0][...].astype(BF16)).astype(o_ref.dtype)


def _normmod_matmul(x, gain, mod, shift_chunk, w, *, n_out, tm, tn, swiglu, out_dtype,
                    w_extra=None):
    m, d = x.shape
    groups, r, _ = mod.shape
    tiles_per_group = (m // tm) // groups
    nj = n_out // tn
    row_chunk = min(tm, 128)

    def mod_spec(chunk):
        return pl.BlockSpec((None, r, d), lambda i, j: (i // tiles_per_group, 0, chunk))

    in_specs = [pl.BlockSpec((tm, d), lambda i, j: (i, 0)),
                pl.BlockSpec((1, d), lambda i, j: (0, 0)),
                mod_spec(shift_chunk), mod_spec(shift_chunk + 1),
                pl.BlockSpec((d, tn), lambda i, j: (0, j))]
    args = [x, gain.reshape(1, d), mod, mod, w]
    n_w = 1
    if swiglu:
        in_specs.append(pl.BlockSpec((d, tn), lambda i, j: (0, j + nj)))
        args.append(w)
        n_w = 2
    out_specs = [pl.BlockSpec((tm, tn), lambda i, j: (i, j))]
    out_shape = [jax.ShapeDtypeStruct((m, n_out), out_dtype)]
    if w_extra is not None:
        nx = w_extra.shape[1]
        in_specs.append(pl.BlockSpec((d, nx), lambda i, j: (0, 0)))
        args.append(w_extra)
        out_specs.append(pl.BlockSpec((tm, nx), lambda i, j: (i, 0)))
        out_shape.append(jax.ShapeDtypeStruct((m, nx), F32))
    outs = pl.pallas_call(
        functools.partial(_normmod_matmul_kernel, n_w=n_w, swiglu=swiglu,
                          has_extra=w_extra is not None, row_chunk=row_chunk),
        grid=(m // tm, nj),
        in_specs=in_specs,
        out_specs=out_specs,
        out_shape=out_shape,
        scratch_shapes=[pltpu.VMEM((tm, d), BF16)],
        compiler_params=_params(("parallel", "arbitrary")),
        name="normmod_matmul_swiglu" if swiglu else "normmod_matmul",
    )(*args)
    return outs if w_extra is not None else outs[0]


def _matmul_resid_kernel(*refs, n_lhs, factor):
    lhs_refs = refs[:n_lhs]
    w_refs = refs[n_lhs:2 * n_lhs]
    x_ref, gate_ref, o_ref = refs[2 * n_lhs:]
    acc = _bdot(lhs_refs[0][...], w_refs[0][...].astype(BF16))
    for l_ref, w_ref in zip(lhs_refs[1:], w_refs[1:]):
        acc = acc + _bdot(l_ref[...], w_ref[...].astype(BF16))
    o_ref[...] = x_ref[...] + (factor * gate_ref[...]) * acc


def _matmul_resid(lhs_list, w, x, mod, gate_chunk, *, factor, tm, tn):
    m, d = x.shape
    groups, r, _ = mod.shape
    tiles_per_group = (m // tm) // groups
    kp = lhs_list[0].shape[1]
    gate_blocks = d // tn
    in_specs = [pl.BlockSpec((tm, kp), lambda i, j: (i, 0)) for _ in lhs_list]
    in_specs += [pl.BlockSpec((kp, tn), lambda i, j, k=k: (k, j)) for k in range(len(lhs_list))]
    in_specs += [pl.BlockSpec((tm, tn), lambda i, j: (i, j)),
                 pl.BlockSpec((None, r, tn),
                              lambda i, j: (i // tiles_per_group, 0, gate_chunk * gate_blocks + j))]
    return pl.pallas_call(
        functools.partial(_matmul_resid_kernel, n_lhs=len(lhs_list), factor=factor),
        grid=(m // tm, d // tn),
        in_specs=in_specs,
        out_specs=pl.BlockSpec((tm, tn), lambda i, j: (i, j)),
        out_shape=jax.ShapeDtypeStruct((m, d), F32),
        compiler_params=_params(("parallel", "arbitrary")),
        name="matmul_resid",
    )(*lhs_list, *([w] * len(lhs_list)), x, mod)


def _final_kernel(x_ref, gain_ref, sh_ref, sc_ref, o_ref):
    o_ref[...] = _norm_modulate(x_ref[...], gain_ref[...], sh_ref[...], sc_ref[...])


def _final_norm(x, gain, mod, *, tm):
    m, d = x.shape
    groups, r, _ = mod.shape
    tiles_per_group = (m // tm) // groups
    return pl.pallas_call(
        _final_kernel,
        grid=(m // tm,),
        in_specs=[pl.BlockSpec((tm, d), lambda i: (i, 0)),
                  pl.BlockSpec((1, d), lambda i: (0, 0)),
                  pl.BlockSpec((None, r, d), lambda i: (i // tiles_per_group, 0, 0)),
                  pl.BlockSpec((None, r, d), lambda i: (i // tiles_per_group, 0, 1))],
        out_specs=pl.BlockSpec((tm, d), lambda i: (i, 0)),
        out_shape=jax.ShapeDtypeStruct((m, d), F32),
        compiler_params=_params(("parallel",)),
        name="final_norm",
    )(x, gain.reshape(1, d), mod, mod)


def _lru_gates(xc, wg_ref, ba, bi, sp):
    a_parts, b_parts = [], []
    for g in range(W_LRU // LRU_GATE_GROUP):
        cols = slice(g * LRU_GATE_GROUP, (g + 1) * LRU_GATE_GROUP)
        xg = xc[:, cols]
        ri = _bdot(xg.astype(BF16), wg_ref[g].astype(BF16))
        r = jax.nn.sigmoid(ri[:, :LRU_GATE_GROUP] + ba[:, cols])
        i = jax.nn.sigmoid(ri[:, LRU_GATE_GROUP:] + bi[:, cols])
        log_a = (-LRU_C * r) * sp[:, cols]
        a = jnp.exp(log_a)
        a_parts.append(a)
        b_parts.append(jnp.sqrt(1.0 - a * a) * (i * xg))
    return jnp.concatenate(a_parts, axis=1), jnp.concatenate(b_parts, axis=1)


def _causal_conv_from_buf(buf_ref, x, w_ref, b_ref, rows):
    y = b_ref[...] + w_ref[0:1, :] * buf_ref[pl.ds(SUBLANES - 3, rows), :]
    y = y + w_ref[1:2, :] * buf_ref[pl.ds(SUBLANES - 2, rows), :]
    y = y + w_ref[2:3, :] * buf_ref[pl.ds(SUBLANES - 1, rows), :]
    return y + w_ref[3:4, :] * x


def _lru_prompt_kernel(xl_ref, gl_ref, cw_ref, cb_ref, wg_ref, ba_ref, bi_ref, lam_ref,
                       o_ref, hT_ref, xbuf, a_scr, b_scr, hcar):
    t = pl.program_id(1)
    tt = xl_ref.shape[0]

    @pl.when(t == 0)
    def _():
        xbuf[0:SUBLANES, :] = jnp.zeros((SUBLANES, W_LRU), F32)
        hcar[...] = jnp.zeros_like(hcar)

    x = xl_ref[...]
    xbuf[SUBLANES:SUBLANES + tt, :] = x
    xc = _causal_conv_from_buf(xbuf, x, cw_ref, cb_ref, tt)
    xbuf[0:SUBLANES, :] = x[tt - SUBLANES:, :]

    sp = _softplus(-lam_ref[...])
    a, bt = _lru_gates(xc, wg_ref, ba_ref[...], bi_ref[...], sp)
    a_scr[...] = a
    b_scr[...] = bt

    rid = lax.broadcasted_iota(jnp.int32, (SUBLANES, W_LRU), 0)

    def scan8(a8, b8, h_in):
        for s in (1, 2, 4):
            a_sh = pltpu.roll(a8, s, 0)
            b_sh = pltpu.roll(b8, s, 0)
            m = rid >= s
            b8 = jnp.where(m, a8 * b_sh + b8, b8)
            a8 = jnp.where(m, a8 * a_sh, a8)
        h8 = a8 * h_in + b8
        return h8, jnp.broadcast_to(h8[SUBLANES - 1:SUBLANES, :], (SUBLANES, W_LRU))

    def body(g, h_in):
        r0 = pl.multiple_of(g * SCAN_ROWS, SCAN_ROWS)
        lo = pl.ds(r0, SUBLANES)
        hi = pl.ds(r0 + SUBLANES, SUBLANES)
        h_lo, h_mid = scan8(a_scr[lo, :], b_scr[lo, :], h_in)
        h_hi, h_out = scan8(a_scr[hi, :], b_scr[hi, :], h_mid)
        rows = pl.ds(r0, SCAN_ROWS)
        h16 = jnp.concatenate([h_lo, h_hi], axis=0)
        o_ref[rows, :] = (h16 * _gelu_tanh(gl_ref[rows, :])).astype(o_ref.dtype)
        return h_out

    h_last = lax.fori_loop(0, tt // SCAN_ROWS, body, hcar[...])
    hcar[...] = h_last

    @pl.when(t == pl.num_programs(1) - 1)
    def _():
        hT_ref[...] = h_last[0:1, :]


def _lru_prompt(proj, batch, seq, cw, cb, wg, ba, bi, lam):
    tt = LRU_TIME_TILE
    nt = seq // tt
    row = lambda v: v.reshape(1, W_LRU)
    full = lambda shape: pl.BlockSpec(shape, lambda b, t: (0,) * len(shape))
    out, h_t = pl.pallas_call(
        _lru_prompt_kernel,
        grid=(batch, nt),
        in_specs=[pl.BlockSpec((tt, W_LRU), lambda b, t: (b * nt + t, 0)),
                  pl.BlockSpec((tt, W_LRU), lambda b, t: (b * nt + t, 1)),
                  full((CONV_W, W_LRU)), full((1, W_LRU)), full(wg.shape),
                  full((1, W_LRU)), full((1, W_LRU)), full((1, W_LRU))],
        out_specs=[pl.BlockSpec((tt, W_LRU), lambda b, t: (b * nt + t, 0)),
                   pl.BlockSpec((None, 1, W_LRU), lambda b, t: (b, 0, 0))],
        out_shape=[jax.ShapeDtypeStruct((batch * seq, W_LRU), BF16),
                   jax.ShapeDtypeStruct((batch, 1, W_LRU), F32)],
        scratch_shapes=[pltpu.VMEM((tt + SUBLANES, W_LRU), F32),
                        pltpu.VMEM((tt, W_LRU), F32),
                        pltpu.VMEM((tt, W_LRU), F32),
                        pltpu.VMEM((SUBLANES, W_LRU), F32)],
        compiler_params=_params(("parallel", "arbitrary")),
        name="lru_prompt",
    )(proj, proj, cw, row(cb), wg, row(ba), row(bi), row(lam))
    return out, h_t.reshape(batch, W_LRU)


def _ssd_prompt_kernel(z_ref, xbc_ref, dt_ref, cw_ref, cb_ref, dtb_ref, alog_ref, dexp_ref,
                       ng_ref, y_ref, st_ref, xbuf, st_scr, y_scr):
    c = pl.program_id(1)
    lc = SSD_CHUNK

    @pl.when(c == 0)
    def _():
        xbuf[0:SUBLANES, :] = jnp.zeros((SUBLANES, SSD_CONV_DIM), F32)
        st_scr[...] = jnp.zeros_like(st_scr)

    x = xbc_ref[...]
    xbuf[SUBLANES:SUBLANES + lc, :] = x
    act = _silu(_causal_conv_from_buf(xbuf, x, cw_ref, cb_ref, lc))
    xbuf[0:SUBLANES, :] = x[lc - SUBLANES:, :]
    xs = act[:, :W_SSD]
    bm = act[:, W_SSD:W_SSD + SSD_GROUPS * SSD_STATE]
    cm = act[:, W_SSD + SSD_GROUPS * SSD_STATE:]

    dt = _softplus(dt_ref[...] + dtb_ref[...])
    d_a = dt * (-jnp.exp(alog_ref[...]))
    row_i = lax.broadcasted_iota(jnp.int32, (lc, lc), 0)
    col_i = lax.broadcasted_iota(jnp.int32, (lc, lc), 1)
    causal = row_i >= col_i
    tril = jnp.where(causal, 1.0, 0.0).astype(F32)
    cs = jnp.dot(tril, d_a, preferred_element_type=F32, precision=lax.Precision.HIGHEST)
    cs_t = cs.T
    dt_t = dt.T
    cs_last_col = cs_t[:, lc - 1:lc]
    w_t = jnp.exp(cs_last_col - cs_t) * dt_t
    chunk_decay_col = jnp.exp(cs_last_col)

    for g in range(SSD_GROUPS):
        ncols = slice(g * SSD_STATE, (g + 1) * SSD_STATE)
        b_g = bm[:, ncols]
        c_g = cm[:, ncols]
        cb_mat = lax.dot_general(c_g.astype(BF16), b_g.astype(BF16), (((1,), (1,)), ((), ())),
                                 preferred_element_type=F32)
        b_gt = b_g.T
        for e in range(SSD_HPG):
            h = g * SSD_HPG + e
            pcols = slice(h * SSD_HEAD_DIM, (h + 1) * SSD_HEAD_DIM)
            cs_col = jnp.broadcast_to(cs[:, h:h + 1], (lc, lc))
            cs_row = cs_t[h:h + 1, :]
            l_mat = jnp.exp(jnp.where(causal, cs_col - cs_row, -jnp.inf))
            m_h = (cb_mat * l_mat * dt_t[h:h + 1, :]).astype(BF16)
            c_sc = (c_g * jnp.exp(cs_col)).astype(BF16)
            xs_h = xs[:, pcols]
            xs_hb = xs_h.astype(BF16)
            st_h = st_scr[:, pcols]
            lhs = jnp.concatenate([m_h, c_sc], axis=1)
            rhs = jnp.concatenate([xs_hb, st_h.astype(BF16)], axis=0)
            y_scr[:, pcols] = _bdot(lhs, rhs) + dexp_ref[:, pcols] * xs_h
            b_sc = (b_gt * w_t[h:h + 1, :]).astype(BF16)
            st_scr[:, pcols] = chunk_decay_col[h:h + 1, :] * st_h + _bdot(b_sc, xs_hb)

    yg = y_scr[...] * _silu(z_ref[...])
    ms = jnp.mean(yg * yg, axis=-1, keepdims=True)
    y_ref[...] = (yg * lax.rsqrt(ms + EPS) * ng_ref[...]).astype(y_ref.dtype)

    @pl.when(c == pl.num_programs(1) - 1)
    def _():
        st_ref[...] = st_scr[...].T


def _ssd_prompt(proj, dt_raw, batch, seq, cw, cb, dtb, alog, dexp, ng):
    lc = SSD_CHUNK
    nc = seq // lc
    full = lambda shape: pl.BlockSpec(shape, lambda b, c: (0,) * len(shape))
    z_blk = (2 * W_LRU) // W_SSD
    xbc_blk = (2 * W_LRU + W_SSD) // SSD_CONV_DIM
    y, st = pl.pallas_call(
        _ssd_prompt_kernel,
        grid=(batch, nc),
        in_specs=[pl.BlockSpec((lc, W_SSD), lambda b, c: (b * nc + c, z_blk)),
                  pl.BlockSpec((lc, SSD_CONV_DIM), lambda b, c: (b * nc + c, xbc_blk)),
                  pl.BlockSpec((lc, LANES), lambda b, c: (b * nc + c, 0)),
                  full((CONV_W, SSD_CONV_DIM)), full((1, SSD_CONV_DIM)),
                  full((1, LANES)), full((1, LANES)), full((1, W_SSD)), full((1, W_SSD))],
        out_specs=[pl.BlockSpec((lc, W_SSD), lambda b, c: (b * nc + c, 0)),
                   pl.BlockSpec((None, W_SSD, SSD_STATE), lambda b, c: (b, 0, 0))],
        out_shape=[jax.ShapeDtypeStruct((batch * seq, W_SSD), BF16),
                   jax.ShapeDtypeStruct((batch, W_SSD, SSD_STATE), F32)],
        scratch_shapes=[pltpu.VMEM((lc + SUBLANES, SSD_CONV_DIM), F32),
                        pltpu.VMEM((SSD_STATE, W_SSD), F32),
                        pltpu.VMEM((lc, W_SSD), F32)],
        compiler_params=_params(("parallel", "arbitrary")),
        name="ssd_prompt",
    )(proj, proj, dt_raw, cw, cb, dtb, alog, dexp, ng)
    return y, st.reshape(batch, SSD_HEADS, SSD_HEAD_DIM, SSD_STATE)


def _sample_pre_kernel(proj_ref, dt_ref, h0_ref, lconv_ref, sconv_ref,
                       lcw_ref, lcb_ref, wg_ref, ba_ref, bi_ref, lam_ref,
                       scw_ref, scb_ref, dtb_ref, alog_ref,
                       outl_ref, hnew_ref, lconv_new_ref, sconv_new_ref,
                       xs_ref, xdt_ref, bc_ref, dec_ref):
    nb = proj_ref.shape[0]
    xl = proj_ref[:, 0:W_LRU]
    gl = proj_ref[:, W_LRU:2 * W_LRU]
    xbc = proj_ref[:, 2 * W_LRU + W_SSD:IN_MAIN]

    def conv1(state_ref, width, x_new, w_ref, b_ref):
        y = b_ref[...] + w_ref[0:1, :] * state_ref[:, 0:width]
        y = y + w_ref[1:2, :] * state_ref[:, width:2 * width]
        y = y + w_ref[2:3, :] * state_ref[:, 2 * width:3 * width]
        return y + w_ref[3:4, :] * x_new

    xc = conv1(lconv_ref, W_LRU, xl, lcw_ref, lcb_ref)
    a, bt = _lru_gates(xc, wg_ref, ba_ref[...], bi_ref[...], _softplus(-lam_ref[...]))
    h_new = a * h0_ref[...] + bt
    hnew_ref[...] = h_new
    outl_ref[...] = (h_new * _gelu_tanh(gl)).astype(outl_ref.dtype)
    lconv_new_ref[:, 0:2 * W_LRU] = lconv_ref[:, W_LRU:3 * W_LRU]
    lconv_new_ref[:, 2 * W_LRU:3 * W_LRU] = xl

    act = _silu(conv1(sconv_ref, SSD_CONV_DIM, xbc, scw_ref, scb_ref))
    sconv_new_ref[:, 0:2 * SSD_CONV_DIM] = sconv_ref[:, SSD_CONV_DIM:3 * SSD_CONV_DIM]
    sconv_new_ref[:, 2 * SSD_CONV_DIM:3 * SSD_CONV_DIM] = xbc
    xs = act[:, :W_SSD]
    xs_ref[...] = xs
    bc_ref[...] = act[:, W_SSD:]
    dt = _softplus(dt_ref[...] + dtb_ref[...])
    dec = jnp.exp(dt * (-jnp.exp(alog_ref[...])))
    for h in range(SSD_HEADS):
        pcols = slice(h * SSD_HEAD_DIM, (h + 1) * SSD_HEAD_DIM)
        xdt_ref[:, pcols] = xs[:, pcols] * jnp.broadcast_to(dt[:, h:h + 1], (nb, SSD_HEAD_DIM))
        dec_ref[h] = jnp.broadcast_to(dec[:, h:h + 1], (nb, SSD_STATE))


def _sample_pre(proj, dt_raw, h0, lconv, sconv, p):
    nb = proj.shape[0]
    out_shape = [jax.ShapeDtypeStruct((nb, W_LRU), BF16),
                 jax.ShapeDtypeStruct((nb, W_LRU), F32),
                 jax.ShapeDtypeStruct((nb, 3 * W_LRU), F32),
                 jax.ShapeDtypeStruct((nb, 3 * SSD_CONV_DIM), F32),
                 jax.ShapeDtypeStruct((nb, W_SSD), F32),
                 jax.ShapeDtypeStruct((nb, W_SSD), F32),
                 jax.ShapeDtypeStruct((nb, 2 * SSD_GROUPS * SSD_STATE), F32),
                 jax.ShapeDtypeStruct((SSD_HEADS, nb, SSD_STATE), F32)]
    return pl.pallas_call(
        _sample_pre_kernel,
        out_shape=out_shape,
        compiler_params=pltpu.CompilerParams(vmem_limit_bytes=VMEM_LIMIT_BYTES),
        name="sample_pre",
    )(proj, dt_raw, h0, lconv, sconv,
      p["lru_cw"], p["lru_cb"], p["lru_wg"], p["lru_ba"], p["lru_bi"], p["lru_lam"],
      p["ssd_cw"], p["ssd_cb"], p["ssd_dtb"], p["ssd_alog"])


def _sample_state_kernel(s_ref, xdt_ref, bc_ref, dec_ref, o_ref, y_ref):
    bb = s_ref.shape[0]
    half = SSD_HPG * SSD_HEAD_DIM
    rid = lax.broadcasted_iota(jnp.int32, (bb, W_SSD), 0)
    xdt = xdt_ref[...]
    bcb = bc_ref[...].astype(BF16)
    for k in range(bb):
        xk = jnp.where(rid == k, xdt, 0.0).astype(BF16)
        for g in range(SSD_GROUPS):
            rows = slice(g * half, (g + 1) * half)
            b_g = bcb[:, g * SSD_STATE:(g + 1) * SSD_STATE]
            c_g = bcb[:, (SSD_GROUPS + g) * SSD_STATE:(SSD_GROUPS + g + 1) * SSD_STATE]
            outer = lax.dot_general(xk[:, rows], b_g, (((0,), (0,)), ((), ())),
                                    preferred_element_type=F32)
            dec = jnp.concatenate(
                [jnp.broadcast_to(dec_ref[g * SSD_HPG + e, k:k + 1, :], (SSD_HEAD_DIM, SSD_STATE))
                 for e in range(SSD_HPG)], axis=0)
            s_new = dec * s_ref[k, rows, :] + outer
            o_ref[k, rows, :] = s_new
            yk = lax.dot_general(c_g, s_new.astype(BF16), (((1,), (1,)), ((), ())),
                                 preferred_element_type=F32)
            y_ref[k:k + 1, rows] = yk[k:k + 1, :]


def _sample_state(ssm, xdt, bc, dec, bb=8):
    nb = ssm.shape[0]
    return pl.pallas_call(
        _sample_state_kernel,
        grid=(nb // bb,),
        in_specs=[pl.BlockSpec((bb, W_SSD, SSD_STATE), lambda i: (i, 0, 0)),
                  pl.BlockSpec((bb, W_SSD), lambda i: (i, 0)),
                  pl.BlockSpec((bb, 2 * SSD_GROUPS * SSD_STATE), lambda i: (i, 0)),
                  pl.BlockSpec((SSD_HEADS, bb, SSD_STATE), lambda i: (0, i, 0))],
        out_specs=[pl.BlockSpec((bb, W_SSD, SSD_STATE), lambda i: (i, 0, 0)),
                   pl.BlockSpec((bb, W_SSD), lambda i: (i, 0))],
        out_shape=[jax.ShapeDtypeStruct(ssm.shape, F32),
                   jax.ShapeDtypeStruct((nb, W_SSD), F32)],
        compiler_params=_params(("parallel",)),
        name="sample_state",
    )(ssm, xdt, bc, dec)


def _sample_post_kernel(y_ref, xs_ref, proj_ref, dexp_ref, ng_ref, o_ref):
    z = proj_ref[:, 2 * W_LRU:2 * W_LRU + W_SSD]
    yg = (y_ref[...] + dexp_ref[...] * xs_ref[...]) * _silu(z)
    ms = jnp.mean(yg * yg, axis=-1, keepdims=True)
    o_ref[...] = (yg * lax.rsqrt(ms + EPS) * ng_ref[...]).astype(o_ref.dtype)


def _sample_post(y_raw, xs, proj, dexp, ng):
    return pl.pallas_call(
        _sample_post_kernel,
        out_shape=jax.ShapeDtypeStruct(y_raw.shape, BF16),
        compiler_params=pltpu.CompilerParams(vmem_limit_bytes=VMEM_LIMIT_BYTES),
        name="sample_post",
    )(y_raw, xs, proj, dexp, ng)


def _block_diag_groups(w):
    per = LRU_GATE_GROUP // LRU_BLOCK
    w4 = w.reshape(LRU_HEADS // per, per, LRU_BLOCK, LRU_BLOCK)
    bd = jnp.einsum("ghij,hk->ghikj", w4, jnp.eye(per, dtype=w.dtype))
    return bd.reshape(LRU_HEADS // per, LRU_GATE_GROUP, LRU_GATE_GROUP)


def _pad_lanes(v):
    v = v.reshape(1, -1)
    return jnp.pad(v, ((0, 0), (0, LANES - v.shape[1])))


def kernel(x_prompt, x_sample, c_prompt, c_sample, state_lru_h, state_lru_conv, state_ssm, state_ssd_conv, w_ada, b_ada, g_ffn1, w_up1, w_down1, g_mix, w_in, lru_conv_w, lru_conv_b, lru_wa, lru_ba, lru_wi, lru_bi, lru_lambda, ssd_conv_w, ssd_conv_b, ssd_dt_bias, ssd_A_log, ssd_D, ssd_norm_g, w_out, g_ffn2, w_up2, w_down2, w_ada_f, b_ada_f, g_final):
    bp, seq, d = x_prompt.shape
    bs = x_sample.shape[0]
    depth = w_ada.shape[0]
    assert depth == 1 and x_sample.shape[1] == 1 and d == D_MODEL

    pad_rows = (-(bs + bp)) % (2 * SUBLANES)
    c_all = jnp.concatenate([c_sample, c_prompt, jnp.zeros((pad_rows, d), F32)], axis=0)
    mod_all = _ada(c_all, w_ada[0], b_ada[0])
    modf_all = _ada(c_all, w_ada_f, b_ada_f)
    mod_s = mod_all.reshape(1, bs + bp + pad_rows, N_MOD * d)[:, :bs]
    mod_p = mod_all[bs:bs + bp].reshape(bp, 1, N_MOD * d)
    modf_s = modf_all.reshape(1, bs + bp + pad_rows, 2 * d)[:, :bs]
    modf_p = modf_all[bs:bs + bp].reshape(bp, 1, 2 * d)

    w_dt = jnp.pad(w_in[0][:, IN_MAIN:], ((0, 0), (0, LANES - SSD_HEADS)))
    p = {
        "lru_cw": lru_conv_w[0], "lru_cb": lru_conv_b[0].reshape(1, W_LRU),
        "lru_wg": jnp.concatenate([_block_diag_groups(lru_wa[0]), _block_diag_groups(lru_wi[0])],
                                  axis=-1),
        "lru_ba": lru_ba[0].reshape(1, W_LRU), "lru_bi": lru_bi[0].reshape(1, W_LRU),
        "lru_lam": lru_lambda[0].reshape(1, W_LRU),
        "ssd_cw": ssd_conv_w[0], "ssd_cb": ssd_conv_b[0].reshape(1, SSD_CONV_DIM),
        "ssd_dtb": _pad_lanes(ssd_dt_bias[0]), "ssd_alog": _pad_lanes(ssd_A_log[0]),
        "ssd_dexp": jnp.repeat(ssd_D[0], SSD_HEAD_DIM).reshape(1, W_SSD),
        "ssd_ng": ssd_norm_g[0].reshape(1, W_SSD),
    }

    def trunk(x, mod, modf, tm, mixer):
        hmid = _normmod_matmul(x, g_ffn1[0], mod, 0, w_up1[0], n_out=D_FF, tm=tm, tn=512,
                               swiglu=True, out_dtype=BF16)
        x = _matmul_resid([hmid], w_down1[0], x, mod, 2, factor=0.5, tm=tm, tn=256)
        proj, dt_raw = _normmod_matmul(x, g_mix[0], mod, 3, w_in[0], n_out=IN_MAIN, tm=tm, tn=512,
                                       swiglu=False, out_dtype=F32, w_extra=w_dt)
        out_l, y_ssd, new_state = mixer(proj, dt_raw)
        x = _matmul_resid([out_l, y_ssd], w_out[0], x, mod, 5, factor=1.0, tm=tm, tn=512)
        hmid = _normmod_matmul(x, g_ffn2[0], mod, 6, w_up2[0], n_out=D_FF, tm=tm, tn=512,
                               swiglu=True, out_dtype=BF16)
        x = _matmul_resid([hmid], w_down2[0], x, mod, 8, factor=0.5, tm=tm, tn=256)
        return _final_norm(x, g_final, modf, tm=min(tm, 512)), new_state

    def prompt_mixer(proj, dt_raw):
        out_l, lru_h = _lru_prompt(proj, bp, seq, p["lru_cw"], p["lru_cb"], p["lru_wg"],
                                   p["lru_ba"], p["lru_bi"], p["lru_lam"])
        y_ssd, ssm = _ssd_prompt(proj, dt_raw, bp, seq, p["ssd_cw"], p["ssd_cb"], p["ssd_dtb"],
                                 p["ssd_alog"], p["ssd_dexp"], p["ssd_ng"])
        proj3 = proj.reshape(bp, seq, IN_MAIN)
        lru_buf = proj3[:, seq - (CONV_W - 1):, :W_LRU]
        ssd_buf = proj3[:, seq - (CONV_W - 1):, 2 * W_LRU + W_SSD:]
        return out_l, y_ssd, (lru_h, lru_buf, ssm, ssd_buf)

    def sample_mixer(proj, dt_raw):
        lconv = state_lru_conv[0].reshape(bs, (CONV_W - 1) * W_LRU)
        sconv = state_ssd_conv[0].reshape(bs, (CONV_W - 1) * SSD_CONV_DIM)
        out_l, h_new, lconv_new, sconv_new, xs, xdt, bc, dec = _sample_pre(
            proj, dt_raw, state_lru_h[0], lconv, sconv, p)
        ssm_new, y_raw = _sample_state(state_ssm[0].reshape(bs, W_SSD, SSD_STATE), xdt, bc, dec)
        y_ssd = _sample_post(y_raw, xs, proj, p["ssd_dexp"], p["ssd_ng"])
        return out_l, y_ssd, (h_new, lconv_new.reshape(bs, CONV_W - 1, W_LRU),
                              ssm_new.reshape(bs, SSD_HEADS, SSD_HEAD_DIM, SSD_STATE),
                              sconv_new.reshape(bs, CONV_W - 1, SSD_CONV_DIM))

    yp, sp = trunk(x_prompt.reshape(bp * seq, d), mod_p, modf_p, 1024, prompt_mixer)
    ys, ss = trunk(x_sample.reshape(bs, d), mod_s, modf_s, bs, sample_mixer)

    stack = lambda v: v[None]
    return (yp.reshape(bp, seq, d), ys.reshape(bs, 1, d),
            stack(sp[0]), stack(sp[1]), stack(sp[2]), stack(sp[3]),
            stack(ss[0]), stack(ss[1]), stack(ss[2]), stack(ss[3]))
```

```python
import functools

import jax
import jax.numpy as jnp
from jax import lax
from jax.experimental import pallas as pl
from jax.experimental.pallas import tpu as pltpu

F32 = jnp.float32
BF16 = jnp.bfloat16

D_MODEL = 2048
D_FF = 5632
W_LRU = 1024
W_SSD = 1024
LRU_HEADS = 16
LRU_BLOCK = 64
LRU_C = 8.0
SSD_HEADS = 16
SSD_HEAD_DIM = 64
SSD_GROUPS = 2
SSD_HPG = 8
SSD_STATE = 128
SSD_CHUNK = 128
CONV_W = 4
SSD_CONV_DIM = W_SSD + 2 * SSD_GROUPS * SSD_STATE
IN_MAIN = 2 * W_LRU + W_SSD + SSD_CONV_DIM
N_MOD = 9
EPS = 1e-6

LANES = 128
SUBLANES = 8
VMEM_LIMIT_BYTES = 56 * 1024 * 1024

LRU_GATE_GROUP = 256
LRU_TIME_TILE = 256
SCAN_ROWS = 2 * SUBLANES


def _silu(v):
    return v * jax.nn.sigmoid(v)


def _softplus(v):
    return jnp.maximum(v, 0.0) + jnp.log1p(jnp.exp(-jnp.abs(v)))


def _gelu_tanh(v):
    return 0.5 * v * (1.0 + jnp.tanh(0.7978845608028654 * (v + 0.044715 * (v * v * v))))


def _bdot(a, b):
    return jnp.dot(a, b, preferred_element_type=F32)


def _params(sem):
    return pltpu.CompilerParams(dimension_semantics=sem, vmem_limit_bytes=VMEM_LIMIT_BYTES)


def _ada_kernel(c_ref, w_ref, b_ref, o_ref):
    s = _silu(c_ref[...]).astype(BF16)
    o_ref[...] = _bdot(s, w_ref[...].astype(BF16)) + b_ref[...]


def _ada(c, w, b, tn=1024):
    m, k = c.shape
    n = w.shape[1]
    return pl.pallas_call(
        _ada_kernel,
        grid=(n // tn,),
        in_specs=[pl.BlockSpec((m, k), lambda j: (0, 0)),
                  pl.BlockSpec((k, tn), lambda j: (0, j)),
                  pl.BlockSpec((1, tn), lambda j: (0, j))],
        out_specs=pl.BlockSpec((m, tn), lambda j: (0, j)),
        out_shape=jax.ShapeDtypeStruct((m, n), F32),
        compiler_params=_params(("parallel",)),
        name="ada_proj",
    )(c, w, b.reshape(1, n))


def _norm_modulate(x, gain, shift, scale):
    ms = jnp.mean(x * x, axis=-1, keepdims=True)
    y = x * lax.rsqrt(ms + EPS) * gain
    return y * (1.0 + scale) + shift


def _wdot(h, w, trans_w):
    if trans_w:
        return lax.dot_general(h, w, (((1,), (1,)), ((), ())), preferred_element_type=F32)
    return _bdot(h, w)


def _normmod_matmul_kernel(*refs, n_w, swiglu, has_extra, emit_bf16, trans_w, row_chunk):
    x_ref, gain_ref, sh_ref, sc_ref = refs[:4]
    w_refs = refs[4:4 + n_w]
    pos = 4 + n_w
    wx_ref = refs[pos] if has_extra else None
    pos += int(has_extra)
    o_ref = refs[pos]
    pos += 1
    ox_ref = refs[pos] if has_extra else None
    pos += int(has_extra)
    wo_refs = refs[pos:pos + n_w] if emit_bf16 else ()
    h_scr = refs[-1]
    tm = x_ref.shape[0]
    per_row = sh_ref.shape[0] != 1

    @pl.when(pl.program_id(1) == 0)
    def _():
        gain = gain_ref[...]

        def body(r, carry):
            rows = pl.ds(pl.multiple_of(r * row_chunk, row_chunk), row_chunk)
            sh = sh_ref[rows, :] if per_row else sh_ref[...]
            sc = sc_ref[rows, :] if per_row else sc_ref[...]
            h_scr[rows, :] = _norm_modulate(x_ref[rows, :], gain, sh, sc).astype(BF16)
            return carry

        lax.fori_loop(0, tm // row_chunk, body, 0)
        if has_extra:
            ox_ref[...] = _wdot(h_scr[...], wx_ref[...].astype(BF16), trans_w)

    h = h_scr[...]
    wbs = [w_ref[...].astype(BF16) for w_ref in w_refs]
    for wo_ref, wb in zip(wo_refs, wbs):
        wo_ref[...] = wb
    if swiglu:
        g = _wdot(h, wbs[0], trans_w)
        u = _wdot(h, wbs[1], trans_w)
        o_ref[...] = (_silu(g) * u).astype(o_ref.dtype)
    else:
        o_ref[...] = _wdot(h, wbs[0], trans_w).astype(o_ref.dtype)


def _normmod_matmul(x, gain, mod, shift_chunk, ws, *, n_out, tm, tn, swiglu, out_dtype,
                    trans_w=False, w_extra=None, emit_bf16=False):
    m, d = x.shape
    groups, r, _ = mod.shape
    tiles_per_group = (m // tm) // groups
    nj = n_out // tn
    row_chunk = min(tm, 128)

    def mod_spec(chunk):
        return pl.BlockSpec((None, r, d), lambda i, j: (i // tiles_per_group, 0, chunk))

    def w_spec(off):
        if trans_w:
            return pl.BlockSpec((tn, d), lambda i, j: (j + off, 0))
        return pl.BlockSpec((d, tn), lambda i, j: (0, j + off))

    in_specs = [pl.BlockSpec((tm, d), lambda i, j: (i, 0)),
                pl.BlockSpec((1, d), lambda i, j: (0, 0)),
                mod_spec(shift_chunk), mod_spec(shift_chunk + 1)]
    in_specs += [w_spec(off) for _, off in ws]
    args = [x, gain.reshape(1, d), mod, mod] + [w for w, _ in ws]
    out_specs = [pl.BlockSpec((tm, tn), lambda i, j: (i, j))]
    out_shape = [jax.ShapeDtypeStruct((m, n_out), out_dtype)]
    if w_extra is not None:
        nx = w_extra.shape[0] if trans_w else w_extra.shape[1]
        in_specs.append(pl.BlockSpec(w_extra.shape, lambda i, j: (0, 0)))
        args.append(w_extra)
        out_specs.append(pl.BlockSpec((tm, nx), lambda i, j: (i, 0)))
        out_shape.append(jax.ShapeDtypeStruct((m, nx), F32))
    if emit_bf16:
        assert m == tm, "weight copies are complete only when every weight tile is visited once"
        for _ in ws:
            out_specs.append(w_spec(0))
            out_shape.append(jax.ShapeDtypeStruct((n_out, d) if trans_w else (d, n_out), BF16))
    return pl.pallas_call(
        functools.partial(_normmod_matmul_kernel, n_w=len(ws), swiglu=swiglu,
                          has_extra=w_extra is not None, emit_bf16=emit_bf16, trans_w=trans_w,
                          row_chunk=row_chunk),
        grid=(m // tm, nj),
        in_specs=in_specs,
        out_specs=out_specs,
        out_shape=out_shape,
        scratch_shapes=[pltpu.VMEM((tm, d), BF16)],
        compiler_params=_params(("parallel", "arbitrary")),
        name="normmod_matmul_swiglu" if swiglu else "normmod_matmul",
    )(*args)


def _matmul_resid_kernel(*refs, n_lhs, factor, emit_bf16):
    lhs_refs = refs[:n_lhs]
    w_refs = refs[n_lhs:2 * n_lhs]
    x_ref, gate_ref, o_ref = refs[2 * n_lhs:2 * n_lhs + 3]
    wo_refs = refs[2 * n_lhs + 3:] if emit_bf16 else ()
    wbs = [w_ref[...].astype(BF16) for w_ref in w_refs]
    for wo_ref, wb in zip(wo_refs, wbs):
        wo_ref[...] = wb
    acc = _bdot(lhs_refs[0][...], wbs[0])
    for l_ref, wb in zip(lhs_refs[1:], wbs[1:]):
        acc = acc + _bdot(l_ref[...], wb)
    o_ref[...] = x_ref[...] + (factor * gate_ref[...]) * acc


def _matmul_resid(lhs_list, ws, x, mod, gate_chunk, *, factor, tm, tn, emit_bf16=False):
    m, d = x.shape
    groups, r, _ = mod.shape
    tiles_per_group = (m // tm) // groups
    kp = lhs_list[0].shape[1]
    gate_blocks = d // tn
    in_specs = [pl.BlockSpec((tm, kp), lambda i, j: (i, 0)) for _ in lhs_list]
    in_specs += [pl.BlockSpec((kp, tn), lambda i, j, k=k: (k, j)) for _, k in ws]
    in_specs += [pl.BlockSpec((tm, tn), lambda i, j: (i, j)),
                 pl.BlockSpec((None, r, tn),
                              lambda i, j: (i // tiles_per_group, 0, gate_chunk * gate_blocks + j))]
    out_specs = [pl.BlockSpec((tm, tn), lambda i, j: (i, j))]
    out_shape = [jax.ShapeDtypeStruct((m, d), F32)]
    if emit_bf16:
        assert m == tm, "weight copies are complete only when every weight tile is visited once"
        for _ in ws:
            out_specs.append(pl.BlockSpec((kp, tn), lambda i, j: (0, j)))
            out_shape.append(jax.ShapeDtypeStruct((kp, d), BF16))
    return pl.pallas_call(
        functools.partial(_matmul_resid_kernel, n_lhs=len(lhs_list), factor=factor,
                          emit_bf16=emit_bf16),
        grid=(m // tm, d // tn),
        in_specs=in_specs,
        out_specs=out_specs,
        out_shape=out_shape,
        compiler_params=_params(("parallel", "arbitrary")),
        name="matmul_resid",
    )(*lhs_list, *[w for w, _ in ws], x, mod)


def _final_kernel(x_ref, gain_ref, sh_ref, sc_ref, o_ref):
    o_ref[...] = _norm_modulate(x_ref[...], gain_ref[...], sh_ref[...], sc_ref[...])


def _final_norm(x, gain, mod, *, tm):
    m, d = x.shape
    groups, r, _ = mod.shape
    tiles_per_group = (m // tm) // groups
    return pl.pallas_call(
        _final_kernel,
        grid=(m // tm,),
        in_specs=[pl.BlockSpec((tm, d), lambda i: (i, 0)),
                  pl.BlockSpec((1, d), lambda i: (0, 0)),
                  pl.BlockSpec((None, r, d), lambda i: (i // tiles_per_group, 0, 0)),
                  pl.BlockSpec((None, r, d), lambda i: (i // tiles_per_group, 0, 1))],
        out_specs=pl.BlockSpec((tm, d), lambda i: (i, 0)),
        out_shape=jax.ShapeDtypeStruct((m, d), F32),
        compiler_params=_params(("parallel",)),
        name="final_norm",
    )(x, gain.reshape(1, d), mod, mod)


def _lru_gates(xc, wg_ref, ba, bi, sp):
    a_parts, b_parts = [], []
    for g in range(W_LRU // LRU_GATE_GROUP):
        cols = slice(g * LRU_GATE_GROUP, (g + 1) * LRU_GATE_GROUP)
        xg = xc[:, cols]
        ri = _bdot(xg.astype(BF16), wg_ref[g].astype(BF16))
        r = jax.nn.sigmoid(ri[:, :LRU_GATE_GROUP] + ba[:, cols])
        i = jax.nn.sigmoid(ri[:, LRU_GATE_GROUP:] + bi[:, cols])
        log_a = (-LRU_C * r) * sp[:, cols]
        a = jnp.exp(log_a)
        a_parts.append(a)
        b_parts.append(jnp.sqrt(1.0 - a * a) * (i * xg))
    return jnp.concatenate(a_parts, axis=1), jnp.concatenate(b_parts, axis=1)


def _causal_conv_from_buf(buf_ref, x, w_ref, b_ref, rows):
    y = b_ref[...] + w_ref[0:1, :] * buf_ref[pl.ds(SUBLANES - 3, rows), :]
    y = y + w_ref[1:2, :] * buf_ref[pl.ds(SUBLANES - 2, rows), :]
    y = y + w_ref[2:3, :] * buf_ref[pl.ds(SUBLANES - 1, rows), :]
    return y + w_ref[3:4, :] * x


def _lru_prompt_kernel(xl_ref, gl_ref, cw_ref, cb_ref, wg_ref, ba_ref, bi_ref, lam_ref,
                       o_ref, hT_ref, xbuf, a_scr, b_scr, hcar):
    t = pl.program_id(1)
    tt = xl_ref.shape[0]

    @pl.when(t == 0)
    def _():
        xbuf[0:SUBLANES, :] = jnp.zeros((SUBLANES, W_LRU), F32)
        hcar[...] = jnp.zeros_like(hcar)

    x = xl_ref[...]
    xbuf[SUBLANES:SUBLANES + tt, :] = x
    xc = _causal_conv_from_buf(xbuf, x, cw_ref, cb_ref, tt)
    xbuf[0:SUBLANES, :] = x[tt - SUBLANES:, :]

    sp = _softplus(-lam_ref[...])
    a, bt = _lru_gates(xc, wg_ref, ba_ref[...], bi_ref[...], sp)
    a_scr[...] = a
    b_scr[...] = bt

    rid = lax.broadcasted_iota(jnp.int32, (SUBLANES, W_LRU), 0)

    def scan8(a8, b8, h_in):
        for s in (1, 2, 4):
            a_sh = pltpu.roll(a8, s, 0)
            b_sh = pltpu.roll(b8, s, 0)
            m = rid >= s
            b8 = jnp.where(m, a8 * b_sh + b8, b8)
            a8 = jnp.where(m, a8 * a_sh, a8)
        h8 = a8 * h_in + b8
        return h8, jnp.broadcast_to(h8[SUBLANES - 1:SUBLANES, :], (SUBLANES, W_LRU))

    def body(g, h_in):
        r0 = pl.multiple_of(g * SCAN_ROWS, SCAN_ROWS)
        lo = pl.ds(r0, SUBLANES)
        hi = pl.ds(r0 + SUBLANES, SUBLANES)
        h_lo, h_mid = scan8(a_scr[lo, :], b_scr[lo, :], h_in)
        h_hi, h_out = scan8(a_scr[hi, :], b_scr[hi, :], h_mid)
        rows = pl.ds(r0, SCAN_ROWS)
        h16 = jnp.concatenate([h_lo, h_hi], axis=0)
        o_ref[rows, :] = (h16 * _gelu_tanh(gl_ref[rows, :])).astype(o_ref.dtype)
        return h_out

    h_last = lax.fori_loop(0, tt // SCAN_ROWS, body, hcar[...])
    hcar[...] = h_last

    @pl.when(t == pl.num_programs(1) - 1)
    def _():
        hT_ref[...] = h_last[0:1, :]


def _lru_prompt(proj, batch, seq, cw, cb, wg, ba, bi, lam):
    tt = LRU_TIME_TILE
    nt = seq // tt
    row = lambda v: v.reshape(1, W_LRU)
    full = lambda shape: pl.BlockSpec(shape, lambda b, t: (0,) * len(shape))
    out, h_t = pl.pallas_call(
        _lru_prompt_kernel,
        grid=(batch, nt),
        in_specs=[pl.BlockSpec((tt, W_LRU), lambda b, t: (b * nt + t, 0)),
                  pl.BlockSpec((tt, W_LRU), lambda b, t: (b * nt + t, 1)),
                  full((CONV_W, W_LRU)), full((1, W_LRU)), full(wg.shape),
                  full((1, W_LRU)), full((1, W_LRU)), full((1, W_LRU))],
        out_specs=[pl.BlockSpec((tt, W_LRU), lambda b, t: (b * nt + t, 0)),
                   pl.BlockSpec((None, 1, W_LRU), lambda b, t: (b, 0, 0))],
        out_shape=[jax.ShapeDtypeStruct((batch * seq, W_LRU), BF16),
                   jax.ShapeDtypeStruct((batch, 1, W_LRU), F32)],
        scratch_shapes=[pltpu.VMEM((tt + SUBLANES, W_LRU), F32),
                        pltpu.VMEM((tt, W_LRU), F32),
                        pltpu.VMEM((tt, W_LRU), F32),
                        pltpu.VMEM((SUBLANES, W_LRU), F32)],
        compiler_params=_params(("parallel", "arbitrary")),
        name="lru_prompt",
    )(proj, proj, cw, row(cb), wg, row(ba), row(bi), row(lam))
    return out, h_t.reshape(batch, W_LRU)


def _ssd_prompt_kernel(z_ref, xbc_ref, dt_ref, cw_ref, cb_ref, dtb_ref, alog_ref, dexp_ref,
                       ng_ref, y_ref, st_ref, xbuf, st_scr, y_scr):
    c = pl.program_id(1)
    lc = SSD_CHUNK

    @pl.when(c == 0)
    def _():
        xbuf[0:SUBLANES, :] = jnp.zeros((SUBLANES, SSD_CONV_DIM), F32)
        st_scr[...] = jnp.zeros_like(st_scr)

    x = xbc_ref[...]
    xbuf[SUBLANES:SUBLANES + lc, :] = x
    act = _silu(_causal_conv_from_buf(xbuf, x, cw_ref, cb_ref, lc))
    xbuf[0:SUBLANES, :] = x[lc - SUBLANES:, :]
    xs = act[:, :W_SSD]
    bm = act[:, W_SSD:W_SSD + SSD_GROUPS * SSD_STATE]
    cm = act[:, W_SSD + SSD_GROUPS * SSD_STATE:]

    dt = _softplus(dt_ref[...] + dtb_ref[...])
    d_a = dt * (-jnp.exp(alog_ref[...]))
    row_i = lax.broadcasted_iota(jnp.int32, (lc, lc), 0)
    col_i = lax.broadcasted_iota(jnp.int32, (lc, lc), 1)
    causal = row_i >= col_i
    tril = jnp.where(causal, 1.0, 0.0).astype(F32)
    cs = jnp.dot(tril, d_a, preferred_element_type=F32, precision=lax.Precision.HIGHEST)
    cs_t = cs.T
    dt_t = dt.T
    cs_last_col = cs_t[:, lc - 1:lc]
    w_t = jnp.exp(cs_last_col - cs_t) * dt_t
    chunk_decay_col = jnp.exp(cs_last_col)

    for g in range(SSD_GROUPS):
        ncols = slice(g * SSD_STATE, (g + 1) * SSD_STATE)
        b_g = bm[:, ncols]
        c_g = cm[:, ncols]
        cb_mat = lax.dot_general(c_g.astype(BF16), b_g.astype(BF16), (((1,), (1,)), ((), ())),
                                 preferred_element_type=F32)
        b_gt = b_g.T
        for e in range(SSD_HPG):
            h = g * SSD_HPG + e
            pcols = slice(h * SSD_HEAD_DIM, (h + 1) * SSD_HEAD_DIM)
            cs_col = jnp.broadcast_to(cs[:, h:h + 1], (lc, lc))
            cs_row = cs_t[h:h + 1, :]
            l_mat = jnp.exp(jnp.where(causal, cs_col - cs_row, -jnp.inf))
            m_h = (cb_mat * l_mat * dt_t[h:h + 1, :]).astype(BF16)
            c_sc = (c_g * jnp.exp(cs_col)).astype(BF16)
            xs_h = xs[:, pcols]
            xs_hb = xs_h.astype(BF16)
            st_h = st_scr[:, pcols]
            lhs = jnp.concatenate([m_h, c_sc], axis=1)
            rhs = jnp.concatenate([xs_hb, st_h.astype(BF16)], axis=0)
            y_scr[:, pcols] = _bdot(lhs, rhs) + dexp_ref[:, pcols] * xs_h
            b_sc = (b_gt * w_t[h:h + 1, :]).astype(BF16)
            st_scr[:, pcols] = chunk_decay_col[h:h + 1, :] * st_h + _bdot(b_sc, xs_hb)

    yg = y_scr[...] * _silu(z_ref[...])
    ms = jnp.mean(yg * yg, axis=-1, keepdims=True)
    y_ref[...] = (yg * lax.rsqrt(ms + EPS) * ng_ref[...]).astype(y_ref.dtype)

    @pl.when(c == pl.num_programs(1) - 1)
    def _():
        st_ref[...] = st_scr[...].T


def _ssd_prompt(proj, dt_raw, batch, seq, cw, cb, dtb, alog, dexp, ng):
    lc = SSD_CHUNK
    nc = seq // lc
    full = lambda shape: pl.BlockSpec(shape, lambda b, c: (0,) * len(shape))
    z_blk = (2 * W_LRU) // W_SSD
    xbc_blk = (2 * W_LRU + W_SSD) // SSD_CONV_DIM
    y, st = pl.pallas_call(
        _ssd_prompt_kernel,
        grid=(batch, nc),
        in_specs=[pl.BlockSpec((lc, W_SSD), lambda b, c: (b * nc + c, z_blk)),
                  pl.BlockSpec((lc, SSD_CONV_DIM), lambda b, c: (b * nc + c, xbc_blk)),
                  pl.BlockSpec((lc, LANES), lambda b, c: (b * nc + c, 0)),
                  full((CONV_W, SSD_CONV_DIM)), full((1, SSD_CONV_DIM)),
                  full((1, LANES)), full((1, LANES)), full((1, W_SSD)), full((1, W_SSD))],
        out_specs=[pl.BlockSpec((lc, W_SSD), lambda b, c: (b * nc + c, 0)),
                   pl.BlockSpec((None, W_SSD, SSD_STATE), lambda b, c: (b, 0, 0))],
        out_shape=[jax.ShapeDtypeStruct((batch * seq, W_SSD), BF16),
                   jax.ShapeDtypeStruct((batch, W_SSD, SSD_STATE), F32)],
        scratch_shapes=[pltpu.VMEM((lc + SUBLANES, SSD_CONV_DIM), F32),
                        pltpu.VMEM((SSD_STATE, W_SSD), F32),
                        pltpu.VMEM((lc, W_SSD), F32)],
        compiler_params=_params(("parallel", "arbitrary")),
        name="ssd_prompt",
    )(proj, proj, dt_raw, cw, cb, dtb, alog, dexp, ng)
    return y, st.reshape(batch, SSD_HEADS, SSD_HEAD_DIM, SSD_STATE)


def _sample_pre_kernel(proj_ref, dt_ref, h0_ref, lconv_ref, sconv_ref,
                       lcw_ref, lcb_ref, wg_ref, ba_ref, bi_ref, lam_ref,
                       scw_ref, scb_ref, dtb_ref, alog_ref,
                       outl_ref, hnew_ref, lconv_new_ref, sconv_new_ref,
                       xs_ref, xdt_ref, bc_ref, dec_ref):
    nb = proj_ref.shape[0]
    xl = proj_ref[:, 0:W_LRU]
    gl = proj_ref[:, W_LRU:2 * W_LRU]
    xbc = proj_ref[:, 2 * W_LRU + W_SSD:IN_MAIN]

    def conv1(state_ref, width, x_new, w_ref, b_ref):
        y = b_ref[...] + w_ref[0:1, :] * state_ref[:, 0:width]
        y = y + w_ref[1:2, :] * state_ref[:, width:2 * width]
        y = y + w_ref[2:3, :] * state_ref[:, 2 * width:3 * width]
        return y + w_ref[3:4, :] * x_new

    xc = conv1(lconv_ref, W_LRU, xl, lcw_ref, lcb_ref)
    a, bt = _lru_gates(xc, wg_ref, ba_ref[...], bi_ref[...], _softplus(-lam_ref[...]))
    h_new = a * h0_ref[...] + bt
    hnew_ref[...] = h_new
    outl_ref[...] = (h_new * _gelu_tanh(gl)).astype(outl_ref.dtype)
    lconv_new_ref[:, 0:2 * W_LRU] = lconv_ref[:, W_LRU:3 * W_LRU]
    lconv_new_ref[:, 2 * W_LRU:3 * W_LRU] = xl

    act = _silu(conv1(sconv_ref, SSD_CONV_DIM, xbc, scw_ref, scb_ref))
    sconv_new_ref[:, 0:2 * SSD_CONV_DIM] = sconv_ref[:, SSD_CONV_DIM:3 * SSD_CONV_DIM]
    sconv_new_ref[:, 2 * SSD_CONV_DIM:3 * SSD_CONV_DIM] = xbc
    xs = act[:, :W_SSD]
    xs_ref[...] = xs
    bc_ref[...] = act[:, W_SSD:]
    dt = _softplus(dt_ref[...] + dtb_ref[...])
    dec = jnp.exp(dt * (-jnp.exp(alog_ref[...])))
    for h in range(SSD_HEADS):
        pcols = slice(h * SSD_HEAD_DIM, (h + 1) * SSD_HEAD_DIM)
        xdt_ref[:, pcols] = xs[:, pcols] * jnp.broadcast_to(dt[:, h:h + 1], (nb, SSD_HEAD_DIM))
        dec_ref[h] = jnp.broadcast_to(dec[:, h:h + 1], (nb, SSD_STATE))


def _sample_pre(proj, dt_raw, h0, lconv, sconv, p):
    nb = proj.shape[0]
    out_shape = [jax.ShapeDtypeStruct((nb, W_LRU), BF16),
                 jax.ShapeDtypeStruct((nb, W_LRU), F32),
                 jax.ShapeDtypeStruct((nb, 3 * W_LRU), F32),
                 jax.ShapeDtypeStruct((nb, 3 * SSD_CONV_DIM), F32),
                 jax.ShapeDtypeStruct((nb, W_SSD), F32),
                 jax.ShapeDtypeStruct((nb, W_SSD), F32),
                 jax.ShapeDtypeStruct((nb, 2 * SSD_GROUPS * SSD_STATE), F32),
                 jax.ShapeDtypeStruct((SSD_HEADS, nb, SSD_STATE), F32)]
    return pl.pallas_call(
        _sample_pre_kernel,
        out_shape=out_shape,
        compiler_params=pltpu.CompilerParams(vmem_limit_bytes=VMEM_LIMIT_BYTES),
        name="sample_pre",
    )(proj, dt_raw, h0, lconv, sconv,
      p["lru_cw"], p["lru_cb"], p["lru_wg"], p["lru_ba"], p["lru_bi"], p["lru_lam"],
      p["ssd_cw"], p["ssd_cb"], p["ssd_dtb"], p["ssd_alog"])


def _sample_state_kernel(s_ref, xdt_ref, bc_ref, dec_ref, o_ref, y_ref):
    bb = s_ref.shape[0]
    half = SSD_HPG * SSD_HEAD_DIM
    rid = lax.broadcasted_iota(jnp.int32, (bb, W_SSD), 0)
    xdt = xdt_ref[...]
    bcb = bc_ref[...].astype(BF16)
    for k in range(bb):
        xk = jnp.where(rid == k, xdt, 0.0).astype(BF16)
        for g in range(SSD_GROUPS):
            rows = slice(g * half, (g + 1) * half)
            b_g = bcb[:, g * SSD_STATE:(g + 1) * SSD_STATE]
            c_g = bcb[:, (SSD_GROUPS + g) * SSD_STATE:(SSD_GROUPS + g + 1) * SSD_STATE]
            outer = lax.dot_general(xk[:, rows], b_g, (((0,), (0,)), ((), ())),
                                    preferred_element_type=F32)
            dec = jnp.concatenate(
                [jnp.broadcast_to(dec_ref[g * SSD_HPG + e, k:k + 1, :], (SSD_HEAD_DIM, SSD_STATE))
                 for e in range(SSD_HPG)], axis=0)
            s_new = dec * s_ref[k, rows, :] + outer
            o_ref[k, rows, :] = s_new
            yk = lax.dot_general(c_g, s_new.astype(BF16), (((1,), (1,)), ((), ())),
                                 preferred_element_type=F32)
            y_ref[k:k + 1, rows] = yk[k:k + 1, :]


def _sample_state(ssm, xdt, bc, dec, bb=8):
    nb = ssm.shape[0]
    return pl.pallas_call(
        _sample_state_kernel,
        grid=(nb // bb,),
        in_specs=[pl.BlockSpec((bb, W_SSD, SSD_STATE), lambda i: (i, 0, 0)),
                  pl.BlockSpec((bb, W_SSD), lambda i: (i, 0)),
                  pl.BlockSpec((bb, 2 * SSD_GROUPS * SSD_STATE), lambda i: (i, 0)),
                  pl.BlockSpec((SSD_HEADS, bb, SSD_STATE), lambda i: (0, i, 0))],
        out_specs=[pl.BlockSpec((bb, W_SSD, SSD_STATE), lambda i: (i, 0, 0)),
                   pl.BlockSpec((bb, W_SSD), lambda i: (i, 0))],
        out_shape=[jax.ShapeDtypeStruct(ssm.shape, F32),
                   jax.ShapeDtypeStruct((nb, W_SSD), F32)],
        compiler_params=_params(("parallel",)),
        name="sample_state",
    )(ssm, xdt, bc, dec)


def _sample_post_kernel(y_ref, xs_ref, proj_ref, dexp_ref, ng_ref, o_ref):
    z = proj_ref[:, 2 * W_LRU:2 * W_LRU + W_SSD]
    yg = (y_ref[...] + dexp_ref[...] * xs_ref[...]) * _silu(z)
    ms = jnp.mean(yg * yg, axis=-1, keepdims=True)
    o_ref[...] = (yg * lax.rsqrt(ms + EPS) * ng_ref[...]).astype(o_ref.dtype)


def _sample_post(y_raw, xs, proj, dexp, ng):
    return pl.pallas_call(
        _sample_post_kernel,
        out_shape=jax.ShapeDtypeStruct(y_raw.shape, BF16),
        compiler_params=pltpu.CompilerParams(vmem_limit_bytes=VMEM_LIMIT_BYTES),
        name="sample_post",
    )(y_raw, xs, proj, dexp, ng)


def _block_diag_groups(w):
    per = LRU_GATE_GROUP // LRU_BLOCK
    w4 = w.reshape(LRU_HEADS // per, per, LRU_BLOCK, LRU_BLOCK)
    bd = jnp.einsum("ghij,hk->ghikj", w4, jnp.eye(per, dtype=w.dtype))
    return bd.reshape(LRU_HEADS // per, LRU_GATE_GROUP, LRU_GATE_GROUP)


def _pad_lanes(v):
    v = v.reshape(1, -1)
    return jnp.pad(v, ((0, 0), (0, LANES - v.shape[1])))


def kernel(x_prompt, x_sample, c_prompt, c_sample, state_lru_h, state_lru_conv, state_ssm, state_ssd_conv, w_ada, b_ada, g_ffn1, w_up1, w_down1, g_mix, w_in, lru_conv_w, lru_conv_b, lru_wa, lru_ba, lru_wi, lru_bi, lru_lambda, ssd_conv_w, ssd_conv_b, ssd_dt_bias, ssd_A_log, ssd_D, ssd_norm_g, w_out, g_ffn2, w_up2, w_down2, w_ada_f, b_ada_f, g_final):
    bp, seq, d = x_prompt.shape
    bs = x_sample.shape[0]
    depth = w_ada.shape[0]
    assert depth == 1 and x_sample.shape[1] == 1 and d == D_MODEL

    pad_rows = (-(bs + bp)) % (2 * SUBLANES)
    c_all = jnp.concatenate([c_sample, c_prompt, jnp.zeros((pad_rows, d), F32)], axis=0)
    mod_all = _ada(c_all, w_ada[0], b_ada[0])
    modf_all = _ada(c_all, w_ada_f, b_ada_f)
    mod_s = mod_all.reshape(1, bs + bp + pad_rows, N_MOD * d)[:, :bs]
    mod_p = mod_all[bs:bs + bp].reshape(bp, 1, N_MOD * d)
    modf_s = modf_all.reshape(1, bs + bp + pad_rows, 2 * d)[:, :bs]
    modf_p = modf_all[bs:bs + bp].reshape(bp, 1, 2 * d)

    w_in_t = jnp.swapaxes(w_in[0], 0, 1)
    w_dt_t = jnp.pad(w_in_t[IN_MAIN:], ((0, LANES - SSD_HEADS), (0, 0)))
    up_blocks = D_FF // 512
    w_f32 = {
        "up1": [(w_up1[0], 0), (w_up1[0], up_blocks)], "down1": [(w_down1[0], 0)],
        "in": [(w_in_t, 0)], "out": [(w_out[0], 0), (w_out[0], 1)],
        "up2": [(w_up2[0], 0), (w_up2[0], up_blocks)], "down2": [(w_down2[0], 0)],
    }
    p = {
        "lru_cw": lru_conv_w[0], "lru_cb": lru_conv_b[0].reshape(1, W_LRU),
        "lru_wg": jnp.concatenate([_block_diag_groups(lru_wa[0]), _block_diag_groups(lru_wi[0])],
                                  axis=-1),
        "lru_ba": lru_ba[0].reshape(1, W_LRU), "lru_bi": lru_bi[0].reshape(1, W_LRU),
        "lru_lam": lru_lambda[0].reshape(1, W_LRU),
        "ssd_cw": ssd_conv_w[0], "ssd_cb": ssd_conv_b[0].reshape(1, SSD_CONV_DIM),
        "ssd_dtb": _pad_lanes(ssd_dt_bias[0]), "ssd_alog": _pad_lanes(ssd_A_log[0]),
        "ssd_dexp": jnp.repeat(ssd_D[0], SSD_HEAD_DIM).reshape(1, W_SSD),
        "ssd_ng": ssd_norm_g[0].reshape(1, W_SSD),
    }

    def trunk(x, mod, modf, tm, mixer, wts, emit, tn_down):
        wb = {}

        def zero_off(copies):
            return [(w, 0) for w in copies]

        def ffn(x, gain, chunk, up, down):
            hmid, *wb[up] = _normmod_matmul(x, gain, mod, chunk, wts[up], n_out=D_FF, tm=tm,
                                            tn=512, swiglu=True, out_dtype=BF16, emit_bf16=emit)
            x, *wb[down] = _matmul_resid([hmid], wts[down], x, mod, chunk + 2, factor=0.5,
                                         tm=tm, tn=tn_down, emit_bf16=emit)
            return x

        x = ffn(x, g_ffn1[0], 0, "up1", "down1")
        proj, dt_raw, *wb["in"] = _normmod_matmul(
            x, g_mix[0], mod, 3, wts["in"], n_out=IN_MAIN, tm=tm, tn=512, swiglu=False,
            out_dtype=F32, trans_w=True, w_extra=w_dt_t, emit_bf16=emit)
        out_l, y_ssd, new_state = mixer(proj, dt_raw)
        x, *wb["out"] = _matmul_resid([out_l, y_ssd], wts["out"], x, mod, 5, factor=1.0, tm=tm,
                                      tn=512, emit_bf16=emit)
        x = ffn(x, g_ffn2[0], 6, "up2", "down2")
        y = _final_norm(x, g_final, modf, tm=min(tm, 512))
        return y, new_state, {k: zero_off(v) for k, v in wb.items()}

    def prompt_mixer(proj, dt_raw):
        out_l, lru_h = _lru_prompt(proj, bp, seq, p["lru_cw"], p["lru_cb"], p["lru_wg"],
                                   p["lru_ba"], p["lru_bi"], p["lru_lam"])
        y_ssd, ssm = _ssd_prompt(proj, dt_raw, bp, seq, p["ssd_cw"], p["ssd_cb"], p["ssd_dtb"],
                                 p["ssd_alog"], p["ssd_dexp"], p["ssd_ng"])
        proj3 = proj.reshape(bp, seq, IN_MAIN)
        lru_buf = proj3[:, seq - (CONV_W - 1):, :W_LRU]
        ssd_buf = proj3[:, seq - (CONV_W - 1):, 2 * W_LRU + W_SSD:]
        return out_l, y_ssd, (lru_h, lru_buf, ssm, ssd_buf)

    def sample_mixer(proj, dt_raw):
        lconv = state_lru_conv[0].reshape(bs, (CONV_W - 1) * W_LRU)
        sconv = state_ssd_conv[0].reshape(bs, (CONV_W - 1) * SSD_CONV_DIM)
        out_l, h_new, lconv_new, sconv_new, xs, xdt, bc, dec = _sample_pre(
            proj, dt_raw, state_lru_h[0], lconv, sconv, p)
        ssm_new, y_raw = _sample_state(state_ssm[0].reshape(bs, W_SSD, SSD_STATE), xdt, bc, dec)
        y_ssd = _sample_post(y_raw, xs, proj, p["ssd_dexp"], p["ssd_ng"])
        return out_l, y_ssd, (h_new, lconv_new.reshape(bs, CONV_W - 1, W_LRU),
                              ssm_new.reshape(bs, SSD_HEADS, SSD_HEAD_DIM, SSD_STATE),
                              sconv_new.reshape(bs, CONV_W - 1, SSD_CONV_DIM))

    ys, ss, w_bf16 = trunk(x_sample.reshape(bs, d), mod_s, modf_s, bs, sample_mixer, w_f32,
                           emit=True, tn_down=256)
    yp, sp, _ = trunk(x_prompt.reshape(bp * seq, d), mod_p, modf_p, 1024, prompt_mixer, w_bf16,
                      emit=False, tn_down=512)

    stack = lambda v: v[None]
    return (yp.reshape(bp, seq, d), ys.reshape(bs, 1, d),
            stack(sp[0]), stack(sp[1]), stack(sp[2]), stack(sp[3]),
            stack(ss[0]), stack(ss[1]), stack(ss[2]), stack(ss[3]))
```

```python
import functools

import jax
import jax.numpy as jnp
from jax import lax
from jax.experimental import pallas as pl
from jax.experimental.pallas import tpu as pltpu

F32 = jnp.float32
BF16 = jnp.bfloat16

D_MODEL = 2048
D_FF = 5632
W_LRU = 1024
W_SSD = 1024
LRU_HEADS = 16
LRU_BLOCK = 64
LRU_C = 8.0
SSD_HEADS = 16
SSD_HEAD_DIM = 64
SSD_GROUPS = 2
SSD_HPG = 8
SSD_STATE = 128
SSD_CHUNK = 128
CONV_W = 4
SSD_CONV_DIM = W_SSD + 2 * SSD_GROUPS * SSD_STATE
IN_MAIN = 2 * W_LRU + W_SSD + SSD_CONV_DIM
N_MOD = 9
EPS = 1e-6

LANES = 128
SUBLANES = 8
VMEM_LIMIT_BYTES = 56 * 1024 * 1024

LRU_GATE_GROUP = 256
LRU_TIME_TILE = 256
SCAN_ROWS = 2 * SUBLANES


def _silu(v):
    return v * jax.nn.sigmoid(v)


def _softplus(v):
    return jnp.maximum(v, 0.0) + jnp.log1p(jnp.exp(-jnp.abs(v)))


def _gelu_tanh(v):
    return 0.5 * v * (1.0 + jnp.tanh(0.7978845608028654 * (v + 0.044715 * (v * v * v))))


def _bdot(a, b):
    return jnp.dot(a, b, preferred_element_type=F32)


def _params(sem):
    return pltpu.CompilerParams(dimension_semantics=sem, vmem_limit_bytes=VMEM_LIMIT_BYTES)


def _ada_kernel(c_ref, w_ref, b_ref, o_ref):
    s = _silu(c_ref[...]).astype(BF16)
    o_ref[...] = _bdot(s, w_ref[...].astype(BF16)) + b_ref[...]


def _ada(c, w, b, tn=1024):
    m, k = c.shape
    n = w.shape[1]
    return pl.pallas_call(
        _ada_kernel,
        grid=(n // tn,),
        in_specs=[pl.BlockSpec((m, k), lambda j: (0, 0)),
                  pl.BlockSpec((k, tn), lambda j: (0, j)),
                  pl.BlockSpec((1, tn), lambda j: (0, j))],
        out_specs=pl.BlockSpec((m, tn), lambda j: (0, j)),
        out_shape=jax.ShapeDtypeStruct((m, n), F32),
        compiler_params=_params(("parallel",)),
        name="ada_proj",
    )(c, w, b.reshape(1, n))


def _norm_modulate(x, gain, shift, scale):
    ms = jnp.mean(x * x, axis=-1, keepdims=True)
    y = x * lax.rsqrt(ms + EPS) * gain
    return y * (1.0 + scale) + shift


def _wdot(h, w, trans_w):
    if trans_w:
        return lax.dot_general(h, w, (((1,), (1,)), ((), ())), preferred_element_type=F32)
    return _bdot(h, w)


def _normmod_matmul_kernel(*refs, n_w, swiglu, has_extra, emit_bf16, trans_w, prenormed,
                           row_chunk):
    n_lead = 1 if prenormed else 4
    w_refs = refs[n_lead:n_lead + n_w]
    pos = n_lead + n_w
    wx_ref = refs[pos] if has_extra else None
    pos += int(has_extra)
    o_ref = refs[pos]
    pos += 1
    ox_ref = refs[pos] if has_extra else None
    pos += int(has_extra)
    wo_refs = refs[pos:pos + n_w] if emit_bf16 else ()

    if prenormed:
        h_src = refs[0]
    else:
        x_ref, gain_ref, sh_ref, sc_ref = refs[:4]
        h_src = refs[-1]
        tm = x_ref.shape[0]
        per_row = sh_ref.shape[0] != 1

    @pl.when(pl.program_id(1) == 0)
    def _():
        if not prenormed:
            gain = gain_ref[...]

            def body(r, carry):
                rows = pl.ds(pl.multiple_of(r * row_chunk, row_chunk), row_chunk)
                sh = sh_ref[rows, :] if per_row else sh_ref[...]
                sc = sc_ref[rows, :] if per_row else sc_ref[...]
                h_src[rows, :] = _norm_modulate(x_ref[rows, :], gain, sh, sc).astype(BF16)
                return carry

            lax.fori_loop(0, tm // row_chunk, body, 0)
        if has_extra:
            ox_ref[...] = _wdot(h_src[...], wx_ref[...].astype(BF16), trans_w)

    h = h_src[...]
    wbs = [w_ref[...].astype(BF16) for w_ref in w_refs]
    for wo_ref, wb in zip(wo_refs, wbs):
        wo_ref[...] = wb
    if swiglu:
        g = _wdot(h, wbs[0], trans_w)
        u = _wdot(h, wbs[1], trans_w)
        o_ref[...] = (_silu(g) * u).astype(o_ref.dtype)
    else:
        o_ref[...] = _wdot(h, wbs[0], trans_w).astype(o_ref.dtype)


def _normmod_matmul(x, gain, mod, shift_chunk, ws, *, n_out, tm, tn, swiglu, out_dtype,
                    trans_w=False, w_extra=None, emit_bf16=False):
    m, d = x.shape
    prenormed = mod is None
    nj = n_out // tn
    row_chunk = min(tm, 128)

    def w_spec(off):
        if trans_w:
            return pl.BlockSpec((tn, d), lambda i, j: (j + off, 0))
        return pl.BlockSpec((d, tn), lambda i, j: (0, j + off))

    in_specs = [pl.BlockSpec((tm, d), lambda i, j: (i, 0))]
    args = [x]
    if not prenormed:
        groups, r, _ = mod.shape
        tiles_per_group = (m // tm) // groups

        def mod_spec(chunk):
            return pl.BlockSpec((None, r, d), lambda i, j: (i // tiles_per_group, 0, chunk))

        in_specs += [pl.BlockSpec((1, d), lambda i, j: (0, 0)),
                     mod_spec(shift_chunk), mod_spec(shift_chunk + 1)]
        args += [gain.reshape(1, d), mod, mod]
    in_specs += [w_spec(off) for _, off in ws]
    args += [w for w, _ in ws]
    out_specs = [pl.BlockSpec((tm, tn), lambda i, j: (i, j))]
    out_shape = [jax.ShapeDtypeStruct((m, n_out), out_dtype)]
    if w_extra is not None:
        nx = w_extra.shape[0] if trans_w else w_extra.shape[1]
        in_specs.append(pl.BlockSpec(w_extra.shape, lambda i, j: (0, 0)))
        args.append(w_extra)
        out_specs.append(pl.BlockSpec((tm, nx), lambda i, j: (i, 0)))
        out_shape.append(jax.ShapeDtypeStruct((m, nx), F32))
    if emit_bf16:
        assert m == tm, "weight copies are complete only when every weight tile is visited once"
        for _ in ws:
            out_specs.append(w_spec(0))
            out_shape.append(jax.ShapeDtypeStruct((n_out, d) if trans_w else (d, n_out), BF16))
    return pl.pallas_call(
        functools.partial(_normmod_matmul_kernel, n_w=len(ws), swiglu=swiglu,
                          has_extra=w_extra is not None, emit_bf16=emit_bf16, trans_w=trans_w,
                          prenormed=prenormed, row_chunk=row_chunk),
        grid=(m // tm, nj),
        in_specs=in_specs,
        out_specs=out_specs,
        out_shape=out_shape,
        scratch_shapes=[] if prenormed else [pltpu.VMEM((tm, d), BF16)],
        compiler_params=_params(("parallel", "arbitrary")),
        name="normmod_matmul_swiglu" if swiglu else "normmod_matmul",
    )(*args)


def _matmul_resid_kernel(*refs, n_lhs, factor, emit_bf16):
    lhs_refs = refs[:n_lhs]
    w_refs = refs[n_lhs:2 * n_lhs]
    x_ref, gate_ref, o_ref = refs[2 * n_lhs:2 * n_lhs + 3]
    wo_refs = refs[2 * n_lhs + 3:] if emit_bf16 else ()
    wbs = [w_ref[...].astype(BF16) for w_ref in w_refs]
    for wo_ref, wb in zip(wo_refs, wbs):
        wo_ref[...] = wb
    acc = _bdot(lhs_refs[0][...], wbs[0])
    for l_ref, wb in zip(lhs_refs[1:], wbs[1:]):
        acc = acc + _bdot(l_ref[...], wb)
    o_ref[...] = x_ref[...] + (factor * gate_ref[...]) * acc


def _matmul_resid(lhs_list, ws, x, mod, gate_chunk, *, factor, tm, tn, emit_bf16=False):
    m, d = x.shape
    groups, r, _ = mod.shape
    tiles_per_group = (m // tm) // groups
    kp = lhs_list[0].shape[1]
    gate_blocks = d // tn
    in_specs = [pl.BlockSpec((tm, kp), lambda i, j: (i, 0)) for _ in lhs_list]
    in_specs += [pl.BlockSpec((kp, tn), lambda i, j, k=k: (k, j)) for _, k in ws]
    in_specs += [pl.BlockSpec((tm, tn), lambda i, j: (i, j)),
                 pl.BlockSpec((None, r, tn),
                              lambda i, j: (i // tiles_per_group, 0, gate_chunk * gate_blocks + j))]
    out_specs = [pl.BlockSpec((tm, tn), lambda i, j: (i, j))]
    out_shape = [jax.ShapeDtypeStruct((m, d), F32)]
    if emit_bf16:
        assert m == tm, "weight copies are complete only when every weight tile is visited once"
        for _ in ws:
            out_specs.append(pl.BlockSpec((kp, tn), lambda i, j: (0, j)))
            out_shape.append(jax.ShapeDtypeStruct((kp, d), BF16))
    return pl.pallas_call(
        functools.partial(_matmul_resid_kernel, n_lhs=len(lhs_list), factor=factor,
                          emit_bf16=emit_bf16),
        grid=(m // tm, d // tn),
        in_specs=in_specs,
        out_specs=out_specs,
        out_shape=out_shape,
        compiler_params=_params(("parallel", "arbitrary")),
        name="matmul_resid",
    )(*lhs_list, *[w for w, _ in ws], x, mod)


def _rowblock_resid_kernel(*refs, n_lhs, factor, emit_x):
    lhs_refs = refs[:n_lhs]
    w_refs = refs[n_lhs:2 * n_lhs]
    x_ref, gate_ref, gain_ref, sh_ref, sc_ref = refs[2 * n_lhs:2 * n_lhs + 5]
    out_refs = refs[2 * n_lhs + 5:]
    acc = _bdot(lhs_refs[0][...], w_refs[0][...])
    for l_ref, w_ref in zip(lhs_refs[1:], w_refs[1:]):
        acc = acc + _bdot(l_ref[...], w_ref[...])
    x_new = x_ref[...] + (factor * gate_ref[...]) * acc
    if emit_x:
        out_refs[0][...] = x_new
    h_ref = out_refs[-1]
    h_ref[...] = _norm_modulate(x_new, gain_ref[...], sh_ref[...], sc_ref[...]).astype(h_ref.dtype)


def _rowblock_resid(lhs_list, ws, x, mod, gate_chunk, gain_next, mod_next, shift_chunk_next, *,
                    factor, tm, emit_x, h_dtype):
    m, d = x.shape
    groups, r, _ = mod.shape
    assert r == 1 and mod_next.shape[:2] == (groups, 1)
    tiles_per_group = (m // tm) // groups
    kp = lhs_list[0].shape[1]

    def mod_spec(chunk):
        return pl.BlockSpec((None, 1, d), lambda i: (i // tiles_per_group, 0, chunk))

    in_specs = [pl.BlockSpec((tm, kp), lambda i: (i, 0)) for _ in lhs_list]
    in_specs += [pl.BlockSpec((kp, d), lambda i: (0, 0), pipeline_mode=pl.Buffered(1))
                 for _ in ws]
    in_specs += [pl.BlockSpec((tm, d), lambda i: (i, 0)), mod_spec(gate_chunk),
                 pl.BlockSpec((1, d), lambda i: (0, 0)),
                 mod_spec(shift_chunk_next), mod_spec(shift_chunk_next + 1)]
    out_specs = [pl.BlockSpec((tm, d), lambda i: (i, 0))]
    out_shape = [jax.ShapeDtypeStruct((m, d), h_dtype)]
    if emit_x:
        out_specs = [pl.BlockSpec((tm, d), lambda i: (i, 0))] + out_specs
        out_shape = [jax.ShapeDtypeStruct((m, d), F32)] + out_shape
    return pl.pallas_call(
        functools.partial(_rowblock_resid_kernel, n_lhs=len(lhs_list), factor=factor,
                          emit_x=emit_x),
        grid=(m // tm,),
        in_specs=in_specs,
        out_specs=out_specs,
        out_shape=out_shape,
        compiler_params=_params(("parallel",)),
        name="rowblock_resid",
    )(*lhs_list, *ws, x, mod, gain_next.reshape(1, d), mod_next, mod_next)


def _final_kernel(x_ref, gain_ref, sh_ref, sc_ref, o_ref):
    o_ref[...] = _norm_modulate(x_ref[...], gain_ref[...], sh_ref[...], sc_ref[...])


def _final_norm(x, gain, mod, *, tm):
    m, d = x.shape
    groups, r, _ = mod.shape
    tiles_per_group = (m // tm) // groups
    return pl.pallas_call(
        _final_kernel,
        grid=(m // tm,),
        in_specs=[pl.BlockSpec((tm, d), lambda i: (i, 0)),
                  pl.BlockSpec((1, d), lambda i: (0, 0)),
                  pl.BlockSpec((None, r, d), lambda i: (i // tiles_per_group, 0, 0)),
                  pl.BlockSpec((None, r, d), lambda i: (i // tiles_per_group, 0, 1))],
        out_specs=pl.BlockSpec((tm, d), lambda i: (i, 0)),
        out_shape=jax.ShapeDtypeStruct((m, d), F32),
        compiler_params=_params(("parallel",)),
        name="final_norm",
    )(x, gain.reshape(1, d), mod, mod)


def _lru_gates(xc, wg_ref, ba, bi, sp):
    a_parts, b_parts = [], []
    for g in range(W_LRU // LRU_GATE_GROUP):
        cols = slice(g * LRU_GATE_GROUP, (g + 1) * LRU_GATE_GROUP)
        xg = xc[:, cols]
        ri = _bdot(xg.astype(BF16), wg_ref[g].astype(BF16))
        r = jax.nn.sigmoid(ri[:, :LRU_GATE_GROUP] + ba[:, cols])
        i = jax.nn.sigmoid(ri[:, LRU_GATE_GROUP:] + bi[:, cols])
        log_a = (-LRU_C * r) * sp[:, cols]
        a = jnp.exp(log_a)
        a_parts.append(a)
        b_parts.append(jnp.sqrt(1.0 - a * a) * (i * xg))
    return jnp.concatenate(a_parts, axis=1), jnp.concatenate(b_parts, axis=1)


def _causal_conv_from_buf(buf_ref, x, w_ref, b_ref, rows):
    y = b_ref[...] + w_ref[0:1, :] * buf_ref[pl.ds(SUBLANES - 3, rows), :]
    y = y + w_ref[1:2, :] * buf_ref[pl.ds(SUBLANES - 2, rows), :]
    y = y + w_ref[2:3, :] * buf_ref[pl.ds(SUBLANES - 1, rows), :]
    return y + w_ref[3:4, :] * x


def _lru_prompt_kernel(xl_ref, gl_ref, cw_ref, cb_ref, wg_ref, ba_ref, bi_ref, lam_ref,
                       o_ref, hT_ref, xbuf, a_scr, b_scr, hcar):
    t = pl.program_id(1)
    tt = xl_ref.shape[0]

    @pl.when(t == 0)
    def _():
        xbuf[0:SUBLANES, :] = jnp.zeros((SUBLANES, W_LRU), F32)
        hcar[...] = jnp.zeros_like(hcar)

    x = xl_ref[...]
    xbuf[SUBLANES:SUBLANES + tt, :] = x
    xc = _causal_conv_from_buf(xbuf, x, cw_ref, cb_ref, tt)
    xbuf[0:SUBLANES, :] = x[tt - SUBLANES:, :]

    sp = _softplus(-lam_ref[...])
    a, bt = _lru_gates(xc, wg_ref, ba_ref[...], bi_ref[...], sp)
    a_scr[...] = a
    b_scr[...] = bt

    rid = lax.broadcasted_iota(jnp.int32, (SUBLANES, W_LRU), 0)

    def scan8(a8, b8, h_in):
        for s in (1, 2, 4):
            a_sh = pltpu.roll(a8, s, 0)
            b_sh = pltpu.roll(b8, s, 0)
            m = rid >= s
            b8 = jnp.where(m, a8 * b_sh + b8, b8)
            a8 = jnp.where(m, a8 * a_sh, a8)
        h8 = a8 * h_in + b8
        return h8, jnp.broadcast_to(h8[SUBLANES - 1:SUBLANES, :], (SUBLANES, W_LRU))

    def body(g, h_in):
        r0 = pl.multiple_of(g * SCAN_ROWS, SCAN_ROWS)
        lo = pl.ds(r0, SUBLANES)
        hi = pl.ds(r0 + SUBLANES, SUBLANES)
        h_lo, h_mid = scan8(a_scr[lo, :], b_scr[lo, :], h_in)
        h_hi, h_out = scan8(a_scr[hi, :], b_scr[hi, :], h_mid)
        rows = pl.ds(r0, SCAN_ROWS)
        h16 = jnp.concatenate([h_lo, h_hi], axis=0)
        o_ref[rows, :] = (h16 * _gelu_tanh(gl_ref[rows, :])).astype(o_ref.dtype)
        return h_out

    h_last = lax.fori_loop(0, tt // SCAN_ROWS, body, hcar[...])
    hcar[...] = h_last

    @pl.when(t == pl.num_programs(1) - 1)
    def _():
        hT_ref[...] = h_last[0:1, :]


def _lru_prompt(proj, batch, seq, cw, cb, wg, ba, bi, lam):
    tt = LRU_TIME_TILE
    nt = seq // tt
    row = lambda v: v.reshape(1, W_LRU)
    full = lambda shape: pl.BlockSpec(shape, lambda b, t: (0,) * len(shape))
    out, h_t = pl.pallas_call(
        _lru_prompt_kernel,
        grid=(batch, nt),
        in_specs=[pl.BlockSpec((tt, W_LRU), lambda b, t: (b * nt + t, 0)),
                  pl.BlockSpec((tt, W_LRU), lambda b, t: (b * nt + t, 1)),
                  full((CONV_W, W_LRU)), full((1, W_LRU)), full(wg.shape),
                  full((1, W_LRU)), full((1, W_LRU)), full((1, W_LRU))],
        out_specs=[pl.BlockSpec((tt, W_LRU), lambda b, t: (b * nt + t, 0)),
                   pl.BlockSpec((None, 1, W_LRU), lambda b, t: (b, 0, 0))],
        out_shape=[jax.ShapeDtypeStruct((batch * seq, W_LRU), BF16),
                   jax.ShapeDtypeStruct((batch, 1, W_LRU), F32)],
        scratch_shapes=[pltpu.VMEM((tt + SUBLANES, W_LRU), F32),
                        pltpu.VMEM((tt, W_LRU), F32),
                        pltpu.VMEM((tt, W_LRU), F32),
                        pltpu.VMEM((SUBLANES, W_LRU), F32)],
        compiler_params=_params(("parallel", "arbitrary")),
        name="lru_prompt",
    )(proj, proj, cw, row(cb), wg, row(ba), row(bi), row(lam))
    return out, h_t.reshape(batch, W_LRU)


def _ssd_prompt_kernel(z_ref, xbc_ref, dt_ref, cw_ref, cb_ref, dtb_ref, alog_ref, dexp_ref,
                       ng_ref, y_ref, st_ref, xbuf, st_scr, y_scr):
    c = pl.program_id(1)
    lc = SSD_CHUNK

    @pl.when(c == 0)
    def _():
        xbuf[0:SUBLANES, :] = jnp.zeros((SUBLANES, SSD_CONV_DIM), F32)
        st_scr[...] = jnp.zeros_like(st_scr)

    x = xbc_ref[...]
    xbuf[SUBLANES:SUBLANES + lc, :] = x
    act = _silu(_causal_conv_from_buf(xbuf, x, cw_ref, cb_ref, lc))
    xbuf[0:SUBLANES, :] = x[lc - SUBLANES:, :]
    xs = act[:, :W_SSD]
    bm = act[:, W_SSD:W_SSD + SSD_GROUPS * SSD_STATE]
    cm = act[:, W_SSD + SSD_GROUPS * SSD_STATE:]

    dt = _softplus(dt_ref[...] + dtb_ref[...])
    d_a = dt * (-jnp.exp(alog_ref[...]))
    row_i = lax.broadcasted_iota(jnp.int32, (lc, lc), 0)
    col_i = lax.broadcasted_iota(jnp.int32, (lc, lc), 1)
    causal = row_i >= col_i
    tril = jnp.where(causal, 1.0, 0.0).astype(F32)
    cs = jnp.dot(tril, d_a, preferred_element_type=F32, precision=lax.Precision.HIGHEST)
    cs_t = cs.T
    dt_t = dt.T
    cs_last_col = cs_t[:, lc - 1:lc]
    w_t = jnp.exp(cs_last_col - cs_t) * dt_t
    chunk_decay_col = jnp.exp(cs_last_col)

    for g in range(SSD_GROUPS):
        ncols = slice(g * SSD_STATE, (g + 1) * SSD_STATE)
        b_g = bm[:, ncols]
        c_g = cm[:, ncols]
        cb_mat = lax.dot_general(c_g.astype(BF16), b_g.astype(BF16), (((1,), (1,)), ((), ())),
                                 preferred_element_type=F32)
        b_gt = b_g.T
        for e in range(SSD_HPG):
            h = g * SSD_HPG + e
            pcols = slice(h * SSD_HEAD_DIM, (h + 1) * SSD_HEAD_DIM)
            cs_col = jnp.broadcast_to(cs[:, h:h + 1], (lc, lc))
            cs_row = cs_t[h:h + 1, :]
            l_mat = jnp.exp(jnp.where(causal, cs_col - cs_row, -jnp.inf))
            m_h = (cb_mat * l_mat * dt_t[h:h + 1, :]).astype(BF16)
            c_sc = (c_g * jnp.exp(cs_col)).astype(BF16)
            xs_h = xs[:, pcols]
            xs_hb = xs_h.astype(BF16)
            st_h = st_scr[:, pcols]
            lhs = jnp.concatenate([m_h, c_sc], axis=1)
            rhs = jnp.concatenate([xs_hb, st_h.astype(BF16)], axis=0)
            y_scr[:, pcols] = _bdot(lhs, rhs) + dexp_ref[:, pcols] * xs_h
            b_sc = (b_gt * w_t[h:h + 1, :]).astype(BF16)
            st_scr[:, pcols] = chunk_decay_col[h:h + 1, :] * st_h + _bdot(b_sc, xs_hb)

    yg = y_scr[...] * _silu(z_ref[...])
    ms = jnp.mean(yg * yg, axis=-1, keepdims=True)
    y_ref[...] = (yg * lax.rsqrt(ms + EPS) * ng_ref[...]).astype(y_ref.dtype)

    @pl.when(c == pl.num_programs(1) - 1)
    def _():
        st_ref[...] = st_scr[...].T


def _ssd_prompt(proj, dt_raw, batch, seq, cw, cb, dtb, alog, dexp, ng):
    lc = SSD_CHUNK
    nc = seq // lc
    full = lambda shape: pl.BlockSpec(shape, lambda b, c: (0,) * len(shape))
    z_blk = (2 * W_LRU) // W_SSD
    xbc_blk = (2 * W_LRU + W_SSD) // SSD_CONV_DIM
    y, st = pl.pallas_call(
        _ssd_prompt_kernel,
        grid=(batch, nc),
        in_specs=[pl.BlockSpec((lc, W_SSD), lambda b, c: (b * nc + c, z_blk)),
                  pl.BlockSpec((lc, SSD_CONV_DIM), lambda b, c: (b * nc + c, xbc_blk)),
                  pl.BlockSpec((lc, LANES), lambda b, c: (b * nc + c, 0)),
                  full((CONV_W, SSD_CONV_DIM)), full((1, SSD_CONV_DIM)),
                  full((1, LANES)), full((1, LANES)), full((1, W_SSD)), full((1, W_SSD))],
        out_specs=[pl.BlockSpec((lc, W_SSD), lambda b, c: (b * nc + c, 0)),
                   pl.BlockSpec((None, W_SSD, SSD_STATE), lambda b, c: (b, 0, 0))],
        out_shape=[jax.ShapeDtypeStruct((batch * seq, W_SSD), BF16),
                   jax.ShapeDtypeStruct((batch, W_SSD, SSD_STATE), F32)],
        scratch_shapes=[pltpu.VMEM((lc + SUBLANES, SSD_CONV_DIM), F32),
                        pltpu.VMEM((SSD_STATE, W_SSD), F32),
                        pltpu.VMEM((lc, W_SSD), F32)],
        compiler_params=_params(("parallel", "arbitrary")),
        name="ssd_prompt",
    )(proj, proj, dt_raw, cw, cb, dtb, alog, dexp, ng)
    return y, st.reshape(batch, SSD_HEADS, SSD_HEAD_DIM, SSD_STATE)


def _sample_pre_kernel(proj_ref, dt_ref, h0_ref, lconv_ref, sconv_ref,
                       lcw_ref, lcb_ref, wg_ref, ba_ref, bi_ref, lam_ref,
                       scw_ref, scb_ref, dtb_ref, alog_ref,
                       outl_ref, hnew_ref, lconv_new_ref, sconv_new_ref,
                       xs_ref, xdt_ref, bc_ref, dec_ref):
    nb = proj_ref.shape[0]
    xl = proj_ref[:, 0:W_LRU]
    gl = proj_ref[:, W_LRU:2 * W_LRU]
    xbc = proj_ref[:, 2 * W_LRU + W_SSD:IN_MAIN]

    def conv1(state_ref, width, x_new, w_ref, b_ref):
        y = b_ref[...] + w_ref[0:1, :] * state_ref[:, 0:width]
        y = y + w_ref[1:2, :] * state_ref[:, width:2 * width]
        y = y + w_ref[2:3, :] * state_ref[:, 2 * width:3 * width]
        return y + w_ref[3:4, :] * x_new

    xc = conv1(lconv_ref, W_LRU, xl, lcw_ref, lcb_ref)
    a, bt = _lru_gates(xc, wg_ref, ba_ref[...], bi_ref[...], _softplus(-lam_ref[...]))
    h_new = a * h0_ref[...] + bt
    hnew_ref[...] = h_new
    outl_ref[...] = (h_new * _gelu_tanh(gl)).astype(outl_ref.dtype)
    lconv_new_ref[:, 0:2 * W_LRU] = lconv_ref[:, W_LRU:3 * W_LRU]
    lconv_new_ref[:, 2 * W_LRU:3 * W_LRU] = xl

    act = _silu(conv1(sconv_ref, SSD_CONV_DIM, xbc, scw_ref, scb_ref))
    sconv_new_ref[:, 0:2 * SSD_CONV_DIM] = sconv_ref[:, SSD_CONV_DIM:3 * SSD_CONV_DIM]
    sconv_new_ref[:, 2 * SSD_CONV_DIM:3 * SSD_CONV_DIM] = xbc
    xs = act[:, :W_SSD]
    xs_ref[...] = xs
    bc_ref[...] = act[:, W_SSD:]
    dt = _softplus(dt_ref[...] + dtb_ref[...])
    dec = jnp.exp(dt * (-jnp.exp(alog_ref[...])))
    for h in range(SSD_HEADS):
        pcols = slice(h * SSD_HEAD_DIM, (h + 1) * SSD_HEAD_DIM)
        xdt_ref[:, pcols] = xs[:, pcols] * jnp.broadcast_to(dt[:, h:h + 1], (nb, SSD_HEAD_DIM))
        dec_ref[h] = jnp.broadcast_to(dec[:, h:h + 1], (nb, SSD_STATE))


def _sample_pre(proj, dt_raw, h0, lconv, sconv, p):
    nb = proj.shape[0]
    out_shape = [jax.ShapeDtypeStruct((nb, W_LRU), BF16),
                 jax.ShapeDtypeStruct((nb, W_LRU), F32),
                 jax.ShapeDtypeStruct((nb, 3 * W_LRU), F32),
                 jax.ShapeDtypeStruct((nb, 3 * SSD_CONV_DIM), F32),
                 jax.ShapeDtypeStruct((nb, W_SSD), F32),
                 jax.ShapeDtypeStruct((nb, W_SSD), F32),
                 jax.ShapeDtypeStruct((nb, 2 * SSD_GROUPS * SSD_STATE), F32),
                 jax.ShapeDtypeStruct((SSD_HEADS, nb, SSD_STATE), F32)]
    return pl.pallas_call(
        _sample_pre_kernel,
        out_shape=out_shape,
        compiler_params=pltpu.CompilerParams(vmem_limit_bytes=VMEM_LIMIT_BYTES),
        name="sample_pre",
    )(proj, dt_raw, h0, lconv, sconv,
      p["lru_cw"], p["lru_cb"], p["lru_wg"], p["lru_ba"], p["lru_bi"], p["lru_lam"],
      p["ssd_cw"], p["ssd_cb"], p["ssd_dtb"], p["ssd_alog"])


def _sample_state_kernel(s_ref, xdt_ref, bc_ref, dec_ref, o_ref, y_ref):
    bb = s_ref.shape[0]
    half = SSD_HPG * SSD_HEAD_DIM
    rid = lax.broadcasted_iota(jnp.int32, (bb, W_SSD), 0)
    xdt = xdt_ref[...]
    bcb = bc_ref[...].astype(BF16)
    for k in range(bb):
        xk = jnp.where(rid == k, xdt, 0.0).astype(BF16)
        for g in range(SSD_GROUPS):
            rows = slice(g * half, (g + 1) * half)
            b_g = bcb[:, g * SSD_STATE:(g + 1) * SSD_STATE]
            c_g = bcb[:, (SSD_GROUPS + g) * SSD_STATE:(SSD_GROUPS + g + 1) * SSD_STATE]
            outer = lax.dot_general(xk[:, rows], b_g, (((0,), (0,)), ((), ())),
                                    preferred_element_type=F32)
            dec = jnp.concatenate(
                [jnp.broadcast_to(dec_ref[g * SSD_HPG + e, k:k + 1, :], (SSD_HEAD_DIM, SSD_STATE))
                 for e in range(SSD_HPG)], axis=0)
            s_new = dec * s_ref[k, rows, :] + outer
            o_ref[k, rows, :] = s_new
            yk = lax.dot_general(c_g, s_new.astype(BF16), (((1,), (1,)), ((), ())),
                                 preferred_element_type=F32)
            y_ref[k:k + 1, rows] = yk[k:k + 1, :]


def _sample_state(ssm, xdt, bc, dec, bb=8):
    nb = ssm.shape[0]
    return pl.pallas_call(
        _sample_state_kernel,
        grid=(nb // bb,),
        in_specs=[pl.BlockSpec((bb, W_SSD, SSD_STATE), lambda i: (i, 0, 0)),
                  pl.BlockSpec((bb, W_SSD), lambda i: (i, 0)),
                  pl.BlockSpec((bb, 2 * SSD_GROUPS * SSD_STATE), lambda i: (i, 0)),
                  pl.BlockSpec((SSD_HEADS, bb, SSD_STATE), lambda i: (0, i, 0))],
        out_specs=[pl.BlockSpec((bb, W_SSD, SSD_STATE), lambda i: (i, 0, 0)),
                   pl.BlockSpec((bb, W_SSD), lambda i: (i, 0))],
        out_shape=[jax.ShapeDtypeStruct(ssm.shape, F32),
                   jax.ShapeDtypeStruct((nb, W_SSD), F32)],
        compiler_params=_params(("parallel",)),
        name="sample_state",
    )(ssm, xdt, bc, dec)


def _sample_post_kernel(y_ref, xs_ref, proj_ref, dexp_ref, ng_ref, o_ref):
    z = proj_ref[:, 2 * W_LRU:2 * W_LRU + W_SSD]
    yg = (y_ref[...] + dexp_ref[...] * xs_ref[...]) * _silu(z)
    ms = jnp.mean(yg * yg, axis=-1, keepdims=True)
    o_ref[...] = (yg * lax.rsqrt(ms + EPS) * ng_ref[...]).astype(o_ref.dtype)


def _sample_post(y_raw, xs, proj, dexp, ng):
    return pl.pallas_call(
        _sample_post_kernel,
        out_shape=jax.ShapeDtypeStruct(y_raw.shape, BF16),
        compiler_params=pltpu.CompilerParams(vmem_limit_bytes=VMEM_LIMIT_BYTES),
        name="sample_post",
    )(y_raw, xs, proj, dexp, ng)


def _block_diag_groups(w):
    per = LRU_GATE_GROUP // LRU_BLOCK
    w4 = w.reshape(LRU_HEADS // per, per, LRU_BLOCK, LRU_BLOCK)
    bd = jnp.einsum("ghij,hk->ghikj", w4, jnp.eye(per, dtype=w.dtype))
    return bd.reshape(LRU_HEADS // per, LRU_GATE_GROUP, LRU_GATE_GROUP)


def _pad_lanes(v):
    v = v.reshape(1, -1)
    return jnp.pad(v, ((0, 0), (0, LANES - v.shape[1])))


def kernel(x_prompt, x_sample, c_prompt, c_sample, state_lru_h, state_lru_conv, state_ssm, state_ssd_conv, w_ada, b_ada, g_ffn1, w_up1, w_down1, g_mix, w_in, lru_conv_w, lru_conv_b, lru_wa, lru_ba, lru_wi, lru_bi, lru_lambda, ssd_conv_w, ssd_conv_b, ssd_dt_bias, ssd_A_log, ssd_D, ssd_norm_g, w_out, g_ffn2, w_up2, w_down2, w_ada_f, b_ada_f, g_final):
    bp, seq, d = x_prompt.shape
    bs = x_sample.shape[0]
    depth = w_ada.shape[0]
    assert depth == 1 and x_sample.shape[1] == 1 and d == D_MODEL

    pad_rows = (-(bs + bp)) % (2 * SUBLANES)
    c_all = jnp.concatenate([c_sample, c_prompt, jnp.zeros((pad_rows, d), F32)], axis=0)
    mod_all = _ada(c_all, w_ada[0], b_ada[0])
    modf_all = _ada(c_all, w_ada_f, b_ada_f)
    mod_s = mod_all.reshape(1, bs + bp + pad_rows, N_MOD * d)[:, :bs]
    mod_p = mod_all[bs:bs + bp].reshape(bp, 1, N_MOD * d)
    modf_s = modf_all.reshape(1, bs + bp + pad_rows, 2 * d)[:, :bs]
    modf_p = modf_all[bs:bs + bp].reshape(bp, 1, 2 * d)

    w_in_t = jnp.swapaxes(w_in[0], 0, 1)
    w_dt_t = jnp.pad(w_in_t[IN_MAIN:], ((0, LANES - SSD_HEADS), (0, 0)))
    up_blocks = D_FF // 512
    w_f32 = {
        "up1": [(w_up1[0], 0), (w_up1[0], up_blocks)], "down1": [(w_down1[0], 0)],
        "in": [(w_in_t, 0)], "out": [(w_out[0], 0), (w_out[0], 1)],
        "up2": [(w_up2[0], 0), (w_up2[0], up_blocks)], "down2": [(w_down2[0], 0)],
    }
    p = {
        "lru_cw": lru_conv_w[0], "lru_cb": lru_conv_b[0].reshape(1, W_LRU),
        "lru_wg": jnp.concatenate([_block_diag_groups(lru_wa[0]), _block_diag_groups(lru_wi[0])],
                                  axis=-1),
        "lru_ba": lru_ba[0].reshape(1, W_LRU), "lru_bi": lru_bi[0].reshape(1, W_LRU),
        "lru_lam": lru_lambda[0].reshape(1, W_LRU),
        "ssd_cw": ssd_conv_w[0], "ssd_cb": ssd_conv_b[0].reshape(1, SSD_CONV_DIM),
        "ssd_dtb": _pad_lanes(ssd_dt_bias[0]), "ssd_alog": _pad_lanes(ssd_A_log[0]),
        "ssd_dexp": jnp.repeat(ssd_D[0], SSD_HEAD_DIM).reshape(1, W_SSD),
        "ssd_ng": ssd_norm_g[0].reshape(1, W_SSD),
    }

    def sample_trunk(x, mod, modf, mixer, wts):
        tm = x.shape[0]
        wb = {}

        def ffn(x, gain, chunk, up, down):
            hmid, *wb[up] = _normmod_matmul(x, gain, mod, chunk, wts[up], n_out=D_FF, tm=tm,
                                            tn=512, swiglu=True, out_dtype=BF16, emit_bf16=True)
            x, *wb[down] = _matmul_resid([hmid], wts[down], x, mod, chunk + 2, factor=0.5,
                                         tm=tm, tn=256, emit_bf16=True)
            return x

        x = ffn(x, g_ffn1[0], 0, "up1", "down1")
        proj, dt_raw, *wb["in"] = _normmod_matmul(
            x, g_mix[0], mod, 3, wts["in"], n_out=IN_MAIN, tm=tm, tn=512, swiglu=False,
            out_dtype=F32, trans_w=True, w_extra=w_dt_t, emit_bf16=True)
        out_l, y_ssd, new_state = mixer(proj, dt_raw)
        x, *wb["out"] = _matmul_resid([out_l, y_ssd], wts["out"], x, mod, 5, factor=1.0, tm=tm,
                                      tn=512, emit_bf16=True)
        x = ffn(x, g_ffn2[0], 6, "up2", "down2")
        return _final_norm(x, g_final, modf, tm=tm), new_state, wb

    def prompt_trunk(x, mod, modf, mixer, wb):
        zero_off = lambda copies: [(w, 0) for w in copies]
        tm = 1024
        hmid = _normmod_matmul(x, g_ffn1[0], mod, 0, zero_off(wb["up1"]), n_out=D_FF, tm=tm,
                               tn=512, swiglu=True, out_dtype=BF16)[0]
        x, h = _rowblock_resid([hmid], wb["down1"], x, mod, 2, g_mix[0], mod, 3, factor=0.5,
                               tm=256, emit_x=True, h_dtype=BF16)
        proj, dt_raw = _normmod_matmul(h, None, None, None, zero_off(wb["in"]), n_out=IN_MAIN,
                                       tm=tm, tn=512, swiglu=False, out_dtype=F32, trans_w=True,
                                       w_extra=w_dt_t)
        out_l, y_ssd, new_state = mixer(proj, dt_raw)
        x, h = _rowblock_resid([out_l, y_ssd], wb["out"], x, mod, 5, g_ffn2[0], mod, 6,
                               factor=1.0, tm=512, emit_x=True, h_dtype=BF16)
        hmid = _normmod_matmul(h, None, None, None, zero_off(wb["up2"]), n_out=D_FF, tm=tm,
                               tn=512, swiglu=True, out_dtype=BF16)[0]
        y, = _rowblock_resid([hmid], wb["down2"], x, mod, 8, g_final, modf, 0, factor=0.5,
                             tm=256, emit_x=False, h_dtype=F32)
        return y, new_state

    def prompt_mixer(proj, dt_raw):
        out_l, lru_h = _lru_prompt(proj, bp, seq, p["lru_cw"], p["lru_cb"], p["lru_wg"],
                                   p["lru_ba"], p["lru_bi"], p["lru_lam"])
        y_ssd, ssm = _ssd_prompt(proj, dt_raw, bp, seq, p["ssd_cw"], p["ssd_cb"], p["ssd_dtb"],
                                 p["ssd_alog"], p["ssd_dexp"], p["ssd_ng"])
        proj3 = proj.reshape(bp, seq, IN_MAIN)
        lru_buf = proj3[:, seq - (CONV_W - 1):, :W_LRU]
        ssd_buf = proj3[:, seq - (CONV_W - 1):, 2 * W_LRU + W_SSD:]
        return out_l, y_ssd, (lru_h, lru_buf, ssm, ssd_buf)

    def sample_mixer(proj, dt_raw):
        lconv = state_lru_conv[0].reshape(bs, (CONV_W - 1) * W_LRU)
        sconv = state_ssd_conv[0].reshape(bs, (CONV_W - 1) * SSD_CONV_DIM)
        out_l, h_new, lconv_new, sconv_new, xs, xdt, bc, dec = _sample_pre(
            proj, dt_raw, state_lru_h[0], lconv, sconv, p)
        ssm_new, y_raw = _sample_state(state_ssm[0].reshape(bs, W_SSD, SSD_STATE), xdt, bc, dec)
        y_ssd = _sample_post(y_raw, xs, proj, p["ssd_dexp"], p["ssd_ng"])
        return out_l, y_ssd, (h_new, lconv_new.reshape(bs, CONV_W - 1, W_LRU),
                              ssm_new.reshape(bs, SSD_HEADS, SSD_HEAD_DIM, SSD_STATE),
                              sconv_new.reshape(bs, CONV_W - 1, SSD_CONV_DIM))

    ys, ss, w_bf16 = sample_trunk(x_sample.reshape(bs, d), mod_s, modf_s, sample_mixer, w_f32)
    yp, sp = prompt_trunk(x_prompt.reshape(bp * seq, d), mod_p, modf_p, prompt_mixer, w_bf16)

    stack = lambda v: v[None]
    return (yp.reshape(bp, seq, d), ys.reshape(bs, 1, d),
            stack(sp[0]), stack(sp[1]), stack(sp[2]), stack(sp[3]),
            stack(ss[0]), stack(ss[1]), stack(ss[2]), stack(ss[3]))
```

```python
import functools

import jax
import jax.numpy as jnp
from jax import lax
from jax.experimental import pallas as pl
from jax.experimental.pallas import tpu as pltpu

F32 = jnp.float32
BF16 = jnp.bfloat16

D_MODEL = 2048
D_FF = 5632
W_LRU = 1024
W_SSD = 1024
LRU_HEADS = 16
LRU_BLOCK = 64
LRU_C = 8.0
SSD_HEADS = 16
SSD_HEAD_DIM = 64
SSD_GROUPS = 2
SSD_HPG = 8
SSD_STATE = 128
SSD_CHUNK = 128
CONV_W = 4
SSD_CONV_DIM = W_SSD + 2 * SSD_GROUPS * SSD_STATE
IN_MAIN = 2 * W_LRU + W_SSD + SSD_CONV_DIM
N_MOD = 9
EPS = 1e-6

LANES = 128
SUBLANES = 8
VMEM_LIMIT_BYTES = 56 * 1024 * 1024

LRU_GATE_GROUP = 256
LRU_TIME_TILE = 256
SCAN_ROWS = 2 * SUBLANES


def _sigmoid(v):
    return 0.5 * (jnp.tanh(0.5 * v) + 1.0)


def _silu(v):
    return v * _sigmoid(v)


def _softplus(v):
    return jnp.maximum(v, 0.0) + jnp.log1p(jnp.exp(-jnp.abs(v)))


def _gelu_tanh(v):
    return 0.5 * v * (1.0 + jnp.tanh(0.7978845608028654 * (v + 0.044715 * (v * v * v))))


def _bdot(a, b):
    return jnp.dot(a, b, preferred_element_type=F32)


def _params(sem):
    return pltpu.CompilerParams(dimension_semantics=sem, vmem_limit_bytes=VMEM_LIMIT_BYTES)


def _ada_kernel(c_ref, w_ref, b_ref, o_ref):
    s = _silu(c_ref[...]).astype(BF16)
    o_ref[...] = _bdot(s, w_ref[...].astype(BF16)) + b_ref[...]


def _ada(c, w, b, tn=1024):
    m, k = c.shape
    n = w.shape[1]
    return pl.pallas_call(
        _ada_kernel,
        grid=(n // tn,),
        in_specs=[pl.BlockSpec((m, k), lambda j: (0, 0)),
                  pl.BlockSpec((k, tn), lambda j: (0, j)),
                  pl.BlockSpec((1, tn), lambda j: (0, j))],
        out_specs=pl.BlockSpec((m, tn), lambda j: (0, j)),
        out_shape=jax.ShapeDtypeStruct((m, n), F32),
        compiler_params=_params(("parallel",)),
        name="ada_proj",
    )(c, w, b.reshape(1, n))


def _norm_modulate(x, gain, shift, scale):
    ms = jnp.mean(x * x, axis=-1, keepdims=True)
    y = x * lax.rsqrt(ms + EPS) * gain
    return y * (1.0 + scale) + shift


def _wdot(h, w, trans_w):
    if trans_w:
        return lax.dot_general(h, w, (((1,), (1,)), ((), ())), preferred_element_type=F32)
    return _bdot(h, w)


def _normmod_matmul_kernel(*refs, n_w, swiglu, has_extra, emit_bf16, trans_w, prenormed,
                           row_chunk):
    n_lead = 1 if prenormed else 4
    w_refs = refs[n_lead:n_lead + n_w]
    pos = n_lead + n_w
    wx_ref = refs[pos] if has_extra else None
    pos += int(has_extra)
    o_ref = refs[pos]
    pos += 1
    ox_ref = refs[pos] if has_extra else None
    pos += int(has_extra)
    wo_refs = refs[pos:pos + n_w] if emit_bf16 else ()

    if prenormed:
        h_src = refs[0]
    else:
        x_ref, gain_ref, sh_ref, sc_ref = refs[:4]
        h_src = refs[-1]
        tm = x_ref.shape[0]
        per_row = sh_ref.shape[0] != 1

    @pl.when(pl.program_id(1) == 0)
    def _():
        if not prenormed:
            gain = gain_ref[...]

            def body(r, carry):
                rows = pl.ds(pl.multiple_of(r * row_chunk, row_chunk), row_chunk)
                sh = sh_ref[rows, :] if per_row else sh_ref[...]
                sc = sc_ref[rows, :] if per_row else sc_ref[...]
                h_src[rows, :] = _norm_modulate(x_ref[rows, :], gain, sh, sc).astype(BF16)
                return carry

            lax.fori_loop(0, tm // row_chunk, body, 0)
        if has_extra:
            ox_ref[...] = _wdot(h_src[...], wx_ref[...].astype(BF16), trans_w)

    h = h_src[...]
    wbs = [w_ref[...].astype(BF16) for w_ref in w_refs]
    for wo_ref, wb in zip(wo_refs, wbs):
        wo_ref[...] = wb
    if swiglu:
        g = _wdot(h, wbs[0], trans_w)
        u = _wdot(h, wbs[1], trans_w)
        o_ref[...] = (_silu(g) * u).astype(o_ref.dtype)
    else:
        o_ref[...] = _wdot(h, wbs[0], trans_w).astype(o_ref.dtype)


def _normmod_matmul(x, gain, mod, shift_chunk, ws, *, n_out, tm, tn, swiglu, out_dtype,
                    trans_w=False, w_extra=None, emit_bf16=False):
    m, d = x.shape
    prenormed = mod is None
    nj = n_out // tn
    row_chunk = min(tm, 128)

    def w_spec(off):
        if trans_w:
            return pl.BlockSpec((tn, d), lambda i, j: (j + off, 0))
        return pl.BlockSpec((d, tn), lambda i, j: (0, j + off))

    in_specs = [pl.BlockSpec((tm, d), lambda i, j: (i, 0))]
    args = [x]
    if not prenormed:
        groups, r, _ = mod.shape
        tiles_per_group = (m // tm) // groups

        def mod_spec(chunk):
            return pl.BlockSpec((None, r, d), lambda i, j: (i // tiles_per_group, 0, chunk))

        in_specs += [pl.BlockSpec((1, d), lambda i, j: (0, 0)),
                     mod_spec(shift_chunk), mod_spec(shift_chunk + 1)]
        args += [gain.reshape(1, d), mod, mod]
    in_specs += [w_spec(off) for _, off in ws]
    args += [w for w, _ in ws]
    out_specs = [pl.BlockSpec((tm, tn), lambda i, j: (i, j))]
    out_shape = [jax.ShapeDtypeStruct((m, n_out), out_dtype)]
    if w_extra is not None:
        nx = w_extra.shape[0] if trans_w else w_extra.shape[1]
        in_specs.append(pl.BlockSpec(w_extra.shape, lambda i, j: (0, 0)))
        args.append(w_extra)
        out_specs.append(pl.BlockSpec((tm, nx), lambda i, j: (i, 0)))
        out_shape.append(jax.ShapeDtypeStruct((m, nx), F32))
    if emit_bf16:
        assert m == tm, "weight copies are complete only when every weight tile is visited once"
        for _ in ws:
            out_specs.append(w_spec(0))
            out_shape.append(jax.ShapeDtypeStruct((n_out, d) if trans_w else (d, n_out), BF16))
    return pl.pallas_call(
        functools.partial(_normmod_matmul_kernel, n_w=len(ws), swiglu=swiglu,
                          has_extra=w_extra is not None, emit_bf16=emit_bf16, trans_w=trans_w,
                          prenormed=prenormed, row_chunk=row_chunk),
        grid=(m // tm, nj),
        in_specs=in_specs,
        out_specs=out_specs,
        out_shape=out_shape,
        scratch_shapes=[] if prenormed else [pltpu.VMEM((tm, d), BF16)],
        compiler_params=_params(("parallel", "arbitrary")),
        name="normmod_matmul_swiglu" if swiglu else "normmod_matmul",
    )(*args)


def _matmul_resid_kernel(*refs, n_lhs, factor, emit_bf16):
    lhs_refs = refs[:n_lhs]
    w_refs = refs[n_lhs:2 * n_lhs]
    x_ref, gate_ref, o_ref = refs[2 * n_lhs:2 * n_lhs + 3]
    wo_refs = refs[2 * n_lhs + 3:] if emit_bf16 else ()
    wbs = [w_ref[...].astype(BF16) for w_ref in w_refs]
    for wo_ref, wb in zip(wo_refs, wbs):
        wo_ref[...] = wb
    acc = _bdot(lhs_refs[0][...], wbs[0])
    for l_ref, wb in zip(lhs_refs[1:], wbs[1:]):
        acc = acc + _bdot(l_ref[...], wb)
    o_ref[...] = x_ref[...] + (factor * gate_ref[...]) * acc


def _matmul_resid(lhs_list, ws, x, mod, gate_chunk, *, factor, tm, tn, emit_bf16=False):
    m, d = x.shape
    groups, r, _ = mod.shape
    tiles_per_group = (m // tm) // groups
    kp = lhs_list[0].shape[1]
    gate_blocks = d // tn
    in_specs = [pl.BlockSpec((tm, kp), lambda i, j: (i, 0)) for _ in lhs_list]
    in_specs += [pl.BlockSpec((kp, tn), lambda i, j, k=k: (k, j)) for _, k in ws]
    in_specs += [pl.BlockSpec((tm, tn), lambda i, j: (i, j)),
                 pl.BlockSpec((None, r, tn),
                              lambda i, j: (i // tiles_per_group, 0, gate_chunk * gate_blocks + j))]
    out_specs = [pl.BlockSpec((tm, tn), lambda i, j: (i, j))]
    out_shape = [jax.ShapeDtypeStruct((m, d), F32)]
    if emit_bf16:
        assert m == tm, "weight copies are complete only when every weight tile is visited once"
        for _ in ws:
            out_specs.append(pl.BlockSpec((kp, tn), lambda i, j: (0, j)))
            out_shape.append(jax.ShapeDtypeStruct((kp, d), BF16))
    return pl.pallas_call(
        functools.partial(_matmul_resid_kernel, n_lhs=len(lhs_list), factor=factor,
                          emit_bf16=emit_bf16),
        grid=(m // tm, d // tn),
        in_specs=in_specs,
        out_specs=out_specs,
        out_shape=out_shape,
        compiler_params=_params(("parallel", "arbitrary")),
        name="matmul_resid",
    )(*lhs_list, *[w for w, _ in ws], x, mod)


def _rowblock_resid_kernel(*refs, n_lhs, factor, emit_x):
    lhs_refs = refs[:n_lhs]
    w_refs = refs[n_lhs:2 * n_lhs]
    x_ref, gate_ref, gain_ref, sh_ref, sc_ref = refs[2 * n_lhs:2 * n_lhs + 5]
    out_refs = refs[2 * n_lhs + 5:]
    acc = _bdot(lhs_refs[0][...], w_refs[0][...])
    for l_ref, w_ref in zip(lhs_refs[1:], w_refs[1:]):
        acc = acc + _bdot(l_ref[...], w_ref[...])
    x_new = x_ref[...] + (factor * gate_ref[...]) * acc
    if emit_x:
        out_refs[0][...] = x_new
    h_ref = out_refs[-1]
    h_ref[...] = _norm_modulate(x_new, gain_ref[...], sh_ref[...], sc_ref[...]).astype(h_ref.dtype)


def _rowblock_resid(lhs_list, ws, x, mod, gate_chunk, gain_next, mod_next, shift_chunk_next, *,
                    factor, tm, emit_x, h_dtype):
    m, d = x.shape
    groups, r, _ = mod.shape
    assert r == 1 and mod_next.shape[:2] == (groups, 1)
    tiles_per_group = (m // tm) // groups
    kp = lhs_list[0].shape[1]

    def mod_spec(chunk):
        return pl.BlockSpec((None, 1, d), lambda i: (i // tiles_per_group, 0, chunk))

    in_specs = [pl.BlockSpec((tm, kp), lambda i: (i, 0)) for _ in lhs_list]
    in_specs += [pl.BlockSpec((kp, d), lambda i: (0, 0), pipeline_mode=pl.Buffered(1))
                 for _ in ws]
    in_specs += [pl.BlockSpec((tm, d), lambda i: (i, 0)), mod_spec(gate_chunk),
                 pl.BlockSpec((1, d), lambda i: (0, 0)),
                 mod_spec(shift_chunk_next), mod_spec(shift_chunk_next + 1)]
    out_specs = [pl.BlockSpec((tm, d), lambda i: (i, 0))]
    out_shape = [jax.ShapeDtypeStruct((m, d), h_dtype)]
    if emit_x:
        out_specs = [pl.BlockSpec((tm, d), lambda i: (i, 0))] + out_specs
        out_shape = [jax.ShapeDtypeStruct((m, d), F32)] + out_shape
    return pl.pallas_call(
        functools.partial(_rowblock_resid_kernel, n_lhs=len(lhs_list), factor=factor,
                          emit_x=emit_x),
        grid=(m // tm,),
        in_specs=in_specs,
        out_specs=out_specs,
        out_shape=out_shape,
        compiler_params=_params(("parallel",)),
        name="rowblock_resid",
    )(*lhs_list, *ws, x, mod, gain_next.reshape(1, d), mod_next, mod_next)


def _final_kernel(x_ref, gain_ref, sh_ref, sc_ref, o_ref):
    o_ref[...] = _norm_modulate(x_ref[...], gain_ref[...], sh_ref[...], sc_ref[...])


def _final_norm(x, gain, mod, *, tm):
    m, d = x.shape
    groups, r, _ = mod.shape
    tiles_per_group = (m // tm) // groups
    return pl.pallas_call(
        _final_kernel,
        grid=(m // tm,),
        in_specs=[pl.BlockSpec((tm, d), lambda i: (i, 0)),
                  pl.BlockSpec((1, d), lambda i: (0, 0)),
                  pl.BlockSpec((None, r, d), lambda i: (i // tiles_per_group, 0, 0)),
                  pl.BlockSpec((None, r, d), lambda i: (i // tiles_per_group, 0, 1))],
        out_specs=pl.BlockSpec((tm, d), lambda i: (i, 0)),
        out_shape=jax.ShapeDtypeStruct((m, d), F32),
        compiler_params=_params(("parallel",)),
        name="final_norm",
    )(x, gain.reshape(1, d), mod, mod)


def _lru_gates(xc, wg_ref, ba, bi, sp):
    a_parts, b_parts = [], []
    for g in range(W_LRU // LRU_GATE_GROUP):
        cols = slice(g * LRU_GATE_GROUP, (g + 1) * LRU_GATE_GROUP)
        xg = xc[:, cols]
        ri = _bdot(xg.astype(BF16), wg_ref[g].astype(BF16))
        r = _sigmoid(ri[:, :LRU_GATE_GROUP] + ba[:, cols])
        i = _sigmoid(ri[:, LRU_GATE_GROUP:] + bi[:, cols])
        log_a = (-LRU_C * r) * sp[:, cols]
        a = jnp.exp(log_a)
        a_parts.append(a)
        b_parts.append(jnp.sqrt(1.0 - a * a) * (i * xg))
    return jnp.concatenate(a_parts, axis=1), jnp.concatenate(b_parts, axis=1)


def _causal_conv_from_buf(buf_ref, x, w_ref, b_ref, rows):
    xe = buf_ref[0:SUBLANES + rows, :]
    shifted = lambda k: pltpu.roll(xe, k, 0)[SUBLANES:, :]
    y = b_ref[...] + w_ref[0:1, :] * shifted(3)
    y = y + w_ref[1:2, :] * shifted(2)
    y = y + w_ref[2:3, :] * shifted(1)
    return y + w_ref[3:4, :] * x


def _lru_prompt_kernel(xl_ref, gl_ref, cw_ref, cb_ref, wg_ref, ba_ref, bi_ref, lam_ref,
                       o_ref, hT_ref, xbuf, a_scr, b_scr, hcar):
    t = pl.program_id(1)
    tt = xl_ref.shape[0]

    @pl.when(t == 0)
    def _():
        xbuf[0:SUBLANES, :] = jnp.zeros((SUBLANES, W_LRU), F32)
        hcar[...] = jnp.zeros_like(hcar)

    x = xl_ref[...]
    xbuf[SUBLANES:SUBLANES + tt, :] = x
    xc = _causal_conv_from_buf(xbuf, x, cw_ref, cb_ref, tt)
    xbuf[0:SUBLANES, :] = x[tt - SUBLANES:, :]

    sp = _softplus(-lam_ref[...])
    a, bt = _lru_gates(xc, wg_ref, ba_ref[...], bi_ref[...], sp)
    a_scr[...] = a
    b_scr[...] = bt

    rid = lax.broadcasted_iota(jnp.int32, (SUBLANES, W_LRU), 0)

    def scan8(a8, b8, h_in):
        for s in (1, 2, 4):
            a_sh = pltpu.roll(a8, s, 0)
            b_sh = pltpu.roll(b8, s, 0)
            m = rid >= s
            b8 = jnp.where(m, a8 * b_sh + b8, b8)
            a8 = jnp.where(m, a8 * a_sh, a8)
        h8 = a8 * h_in + b8
        return h8, jnp.broadcast_to(h8[SUBLANES - 1:SUBLANES, :], (SUBLANES, W_LRU))

    def body(g, h_in):
        r0 = pl.multiple_of(g * SCAN_ROWS, SCAN_ROWS)
        lo = pl.ds(r0, SUBLANES)
        hi = pl.ds(r0 + SUBLANES, SUBLANES)
        h_lo, h_mid = scan8(a_scr[lo, :], b_scr[lo, :], h_in)
        h_hi, h_out = scan8(a_scr[hi, :], b_scr[hi, :], h_mid)
        rows = pl.ds(r0, SCAN_ROWS)
        h16 = jnp.concatenate([h_lo, h_hi], axis=0)
        o_ref[rows, :] = (h16 * _gelu_tanh(gl_ref[rows, :])).astype(o_ref.dtype)
        return h_out

    h_last = lax.fori_loop(0, tt // SCAN_ROWS, body, hcar[...])
    hcar[...] = h_last

    @pl.when(t == pl.num_programs(1) - 1)
    def _():
        hT_ref[...] = h_last[0:1, :]


def _lru_prompt(proj, batch, seq, cw, cb, wg, ba, bi, lam):
    tt = LRU_TIME_TILE
    nt = seq // tt
    row = lambda v: v.reshape(1, W_LRU)
    full = lambda shape: pl.BlockSpec(shape, lambda b, t: (0,) * len(shape))
    out, h_t = pl.pallas_call(
        _lru_prompt_kernel,
        grid=(batch, nt),
        in_specs=[pl.BlockSpec((tt, W_LRU), lambda b, t: (b * nt + t, 0)),
                  pl.BlockSpec((tt, W_LRU), lambda b, t: (b * nt + t, 1)),
                  full((CONV_W, W_LRU)), full((1, W_LRU)), full(wg.shape),
                  full((1, W_LRU)), full((1, W_LRU)), full((1, W_LRU))],
        out_specs=[pl.BlockSpec((tt, W_LRU), lambda b, t: (b * nt + t, 0)),
                   pl.BlockSpec((None, 1, W_LRU), lambda b, t: (b, 0, 0))],
        out_shape=[jax.ShapeDtypeStruct((batch * seq, W_LRU), BF16),
                   jax.ShapeDtypeStruct((batch, 1, W_LRU), F32)],
        scratch_shapes=[pltpu.VMEM((tt + SUBLANES, W_LRU), F32),
                        pltpu.VMEM((tt, W_LRU), F32),
                        pltpu.VMEM((tt, W_LRU), F32),
                        pltpu.VMEM((SUBLANES, W_LRU), F32)],
        compiler_params=_params(("parallel", "arbitrary")),
        name="lru_prompt",
    )(proj, proj, cw, row(cb), wg, row(ba), row(bi), row(lam))
    return out, h_t.reshape(batch, W_LRU)


def _ssd_prompt_kernel(z_ref, xbc_ref, dt_ref, cw_ref, cb_ref, dtb_ref, alog_ref, dexp_ref,
                       ng_ref, y_ref, st_ref, xbuf, st_scr, y_scr, m_scr, xbd_scr):
    c = pl.program_id(1)
    lc = SSD_CHUNK

    @pl.when(c == 0)
    def _():
        xbuf[0:SUBLANES, :] = jnp.zeros((SUBLANES, SSD_CONV_DIM), F32)
        st_scr[...] = jnp.zeros_like(st_scr)
        xbd_scr[...] = jnp.zeros_like(xbd_scr)

    x = xbc_ref[...]
    xbuf[SUBLANES:SUBLANES + lc, :] = x
    act = _silu(_causal_conv_from_buf(xbuf, x, cw_ref, cb_ref, lc))
    xbuf[0:SUBLANES, :] = x[lc - SUBLANES:, :]
    xs = act[:, :W_SSD]
    bm = act[:, W_SSD:W_SSD + SSD_GROUPS * SSD_STATE]
    cm = act[:, W_SSD + SSD_GROUPS * SSD_STATE:]

    dt = _softplus(dt_ref[...] + dtb_ref[...])
    d_a = dt * (-jnp.exp(alog_ref[...]))
    row_i = lax.broadcasted_iota(jnp.int32, (lc, lc), 0)
    col_i = lax.broadcasted_iota(jnp.int32, (lc, lc), 1)
    causal = row_i >= col_i
    tril = jnp.where(causal, 1.0, 0.0).astype(F32)
    cs = jnp.dot(tril, d_a, preferred_element_type=F32, precision=lax.Precision.HIGHEST)
    cs_t = cs.T
    dt_t = dt.T
    cs_last = cs[lc - 1:lc, :]

    def per_head_lanes(v):
        rows = v.shape[0]
        return jnp.concatenate(
            [jnp.broadcast_to(v[:, h:h + 1], (rows, SSD_HEAD_DIM)) for h in range(SSD_HEADS)],
            axis=1)

    w_exp = per_head_lanes(jnp.exp(cs_last - cs) * dt)
    ecs_exp = per_head_lanes(jnp.exp(cs))
    cd_exp = per_head_lanes(jnp.exp(cs_last))
    gw = SSD_HPG * SSD_HEAD_DIM
    low_half = col_i < SSD_HEAD_DIM

    for g in range(SSD_GROUPS):
        ncols = slice(g * SSD_STATE, (g + 1) * SSD_STATE)
        gcols = slice(g * gw, (g + 1) * gw)
        b_g = bm[:, ncols].astype(BF16)
        c_g = cm[:, ncols].astype(BF16)
        cb_mat = lax.dot_general(c_g, b_g, (((1,), (1,)), ((), ())),
                                 preferred_element_type=F32)
        for e in range(SSD_HPG):
            h = g * SSD_HPG + e
            cs_col = jnp.broadcast_to(cs[:, h:h + 1], (lc, lc))
            l_mat = jnp.exp(jnp.where(causal, cs_col - cs_t[h:h + 1, :], -jnp.inf))
            m_scr[g, :, e * lc:(e + 1) * lc] = (cb_mat * l_mat * dt_t[h:h + 1, :]).astype(BF16)
        for q in range(SSD_HPG // 2):
            lanes = slice(q * LANES, (q + 1) * LANES)
            slab = xs[:, g * gw + q * LANES:g * gw + (q + 1) * LANES]
            xbd_scr[g, (2 * q) * lc:(2 * q + 1) * lc, lanes] = jnp.where(
                low_half, slab, 0.0).astype(BF16)
            xbd_scr[g, (2 * q + 1) * lc:(2 * q + 2) * lc, lanes] = jnp.where(
                low_half, 0.0, slab).astype(BF16)
        st_g = st_scr[:, gcols]
        y_off = _bdot(c_g, st_g.astype(BF16)) * ecs_exp[:, gcols]
        y_scr[:, gcols] = (_bdot(m_scr[g], xbd_scr[g]) + y_off
                           + dexp_ref[:, gcols] * xs[:, gcols])
        xw = (xs[:, gcols] * w_exp[:, gcols]).astype(BF16)
        st_scr[:, gcols] = cd_exp[:, gcols] * st_g + lax.dot_general(
            b_g, xw, (((0,), (0,)), ((), ())), preferred_element_type=F32)

    yg = y_scr[...] * _silu(z_ref[...])
    ms = jnp.mean(yg * yg, axis=-1, keepdims=True)
    y_ref[...] = (yg * lax.rsqrt(ms + EPS) * ng_ref[...]).astype(y_ref.dtype)

    @pl.when(c == pl.num_programs(1) - 1)
    def _():
        st_ref[...] = st_scr[...].T


def _ssd_prompt(proj, dt_raw, batch, seq, cw, cb, dtb, alog, dexp, ng):
    lc = SSD_CHUNK
    nc = seq // lc
    full = lambda shape: pl.BlockSpec(shape, lambda b, c: (0,) * len(shape))
    z_blk = (2 * W_LRU) // W_SSD
    xbc_blk = (2 * W_LRU + W_SSD) // SSD_CONV_DIM
    y, st = pl.pallas_call(
        _ssd_prompt_kernel,
        grid=(batch, nc),
        in_specs=[pl.BlockSpec((lc, W_SSD), lambda b, c: (b * nc + c, z_blk)),
                  pl.BlockSpec((lc, SSD_CONV_DIM), lambda b, c: (b * nc + c, xbc_blk)),
                  pl.BlockSpec((lc, LANES), lambda b, c: (b * nc + c, 0)),
                  full((CONV_W, SSD_CONV_DIM)), full((1, SSD_CONV_DIM)),
                  full((1, LANES)), full((1, LANES)), full((1, W_SSD)), full((1, W_SSD))],
        out_specs=[pl.BlockSpec((lc, W_SSD), lambda b, c: (b * nc + c, 0)),
                   pl.BlockSpec((None, W_SSD, SSD_STATE), lambda b, c: (b, 0, 0))],
        out_shape=[jax.ShapeDtypeStruct((batch * seq, W_SSD), BF16),
                   jax.ShapeDtypeStruct((batch, W_SSD, SSD_STATE), F32)],
        scratch_shapes=[pltpu.VMEM((lc + SUBLANES, SSD_CONV_DIM), F32),
                        pltpu.VMEM((SSD_STATE, W_SSD), F32),
                        pltpu.VMEM((lc, W_SSD), F32),
                        pltpu.VMEM((SSD_GROUPS, lc, SSD_HPG * lc), BF16),
                        pltpu.VMEM((SSD_GROUPS, SSD_HPG * lc, SSD_HPG * SSD_HEAD_DIM), BF16)],
        compiler_params=_params(("parallel", "arbitrary")),
        name="ssd_prompt",
    )(proj, proj, dt_raw, cw, cb, dtb, alog, dexp, ng)
    return y, st.reshape(batch, SSD_HEADS, SSD_HEAD_DIM, SSD_STATE)


def _sample_pre_kernel(proj_ref, dt_ref, h0_ref, lconv_ref, sconv_ref,
                       lcw_ref, lcb_ref, wg_ref, ba_ref, bi_ref, lam_ref,
                       scw_ref, scb_ref, dtb_ref, alog_ref,
                       outl_ref, hnew_ref, lconv_new_ref, sconv_new_ref,
                       xs_ref, xdt_ref, bc_ref, dec_ref):
    nb = proj_ref.shape[0]
    xl = proj_ref[:, 0:W_LRU]
    gl = proj_ref[:, W_LRU:2 * W_LRU]
    xbc = proj_ref[:, 2 * W_LRU + W_SSD:IN_MAIN]

    def conv1(state_ref, width, x_new, w_ref, b_ref):
        y = b_ref[...] + w_ref[0:1, :] * state_ref[:, 0:width]
        y = y + w_ref[1:2, :] * state_ref[:, width:2 * width]
        y = y + w_ref[2:3, :] * state_ref[:, 2 * width:3 * width]
        return y + w_ref[3:4, :] * x_new

    xc = conv1(lconv_ref, W_LRU, xl, lcw_ref, lcb_ref)
    a, bt = _lru_gates(xc, wg_ref, ba_ref[...], bi_ref[...], _softplus(-lam_ref[...]))
    h_new = a * h0_ref[...] + bt
    hnew_ref[...] = h_new
    outl_ref[...] = (h_new * _gelu_tanh(gl)).astype(outl_ref.dtype)
    lconv_new_ref[:, 0:2 * W_LRU] = lconv_ref[:, W_LRU:3 * W_LRU]
    lconv_new_ref[:, 2 * W_LRU:3 * W_LRU] = xl

    act = _silu(conv1(sconv_ref, SSD_CONV_DIM, xbc, scw_ref, scb_ref))
    sconv_new_ref[:, 0:2 * SSD_CONV_DIM] = sconv_ref[:, SSD_CONV_DIM:3 * SSD_CONV_DIM]
    sconv_new_ref[:, 2 * SSD_CONV_DIM:3 * SSD_CONV_DIM] = xbc
    xs = act[:, :W_SSD]
    xs_ref[...] = xs
    bc_ref[...] = act[:, W_SSD:]
    dt = _softplus(dt_ref[...] + dtb_ref[...])
    dec = jnp.exp(dt * (-jnp.exp(alog_ref[...])))
    for h in range(SSD_HEADS):
        pcols = slice(h * SSD_HEAD_DIM, (h + 1) * SSD_HEAD_DIM)
        xdt_ref[:, pcols] = xs[:, pcols] * jnp.broadcast_to(dt[:, h:h + 1], (nb, SSD_HEAD_DIM))
        dec_ref[h] = jnp.broadcast_to(dec[:, h:h + 1], (nb, SSD_STATE))


def _sample_pre(proj, dt_raw, h0, lconv, sconv, p):
    nb = proj.shape[0]
    out_shape = [jax.ShapeDtypeStruct((nb, W_LRU), BF16),
                 jax.ShapeDtypeStruct((nb, W_LRU), F32),
                 jax.ShapeDtypeStruct((nb, 3 * W_LRU), F32),
                 jax.ShapeDtypeStruct((nb, 3 * SSD_CONV_DIM), F32),
                 jax.ShapeDtypeStruct((nb, W_SSD), F32),
                 jax.ShapeDtypeStruct((nb, W_SSD), F32),
                 jax.ShapeDtypeStruct((nb, 2 * SSD_GROUPS * SSD_STATE), F32),
                 jax.ShapeDtypeStruct((SSD_HEADS, nb, SSD_STATE), F32)]
    return pl.pallas_call(
        _sample_pre_kernel,
        out_shape=out_shape,
        compiler_params=pltpu.CompilerParams(vmem_limit_bytes=VMEM_LIMIT_BYTES),
        name="sample_pre",
    )(proj, dt_raw, h0, lconv, sconv,
      p["lru_cw"], p["lru_cb"], p["lru_wg"], p["lru_ba"], p["lru_bi"], p["lru_lam"],
      p["ssd_cw"], p["ssd_cb"], p["ssd_dtb"], p["ssd_alog"])


def _sample_state_kernel(s_ref, xdt_ref, bc_ref, dec_ref, o_ref, y_ref):
    bb = s_ref.shape[0]
    half = SSD_HPG * SSD_HEAD_DIM
    rid = lax.broadcasted_iota(jnp.int32, (bb, W_SSD), 0)
    xdt = xdt_ref[...]
    bcb = bc_ref[...].astype(BF16)
    for k in range(bb):
        xk = jnp.where(rid == k, xdt, 0.0).astype(BF16)
        for g in range(SSD_GROUPS):
            rows = slice(g * half, (g + 1) * half)
            b_g = bcb[:, g * SSD_STATE:(g + 1) * SSD_STATE]
            c_g = bcb[:, (SSD_GROUPS + g) * SSD_STATE:(SSD_GROUPS + g + 1) * SSD_STATE]
            outer = lax.dot_general(xk[:, rows], b_g, (((0,), (0,)), ((), ())),
                                    preferred_element_type=F32)
            dec = jnp.concatenate(
                [jnp.broadcast_to(dec_ref[g * SSD_HPG + e, k:k + 1, :], (SSD_HEAD_DIM, SSD_STATE))
                 for e in range(SSD_HPG)], axis=0)
            s_new = dec * s_ref[k, rows, :] + outer
            o_ref[k, rows, :] = s_new
            yk = lax.dot_general(c_g, s_new.astype(BF16), (((1,), (1,)), ((), ())),
                                 preferred_element_type=F32)
            y_ref[k:k + 1, rows] = yk[k:k + 1, :]


def _sample_state(ssm, xdt, bc, dec, bb=8):
    nb = ssm.shape[0]
    return pl.pallas_call(
        _sample_state_kernel,
        grid=(nb // bb,),
        in_specs=[pl.BlockSpec((bb, W_SSD, SSD_STATE), lambda i: (i, 0, 0)),
                  pl.BlockSpec((bb, W_SSD), lambda i: (i, 0)),
                  pl.BlockSpec((bb, 2 * SSD_GROUPS * SSD_STATE), lambda i: (i, 0)),
                  pl.BlockSpec((SSD_HEADS, bb, SSD_STATE), lambda i: (0, i, 0))],
        out_specs=[pl.BlockSpec((bb, W_SSD, SSD_STATE), lambda i: (i, 0, 0)),
                   pl.BlockSpec((bb, W_SSD), lambda i: (i, 0))],
        out_shape=[jax.ShapeDtypeStruct(ssm.shape, F32),
                   jax.ShapeDtypeStruct((nb, W_SSD), F32)],
        compiler_params=_params(("parallel",)),
        name="sample_state",
    )(ssm, xdt, bc, dec)


def _sample_post_kernel(y_ref, xs_ref, proj_ref, dexp_ref, ng_ref, o_ref):
    z = proj_ref[:, 2 * W_LRU:2 * W_LRU + W_SSD]
    yg = (y_ref[...] + dexp_ref[...] * xs_ref[...]) * _silu(z)
    ms = jnp.mean(yg * yg, axis=-1, keepdims=True)
    o_ref[...] = (yg * lax.rsqrt(ms + EPS) * ng_ref[...]).astype(o_ref.dtype)


def _sample_post(y_raw, xs, proj, dexp, ng):
    return pl.pallas_call(
        _sample_post_kernel,
        out_shape=jax.ShapeDtypeStruct(y_raw.shape, BF16),
        compiler_params=pltpu.CompilerParams(vmem_limit_bytes=VMEM_LIMIT_BYTES),
        name="sample_post",
    )(y_raw, xs, proj, dexp, ng)


def _block_diag_groups(w):
    per = LRU_GATE_GROUP // LRU_BLOCK
    w4 = w.reshape(LRU_HEADS // per, per, LRU_BLOCK, LRU_BLOCK)
    bd = jnp.einsum("ghij,hk->ghikj", w4, jnp.eye(per, dtype=w.dtype))
    return bd.reshape(LRU_HEADS // per, LRU_GATE_GROUP, LRU_GATE_GROUP)


def _pad_lanes(v):
    v = v.reshape(1, -1)
    return jnp.pad(v, ((0, 0), (0, LANES - v.shape[1])))


def kernel(x_prompt, x_sample, c_prompt, c_sample, state_lru_h, state_lru_conv, state_ssm, state_ssd_conv, w_ada, b_ada, g_ffn1, w_up1, w_down1, g_mix, w_in, lru_conv_w, lru_conv_b, lru_wa, lru_ba, lru_wi, lru_bi, lru_lambda, ssd_conv_w, ssd_conv_b, ssd_dt_bias, ssd_A_log, ssd_D, ssd_norm_g, w_out, g_ffn2, w_up2, w_down2, w_ada_f, b_ada_f, g_final):
    bp, seq, d = x_prompt.shape
    bs = x_sample.shape[0]
    depth = w_ada.shape[0]
    assert depth == 1 and x_sample.shape[1] == 1 and d == D_MODEL

    pad_rows = (-(bs + bp)) % (2 * SUBLANES)
    c_all = jnp.concatenate([c_sample, c_prompt, jnp.zeros((pad_rows, d), F32)], axis=0)
    mod_all = _ada(c_all, w_ada[0], b_ada[0])
    modf_all = _ada(c_all, w_ada_f, b_ada_f)
    mod_s = mod_all.reshape(1, bs + bp + pad_rows, N_MOD * d)[:, :bs]
    mod_p = mod_all[bs:bs + bp].reshape(bp, 1, N_MOD * d)
    modf_s = modf_all.reshape(1, bs + bp + pad_rows, 2 * d)[:, :bs]
    modf_p = modf_all[bs:bs + bp].reshape(bp, 1, 2 * d)

    w_in_t = jnp.swapaxes(w_in[0], 0, 1)
    w_dt_t = jnp.pad(w_in_t[IN_MAIN:], ((0, LANES - SSD_HEADS), (0, 0)))
    up_blocks = D_FF // 512
    w_f32 = {
        "up1": [(w_up1[0], 0), (w_up1[0], up_blocks)], "down1": [(w_down1[0], 0)],
        "in": [(w_in_t, 0)], "out": [(w_out[0], 0), (w_out[0], 1)],
        "up2": [(w_up2[0], 0), (w_up2[0], up_blocks)], "down2": [(w_down2[0], 0)],
    }
    p = {
        "lru_cw": lru_conv_w[0], "lru_cb": lru_conv_b[0].reshape(1, W_LRU),
        "lru_wg": jnp.concatenate([_block_diag_groups(lru_wa[0]), _block_diag_groups(lru_wi[0])],
                                  axis=-1),
        "lru_ba": lru_ba[0].reshape(1, W_LRU), "lru_bi": lru_bi[0].reshape(1, W_LRU),
        "lru_lam": lru_lambda[0].reshape(1, W_LRU),
        "ssd_cw": ssd_conv_w[0], "ssd_cb": ssd_conv_b[0].reshape(1, SSD_CONV_DIM),
        "ssd_dtb": _pad_lanes(ssd_dt_bias[0]), "ssd_alog": _pad_lanes(ssd_A_log[0]),
        "ssd_dexp": jnp.repeat(ssd_D[0], SSD_HEAD_DIM).reshape(1, W_SSD),
        "ssd_ng": ssd_norm_g[0].reshape(1, W_SSD),
    }

    def sample_trunk(x, mod, modf, mixer, wts):
        tm = x.shape[0]
        wb = {}

        def ffn(x, gain, chunk, up, down):
            hmid, *wb[up] = _normmod_matmul(x, gain, mod, chunk, wts[up], n_out=D_FF, tm=tm,
                                            tn=512, swiglu=True, out_dtype=BF16, emit_bf16=True)
            x, *wb[down] = _matmul_resid([hmid], wts[down], x, mod, chunk + 2, factor=0.5,
                                         tm=tm, tn=256, emit_bf16=True)
            return x

        x = ffn(x, g_ffn1[0], 0, "up1", "down1")
        proj, dt_raw, *wb["in"] = _normmod_matmul(
            x, g_mix[0], mod, 3, wts["in"], n_out=IN_MAIN, tm=tm, tn=512, swiglu=False,
            out_dtype=F32, trans_w=True, w_extra=w_dt_t, emit_bf16=True)
        out_l, y_ssd, new_state = mixer(proj, dt_raw)
        x, *wb["out"] = _matmul_resid([out_l, y_ssd], wts["out"], x, mod, 5, factor=1.0, tm=tm,
                                      tn=512, emit_bf16=True)
        x = ffn(x, g_ffn2[0], 6, "up2", "down2")
        return _final_norm(x, g_final, modf, tm=tm), new_state, wb

    def prompt_trunk(x, mod, modf, mixer, wb):
        zero_off = lambda copies: [(w, 0) for w in copies]
        tm = 1024
        hmid = _normmod_matmul(x, g_ffn1[0], mod, 0, zero_off(wb["up1"]), n_out=D_FF, tm=tm,
                               tn=512, swiglu=True, out_dtype=BF16)[0]
        x, h = _rowblock_resid([hmid], wb["down1"], x, mod, 2, g_mix[0], mod, 3, factor=0.5,
                               tm=256, emit_x=True, h_dtype=BF16)
        proj, dt_raw = _normmod_matmul(h, None, None, None, zero_off(wb["in"]), n_out=IN_MAIN,
                                       tm=tm, tn=512, swiglu=False, out_dtype=F32, trans_w=True,
                                       w_extra=w_dt_t)
        out_l, y_ssd, new_state = mixer(proj, dt_raw)
        x, h = _rowblock_resid([out_l, y_ssd], wb["out"], x, mod, 5, g_ffn2[0], mod, 6,
                               factor=1.0, tm=512, emit_x=True, h_dtype=BF16)
        hmid = _normmod_matmul(h, None, None, None, zero_off(wb["up2"]), n_out=D_FF, tm=tm,
                               tn=512, swiglu=True, out_dtype=BF16)[0]
        y, = _rowblock_resid([hmid], wb["down2"], x, mod, 8, g_final, modf, 0, factor=0.5,
                             tm=256, emit_x=False, h_dtype=F32)
        return y, new_state

    def prompt_mixer(proj, dt_raw):
        out_l, lru_h = _lru_prompt(proj, bp, seq, p["lru_cw"], p["lru_cb"], p["lru_wg"],
                                   p["lru_ba"], p["lru_bi"], p["lru_lam"])
        y_ssd, ssm = _ssd_prompt(proj, dt_raw, bp, seq, p["ssd_cw"], p["ssd_cb"], p["ssd_dtb"],
                                 p["ssd_alog"], p["ssd_dexp"], p["ssd_ng"])
        proj3 = proj.reshape(bp, seq, IN_MAIN)
        lru_buf = proj3[:, seq - (CONV_W - 1):, :W_LRU]
        ssd_buf = proj3[:, seq - (CONV_W - 1):, 2 * W_LRU + W_SSD:]
        return out_l, y_ssd, (lru_h, lru_buf, ssm, ssd_buf)

    def sample_mixer(proj, dt_raw):
        lconv = state_lru_conv[0].reshape(bs, (CONV_W - 1) * W_LRU)
        sconv = state_ssd_conv[0].reshape(bs, (CONV_W - 1) * SSD_CONV_DIM)
        out_l, h_new, lconv_new, sconv_new, xs, xdt, bc, dec = _sample_pre(
            proj, dt_raw, state_lru_h[0], lconv, sconv, p)
        ssm_new, y_raw = _sample_state(state_ssm[0].reshape(bs, W_SSD, SSD_STATE), xdt, bc, dec)
        y_ssd = _sample_post(y_raw, xs, proj, p["ssd_dexp"], p["ssd_ng"])
        return out_l, y_ssd, (h_new, lconv_new.reshape(bs, CONV_W - 1, W_LRU),
                              ssm_new.reshape(bs, SSD_HEADS, SSD_HEAD_DIM, SSD_STATE),
                              sconv_new.reshape(bs, CONV_W - 1, SSD_CONV_DIM))

    ys, ss, w_bf16 = sample_trunk(x_sample.reshape(bs, d), mod_s, modf_s, sample_mixer, w_f32)
    yp, sp = prompt_trunk(x_prompt.reshape(bp * seq, d), mod_p, modf_p, prompt_mixer, w_bf16)

    stack = lambda v: v[None]
    return (yp.reshape(bp, seq, d), ys.reshape(bs, 1, d),
            stack(sp[0]), stack(sp[1]), stack(sp[2]), stack(sp[3]),
            stack(ss[0]), stack(ss[1]), stack(ss[2]), stack(ss[3]))
```

```python
import functools

import jax
import jax.numpy as jnp
from jax import lax
from jax.experimental import pallas as pl
from jax.experimental.pallas import tpu as pltpu

F32 = jnp.float32
BF16 = jnp.bfloat16

D_MODEL = 2048
D_FF = 5632
W_LRU = 1024
W_SSD = 1024
LRU_HEADS = 16
LRU_BLOCK = 64
LRU_C = 8.0
SSD_HEADS = 16
SSD_HEAD_DIM = 64
SSD_GROUPS = 2
SSD_HPG = 8
SSD_STATE = 128
SSD_CHUNK = 128
CONV_W = 4
SSD_CONV_DIM = W_SSD + 2 * SSD_GROUPS * SSD_STATE
IN_MAIN = 2 * W_LRU + W_SSD + SSD_CONV_DIM
N_MOD = 9
EPS = 1e-6

LANES = 128
SUBLANES = 8
VMEM_LIMIT_BYTES = 56 * 1024 * 1024

LRU_GATE_GROUP = 256
LRU_TIME_TILE = 256
SCAN_ROWS = 2 * SUBLANES


def _sigmoid(v):
    return 0.5 * (jnp.tanh(0.5 * v) + 1.0)


def _silu(v):
    return v * _sigmoid(v)


def _softplus(v):
    return jnp.maximum(v, 0.0) + jnp.log1p(jnp.exp(-jnp.abs(v)))


def _gelu_tanh(v):
    return 0.5 * v * (1.0 + jnp.tanh(0.7978845608028654 * (v + 0.044715 * (v * v * v))))


def _bdot(a, b):
    return jnp.dot(a, b, preferred_element_type=F32)


def _params(sem):
    return pltpu.CompilerParams(dimension_semantics=sem, vmem_limit_bytes=VMEM_LIMIT_BYTES)


def _ada_kernel(c_ref, w_ref, b_ref, o_ref):
    s = _silu(c_ref[...]).astype(BF16)
    o_ref[...] = _bdot(s, w_ref[...].astype(BF16)) + b_ref[...]


def _ada(c, w, b, tn=1024):
    m, k = c.shape
    n = w.shape[1]
    return pl.pallas_call(
        _ada_kernel,
        grid=(n // tn,),
        in_specs=[pl.BlockSpec((m, k), lambda j: (0, 0)),
                  pl.BlockSpec((k, tn), lambda j: (0, j)),
                  pl.BlockSpec((1, tn), lambda j: (0, j))],
        out_specs=pl.BlockSpec((m, tn), lambda j: (0, j)),
        out_shape=jax.ShapeDtypeStruct((m, n), F32),
        compiler_params=_params(("parallel",)),
        name="ada_proj",
    )(c, w, b.reshape(1, n))


def _norm_modulate(x, gain, shift, scale):
    ms = jnp.mean(x * x, axis=-1, keepdims=True)
    y = x * lax.rsqrt(ms + EPS) * gain
    return y * (1.0 + scale) + shift


def _wdot(h, w, trans_w):
    if trans_w:
        return lax.dot_general(h, w, (((1,), (1,)), ((), ())), preferred_element_type=F32)
    return _bdot(h, w)


def _proj_kernel(*refs, n_w, swiglu, trans_w, prenormed, has_extra, has_side, emit_bf16,
                 cast_chunks, n_prev, nj, row_chunk):
    it = iter(refs)
    x_ref = next(it)
    gain_ref, sh_ref, sc_ref = (None, None, None) if prenormed else (next(it), next(it), next(it))
    xs_ref = next(it) if has_side else None
    shs_ref, scs_ref = (next(it), next(it)) if has_side and not prenormed else (None, None)
    w_refs = [next(it) for _ in range(n_w)]
    wx_ref = next(it) if has_extra else None
    cast_in = next(it) if cast_chunks else None
    for _ in range(n_prev):
        next(it)
    o_ref = next(it)
    ox_ref = next(it) if has_extra else None
    os_ref = next(it) if has_side else None
    osx_ref = next(it) if has_side and has_extra else None
    wo_refs = [next(it) for _ in range(n_w)] if emit_bf16 else []
    cast_out = next(it) if cast_chunks else None
    h_scr = None if prenormed else next(it)
    hs_scr = next(it) if has_side and not prenormed else None

    j = pl.program_id(1)

    @pl.when(j == 0)
    def _():
        if not prenormed:
            gain = gain_ref[...]
            tm = x_ref.shape[0]

            def body(r, carry):
                rows = pl.ds(pl.multiple_of(r * row_chunk, row_chunk), row_chunk)
                h_scr[rows, :] = _norm_modulate(x_ref[rows, :], gain, sh_ref[...],
                                                sc_ref[...]).astype(BF16)
                return carry

            lax.fori_loop(0, tm // row_chunk, body, 0)
            if has_side:
                hs_scr[...] = _norm_modulate(xs_ref[...], gain, shs_ref[...],
                                             scs_ref[...]).astype(BF16)
        if has_extra:
            wxb = wx_ref[...].astype(BF16)
            ox_ref[...] = _wdot((x_ref if prenormed else h_scr)[...], wxb, trans_w)
            if has_side:
                osx_ref[...] = _wdot((xs_ref if prenormed else hs_scr)[...], wxb, trans_w)

    wbs = [w_ref[...].astype(BF16) for w_ref in w_refs]
    for wo_ref, wb in zip(wo_refs, wbs):
        wo_ref[...] = wb

    def project(h, out_ref):
        if swiglu:
            g = _wdot(h, wbs[0], trans_w)
            u = _wdot(h, wbs[1], trans_w)
            out_ref[...] = (_silu(g) * u).astype(out_ref.dtype)
        else:
            out_ref[...] = _wdot(h, wbs[0], trans_w).astype(out_ref.dtype)

    project((x_ref if prenormed else h_scr)[...], o_ref)
    if has_side:
        project((xs_ref if prenormed else hs_scr)[...], os_ref)

    if cast_chunks:
        @pl.when(pl.program_id(0) * nj + j < cast_chunks)
        def _():
            cast_out[...] = cast_in[...].astype(BF16)


def _proj(x, ws, *, n_out, tm, tn, swiglu, out_dtype, row_tiles, gain=None, mod=None,
          shift_chunk=None, side=None, mod_side=None, trans_w=False, w_extra=None,
          emit_bf16=False, cast=None, prev=None):
    m, d = x.shape
    prenormed = mod is None
    t0, t1 = row_tiles
    nj = n_out // tn
    has_side = side is not None
    has_extra = w_extra is not None
    single_row_tile = t1 - t0 == 1
    once = dict(pipeline_mode=pl.Buffered(1))

    def w_spec(off):
        if trans_w:
            return pl.BlockSpec((tn, d), lambda i, j: (j + off, 0))
        return pl.BlockSpec((d, tn), lambda i, j: (0, j + off))

    x_mode = once if single_row_tile else {}
    in_specs = [pl.BlockSpec((tm, d), lambda i, j: (i + t0, 0), **x_mode)]
    args = [x]
    if not prenormed:
        groups = mod.shape[0]
        tiles_per_group = (m // tm) // groups

        def mod_spec(chunk):
            return pl.BlockSpec((None, 1, d), lambda i, j: ((i + t0) // tiles_per_group, 0, chunk))

        in_specs += [pl.BlockSpec((1, d), lambda i, j: (0, 0)),
                     mod_spec(shift_chunk), mod_spec(shift_chunk + 1)]
        args += [gain.reshape(1, d), mod, mod]
    if has_side:
        ns = side.shape[0]
        in_specs.append(pl.BlockSpec((ns, d), lambda i, j: (0, 0), **once))
        args.append(side)
        if not prenormed:
            in_specs += [pl.BlockSpec((None, ns, d), lambda i, j, c=c: (0, 0, c), **once)
                         for c in (shift_chunk, shift_chunk + 1)]
            args += [mod_side, mod_side]
    in_specs += [w_spec(off) for _, off in ws]
    args += [w for w, _ in ws]
    if has_extra:
        nx = w_extra.shape[0] if trans_w else w_extra.shape[1]
        in_specs.append(pl.BlockSpec(w_extra.shape, lambda i, j: (0, 0)))
        args.append(w_extra)
    cast_chunks = 0
    if cast is not None:
        cast_arr, cast_chunks = cast
        cast_rows = cast_arr.shape[0] // cast_chunks
        assert cast_chunks <= (t1 - t0) * nj
        cast_spec = pl.BlockSpec(
            (cast_rows, cast_arr.shape[1]),
            lambda i, j: (jnp.minimum(i * nj + j, cast_chunks - 1), 0))
        in_specs.append(cast_spec)
        args.append(cast_arr)
    prev = list(prev or [])
    aliases = {}
    for k, buf in enumerate(prev):
        aliases[len(args)] = k
        in_specs.append(pl.BlockSpec(memory_space=pl.ANY))
        args.append(buf)

    out_specs = [pl.BlockSpec((tm, tn), lambda i, j: (i + t0, j))]
    out_shape = [jax.ShapeDtypeStruct((m, n_out), out_dtype)]
    if has_extra:
        out_specs.append(pl.BlockSpec((tm, nx), lambda i, j: (i + t0, 0)))
        out_shape.append(jax.ShapeDtypeStruct((m, nx), F32))
    n_main = len(out_shape)
    if has_side:
        out_specs.append(pl.BlockSpec((ns, tn), lambda i, j: (0, j)))
        out_shape.append(jax.ShapeDtypeStruct((ns, n_out), out_dtype))
        if has_extra:
            out_specs.append(pl.BlockSpec((ns, nx), lambda i, j: (0, 0)))
            out_shape.append(jax.ShapeDtypeStruct((ns, nx), F32))
    n_side = len(out_shape) - n_main
    if emit_bf16 or has_side:
        assert single_row_tile, "weight copies / side outputs are written once per column tile"
    if emit_bf16:
        for _ in ws:
            out_specs.append(w_spec(0))
            out_shape.append(jax.ShapeDtypeStruct((n_out, d) if trans_w else (d, n_out), BF16))
    if cast is not None:
        out_specs.append(cast_spec)
        out_shape.append(jax.ShapeDtypeStruct(cast_arr.shape, BF16))
    scratch = []
    if not prenormed:
        scratch.append(pltpu.VMEM((tm, d), BF16))
        if has_side:
            scratch.append(pltpu.VMEM((ns, d), BF16))
    outs = pl.pallas_call(
        functools.partial(_proj_kernel, n_w=len(ws), swiglu=swiglu, trans_w=trans_w,
                          prenormed=prenormed, has_extra=has_extra, has_side=has_side,
                          emit_bf16=emit_bf16, cast_chunks=cast_chunks, n_prev=len(prev), nj=nj,
                          row_chunk=min(tm, 128)),
        grid=(t1 - t0, nj),
        in_specs=in_specs,
        out_specs=out_specs,
        out_shape=out_shape,
        scratch_shapes=scratch,
        input_output_aliases=aliases,
        compiler_params=_params(("arbitrary" if cast_chunks else "parallel", "arbitrary")),
        name="proj_swiglu" if swiglu else "proj",
    )(*args)
    n_wb = len(ws) if emit_bf16 else 0
    main, rest = outs[:n_main], outs[n_main:]
    side_outs, rest = rest[:n_side], rest[n_side:]
    return main, side_outs, rest[:n_wb], rest[n_wb:]


def _proj_all_rows(x, ws_f32, *, tm, tn_first, tn_rest, side, cast=None, **kw):
    m = x.shape[0]
    main, side_outs, wb, _ = _proj(x, ws_f32, tm=tm, tn=tn_first, row_tiles=(0, 1), side=side,
                                   emit_bf16=True, **kw)
    kw.pop("mod_side", None)
    main, _, _, cast_out = _proj(x, [(w, 0) for w in wb], tm=tm, tn=tn_rest,
                                 row_tiles=(1, m // tm), prev=main, cast=cast, **kw)
    return main, side_outs, wb, cast_out


def _resid_kernel(*refs, n_lhs, factor, emit_x):
    it = iter(refs)
    lhs_refs = [next(it) for _ in range(n_lhs)]
    lhs_s_refs = [next(it) for _ in range(n_lhs)]
    w_refs = [next(it) for _ in range(n_lhs)]
    x_ref, gate_ref, gain_ref, sh_ref, sc_ref = (next(it) for _ in range(5))
    xs_ref, gate_s_ref, sh_s_ref, sc_s_ref = (next(it) for _ in range(4))
    n_out = 2 if emit_x else 1
    outs = [next(it) for _ in range(n_out)]
    outs_s = [next(it) for _ in range(n_out)]

    def update(lhs, x_in, gate, sh, sc, out_refs):
        acc = _bdot(lhs[0][...], w_refs[0][...])
        for l_ref, w_ref in zip(lhs[1:], w_refs[1:]):
            acc = acc + _bdot(l_ref[...], w_ref[...])
        x_new = x_in[...] + (factor * gate[...]) * acc
        if emit_x:
            out_refs[0][...] = x_new
        h_ref = out_refs[-1]
        h_ref[...] = _norm_modulate(x_new, gain_ref[...], sh[...], sc[...]).astype(h_ref.dtype)

    update(lhs_refs, x_ref, gate_ref, sh_ref, sc_ref, outs)

    @pl.when(pl.program_id(0) == 0)
    def _():
        update(lhs_s_refs, xs_ref, gate_s_ref, sh_s_ref, sc_s_ref, outs_s)


def _resid(lhs_list, lhs_s_list, ws, x, x_s, mod, mod_s, gate_chunk, gain_next, mod_next,
           mod_next_s, shift_chunk_next, *, factor, tm, emit_x, h_dtype):
    m, d = x.shape
    ns = x_s.shape[0]
    groups = mod.shape[0]
    tiles_per_group = (m // tm) // groups
    kp = lhs_list[0].shape[1]
    once = dict(pipeline_mode=pl.Buffered(1))

    def mod_spec(chunk):
        return pl.BlockSpec((None, 1, d), lambda i: (i // tiles_per_group, 0, chunk))

    def mod_s_spec(chunk):
        return pl.BlockSpec((None, ns, d), lambda i: (0, 0, chunk), **once)

    in_specs = [pl.BlockSpec((tm, kp), lambda i: (i, 0)) for _ in lhs_list]
    in_specs += [pl.BlockSpec((ns, kp), lambda i: (0, 0), **once) for _ in lhs_s_list]
    in_specs += [pl.BlockSpec((kp, d), lambda i, k=k: (k, 0), **once) for _, k in ws]
    in_specs += [pl.BlockSpec((tm, d), lambda i: (i, 0)), mod_spec(gate_chunk),
                 pl.BlockSpec((1, d), lambda i: (0, 0)),
                 mod_spec(shift_chunk_next), mod_spec(shift_chunk_next + 1),
                 pl.BlockSpec((ns, d), lambda i: (0, 0), **once), mod_s_spec(gate_chunk),
                 mod_s_spec(shift_chunk_next), mod_s_spec(shift_chunk_next + 1)]
    row = pl.BlockSpec((tm, d), lambda i: (i, 0))
    row_s = pl.BlockSpec((ns, d), lambda i: (0, 0))
    dtypes = ([F32] if emit_x else []) + [h_dtype]
    out_specs = [row for _ in dtypes] + [row_s for _ in dtypes]
    out_shape = ([jax.ShapeDtypeStruct((m, d), t) for t in dtypes]
                 + [jax.ShapeDtypeStruct((ns, d), t) for t in dtypes])
    outs = pl.pallas_call(
        functools.partial(_resid_kernel, n_lhs=len(lhs_list), factor=factor, emit_x=emit_x),
        grid=(m // tm,),
        in_specs=in_specs,
        out_specs=out_specs,
        out_shape=out_shape,
        compiler_params=_params(("arbitrary",)),
        name="resid",
    )(*lhs_list, *lhs_s_list, *[w for w, _ in ws], x, mod, gain_next.reshape(1, d), mod_next,
      mod_next, x_s, mod_s, mod_next_s, mod_next_s)
    return outs[:len(dtypes)], outs[len(dtypes):]


def _lru_gates(xc, wg_ref, ba, bi, sp):
    a_parts, b_parts = [], []
    for g in range(W_LRU // LRU_GATE_GROUP):
        cols = slice(g * LRU_GATE_GROUP, (g + 1) * LRU_GATE_GROUP)
        xg = xc[:, cols]
        ri = _bdot(xg.astype(BF16), wg_ref[g].astype(BF16))
        r = _sigmoid(ri[:, :LRU_GATE_GROUP] + ba[:, cols])
        i = _sigmoid(ri[:, LRU_GATE_GROUP:] + bi[:, cols])
        log_a = (-LRU_C * r) * sp[:, cols]
        a = jnp.exp(log_a)
        a_parts.append(a)
        b_parts.append(jnp.sqrt(1.0 - a * a) * (i * xg))
    return jnp.concatenate(a_parts, axis=1), jnp.concatenate(b_parts, axis=1)


def _causal_conv_from_buf(buf_ref, x, w_ref, b_ref, rows):
    xe = buf_ref[0:SUBLANES + rows, :]
    shifted = lambda k: pltpu.roll(xe, k, 0)[SUBLANES:, :]
    y = b_ref[...] + w_ref[0:1, :] * shifted(3)
    y = y + w_ref[1:2, :] * shifted(2)
    y = y + w_ref[2:3, :] * shifted(1)
    return y + w_ref[3:4, :] * x


def _lru_prompt_kernel(xl_ref, gl_ref, cw_ref, cb_ref, wg_ref, ba_ref, bi_ref, lam_ref,
                       o_ref, hT_ref, xbuf, a_scr, b_scr, hcar):
    t = pl.program_id(1)
    tt = xl_ref.shape[0]

    @pl.when(t == 0)
    def _():
        xbuf[0:SUBLANES, :] = jnp.zeros((SUBLANES, W_LRU), F32)
        hcar[...] = jnp.zeros_like(hcar)

    x = xl_ref[...]
    xbuf[SUBLANES:SUBLANES + tt, :] = x
    xc = _causal_conv_from_buf(xbuf, x, cw_ref, cb_ref, tt)
    xbuf[0:SUBLANES, :] = x[tt - SUBLANES:, :]

    sp = _softplus(-lam_ref[...])
    a, bt = _lru_gates(xc, wg_ref, ba_ref[...], bi_ref[...], sp)
    a_scr[...] = a
    b_scr[...] = bt

    rid = lax.broadcasted_iota(jnp.int32, (SUBLANES, W_LRU), 0)

    def scan8(a8, b8, h_in):
        for s in (1, 2, 4):
            a_sh = pltpu.roll(a8, s, 0)
            b_sh = pltpu.roll(b8, s, 0)
            m = rid >= s
            b8 = jnp.where(m, a8 * b_sh + b8, b8)
            a8 = jnp.where(m, a8 * a_sh, a8)
        h8 = a8 * h_in + b8
        return h8, jnp.broadcast_to(h8[SUBLANES - 1:SUBLANES, :], (SUBLANES, W_LRU))

    def body(g, h_in):
        r0 = pl.multiple_of(g * SCAN_ROWS, SCAN_ROWS)
        lo = pl.ds(r0, SUBLANES)
        hi = pl.ds(r0 + SUBLANES, SUBLANES)
        h_lo, h_mid = scan8(a_scr[lo, :], b_scr[lo, :], h_in)
        h_hi, h_out = scan8(a_scr[hi, :], b_scr[hi, :], h_mid)
        rows = pl.ds(r0, SCAN_ROWS)
        h16 = jnp.concatenate([h_lo, h_hi], axis=0)
        o_ref[rows, :] = (h16 * _gelu_tanh(gl_ref[rows, :])).astype(o_ref.dtype)
        return h_out

    h_last = lax.fori_loop(0, tt // SCAN_ROWS, body, hcar[...])
    hcar[...] = h_last

    @pl.when(t == pl.num_programs(1) - 1)
    def _():
        hT_ref[...] = h_last[0:1, :]


def _lru_prompt(proj, batch, seq, cw, cb, wg, ba, bi, lam):
    tt = LRU_TIME_TILE
    nt = seq // tt
    row = lambda v: v.reshape(1, W_LRU)
    full = lambda shape: pl.BlockSpec(shape, lambda b, t: (0,) * len(shape))
    out, h_t = pl.pallas_call(
        _lru_prompt_kernel,
        grid=(batch, nt),
        in_specs=[pl.BlockSpec((tt, W_LRU), lambda b, t: (b * nt + t, 0)),
                  pl.BlockSpec((tt, W_LRU), lambda b, t: (b * nt + t, 1)),
                  full((CONV_W, W_LRU)), full((1, W_LRU)), full(wg.shape),
                  full((1, W_LRU)), full((1, W_LRU)), full((1, W_LRU))],
        out_specs=[pl.BlockSpec((tt, W_LRU), lambda b, t: (b * nt + t, 0)),
                   pl.BlockSpec((None, 1, W_LRU), lambda b, t: (b, 0, 0))],
        out_shape=[jax.ShapeDtypeStruct((batch * seq, W_LRU), BF16),
                   jax.ShapeDtypeStruct((batch, 1, W_LRU), F32)],
        scratch_shapes=[pltpu.VMEM((tt + SUBLANES, W_LRU), F32),
                        pltpu.VMEM((tt, W_LRU), F32),
                        pltpu.VMEM((tt, W_LRU), F32),
                        pltpu.VMEM((SUBLANES, W_LRU), F32)],
        compiler_params=_params(("parallel", "arbitrary")),
        name="lru_prompt",
    )(proj, proj, cw, row(cb), wg, row(ba), row(bi), row(lam))
    return out, h_t.reshape(batch, W_LRU)


def _ssd_prompt_kernel(z_ref, xbc_ref, dt_ref, cw_ref, cb_ref, dtb_ref, alog_ref, dexp_ref,
                       ng_ref, y_ref, st_ref, xbuf, st_scr, y_scr, m_scr, xbd_scr):
    c = pl.program_id(1)
    lc = SSD_CHUNK

    @pl.when(c == 0)
    def _():
        xbuf[0:SUBLANES, :] = jnp.zeros((SUBLANES, SSD_CONV_DIM), F32)
        st_scr[...] = jnp.zeros_like(st_scr)
        xbd_scr[...] = jnp.zeros_like(xbd_scr)

    x = xbc_ref[...]
    xbuf[SUBLANES:SUBLANES + lc, :] = x
    act = _silu(_causal_conv_from_buf(xbuf, x, cw_ref, cb_ref, lc))
    xbuf[0:SUBLANES, :] = x[lc - SUBLANES:, :]
    xs = act[:, :W_SSD]
    bm = act[:, W_SSD:W_SSD + SSD_GROUPS * SSD_STATE]
    cm = act[:, W_SSD + SSD_GROUPS * SSD_STATE:]

    dt = _softplus(dt_ref[...] + dtb_ref[...])
    d_a = dt * (-jnp.exp(alog_ref[...]))
    row_i = lax.broadcasted_iota(jnp.int32, (lc, lc), 0)
    col_i = lax.broadcasted_iota(jnp.int32, (lc, lc), 1)
    causal = row_i >= col_i
    tril = jnp.where(causal, 1.0, 0.0).astype(F32)
    cs = jnp.dot(tril, d_a, preferred_element_type=F32, precision=lax.Precision.HIGHEST)
    cs_t = cs.T
    dt_t = dt.T
    cs_last = cs[lc - 1:lc, :]

    def per_head_lanes(v):
        rows = v.shape[0]
        return jnp.concatenate(
            [jnp.broadcast_to(v[:, h:h + 1], (rows, SSD_HEAD_DIM)) for h in range(SSD_HEADS)],
            axis=1)

    w_exp = per_head_lanes(jnp.exp(cs_last - cs) * dt)
    ecs_exp = per_head_lanes(jnp.exp(cs))
    cd_exp = per_head_lanes(jnp.exp(cs_last))
    gw = SSD_HPG * SSD_HEAD_DIM
    low_half = col_i < SSD_HEAD_DIM

    for g in range(SSD_GROUPS):
        ncols = slice(g * SSD_STATE, (g + 1) * SSD_STATE)
        gcols = slice(g * gw, (g + 1) * gw)
        b_g = bm[:, ncols].astype(BF16)
        c_g = cm[:, ncols].astype(BF16)
        cb_mat = lax.dot_general(c_g, b_g, (((1,), (1,)), ((), ())),
                                 preferred_element_type=F32)
        for e in range(SSD_HPG):
            h = g * SSD_HPG + e
            cs_col = jnp.broadcast_to(cs[:, h:h + 1], (lc, lc))
            l_mat = jnp.exp(jnp.where(causal, cs_col - cs_t[h:h + 1, :], -jnp.inf))
            m_scr[g, :, e * lc:(e + 1) * lc] = (cb_mat * l_mat * dt_t[h:h + 1, :]).astype(BF16)
        for q in range(SSD_HPG // 2):
            lanes = slice(q * LANES, (q + 1) * LANES)
            slab = xs[:, g * gw + q * LANES:g * gw + (q + 1) * LANES]
            xbd_scr[g, (2 * q) * lc:(2 * q + 1) * lc, lanes] = jnp.where(
                low_half, slab, 0.0).astype(BF16)
            xbd_scr[g, (2 * q + 1) * lc:(2 * q + 2) * lc, lanes] = jnp.where(
                low_half, 0.0, slab).astype(BF16)
        st_g = st_scr[:, gcols]
        y_off = _bdot(c_g, st_g.astype(BF16)) * ecs_exp[:, gcols]
        y_scr[:, gcols] = (_bdot(m_scr[g], xbd_scr[g]) + y_off
                           + dexp_ref[:, gcols] * xs[:, gcols])
        xw = (xs[:, gcols] * w_exp[:, gcols]).astype(BF16)
        st_scr[:, gcols] = cd_exp[:, gcols] * st_g + lax.dot_general(
            b_g, xw, (((0,), (0,)), ((), ())), preferred_element_type=F32)

    yg = y_scr[...] * _silu(z_ref[...])
    ms = jnp.mean(yg * yg, axis=-1, keepdims=True)
    y_ref[...] = (yg * lax.rsqrt(ms + EPS) * ng_ref[...]).astype(y_ref.dtype)

    @pl.when(c == pl.num_programs(1) - 1)
    def _():
        st_ref[...] = st_scr[...].T


def _ssd_prompt(proj, dt_raw, batch, seq, cw, cb, dtb, alog, dexp, ng):
    lc = SSD_CHUNK
    nc = seq // lc
    full = lambda shape: pl.BlockSpec(shape, lambda b, c: (0,) * len(shape))
    z_blk = (2 * W_LRU) // W_SSD
    xbc_blk = (2 * W_LRU + W_SSD) // SSD_CONV_DIM
    y, st = pl.pallas_call(
        _ssd_prompt_kernel,
        grid=(batch, nc),
        in_specs=[pl.BlockSpec((lc, W_SSD), lambda b, c: (b * nc + c, z_blk)),
                  pl.BlockSpec((lc, SSD_CONV_DIM), lambda b, c: (b * nc + c, xbc_blk)),
                  pl.BlockSpec((lc, LANES), lambda b, c: (b * nc + c, 0)),
                  full((CONV_W, SSD_CONV_DIM)), full((1, SSD_CONV_DIM)),
                  full((1, LANES)), full((1, LANES)), full((1, W_SSD)), full((1, W_SSD))],
        out_specs=[pl.BlockSpec((lc, W_SSD), lambda b, c: (b * nc + c, 0)),
                   pl.BlockSpec((None, W_SSD, SSD_STATE), lambda b, c: (b, 0, 0))],
        out_shape=[jax.ShapeDtypeStruct((batch * seq, W_SSD), BF16),
                   jax.ShapeDtypeStruct((batch, W_SSD, SSD_STATE), F32)],
        scratch_shapes=[pltpu.VMEM((lc + SUBLANES, SSD_CONV_DIM), F32),
                        pltpu.VMEM((SSD_STATE, W_SSD), F32),
                        pltpu.VMEM((lc, W_SSD), F32),
                        pltpu.VMEM((SSD_GROUPS, lc, SSD_HPG * lc), BF16),
                        pltpu.VMEM((SSD_GROUPS, SSD_HPG * lc, SSD_HPG * SSD_HEAD_DIM), BF16)],
        compiler_params=_params(("parallel", "arbitrary")),
        name="ssd_prompt",
    )(proj, proj, dt_raw, cw, cb, dtb, alog, dexp, ng)
    return y, st.reshape(batch, SSD_HEADS, SSD_HEAD_DIM, SSD_STATE)


def _sample_pre_kernel(proj_ref, dt_ref, h0_ref, lconv_ref, sconv_ref,
                       lcw_ref, lcb_ref, wg_ref, ba_ref, bi_ref, lam_ref,
                       scw_ref, scb_ref, dtb_ref, alog_ref,
                       outl_ref, hnew_ref, lconv_new_ref, sconv_new_ref,
                       xs_ref, xdt_ref, bc_ref, dec_ref):
    nb = proj_ref.shape[0]
    xl = proj_ref[:, 0:W_LRU]
    gl = proj_ref[:, W_LRU:2 * W_LRU]
    xbc = proj_ref[:, 2 * W_LRU + W_SSD:IN_MAIN]

    def conv1(state_ref, width, x_new, w_ref, b_ref):
        y = b_ref[...] + w_ref[0:1, :] * state_ref[:, 0:width]
        y = y + w_ref[1:2, :] * state_ref[:, width:2 * width]
        y = y + w_ref[2:3, :] * state_ref[:, 2 * width:3 * width]
        return y + w_ref[3:4, :] * x_new

    xc = conv1(lconv_ref, W_LRU, xl, lcw_ref, lcb_ref)
    a, bt = _lru_gates(xc, wg_ref, ba_ref[...], bi_ref[...], _softplus(-lam_ref[...]))
    h_new = a * h0_ref[...] + bt
    hnew_ref[...] = h_new
    outl_ref[...] = (h_new * _gelu_tanh(gl)).astype(outl_ref.dtype)
    lconv_new_ref[:, 0:2 * W_LRU] = lconv_ref[:, W_LRU:3 * W_LRU]
    lconv_new_ref[:, 2 * W_LRU:3 * W_LRU] = xl

    act = _silu(conv1(sconv_ref, SSD_CONV_DIM, xbc, scw_ref, scb_ref))
    sconv_new_ref[:, 0:2 * SSD_CONV_DIM] = sconv_ref[:, SSD_CONV_DIM:3 * SSD_CONV_DIM]
    sconv_new_ref[:, 2 * SSD_CONV_DIM:3 * SSD_CONV_DIM] = xbc
    xs = act[:, :W_SSD]
    xs_ref[...] = xs
    bc_ref[...] = act[:, W_SSD:]
    dt = _softplus(dt_ref[...] + dtb_ref[...])
    dec = jnp.exp(dt * (-jnp.exp(alog_ref[...])))
    for h in range(SSD_HEADS):
        pcols = slice(h * SSD_HEAD_DIM, (h + 1) * SSD_HEAD_DIM)
        xdt_ref[:, pcols] = xs[:, pcols] * jnp.broadcast_to(dt[:, h:h + 1], (nb, SSD_HEAD_DIM))
        dec_ref[h] = jnp.broadcast_to(dec[:, h:h + 1], (nb, SSD_STATE))


def _sample_pre(proj, dt_raw, h0, lconv, sconv, p):
    nb = proj.shape[0]
    out_shape = [jax.ShapeDtypeStruct((nb, W_LRU), BF16),
                 jax.ShapeDtypeStruct((nb, W_LRU), F32),
                 jax.ShapeDtypeStruct((nb, 3 * W_LRU), F32),
                 jax.ShapeDtypeStruct((nb, 3 * SSD_CONV_DIM), F32),
                 jax.ShapeDtypeStruct((nb, W_SSD), F32),
                 jax.ShapeDtypeStruct((nb, W_SSD), F32),
                 jax.ShapeDtypeStruct((nb, 2 * SSD_GROUPS * SSD_STATE), F32),
                 jax.ShapeDtypeStruct((SSD_HEADS, nb, SSD_STATE), F32)]
    return pl.pallas_call(
        _sample_pre_kernel,
        out_shape=out_shape,
        compiler_params=pltpu.CompilerParams(vmem_limit_bytes=VMEM_LIMIT_BYTES),
        name="sample_pre",
    )(proj, dt_raw, h0, lconv, sconv,
      p["lru_cw"], p["lru_cb"], p["lru_wg"], p["lru_ba"], p["lru_bi"], p["lru_lam"],
      p["ssd_cw"], p["ssd_cb"], p["ssd_dtb"], p["ssd_alog"])


def _sample_state_kernel(s_ref, xdt_ref, bc_ref, dec_ref, o_ref, y_ref):
    bb = s_ref.shape[0]
    half = SSD_HPG * SSD_HEAD_DIM
    rid = lax.broadcasted_iota(jnp.int32, (bb, W_SSD), 0)
    xdt = xdt_ref[...]
    bcb = bc_ref[...].astype(BF16)
    for k in range(bb):
        xk = jnp.where(rid == k, xdt, 0.0).astype(BF16)
        for g in range(SSD_GROUPS):
            rows = slice(g * half, (g + 1) * half)
            b_g = bcb[:, g * SSD_STATE:(g + 1) * SSD_STATE]
            c_g = bcb[:, (SSD_GROUPS + g) * SSD_STATE:(SSD_GROUPS + g + 1) * SSD_STATE]
            outer = lax.dot_general(xk[:, rows], b_g, (((0,), (0,)), ((), ())),
                                    preferred_element_type=F32)
            dec = jnp.concatenate(
                [jnp.broadcast_to(dec_ref[g * SSD_HPG + e, k:k + 1, :], (SSD_HEAD_DIM, SSD_STATE))
                 for e in range(SSD_HPG)], axis=0)
            s_new = dec * s_ref[k, rows, :] + outer
            o_ref[k, rows, :] = s_new
            yk = lax.dot_general(c_g, s_new.astype(BF16), (((1,), (1,)), ((), ())),
                                 preferred_element_type=F32)
            y_ref[k:k + 1, rows] = yk[k:k + 1, :]


def _sample_state(ssm, xdt, bc, dec, bb=8):
    nb = ssm.shape[0]
    return pl.pallas_call(
        _sample_state_kernel,
        grid=(nb // bb,),
        in_specs=[pl.BlockSpec((bb, W_SSD, SSD_STATE), lambda i: (i, 0, 0)),
                  pl.BlockSpec((bb, W_SSD), lambda i: (i, 0)),
                  pl.BlockSpec((bb, 2 * SSD_GROUPS * SSD_STATE), lambda i: (i, 0)),
                  pl.BlockSpec((SSD_HEADS, bb, SSD_STATE), lambda i: (0, i, 0))],
        out_specs=[pl.BlockSpec((bb, W_SSD, SSD_STATE), lambda i: (i, 0, 0)),
                   pl.BlockSpec((bb, W_SSD), lambda i: (i, 0))],
        out_shape=[jax.ShapeDtypeStruct(ssm.shape, F32),
                   jax.ShapeDtypeStruct((nb, W_SSD), F32)],
        compiler_params=_params(("parallel",)),
        name="sample_state",
    )(ssm, xdt, bc, dec)


def _sample_post_kernel(y_ref, xs_ref, proj_ref, dexp_ref, ng_ref, o_ref):
    z = proj_ref[:, 2 * W_LRU:2 * W_LRU + W_SSD]
    yg = (y_ref[...] + dexp_ref[...] * xs_ref[...]) * _silu(z)
    ms = jnp.mean(yg * yg, axis=-1, keepdims=True)
    o_ref[...] = (yg * lax.rsqrt(ms + EPS) * ng_ref[...]).astype(o_ref.dtype)


def _sample_post(y_raw, xs, proj, dexp, ng):
    return pl.pallas_call(
        _sample_post_kernel,
        out_shape=jax.ShapeDtypeStruct(y_raw.shape, BF16),
        compiler_params=pltpu.CompilerParams(vmem_limit_bytes=VMEM_LIMIT_BYTES),
        name="sample_post",
    )(y_raw, xs, proj, dexp, ng)


def _block_diag_groups(w):
    per = LRU_GATE_GROUP // LRU_BLOCK
    w4 = w.reshape(LRU_HEADS // per, per, LRU_BLOCK, LRU_BLOCK)
    bd = jnp.einsum("ghij,hk->ghikj", w4, jnp.eye(per, dtype=w.dtype))
    return bd.reshape(LRU_HEADS // per, LRU_GATE_GROUP, LRU_GATE_GROUP)


def _pad_lanes(v):
    v = v.reshape(1, -1)
    return jnp.pad(v, ((0, 0), (0, LANES - v.shape[1])))


def kernel(x_prompt, x_sample, c_prompt, c_sample, state_lru_h, state_lru_conv, state_ssm, state_ssd_conv, w_ada, b_ada, g_ffn1, w_up1, w_down1, g_mix, w_in, lru_conv_w, lru_conv_b, lru_wa, lru_ba, lru_wi, lru_bi, lru_lambda, ssd_conv_w, ssd_conv_b, ssd_dt_bias, ssd_A_log, ssd_D, ssd_norm_g, w_out, g_ffn2, w_up2, w_down2, w_ada_f, b_ada_f, g_final):
    bp, seq, d = x_prompt.shape
    bs = x_sample.shape[0]
    depth = w_ada.shape[0]
    assert depth == 1 and x_sample.shape[1] == 1 and d == D_MODEL

    pad_rows = (-(bs + bp)) % (2 * SUBLANES)
    c_all = jnp.concatenate([c_sample, c_prompt, jnp.zeros((pad_rows, d), F32)], axis=0)
    mod_all = _ada(c_all, w_ada[0], b_ada[0])
    modf_all = _ada(c_all, w_ada_f, b_ada_f)
    mod_s = mod_all.reshape(1, bs + bp + pad_rows, N_MOD * d)[:, :bs]
    mod_p = mod_all[bs:bs + bp].reshape(bp, 1, N_MOD * d)
    modf_s = modf_all.reshape(1, bs + bp + pad_rows, 2 * d)[:, :bs]
    modf_p = modf_all[bs:bs + bp].reshape(bp, 1, 2 * d)

    w_in_t = jnp.swapaxes(w_in[0], 0, 1)
    w_dt_t = jnp.pad(w_in_t[IN_MAIN:], ((0, LANES - SSD_HEADS), (0, 0)))
    up_blocks = D_FF // 512
    p = {
        "lru_cw": lru_conv_w[0], "lru_cb": lru_conv_b[0].reshape(1, W_LRU),
        "lru_wg": jnp.concatenate([_block_diag_groups(lru_wa[0]), _block_diag_groups(lru_wi[0])],
                                  axis=-1),
        "lru_ba": lru_ba[0].reshape(1, W_LRU), "lru_bi": lru_bi[0].reshape(1, W_LRU),
        "lru_lam": lru_lambda[0].reshape(1, W_LRU),
        "ssd_cw": ssd_conv_w[0], "ssd_cb": ssd_conv_b[0].reshape(1, SSD_CONV_DIM),
        "ssd_dtb": _pad_lanes(ssd_dt_bias[0]), "ssd_alog": _pad_lanes(ssd_A_log[0]),
        "ssd_dexp": jnp.repeat(ssd_D[0], SSD_HEAD_DIM).reshape(1, W_SSD),
        "ssd_ng": ssd_norm_g[0].reshape(1, W_SSD),
    }

    xp = x_prompt.reshape(bp * seq, d)
    xs = x_sample.reshape(bs, d)
    tm = 1024

    def ffn(xp, xs, hp, hs, gain, chunk, w_up, w_down, gain_next, mod_next, mod_next_s,
            chunk_next, last):
        prologue = (dict(gain=gain, mod=mod_p, shift_chunk=chunk, mod_side=mod_s)
                    if hp is None else {})
        (hmid,), (hmid_s,), _, (w_down_b,) = _proj_all_rows(
            xp if hp is None else hp, [(w_up, 0), (w_up, up_blocks)], tm=tm, tn_first=512,
            tn_rest=512, side=xs if hp is None else hs, cast=(w_down, 32), n_out=D_FF,
            swiglu=True, out_dtype=BF16, **prologue)
        return _resid([hmid], [hmid_s], [(w_down_b, 0)], xp, xs, mod_p, mod_s, chunk + 2,
                      gain_next, mod_next, mod_next_s, chunk_next, factor=0.5, tm=256,
                      emit_x=not last, h_dtype=F32 if last else BF16)

    (xp, hp), (xs, hs) = ffn(xp, xs, None, None, g_ffn1[0], 0, w_up1[0], w_down1[0], g_mix[0],
                             mod_p, mod_s, 3, last=False)

    (proj, dt_raw), (proj_s, dt_raw_s), _, (w_out_b,) = _proj_all_rows(
        hp, [(w_in_t, 0)], tm=tm, tn_first=512, tn_rest=IN_MAIN // 3, side=hs,
        cast=(w_out[0], 16), n_out=IN_MAIN, swiglu=False, out_dtype=F32, trans_w=True,
        w_extra=w_dt_t)

    out_l, lru_h_p = _lru_prompt(proj, bp, seq, p["lru_cw"], p["lru_cb"], p["lru_wg"],
                                 p["lru_ba"], p["lru_bi"], p["lru_lam"])
    y_ssd, ssm_p = _ssd_prompt(proj, dt_raw, bp, seq, p["ssd_cw"], p["ssd_cb"], p["ssd_dtb"],
                               p["ssd_alog"], p["ssd_dexp"], p["ssd_ng"])
    proj3 = proj.reshape(bp, seq, IN_MAIN)
    lru_buf_p = proj3[:, seq - (CONV_W - 1):, :W_LRU]
    ssd_buf_p = proj3[:, seq - (CONV_W - 1):, 2 * W_LRU + W_SSD:]

    lconv = state_lru_conv[0].reshape(bs, (CONV_W - 1) * W_LRU)
    sconv = state_ssd_conv[0].reshape(bs, (CONV_W - 1) * SSD_CONV_DIM)
    out_l_s, lru_h_s, lconv_new, sconv_new, xs_act, xdt, bc, dec = _sample_pre(
        proj_s, dt_raw_s, state_lru_h[0], lconv, sconv, p)
    ssm_s, y_raw = _sample_state(state_ssm[0].reshape(bs, W_SSD, SSD_STATE), xdt, bc, dec)
    y_ssd_s = _sample_post(y_raw, xs_act, proj_s, p["ssd_dexp"], p["ssd_ng"])

    (xp, hp), (xs, hs) = _resid([out_l, y_ssd], [out_l_s, y_ssd_s], [(w_out_b, 0), (w_out_b, 1)],
                                xp, xs, mod_p, mod_s, 5, g_ffn2[0], mod_p, mod_s, 6, factor=1.0,
                                tm=512, emit_x=True, h_dtype=BF16)
    (yp,), (ys,) = ffn(xp, xs, hp, hs, None, 6, w_up2[0], w_down2[0], g_final, modf_p, modf_s, 0,
                       last=True)

    stack = lambda v: v[None]
    return (yp.reshape(bp, seq, d), ys.reshape(bs, 1, d),
            stack(lru_h_p), stack(lru_buf_p), stack(ssm_p), stack(ssd_buf_p),
            stack(lru_h_s), stack(lconv_new.reshape(bs, CONV_W - 1, W_LRU)),
            stack(ssm_s.reshape(bs, SSD_HEADS, SSD_HEAD_DIM, SSD_STATE)),
            stack(sconv_new.reshape(bs, CONV_W - 1, SSD_CONV_DIM)))
```

```python
import functools
from typing import Callable, NamedTuple

import jax
import jax.numpy as jnp
from jax import lax
from jax.experimental import pallas as pl
from jax.experimental.pallas import tpu as pltpu

F32 = jnp.float32
BF16 = jnp.bfloat16

D_MODEL = 2048
D_FF = 5632
W_LRU = 1024
W_SSD = 1024
LRU_HEADS = 16
LRU_BLOCK = 64
LRU_C = 8.0
SSD_HEADS = 16
SSD_HEAD_DIM = 64
SSD_GROUPS = 2
SSD_HPG = 8
SSD_STATE = 128
SSD_CHUNK = 128
CONV_W = 4
SSD_CONV_DIM = W_SSD + 2 * SSD_GROUPS * SSD_STATE
IN_MAIN = 2 * W_LRU + W_SSD + SSD_CONV_DIM
N_MOD = 9
EPS = 1e-6

LANES = 128
SUBLANES = 8
VMEM_LIMIT_BYTES = 56 * 1024 * 1024

LRU_GATE_GROUP = 256
LRU_TIME_TILE = 256
SCAN_ROWS = 2 * SUBLANES


def _sigmoid(v):
    return 0.5 * (jnp.tanh(0.5 * v) + 1.0)


def _silu(v):
    return v * _sigmoid(v)


def _softplus(v):
    return jnp.maximum(v, 0.0) + jnp.log1p(jnp.exp(-jnp.abs(v)))


def _gelu_tanh(v):
    return 0.5 * v * (1.0 + jnp.tanh(0.7978845608028654 * (v + 0.044715 * (v * v * v))))


def _bdot(a, b):
    return jnp.dot(a, b, preferred_element_type=F32)


def _params(sem):
    return pltpu.CompilerParams(dimension_semantics=sem, vmem_limit_bytes=VMEM_LIMIT_BYTES)


def _ada_kernel(c_ref, w_ref, b_ref, o_ref):
    s = _silu(c_ref[...]).astype(BF16)
    o_ref[...] = _bdot(s, w_ref[...].astype(BF16)) + b_ref[...]


def _ada(c, w, b, cols, tn=1024):
    m, k = c.shape
    n = cols
    return pl.pallas_call(
        _ada_kernel,
        grid=(n // tn,),
        in_specs=[pl.BlockSpec((m, k), lambda j: (0, 0)),
                  pl.BlockSpec((k, tn), lambda j: (0, j)),
                  pl.BlockSpec((1, tn), lambda j: (0, j))],
        out_specs=pl.BlockSpec((m, tn), lambda j: (0, j)),
        out_shape=jax.ShapeDtypeStruct((m, n), F32),
        compiler_params=_params(("parallel",)),
        name="ada_proj",
    )(c, w, b.reshape(1, -1))


class _Stream(NamedTuple):
    ins: list
    outs: list
    body: Callable
    start: int
    steps: int


def _cast_body(src_ref, dst_ref):
    dst_ref[...] = src_ref[...].astype(dst_ref.dtype)


def _cast_stream(w, chunks, start=0):
    block = (w.shape[0] // chunks, w.shape[1])
    rows = lambda k: (k, 0)
    return _Stream([(w, block, rows)], [(jax.ShapeDtypeStruct(w.shape, BF16), block, rows)],
                   _cast_body, start, chunks)


def _ada_stream(c, w, b, col0, cols, tn, start=0):
    m, k = c.shape
    t0 = col0 // tn
    return _Stream(
        [(c, (m, k), lambda s: (0, 0)), (w, (k, tn), lambda s: (0, s + t0)),
         (b.reshape(1, -1), (1, tn), lambda s: (0, s + t0))],
        [(jax.ShapeDtypeStruct((m, cols), F32), (m, tn), lambda s: (0, s))],
        _ada_kernel, start, cols // tn)


def _norm_modulate(x, gain, shift, scale):
    ms = jnp.mean(x * x, axis=-1, keepdims=True)
    y = x * lax.rsqrt(ms + EPS) * gain
    return y * (1.0 + scale) + shift


def _wdot(h, w, trans_w):
    if trans_w:
        return lax.dot_general(h, w, (((1,), (1,)), ((), ())), preferred_element_type=F32)
    return _bdot(h, w)


def _proj_kernel(*refs, n_w, swiglu, trans_w, prenormed, has_extra, has_side, emit_bf16,
                 streams, n_prev, nj, row_chunk):
    it = iter(refs)
    x_ref = next(it)
    gain_ref, sh_ref, sc_ref = (None, None, None) if prenormed else (next(it), next(it), next(it))
    xs_ref = next(it) if has_side else None
    shs_ref, scs_ref = (next(it), next(it)) if has_side and not prenormed else (None, None)
    w_refs = [next(it) for _ in range(n_w)]
    wx_ref = next(it) if has_extra else None
    stream_ins = [[next(it) for _ in st.ins] for st in streams]
    for _ in range(n_prev):
        next(it)
    o_ref = next(it)
    ox_ref = next(it) if has_extra else None
    os_ref = next(it) if has_side else None
    osx_ref = next(it) if has_side and has_extra else None
    wo_refs = [next(it) for _ in range(n_w)] if emit_bf16 else []
    stream_outs = [[next(it) for _ in st.outs] for st in streams]
    h_scr = None if prenormed else next(it)
    hs_scr = next(it) if has_side and not prenormed else None

    j = pl.program_id(1)

    @pl.when(j == 0)
    def _():
        if not prenormed:
            gain = gain_ref[...]
            tm = x_ref.shape[0]

            def body(r, carry):
                rows = pl.ds(pl.multiple_of(r * row_chunk, row_chunk), row_chunk)
                h_scr[rows, :] = _norm_modulate(x_ref[rows, :], gain, sh_ref[...],
                                                sc_ref[...]).astype(BF16)
                return carry

            lax.fori_loop(0, tm // row_chunk, body, 0)
            if has_side:
                hs_scr[...] = _norm_modulate(xs_ref[...], gain, shs_ref[...],
                                             scs_ref[...]).astype(BF16)
        if has_extra:
            wxb = wx_ref[...].astype(BF16)
            ox_ref[...] = _wdot((x_ref if prenormed else h_scr)[...], wxb, trans_w)
            if has_side:
                osx_ref[...] = _wdot((xs_ref if prenormed else hs_scr)[...], wxb, trans_w)

    wbs = [w_ref[...].astype(BF16) for w_ref in w_refs]
    for wo_ref, wb in zip(wo_refs, wbs):
        wo_ref[...] = wb

    def project(h, out_ref):
        if swiglu:
            g = _wdot(h, wbs[0], trans_w)
            u = _wdot(h, wbs[1], trans_w)
            out_ref[...] = (_silu(g) * u).astype(out_ref.dtype)
        else:
            out_ref[...] = _wdot(h, wbs[0], trans_w).astype(out_ref.dtype)

    project((x_ref if prenormed else h_scr)[...], o_ref)
    if has_side:
        project((xs_ref if prenormed else hs_scr)[...], os_ref)

    step = pl.program_id(0) * nj + j
    for st, ins, outs in zip(streams, stream_ins, stream_outs):
        @pl.when((step >= st.start) & (step < st.start + st.steps))
        def _(st=st, ins=ins, outs=outs):
            st.body(*ins, *outs)


def _proj(x, ws, *, n_out, tm, tn, swiglu, out_dtype, row_tiles, gain=None, mod=None,
          shift_chunk=None, side=None, mod_side=None, trans_w=False, w_extra=None,
          emit_bf16=False, streams=(), prev=None):
    m, d = x.shape
    prenormed = mod is None
    t0, t1 = row_tiles
    nj = n_out // tn
    has_side = side is not None
    has_extra = w_extra is not None
    single_row_tile = t1 - t0 == 1
    once = dict(pipeline_mode=pl.Buffered(1))

    def w_spec(off):
        if trans_w:
            return pl.BlockSpec((tn, d), lambda i, j: (j + off, 0))
        return pl.BlockSpec((d, tn), lambda i, j: (0, j + off))

    x_mode = once if single_row_tile else {}
    in_specs = [pl.BlockSpec((tm, d), lambda i, j: (i + t0, 0), **x_mode)]
    args = [x]
    if not prenormed:
        groups = mod.shape[0]
        tiles_per_group = (m // tm) // groups

        def mod_spec(chunk):
            return pl.BlockSpec((None, 1, d), lambda i, j: ((i + t0) // tiles_per_group, 0, chunk))

        in_specs += [pl.BlockSpec((1, d), lambda i, j: (0, 0)),
                     mod_spec(shift_chunk), mod_spec(shift_chunk + 1)]
        args += [gain.reshape(1, d), mod, mod]
    if has_side:
        ns = side.shape[0]
        in_specs.append(pl.BlockSpec((ns, d), lambda i, j: (0, 0), **once))
        args.append(side)
        if not prenormed:
            in_specs += [pl.BlockSpec((None, ns, d), lambda i, j, c=c: (0, 0, c), **once)
                         for c in (shift_chunk, shift_chunk + 1)]
            args += [mod_side, mod_side]
    in_specs += [w_spec(off) for _, off in ws]
    args += [w for w, _ in ws]
    if has_extra:
        nx = w_extra.shape[0] if trans_w else w_extra.shape[1]
        in_specs.append(pl.BlockSpec(w_extra.shape, lambda i, j: (0, 0)))
        args.append(w_extra)
    def stream_spec(st, block, index_fn):
        return pl.BlockSpec(
            block, lambda i, j: index_fn(jnp.clip(i * nj + j - st.start, 0, st.steps - 1)))

    for st in streams:
        assert st.start + st.steps <= (t1 - t0) * nj
        for arr, block, index_fn in st.ins:
            in_specs.append(stream_spec(st, block, index_fn))
            args.append(arr)
    prev = list(prev or [])
    aliases = {}
    for k, buf in enumerate(prev):
        aliases[len(args)] = k
        in_specs.append(pl.BlockSpec(memory_space=pl.ANY))
        args.append(buf)

    out_specs = [pl.BlockSpec((tm, tn), lambda i, j: (i + t0, j))]
    out_shape = [jax.ShapeDtypeStruct((m, n_out), out_dtype)]
    if has_extra:
        out_specs.append(pl.BlockSpec((tm, nx), lambda i, j: (i + t0, 0)))
        out_shape.append(jax.ShapeDtypeStruct((m, nx), F32))
    n_main = len(out_shape)
    if has_side:
        out_specs.append(pl.BlockSpec((ns, tn), lambda i, j: (0, j)))
        out_shape.append(jax.ShapeDtypeStruct((ns, n_out), out_dtype))
        if has_extra:
            out_specs.append(pl.BlockSpec((ns, nx), lambda i, j: (0, 0)))
            out_shape.append(jax.ShapeDtypeStruct((ns, nx), F32))
    n_side = len(out_shape) - n_main
    if emit_bf16 or has_side:
        assert single_row_tile, "weight copies / side outputs are written once per column tile"
    if emit_bf16:
        for _ in ws:
            out_specs.append(w_spec(0))
            out_shape.append(jax.ShapeDtypeStruct((n_out, d) if trans_w else (d, n_out), BF16))
    for st in streams:
        for shape, block, index_fn in st.outs:
            out_specs.append(stream_spec(st, block, index_fn))
            out_shape.append(shape)
    scratch = []
    if not prenormed:
        scratch.append(pltpu.VMEM((tm, d), BF16))
        if has_side:
            scratch.append(pltpu.VMEM((ns, d), BF16))
    outs = pl.pallas_call(
        functools.partial(_proj_kernel, n_w=len(ws), swiglu=swiglu, trans_w=trans_w,
                          prenormed=prenormed, has_extra=has_extra, has_side=has_side,
                          emit_bf16=emit_bf16, streams=tuple(streams), n_prev=len(prev), nj=nj,
                          row_chunk=min(tm, 128)),
        grid=(t1 - t0, nj),
        in_specs=in_specs,
        out_specs=out_specs,
        out_shape=out_shape,
        scratch_shapes=scratch,
        input_output_aliases=aliases,
        compiler_params=_params(("arbitrary" if streams else "parallel", "arbitrary")),
        name="proj_swiglu" if swiglu else "proj",
    )(*args)
    n_wb = len(ws) if emit_bf16 else 0
    main, rest = outs[:n_main], outs[n_main:]
    side_outs, rest = rest[:n_side], rest[n_side:]
    wb, rest = rest[:n_wb], rest[n_wb:]
    stream_outs = []
    for st in streams:
        stream_outs.append(rest[:len(st.outs)])
        rest = rest[len(st.outs):]
    return main, side_outs, wb, stream_outs


def _proj_first_tile(x, ws_f32, *, tm, tn, side, **kw):
    main, side_outs, wb, _ = _proj(x, ws_f32, tm=tm, tn=tn, row_tiles=(0, 1), side=side,
                                   emit_bf16=True, **kw)
    return main, side_outs, wb


def _proj_other_tiles(x, wb, prev, *, tm, tn, streams, **kw):
    main, _, _, stream_outs = _proj(x, [(w, 0) for w in wb], tm=tm, tn=tn,
                                    row_tiles=(1, x.shape[0] // tm), prev=prev, streams=streams,
                                    **kw)
    return main, stream_outs


def _resid_kernel(*refs, n_lhs, factor, emit_x):
    it = iter(refs)
    lhs_refs = [next(it) for _ in range(n_lhs)]
    lhs_s_refs = [next(it) for _ in range(n_lhs)]
    w_refs = [next(it) for _ in range(n_lhs)]
    x_ref, gate_ref, gain_ref, sh_ref, sc_ref = (next(it) for _ in range(5))
    xs_ref, gate_s_ref, sh_s_ref, sc_s_ref = (next(it) for _ in range(4))
    n_out = 2 if emit_x else 1
    outs = [next(it) for _ in range(n_out)]
    outs_s = [next(it) for _ in range(n_out)]

    def update(lhs, x_in, gate, sh, sc, out_refs):
        acc = _bdot(lhs[0][...], w_refs[0][...])
        for l_ref, w_ref in zip(lhs[1:], w_refs[1:]):
            acc = acc + _bdot(l_ref[...], w_ref[...])
        x_new = x_in[...] + (factor * gate[...]) * acc
        if emit_x:
            out_refs[0][...] = x_new
        h_ref = out_refs[-1]
        h_ref[...] = _norm_modulate(x_new, gain_ref[...], sh[...], sc[...]).astype(h_ref.dtype)

    update(lhs_refs, x_ref, gate_ref, sh_ref, sc_ref, outs)

    @pl.when(pl.program_id(0) == 0)
    def _():
        update(lhs_s_refs, xs_ref, gate_s_ref, sh_s_ref, sc_s_ref, outs_s)


def _resid(lhs_list, lhs_s_list, ws, x, x_s, mod, mod_s, gate_chunk, gain_next, mod_next,
           mod_next_s, shift_chunk_next, *, factor, tm, emit_x, h_dtype):
    m, d = x.shape
    ns = x_s.shape[0]
    groups = mod.shape[0]
    tiles_per_group = (m // tm) // groups
    kp = lhs_list[0].shape[1]
    once = dict(pipeline_mode=pl.Buffered(1))

    def mod_spec(chunk):
        return pl.BlockSpec((None, 1, d), lambda i: (i // tiles_per_group, 0, chunk))

    def mod_s_spec(chunk):
        return pl.BlockSpec((None, ns, d), lambda i: (0, 0, chunk), **once)

    in_specs = [pl.BlockSpec((tm, kp), lambda i: (i, 0)) for _ in lhs_list]
    in_specs += [pl.BlockSpec((ns, kp), lambda i: (0, 0), **once) for _ in lhs_s_list]
    in_specs += [pl.BlockSpec((kp, d), lambda i, k=k: (k, 0), **once) for _, k in ws]
    in_specs += [pl.BlockSpec((tm, d), lambda i: (i, 0)), mod_spec(gate_chunk),
                 pl.BlockSpec((1, d), lambda i: (0, 0)),
                 mod_spec(shift_chunk_next), mod_spec(shift_chunk_next + 1),
                 pl.BlockSpec((ns, d), lambda i: (0, 0), **once), mod_s_spec(gate_chunk),
                 mod_s_spec(shift_chunk_next), mod_s_spec(shift_chunk_next + 1)]
    row = pl.BlockSpec((tm, d), lambda i: (i, 0))
    row_s = pl.BlockSpec((ns, d), lambda i: (0, 0))
    dtypes = ([F32] if emit_x else []) + [h_dtype]
    out_specs = [row for _ in dtypes] + [row_s for _ in dtypes]
    out_shape = ([jax.ShapeDtypeStruct((m, d), t) for t in dtypes]
                 + [jax.ShapeDtypeStruct((ns, d), t) for t in dtypes])
    outs = pl.pallas_call(
        functools.partial(_resid_kernel, n_lhs=len(lhs_list), factor=factor, emit_x=emit_x),
        grid=(m // tm,),
        in_specs=in_specs,
        out_specs=out_specs,
        out_shape=out_shape,
        compiler_params=_params(("arbitrary",)),
        name="resid",
    )(*lhs_list, *lhs_s_list, *[w for w, _ in ws], x, mod, gain_next.reshape(1, d), mod_next,
      mod_next, x_s, mod_s, mod_next_s, mod_next_s)
    return outs[:len(dtypes)], outs[len(dtypes):]


def _lru_gates(xc, wg_ref, ba, bi, sp):
    a_parts, b_parts = [], []
    for g in range(W_LRU // LRU_GATE_GROUP):
        cols = slice(g * LRU_GATE_GROUP, (g + 1) * LRU_GATE_GROUP)
        xg = xc[:, cols]
        ri = _bdot(xg.astype(BF16), wg_ref[g].astype(BF16))
        r = _sigmoid(ri[:, :LRU_GATE_GROUP] + ba[:, cols])
        i = _sigmoid(ri[:, LRU_GATE_GROUP:] + bi[:, cols])
        log_a = (-LRU_C * r) * sp[:, cols]
        a = jnp.exp(log_a)
        a_parts.append(a)
        b_parts.append(jnp.sqrt(1.0 - a * a) * (i * xg))
    return jnp.concatenate(a_parts, axis=1), jnp.concatenate(b_parts, axis=1)


def _causal_conv_from_buf(buf_ref, x, w_ref, b_ref, rows):
    xe = buf_ref[0:SUBLANES + rows, :]
    shifted = lambda k: pltpu.roll(xe, k, 0)[SUBLANES:, :]
    y = b_ref[...] + w_ref[0:1, :] * shifted(3)
    y = y + w_ref[1:2, :] * shifted(2)
    y = y + w_ref[2:3, :] * shifted(1)
    return y + w_ref[3:4, :] * x


def _lru_prompt_kernel(xl_ref, gl_ref, cw_ref, cb_ref, wg_ref, ba_ref, bi_ref, lam_ref,
                       o_ref, hT_ref, xbuf, a_scr, b_scr, hcar):
    t = pl.program_id(1)
    tt = xl_ref.shape[0]

    @pl.when(t == 0)
    def _():
        xbuf[0:SUBLANES, :] = jnp.zeros((SUBLANES, W_LRU), F32)
        hcar[...] = jnp.zeros_like(hcar)

    x = xl_ref[...]
    xbuf[SUBLANES:SUBLANES + tt, :] = x
    xc = _causal_conv_from_buf(xbuf, x, cw_ref, cb_ref, tt)
    xbuf[0:SUBLANES, :] = x[tt - SUBLANES:, :]

    sp = _softplus(-lam_ref[...])
    a, bt = _lru_gates(xc, wg_ref, ba_ref[...], bi_ref[...], sp)
    a_scr[...] = a
    b_scr[...] = bt

    rid = lax.broadcasted_iota(jnp.int32, (SUBLANES, W_LRU), 0)

    def scan8(a8, b8, h_in):
        for s in (1, 2, 4):
            a_sh = pltpu.roll(a8, s, 0)
            b_sh = pltpu.roll(b8, s, 0)
            m = rid >= s
            b8 = jnp.where(m, a8 * b_sh + b8, b8)
            a8 = jnp.where(m, a8 * a_sh, a8)
        h8 = a8 * h_in + b8
        return h8, jnp.broadcast_to(h8[SUBLANES - 1:SUBLANES, :], (SUBLANES, W_LRU))

    def body(g, h_in):
        r0 = pl.multiple_of(g * SCAN_ROWS, SCAN_ROWS)
        lo = pl.ds(r0, SUBLANES)
        hi = pl.ds(r0 + SUBLANES, SUBLANES)
        h_lo, h_mid = scan8(a_scr[lo, :], b_scr[lo, :], h_in)
        h_hi, h_out = scan8(a_scr[hi, :], b_scr[hi, :], h_mid)
        rows = pl.ds(r0, SCAN_ROWS)
        h16 = jnp.concatenate([h_lo, h_hi], axis=0)
        o_ref[rows, :] = (h16 * _gelu_tanh(gl_ref[rows, :])).astype(o_ref.dtype)
        return h_out

    h_last = lax.fori_loop(0, tt // SCAN_ROWS, body, hcar[...])
    hcar[...] = h_last

    @pl.when(t == pl.num_programs(1) - 1)
    def _():
        hT_ref[...] = h_last[0:1, :]


def _lru_prompt(proj, batch, seq, cw, cb, wg, ba, bi, lam):
    tt = LRU_TIME_TILE
    nt = seq // tt
    row = lambda v: v.reshape(1, W_LRU)
    full = lambda shape: pl.BlockSpec(shape, lambda b, t: (0,) * len(shape))
    out, h_t = pl.pallas_call(
        _lru_prompt_kernel,
        grid=(batch, nt),
        in_specs=[pl.BlockSpec((tt, W_LRU), lambda b, t: (b * nt + t, 0)),
                  pl.BlockSpec((tt, W_LRU), lambda b, t: (b * nt + t, 1)),
                  full((CONV_W, W_LRU)), full((1, W_LRU)), full(wg.shape),
                  full((1, W_LRU)), full((1, W_LRU)), full((1, W_LRU))],
        out_specs=[pl.BlockSpec((tt, W_LRU), lambda b, t: (b * nt + t, 0)),
                   pl.BlockSpec((None, 1, W_LRU), lambda b, t: (b, 0, 0))],
        out_shape=[jax.ShapeDtypeStruct((batch * seq, W_LRU), BF16),
                   jax.ShapeDtypeStruct((batch, 1, W_LRU), F32)],
        scratch_shapes=[pltpu.VMEM((tt + SUBLANES, W_LRU), F32),
                        pltpu.VMEM((tt, W_LRU), F32),
                        pltpu.VMEM((tt, W_LRU), F32),
                        pltpu.VMEM((SUBLANES, W_LRU), F32)],
        compiler_params=_params(("parallel", "arbitrary")),
        name="lru_prompt",
    )(proj, proj, cw, row(cb), wg, row(ba), row(bi), row(lam))
    return out, h_t.reshape(batch, W_LRU)


def _ssd_prompt_kernel(z_ref, xbc_ref, dt_ref, cw_ref, cb_ref, dtb_ref, alog_ref, dexp_ref,
                       ng_ref, y_ref, st_ref, xbuf, st_scr, y_scr, m_scr, xbd_scr):
    c = pl.program_id(1)
    lc = SSD_CHUNK

    @pl.when(c == 0)
    def _():
        xbuf[0:SUBLANES, :] = jnp.zeros((SUBLANES, SSD_CONV_DIM), F32)
        st_scr[...] = jnp.zeros_like(st_scr)
        xbd_scr[...] = jnp.zeros_like(xbd_scr)

    x = xbc_ref[...]
    xbuf[SUBLANES:SUBLANES + lc, :] = x
    act = _silu(_causal_conv_from_buf(xbuf, x, cw_ref, cb_ref, lc))
    xbuf[0:SUBLANES, :] = x[lc - SUBLANES:, :]
    xs = act[:, :W_SSD]
    bm = act[:, W_SSD:W_SSD + SSD_GROUPS * SSD_STATE]
    cm = act[:, W_SSD + SSD_GROUPS * SSD_STATE:]

    dt = _softplus(dt_ref[...] + dtb_ref[...])
    d_a = dt * (-jnp.exp(alog_ref[...]))
    row_i = lax.broadcasted_iota(jnp.int32, (lc, lc), 0)
    col_i = lax.broadcasted_iota(jnp.int32, (lc, lc), 1)
    causal = row_i >= col_i
    tril = jnp.where(causal, 1.0, 0.0).astype(F32)
    cs = jnp.dot(tril, d_a, preferred_element_type=F32, precision=lax.Precision.HIGHEST)
    cs_t = cs.T
    dt_t = dt.T
    cs_last = cs[lc - 1:lc, :]

    def per_head_lanes(v):
        rows = v.shape[0]
        return jnp.concatenate(
            [jnp.broadcast_to(v[:, h:h + 1], (rows, SSD_HEAD_DIM)) for h in range(SSD_HEADS)],
            axis=1)

    w_exp = per_head_lanes(jnp.exp(cs_last - cs) * dt)
    ecs_exp = per_head_lanes(jnp.exp(cs))
    cd_exp = per_head_lanes(jnp.exp(cs_last))
    gw = SSD_HPG * SSD_HEAD_DIM
    low_half = col_i < SSD_HEAD_DIM

    for g in range(SSD_GROUPS):
        ncols = slice(g * SSD_STATE, (g + 1) * SSD_STATE)
        gcols = slice(g * gw, (g + 1) * gw)
        b_g = bm[:, ncols].astype(BF16)
        c_g = cm[:, ncols].astype(BF16)
        cb_mat = lax.dot_general(c_g, b_g, (((1,), (1,)), ((), ())),
                                 preferred_element_type=F32)
        for e in range(SSD_HPG):
            h = g * SSD_HPG + e
            cs_col = jnp.broadcast_to(cs[:, h:h + 1], (lc, lc))
            l_mat = jnp.exp(jnp.where(causal, cs_col - cs_t[h:h + 1, :], -jnp.inf))
            m_scr[g, :, e * lc:(e + 1) * lc] = (cb_mat * l_mat * dt_t[h:h + 1, :]).astype(BF16)
        for q in range(SSD_HPG // 2):
            lanes = slice(q * LANES, (q + 1) * LANES)
            slab = xs[:, g * gw + q * LANES:g * gw + (q + 1) * LANES]
            xbd_scr[g, (2 * q) * lc:(2 * q + 1) * lc, lanes] = jnp.where(
                low_half, slab, 0.0).astype(BF16)
            xbd_scr[g, (2 * q + 1) * lc:(2 * q + 2) * lc, lanes] = jnp.where(
                low_half, 0.0, slab).astype(BF16)
        st_g = st_scr[:, gcols]
        y_off = _bdot(c_g, st_g.astype(BF16)) * ecs_exp[:, gcols]
        y_scr[:, gcols] = (_bdot(m_scr[g], xbd_scr[g]) + y_off
                           + dexp_ref[:, gcols] * xs[:, gcols])
        xw = (xs[:, gcols] * w_exp[:, gcols]).astype(BF16)
        st_scr[:, gcols] = cd_exp[:, gcols] * st_g + lax.dot_general(
            b_g, xw, (((0,), (0,)), ((), ())), preferred_element_type=F32)

    yg = y_scr[...] * _silu(z_ref[...])
    ms = jnp.mean(yg * yg, axis=-1, keepdims=True)
    y_ref[...] = (yg * lax.rsqrt(ms + EPS) * ng_ref[...]).astype(y_ref.dtype)

    @pl.when(c == pl.num_programs(1) - 1)
    def _():
        st_ref[...] = st_scr[...].T


def _ssd_prompt(proj, dt_raw, batch, seq, cw, cb, dtb, alog, dexp, ng):
    lc = SSD_CHUNK
    nc = seq // lc
    full = lambda shape: pl.BlockSpec(shape, lambda b, c: (0,) * len(shape))
    z_blk = (2 * W_LRU) // W_SSD
    xbc_blk = (2 * W_LRU + W_SSD) // SSD_CONV_DIM
    y, st = pl.pallas_call(
        _ssd_prompt_kernel,
        grid=(batch, nc),
        in_specs=[pl.BlockSpec((lc, W_SSD), lambda b, c: (b * nc + c, z_blk)),
                  pl.BlockSpec((lc, SSD_CONV_DIM), lambda b, c: (b * nc + c, xbc_blk)),
                  pl.BlockSpec((lc, LANES), lambda b, c: (b * nc + c, 0)),
                  full((CONV_W, SSD_CONV_DIM)), full((1, SSD_CONV_DIM)),
                  full((1, LANES)), full((1, LANES)), full((1, W_SSD)), full((1, W_SSD))],
        out_specs=[pl.BlockSpec((lc, W_SSD), lambda b, c: (b * nc + c, 0)),
                   pl.BlockSpec((None, W_SSD, SSD_STATE), lambda b, c: (b, 0, 0))],
        out_shape=[jax.ShapeDtypeStruct((batch * seq, W_SSD), BF16),
                   jax.ShapeDtypeStruct((batch, W_SSD, SSD_STATE), F32)],
        scratch_shapes=[pltpu.VMEM((lc + SUBLANES, SSD_CONV_DIM), F32),
                        pltpu.VMEM((SSD_STATE, W_SSD), F32),
                        pltpu.VMEM((lc, W_SSD), F32),
                        pltpu.VMEM((SSD_GROUPS, lc, SSD_HPG * lc), BF16),
                        pltpu.VMEM((SSD_GROUPS, SSD_HPG * lc, SSD_HPG * SSD_HEAD_DIM), BF16)],
        compiler_params=_params(("parallel", "arbitrary")),
        name="ssd_prompt",
    )(proj, proj, dt_raw, cw, cb, dtb, alog, dexp, ng)
    return y, st.reshape(batch, SSD_HEADS, SSD_HEAD_DIM, SSD_STATE)


def _sample_pre_kernel(proj_ref, dt_ref, h0_ref, lconv_ref, sconv_ref,
                       lcw_ref, lcb_ref, wg_ref, ba_ref, bi_ref, lam_ref,
                       scw_ref, scb_ref, dtb_ref, alog_ref,
                       outl_ref, hnew_ref, lconv_new_ref, sconv_new_ref,
                       xs_ref, xdt_ref, bc_ref, dec_ref):
    nb = proj_ref.shape[0]
    xl = proj_ref[:, 0:W_LRU]
    gl = proj_ref[:, W_LRU:2 * W_LRU]
    xbc = proj_ref[:, 2 * W_LRU + W_SSD:IN_MAIN]

    def conv1(state_ref, width, x_new, w_ref, b_ref):
        y = b_ref[...] + w_ref[0:1, :] * state_ref[:, 0:width]
        y = y + w_ref[1:2, :] * state_ref[:, width:2 * width]
        y = y + w_ref[2:3, :] * state_ref[:, 2 * width:3 * width]
        return y + w_ref[3:4, :] * x_new

    xc = conv1(lconv_ref, W_LRU, xl, lcw_ref, lcb_ref)
    a, bt = _lru_gates(xc, wg_ref, ba_ref[...], bi_ref[...], _softplus(-lam_ref[...]))
    h_new = a * h0_ref[...] + bt
    hnew_ref[...] = h_new
    outl_ref[...] = (h_new * _gelu_tanh(gl)).astype(outl_ref.dtype)
    lconv_new_ref[:, 0:2 * W_LRU] = lconv_ref[:, W_LRU:3 * W_LRU]
    lconv_new_ref[:, 2 * W_LRU:3 * W_LRU] = xl

    act = _silu(conv1(sconv_ref, SSD_CONV_DIM, xbc, scw_ref, scb_ref))
    sconv_new_ref[:, 0:2 * SSD_CONV_DIM] = sconv_ref[:, SSD_CONV_DIM:3 * SSD_CONV_DIM]
    sconv_new_ref[:, 2 * SSD_CONV_DIM:3 * SSD_CONV_DIM] = xbc
    xs = act[:, :W_SSD]
    xs_ref[...] = xs
    bc_ref[...] = act[:, W_SSD:]
    dt = _softplus(dt_ref[...] + dtb_ref[...])
    dec = jnp.exp(dt * (-jnp.exp(alog_ref[...])))
    for h in range(SSD_HEADS):
        pcols = slice(h * SSD_HEAD_DIM, (h + 1) * SSD_HEAD_DIM)
        xdt_ref[:, pcols] = xs[:, pcols] * jnp.broadcast_to(dt[:, h:h + 1], (nb, SSD_HEAD_DIM))
        dec_ref[h] = jnp.broadcast_to(dec[:, h:h + 1], (nb, SSD_STATE))


def _sample_pre(proj, dt_raw, h0, lconv, sconv, p):
    nb = proj.shape[0]
    out_shape = [jax.ShapeDtypeStruct((nb, W_LRU), BF16),
                 jax.ShapeDtypeStruct((nb, W_LRU), F32),
                 jax.ShapeDtypeStruct((nb, 3 * W_LRU), F32),
                 jax.ShapeDtypeStruct((nb, 3 * SSD_CONV_DIM), F32),
                 jax.ShapeDtypeStruct((nb, W_SSD), F32),
                 jax.ShapeDtypeStruct((nb, W_SSD), F32),
                 jax.ShapeDtypeStruct((nb, 2 * SSD_GROUPS * SSD_STATE), F32),
                 jax.ShapeDtypeStruct((SSD_HEADS, nb, SSD_STATE), F32)]
    return pl.pallas_call(
        _sample_pre_kernel,
        out_shape=out_shape,
        compiler_params=pltpu.CompilerParams(vmem_limit_bytes=VMEM_LIMIT_BYTES),
        name="sample_pre",
    )(proj, dt_raw, h0, lconv, sconv,
      p["lru_cw"], p["lru_cb"], p["lru_wg"], p["lru_ba"], p["lru_bi"], p["lru_lam"],
      p["ssd_cw"], p["ssd_cb"], p["ssd_dtb"], p["ssd_alog"])


def _sample_state_kernel(s_ref, xdt_ref, bc_ref, dec_ref, o_ref, y_ref):
    bb = s_ref.shape[0]
    half = SSD_HPG * SSD_HEAD_DIM
    rid = lax.broadcasted_iota(jnp.int32, (bb, W_SSD), 0)
    xdt = xdt_ref[...]
    bcb = bc_ref[...].astype(BF16)
    for k in range(bb):
        xk = jnp.where(rid == k, xdt, 0.0).astype(BF16)
        for g in range(SSD_GROUPS):
            rows = slice(g * half, (g + 1) * half)
            b_g = bcb[:, g * SSD_STATE:(g + 1) * SSD_STATE]
            c_g = bcb[:, (SSD_GROUPS + g) * SSD_STATE:(SSD_GROUPS + g + 1) * SSD_STATE]
            outer = lax.dot_general(xk[:, rows], b_g, (((0,), (0,)), ((), ())),
                                    preferred_element_type=F32)
            dec = jnp.concatenate(
                [jnp.broadcast_to(dec_ref[g * SSD_HPG + e, k:k + 1, :], (SSD_HEAD_DIM, SSD_STATE))
                 for e in range(SSD_HPG)], axis=0)
            s_new = dec * s_ref[k, rows, :] + outer
            o_ref[k, rows, :] = s_new
            yk = lax.dot_general(c_g, s_new.astype(BF16), (((1,), (1,)), ((), ())),
                                 preferred_element_type=F32)
            y_ref[k:k + 1, rows] = yk[k:k + 1, :]


def _state_stream(ssm, xdt, bc, dec, bb=8, start=0):
    nb = ssm.shape[0]
    state_block = (bb, W_SSD, SSD_STATE)
    return _Stream(
        [(ssm, state_block, lambda k: (k, 0, 0)),
         (xdt, (bb, W_SSD), lambda k: (k, 0)),
         (bc, (bb, 2 * SSD_GROUPS * SSD_STATE), lambda k: (k, 0)),
         (dec, (SSD_HEADS, bb, SSD_STATE), lambda k: (0, k, 0))],
        [(jax.ShapeDtypeStruct(ssm.shape, F32), state_block, lambda k: (k, 0, 0)),
         (jax.ShapeDtypeStruct((nb, W_SSD), F32), (bb, W_SSD), lambda k: (k, 0))],
        _sample_state_kernel, start, nb // bb)


def _sample_post_kernel(y_ref, xs_ref, proj_ref, dexp_ref, ng_ref, o_ref):
    z = proj_ref[:, 2 * W_LRU:2 * W_LRU + W_SSD]
    yg = (y_ref[...] + dexp_ref[...] * xs_ref[...]) * _silu(z)
    ms = jnp.mean(yg * yg, axis=-1, keepdims=True)
    o_ref[...] = (yg * lax.rsqrt(ms + EPS) * ng_ref[...]).astype(o_ref.dtype)


def _sample_post(y_raw, xs, proj, dexp, ng):
    return pl.pallas_call(
        _sample_post_kernel,
        out_shape=jax.ShapeDtypeStruct(y_raw.shape, BF16),
        compiler_params=pltpu.CompilerParams(vmem_limit_bytes=VMEM_LIMIT_BYTES),
        name="sample_post",
    )(y_raw, xs, proj, dexp, ng)


def _block_diag_groups(w):
    per = LRU_GATE_GROUP // LRU_BLOCK
    w4 = w.reshape(LRU_HEADS // per, per, LRU_BLOCK, LRU_BLOCK)
    bd = jnp.einsum("ghij,hk->ghikj", w4, jnp.eye(per, dtype=w.dtype))
    return bd.reshape(LRU_HEADS // per, LRU_GATE_GROUP, LRU_GATE_GROUP)


def _pad_lanes(v):
    v = v.reshape(1, -1)
    return jnp.pad(v, ((0, 0), (0, LANES - v.shape[1])))


def kernel(x_prompt, x_sample, c_prompt, c_sample, state_lru_h, state_lru_conv, state_ssm, state_ssd_conv, w_ada, b_ada, g_ffn1, w_up1, w_down1, g_mix, w_in, lru_conv_w, lru_conv_b, lru_wa, lru_ba, lru_wi, lru_bi, lru_lambda, ssd_conv_w, ssd_conv_b, ssd_dt_bias, ssd_A_log, ssd_D, ssd_norm_g, w_out, g_ffn2, w_up2, w_down2, w_ada_f, b_ada_f, g_final):
    bp, seq, d = x_prompt.shape
    bs = x_sample.shape[0]
    depth = w_ada.shape[0]
    assert depth == 1 and x_sample.shape[1] == 1 and d == D_MODEL

    pad_rows = (-(bs + bp)) % (2 * SUBLANES)
    c_rows = bs + bp + pad_rows
    c_all = jnp.concatenate([c_sample, c_prompt, jnp.zeros((pad_rows, d), F32)], axis=0)

    def split_rows(mod_all):
        width = mod_all.shape[1]
        return (mod_all[bs:bs + bp].reshape(bp, 1, width),
                mod_all.reshape(1, c_rows, width)[:, :bs])

    w_in_t = jnp.swapaxes(w_in[0], 0, 1)
    w_dt_t = jnp.pad(w_in_t[IN_MAIN:], ((0, LANES - SSD_HEADS), (0, 0)))
    up_blocks = D_FF // 512
    p = {
        "lru_cw": lru_conv_w[0], "lru_cb": lru_conv_b[0].reshape(1, W_LRU),
        "lru_wg": jnp.concatenate([_block_diag_groups(lru_wa[0]), _block_diag_groups(lru_wi[0])],
                                  axis=-1),
        "lru_ba": lru_ba[0].reshape(1, W_LRU), "lru_bi": lru_bi[0].reshape(1, W_LRU),
        "lru_lam": lru_lambda[0].reshape(1, W_LRU),
        "ssd_cw": ssd_conv_w[0], "ssd_cb": ssd_conv_b[0].reshape(1, SSD_CONV_DIM),
        "ssd_dtb": _pad_lanes(ssd_dt_bias[0]), "ssd_alog": _pad_lanes(ssd_A_log[0]),
        "ssd_dexp": jnp.repeat(ssd_D[0], SSD_HEAD_DIM).reshape(1, W_SSD),
        "ssd_ng": ssd_norm_g[0].reshape(1, W_SSD),
    }

    xp = x_prompt.reshape(bp * seq, d)
    xs = x_sample.reshape(bs, d)
    tm = 1024
    up_kw = dict(n_out=D_FF, swiglu=True, out_dtype=BF16)

    mod_a_p, mod_a_s = split_rows(_ada(c_all, w_ada[0], b_ada[0], 2 * d))

    pro = dict(gain=g_ffn1[0], mod=mod_a_p, shift_chunk=0)
    first, (hmid_s,), wb = _proj_first_tile(
        xp, [(w_up1[0], 0), (w_up1[0], up_blocks)], tm=tm, tn=512, side=xs, mod_side=mod_a_s,
        **up_kw, **pro)
    (hmid,), ((w_down_b,), (mod_b_all,)) = _proj_other_tiles(
        xp, wb, first, tm=tm, tn=512,
        streams=[_cast_stream(w_down1[0], 32),
                 _ada_stream(c_all, w_ada[0], b_ada[0], 2 * d, (N_MOD - 2) * d, 256)],
        **up_kw, **pro)
    mod_b_p, mod_b_s = split_rows(mod_b_all)
    (xp, hp), (xs, hs) = _resid([hmid], [hmid_s], [(w_down_b, 0)], xp, xs, mod_b_p, mod_b_s, 0,
                                g_mix[0], mod_b_p, mod_b_s, 1, factor=0.5, tm=256, emit_x=True,
                                h_dtype=BF16)

    in_kw = dict(n_out=IN_MAIN, swiglu=False, out_dtype=F32, trans_w=True, w_extra=w_dt_t)
    first, (proj_s, dt_raw_s), wb = _proj_first_tile(hp, [(w_in_t, 0)], tm=tm, tn=512, side=hs,
                                                     **in_kw)
    lconv = state_lru_conv[0].reshape(bs, (CONV_W - 1) * W_LRU)
    sconv = state_ssd_conv[0].reshape(bs, (CONV_W - 1) * SSD_CONV_DIM)
    out_l_s, lru_h_s, lconv_new, sconv_new, xs_act, xdt, bc, dec = _sample_pre(
        proj_s, dt_raw_s, state_lru_h[0], lconv, sconv, p)
    (proj, dt_raw), ((w_out_b,), (ssm_s, y_raw)) = _proj_other_tiles(
        hp, wb, first, tm=tm, tn=IN_MAIN // 6,
        streams=[_cast_stream(w_out[0], 16),
                 _state_stream(state_ssm[0].reshape(bs, W_SSD, SSD_STATE), xdt, bc, dec)],
        **in_kw)
    y_ssd_s = _sample_post(y_raw, xs_act, proj_s, p["ssd_dexp"], p["ssd_ng"])

    out_l, lru_h_p = _lru_prompt(proj, bp, seq, p["lru_cw"], p["lru_cb"], p["lru_wg"],
                                 p["lru_ba"], p["lru_bi"], p["lru_lam"])
    y_ssd, ssm_p = _ssd_prompt(proj, dt_raw, bp, seq, p["ssd_cw"], p["ssd_cb"], p["ssd_dtb"],
                               p["ssd_alog"], p["ssd_dexp"], p["ssd_ng"])
    proj3 = proj.reshape(bp, seq, IN_MAIN)
    lru_buf_p = proj3[:, seq - (CONV_W - 1):, :W_LRU]
    ssd_buf_p = proj3[:, seq - (CONV_W - 1):, 2 * W_LRU + W_SSD:]

    (xp, hp), (xs, hs) = _resid([out_l, y_ssd], [out_l_s, y_ssd_s], [(w_out_b, 0), (w_out_b, 1)],
                                xp, xs, mod_b_p, mod_b_s, 3, g_ffn2[0], mod_b_p, mod_b_s, 4,
                                factor=1.0, tm=512, emit_x=True, h_dtype=BF16)

    first, (hmid_s,), wb = _proj_first_tile(
        hp, [(w_up2[0], 0), (w_up2[0], up_blocks)], tm=tm, tn=512, side=hs, **up_kw)
    (hmid,), ((w_down_b,), (modf_all,)) = _proj_other_tiles(
        hp, wb, first, tm=tm, tn=512,
        streams=[_cast_stream(w_down2[0], 32),
                 _ada_stream(c_all, w_ada_f, b_ada_f, 0, 2 * d, 256)],
        **up_kw)
    modf_p, modf_s = split_rows(modf_all)
    (yp,), (ys,) = _resid([hmid], [hmid_s], [(w_down_b, 0)], xp, xs, mod_b_p, mod_b_s, 6, g_final,
                          modf_p, modf_s, 0, factor=0.5, tm=256, emit_x=False, h_dtype=F32)

    stack = lambda v: v[None]
    return (yp.reshape(bp, seq, d), ys.reshape(bs, 1, d),
            stack(lru_h_p), stack(lru_buf_p), stack(ssm_p), stack(ssd_buf_p),
            stack(lru_h_s), stack(lconv_new.reshape(bs, CONV_W - 1, W_LRU)),
            stack(ssm_s.reshape(bs, SSD_HEADS, SSD_HEAD_DIM, SSD_STATE)),
            stack(sconv_new.reshape(bs, CONV_W - 1, SSD_CONV_DIM)))
```

```python
import functools
from typing import Callable, NamedTuple

import jax
import jax.numpy as jnp
from jax import lax
from jax.experimental import pallas as pl
from jax.experimental.pallas import tpu as pltpu

F32 = jnp.float32
BF16 = jnp.bfloat16

D_MODEL = 2048
D_FF = 5632
W_LRU = 1024
W_SSD = 1024
LRU_HEADS = 16
LRU_BLOCK = 64
LRU_C = 8.0
SSD_HEADS = 16
SSD_HEAD_DIM = 64
SSD_GROUPS = 2
SSD_HPG = 8
SSD_STATE = 128
SSD_CHUNK = 128
CONV_W = 4
SSD_CONV_DIM = W_SSD + 2 * SSD_GROUPS * SSD_STATE
IN_MAIN = 2 * W_LRU + W_SSD + SSD_CONV_DIM
N_MOD = 9
EPS = 1e-6

LANES = 128
SUBLANES = 8
VMEM_LIMIT_BYTES = 56 * 1024 * 1024

LRU_GATE_GROUP = 256
LRU_TIME_TILE = 256
SCAN_ROWS = 2 * SUBLANES


def _sigmoid(v):
    return 0.5 * (jnp.tanh(0.5 * v) + 1.0)


def _silu(v):
    return v * _sigmoid(v)


def _softplus(v):
    return jnp.maximum(v, 0.0) + jnp.log1p(jnp.exp(-jnp.abs(v)))


def _gelu_tanh(v):
    return 0.5 * v * (1.0 + jnp.tanh(0.7978845608028654 * (v + 0.044715 * (v * v * v))))


def _bdot(a, b):
    return jnp.dot(a, b, preferred_element_type=F32)


def _params(sem):
    return pltpu.CompilerParams(dimension_semantics=sem, vmem_limit_bytes=VMEM_LIMIT_BYTES)


def _ada_kernel(c_ref, w_ref, b_ref, o_ref):
    s = _silu(c_ref[...]).astype(BF16)
    o_ref[...] = _bdot(s, w_ref[...].astype(BF16)) + b_ref[...]


def _ada(c, w, b, cols, tn=1024):
    m, k = c.shape
    n = cols
    return pl.pallas_call(
        _ada_kernel,
        grid=(n // tn,),
        in_specs=[pl.BlockSpec((m, k), lambda j: (0, 0)),
                  pl.BlockSpec((k, tn), lambda j: (0, j)),
                  pl.BlockSpec((1, tn), lambda j: (0, j))],
        out_specs=pl.BlockSpec((m, tn), lambda j: (0, j)),
        out_shape=jax.ShapeDtypeStruct((m, n), F32),
        compiler_params=_params(("parallel",)),
        name="ada_proj",
    )(c, w, b.reshape(1, -1))


def _norm_modulate(x, gain, shift, scale):
    ms = jnp.mean(x * x, axis=-1, keepdims=True)
    y = x * lax.rsqrt(ms + EPS) * gain
    return y * (1.0 + scale) + shift


def _norm_rows_kernel(x_ref, gain_ref, sh_ref, sc_ref, xs_ref, shs_ref, scs_ref, o_ref, os_ref):
    o_ref[...] = _norm_modulate(x_ref[...], gain_ref[...], sh_ref[...],
                                sc_ref[...]).astype(o_ref.dtype)

    @pl.when(pl.program_id(0) == 0)
    def _():
        os_ref[...] = _norm_modulate(xs_ref[...], gain_ref[...], shs_ref[...],
                                     scs_ref[...]).astype(os_ref.dtype)


def _norm_rows(x, x_s, gain, mod, mod_s, shift_chunk, *, tm):
    m, d = x.shape
    ns = x_s.shape[0]
    tiles_per_group = (m // tm) // mod.shape[0]
    once = dict(pipeline_mode=pl.Buffered(1))
    mod_spec = lambda c: pl.BlockSpec((None, 1, d), lambda i: (i // tiles_per_group, 0, c))
    mod_s_spec = lambda c: pl.BlockSpec((None, ns, d), lambda i: (0, 0, c), **once)
    return pl.pallas_call(
        _norm_rows_kernel,
        grid=(m // tm,),
        in_specs=[pl.BlockSpec((tm, d), lambda i: (i, 0)), pl.BlockSpec((1, d), lambda i: (0, 0)),
                  mod_spec(shift_chunk), mod_spec(shift_chunk + 1),
                  pl.BlockSpec((ns, d), lambda i: (0, 0), **once),
                  mod_s_spec(shift_chunk), mod_s_spec(shift_chunk + 1)],
        out_specs=[pl.BlockSpec((tm, d), lambda i: (i, 0)), pl.BlockSpec((ns, d), lambda i: (0, 0))],
        out_shape=[jax.ShapeDtypeStruct((m, d), BF16), jax.ShapeDtypeStruct((ns, d), BF16)],
        compiler_params=_params(("arbitrary",)),
        name="norm_rows",
    )(x, gain.reshape(1, d), mod, mod, x_s, mod_s, mod_s)


class _Stream(NamedTuple):
    ins: list
    outs: list
    body: Callable
    start: int
    steps: int


def _cast_body(src_ref, dst_ref):
    dst_ref[...] = src_ref[...].astype(dst_ref.dtype)


def _cast_stream(w, chunks, start=0):
    block = (w.shape[0] // chunks, w.shape[1])
    rows = lambda k: (k, 0)
    return _Stream([(w, block, rows)], [(jax.ShapeDtypeStruct(w.shape, BF16), block, rows)],
                   _cast_body, start, chunks)


def _ada_stream(c, w, b, col0, cols, tn, start=0):
    m, k = c.shape
    t0 = col0 // tn
    return _Stream(
        [(c, (m, k), lambda s: (0, 0)), (w, (k, tn), lambda s: (0, s + t0)),
         (b.reshape(1, -1), (1, tn), lambda s: (0, s + t0))],
        [(jax.ShapeDtypeStruct((m, cols), F32), (m, tn), lambda s: (0, s))],
        _ada_kernel, start, cols // tn)


def _wdot(h, w, trans_w):
    if trans_w:
        return lax.dot_general(h, w, (((1,), (1,)), ((), ())), preferred_element_type=F32)
    return _bdot(h, w)


def _proj_kernel(*refs, n_w, swiglu, trans_w, has_extra, has_side, emit_bf16, streams, n_prev,
                 nj):
    it = iter(refs)
    x_ref = next(it)
    xs_ref = next(it) if has_side else None
    w_refs = [next(it) for _ in range(n_w)]
    wx_ref = next(it) if has_extra else None
    stream_ins = [[next(it) for _ in st.ins] for st in streams]
    for _ in range(n_prev):
        next(it)
    o_ref = next(it)
    ox_ref = next(it) if has_extra else None
    os_ref = next(it) if has_side else None
    osx_ref = next(it) if has_side and has_extra else None
    wo_refs = [next(it) for _ in range(n_w)] if emit_bf16 else []
    stream_outs = [[next(it) for _ in st.outs] for st in streams]

    j = pl.program_id(1)

    if has_extra:
        @pl.when(j == 0)
        def _():
            wxb = wx_ref[...].astype(BF16)
            ox_ref[...] = _wdot(x_ref[...], wxb, trans_w)
            if has_side:
                osx_ref[...] = _wdot(xs_ref[...], wxb, trans_w)

    wbs = [w_ref[...].astype(BF16) for w_ref in w_refs]
    for wo_ref, wb in zip(wo_refs, wbs):
        wo_ref[...] = wb

    def project(h, out_ref):
        if swiglu:
            g = _wdot(h, wbs[0], trans_w)
            u = _wdot(h, wbs[1], trans_w)
            out_ref[...] = (_silu(g) * u).astype(out_ref.dtype)
        else:
            out_ref[...] = _wdot(h, wbs[0], trans_w).astype(out_ref.dtype)

    project(x_ref[...], o_ref)
    if has_side:
        project(xs_ref[...], os_ref)

    step = pl.program_id(0) * nj + j
    for st, ins, outs in zip(streams, stream_ins, stream_outs):
        @pl.when((step >= st.start) & (step < st.start + st.steps))
        def _(st=st, ins=ins, outs=outs):
            st.body(*ins, *outs)


def _proj(x, ws, *, n_out, tm, tn, swiglu, out_dtype, row_tiles, side=None, trans_w=False,
          w_extra=None, emit_bf16=False, streams=(), prev=None):
    m, d = x.shape
    t0, t1 = row_tiles
    nj = n_out // tn
    has_side = side is not None
    has_extra = w_extra is not None
    single_row_tile = t1 - t0 == 1
    once = dict(pipeline_mode=pl.Buffered(1))

    def w_spec(off):
        if trans_w:
            return pl.BlockSpec((tn, d), lambda i, j: (j + off, 0))
        return pl.BlockSpec((d, tn), lambda i, j: (0, j + off))

    x_mode = once if single_row_tile else {}
    in_specs = [pl.BlockSpec((tm, d), lambda i, j: (i + t0, 0), **x_mode)]
    args = [x]
    if has_side:
        ns = side.shape[0]
        in_specs.append(pl.BlockSpec((ns, d), lambda i, j: (0, 0), **once))
        args.append(side)
    in_specs += [w_spec(off) for _, off in ws]
    args += [w for w, _ in ws]
    if has_extra:
        nx = w_extra.shape[0] if trans_w else w_extra.shape[1]
        in_specs.append(pl.BlockSpec(w_extra.shape, lambda i, j: (0, 0)))
        args.append(w_extra)
    def stream_spec(st, block, index_fn):
        return pl.BlockSpec(
            block, lambda i, j: index_fn(jnp.clip(i * nj + j - st.start, 0, st.steps - 1)))

    for st in streams:
        assert st.start + st.steps <= (t1 - t0) * nj
        for arr, block, index_fn in st.ins:
            in_specs.append(stream_spec(st, block, index_fn))
            args.append(arr)
    prev = list(prev or [])
    aliases = {}
    for k, buf in enumerate(prev):
        aliases[len(args)] = k
        in_specs.append(pl.BlockSpec(memory_space=pl.ANY))
        args.append(buf)

    out_specs = [pl.BlockSpec((tm, tn), lambda i, j: (i + t0, j))]
    out_shape = [jax.ShapeDtypeStruct((m, n_out), out_dtype)]
    if has_extra:
        out_specs.append(pl.BlockSpec((tm, nx), lambda i, j: (i + t0, 0)))
        out_shape.append(jax.ShapeDtypeStruct((m, nx), F32))
    n_main = len(out_shape)
    if has_side:
        out_specs.append(pl.BlockSpec((ns, tn), lambda i, j: (0, j)))
        out_shape.append(jax.ShapeDtypeStruct((ns, n_out), out_dtype))
        if has_extra:
            out_specs.append(pl.BlockSpec((ns, nx), lambda i, j: (0, 0)))
            out_shape.append(jax.ShapeDtypeStruct((ns, nx), F32))
    n_side = len(out_shape) - n_main
    if emit_bf16 or has_side:
        assert single_row_tile, "weight copies / side outputs are written once per column tile"
    if emit_bf16:
        for _ in ws:
            out_specs.append(w_spec(0))
            out_shape.append(jax.ShapeDtypeStruct((n_out, d) if trans_w else (d, n_out), BF16))
    for st in streams:
        for shape, block, index_fn in st.outs:
            out_specs.append(stream_spec(st, block, index_fn))
            out_shape.append(shape)
    outs = pl.pallas_call(
        functools.partial(_proj_kernel, n_w=len(ws), swiglu=swiglu, trans_w=trans_w,
                          has_extra=has_extra, has_side=has_side, emit_bf16=emit_bf16,
                          streams=tuple(streams), n_prev=len(prev), nj=nj),
        grid=(t1 - t0, nj),
        in_specs=in_specs,
        out_specs=out_specs,
        out_shape=out_shape,
        input_output_aliases=aliases,
        compiler_params=_params(("arbitrary" if streams else "parallel", "arbitrary")),
        name="proj_swiglu" if swiglu else "proj",
    )(*args)
    n_wb = len(ws) if emit_bf16 else 0
    main, rest = outs[:n_main], outs[n_main:]
    side_outs, rest = rest[:n_side], rest[n_side:]
    wb, rest = rest[:n_wb], rest[n_wb:]
    stream_outs = []
    for st in streams:
        stream_outs.append(rest[:len(st.outs)])
        rest = rest[len(st.outs):]
    return main, side_outs, wb, stream_outs


def _proj_first_tile(x, ws_f32, *, tm, tn, side, **kw):
    main, side_outs, wb, _ = _proj(x, ws_f32, tm=tm, tn=tn, row_tiles=(0, 1), side=side,
                                   emit_bf16=True, **kw)
    return main, side_outs, wb


def _proj_other_tiles(x, wb, prev, *, tm, tn, streams, **kw):
    main, _, _, stream_outs = _proj(x, [(w, 0) for w in wb], tm=tm, tn=tn,
                                    row_tiles=(1, x.shape[0] // tm), prev=prev, streams=streams,
                                    **kw)
    return main, stream_outs


def _resid_kernel(*refs, n_lhs, factor, emit_x):
    it = iter(refs)
    lhs_refs = [next(it) for _ in range(n_lhs)]
    lhs_s_refs = [next(it) for _ in range(n_lhs)]
    w_refs = [next(it) for _ in range(n_lhs)]
    x_ref, gate_ref, gain_ref, sh_ref, sc_ref = (next(it) for _ in range(5))
    xs_ref, gate_s_ref, sh_s_ref, sc_s_ref = (next(it) for _ in range(4))
    n_out = 2 if emit_x else 1
    outs = [next(it) for _ in range(n_out)]
    outs_s = [next(it) for _ in range(n_out)]

    def update(lhs, x_in, gate, sh, sc, out_refs):
        acc = _bdot(lhs[0][...], w_refs[0][...])
        for l_ref, w_ref in zip(lhs[1:], w_refs[1:]):
            acc = acc + _bdot(l_ref[...], w_ref[...])
        x_new = x_in[...] + (factor * gate[...]) * acc
        if emit_x:
            out_refs[0][...] = x_new
        h_ref = out_refs[-1]
        h_ref[...] = _norm_modulate(x_new, gain_ref[...], sh[...], sc[...]).astype(h_ref.dtype)

    update(lhs_refs, x_ref, gate_ref, sh_ref, sc_ref, outs)

    @pl.when(pl.program_id(0) == 0)
    def _():
        update(lhs_s_refs, xs_ref, gate_s_ref, sh_s_ref, sc_s_ref, outs_s)


def _resid(lhs_list, lhs_s_list, ws, x, x_s, mod, mod_s, gate_chunk, gain_next, mod_next,
           mod_next_s, shift_chunk_next, *, factor, tm, emit_x, h_dtype):
    m, d = x.shape
    ns = x_s.shape[0]
    groups = mod.shape[0]
    tiles_per_group = (m // tm) // groups
    kp = lhs_list[0].shape[1]
    once = dict(pipeline_mode=pl.Buffered(1))

    def mod_spec(chunk):
        return pl.BlockSpec((None, 1, d), lambda i: (i // tiles_per_group, 0, chunk))

    def mod_s_spec(chunk):
        return pl.BlockSpec((None, ns, d), lambda i: (0, 0, chunk), **once)

    in_specs = [pl.BlockSpec((tm, kp), lambda i: (i, 0)) for _ in lhs_list]
    in_specs += [pl.BlockSpec((ns, kp), lambda i: (0, 0), **once) for _ in lhs_s_list]
    in_specs += [pl.BlockSpec((kp, d), lambda i, k=k: (k, 0), **once) for _, k in ws]
    in_specs += [pl.BlockSpec((tm, d), lambda i: (i, 0)), mod_spec(gate_chunk),
                 pl.BlockSpec((1, d), lambda i: (0, 0)),
                 mod_spec(shift_chunk_next), mod_spec(shift_chunk_next + 1),
                 pl.BlockSpec((ns, d), lambda i: (0, 0), **once), mod_s_spec(gate_chunk),
                 mod_s_spec(shift_chunk_next), mod_s_spec(shift_chunk_next + 1)]
    row = pl.BlockSpec((tm, d), lambda i: (i, 0))
    row_s = pl.BlockSpec((ns, d), lambda i: (0, 0))
    dtypes = ([F32] if emit_x else []) + [h_dtype]
    out_specs = [row for _ in dtypes] + [row_s for _ in dtypes]
    out_shape = ([jax.ShapeDtypeStruct((m, d), t) for t in dtypes]
                 + [jax.ShapeDtypeStruct((ns, d), t) for t in dtypes])
    outs = pl.pallas_call(
        functools.partial(_resid_kernel, n_lhs=len(lhs_list), factor=factor, emit_x=emit_x),
        grid=(m // tm,),
        in_specs=in_specs,
        out_specs=out_specs,
        out_shape=out_shape,
        compiler_params=_params(("arbitrary",)),
        name="resid",
    )(*lhs_list, *lhs_s_list, *[w for w, _ in ws], x, mod, gain_next.reshape(1, d), mod_next,
      mod_next, x_s, mod_s, mod_next_s, mod_next_s)
    return outs[:len(dtypes)], outs[len(dtypes):]


def _lru_gates(xc, wg_ref, ba, bi, sp):
    a_parts, b_parts = [], []
    for g in range(W_LRU // LRU_GATE_GROUP):
        cols = slice(g * LRU_GATE_GROUP, (g + 1) * LRU_GATE_GROUP)
        xg = xc[:, cols]
        ri = _bdot(xg.astype(BF16), wg_ref[g].astype(BF16))
        r = _sigmoid(ri[:, :LRU_GATE_GROUP] + ba[:, cols])
        i = _sigmoid(ri[:, LRU_GATE_GROUP:] + bi[:, cols])
        log_a = (-LRU_C * r) * sp[:, cols]
        a = jnp.exp(log_a)
        a_parts.append(a)
        b_parts.append(jnp.sqrt(1.0 - a * a) * (i * xg))
    return jnp.concatenate(a_parts, axis=1), jnp.concatenate(b_parts, axis=1)


def _causal_conv_from_buf(buf_ref, x, w_ref, b_ref, rows):
    xe = buf_ref[0:SUBLANES + rows, :]
    shifted = lambda k: pltpu.roll(xe, k, 0)[SUBLANES:, :]
    y = b_ref[...] + w_ref[0:1, :] * shifted(3)
    y = y + w_ref[1:2, :] * shifted(2)
    y = y + w_ref[2:3, :] * shifted(1)
    return y + w_ref[3:4, :] * x


def _lru_prompt_kernel(xl_ref, gl_ref, cw_ref, cb_ref, wg_ref, ba_ref, bi_ref, lam_ref,
                       o_ref, hT_ref, xbuf, a_scr, b_scr, hcar):
    t = pl.program_id(1)
    tt = xl_ref.shape[0]

    @pl.when(t == 0)
    def _():
        xbuf[0:SUBLANES, :] = jnp.zeros((SUBLANES, W_LRU), F32)
        hcar[...] = jnp.zeros_like(hcar)

    x = xl_ref[...]
    xbuf[SUBLANES:SUBLANES + tt, :] = x
    xc = _causal_conv_from_buf(xbuf, x, cw_ref, cb_ref, tt)
    xbuf[0:SUBLANES, :] = x[tt - SUBLANES:, :]

    sp = _softplus(-lam_ref[...])
    a, bt = _lru_gates(xc, wg_ref, ba_ref[...], bi_ref[...], sp)
    a_scr[...] = a
    b_scr[...] = bt

    rid = lax.broadcasted_iota(jnp.int32, (SUBLANES, W_LRU), 0)

    def scan8(a8, b8, h_in):
        for s in (1, 2, 4):
            a_sh = pltpu.roll(a8, s, 0)
            b_sh = pltpu.roll(b8, s, 0)
            m = rid >= s
            b8 = jnp.where(m, a8 * b_sh + b8, b8)
            a8 = jnp.where(m, a8 * a_sh, a8)
        h8 = a8 * h_in + b8
        return h8, jnp.broadcast_to(h8[SUBLANES - 1:SUBLANES, :], (SUBLANES, W_LRU))

    def body(g, h_in):
        r0 = pl.multiple_of(g * SCAN_ROWS, SCAN_ROWS)
        lo = pl.ds(r0, SUBLANES)
        hi = pl.ds(r0 + SUBLANES, SUBLANES)
        h_lo, h_mid = scan8(a_scr[lo, :], b_scr[lo, :], h_in)
        h_hi, h_out = scan8(a_scr[hi, :], b_scr[hi, :], h_mid)
        rows = pl.ds(r0, SCAN_ROWS)
        h16 = jnp.concatenate([h_lo, h_hi], axis=0)
        o_ref[rows, :] = (h16 * _gelu_tanh(gl_ref[rows, :])).astype(o_ref.dtype)
        return h_out

    h_last = lax.fori_loop(0, tt // SCAN_ROWS, body, hcar[...])
    hcar[...] = h_last

    @pl.when(t == pl.num_programs(1) - 1)
    def _():
        hT_ref[...] = h_last[0:1, :]


def _lru_prompt(proj, batch, seq, cw, cb, wg, ba, bi, lam):
    tt = LRU_TIME_TILE
    nt = seq // tt
    row = lambda v: v.reshape(1, W_LRU)
    full = lambda shape: pl.BlockSpec(shape, lambda b, t: (0,) * len(shape))
    out, h_t = pl.pallas_call(
        _lru_prompt_kernel,
        grid=(batch, nt),
        in_specs=[pl.BlockSpec((tt, W_LRU), lambda b, t: (b * nt + t, 0)),
                  pl.BlockSpec((tt, W_LRU), lambda b, t: (b * nt + t, 1)),
                  full((CONV_W, W_LRU)), full((1, W_LRU)), full(wg.shape),
                  full((1, W_LRU)), full((1, W_LRU)), full((1, W_LRU))],
        out_specs=[pl.BlockSpec((tt, W_LRU), lambda b, t: (b * nt + t, 0)),
                   pl.BlockSpec((None, 1, W_LRU), lambda b, t: (b, 0, 0))],
        out_shape=[jax.ShapeDtypeStruct((batch * seq, W_LRU), BF16),
                   jax.ShapeDtypeStruct((batch, 1, W_LRU), F32)],
        scratch_shapes=[pltpu.VMEM((tt + SUBLANES, W_LRU), F32),
                        pltpu.VMEM((tt, W_LRU), F32),
                        pltpu.VMEM((tt, W_LRU), F32),
                        pltpu.VMEM((SUBLANES, W_LRU), F32)],
        compiler_params=_params(("parallel", "arbitrary")),
        name="lru_prompt",
    )(proj, proj, cw, row(cb), wg, row(ba), row(bi), row(lam))
    return out, h_t.reshape(batch, W_LRU)


def _ssd_prompt_kernel(z_ref, xbc_ref, dt_ref, cw_ref, cb_ref, dtb_ref, alog_ref, dexp_ref,
                       ng_ref, y_ref, st_ref, xbuf, st_scr, y_scr, m_scr, xbd_scr):
    c = pl.program_id(1)
    lc = SSD_CHUNK

    @pl.when(c == 0)
    def _():
        xbuf[0:SUBLANES, :] = jnp.zeros((SUBLANES, SSD_CONV_DIM), F32)
        st_scr[...] = jnp.zeros_like(st_scr)
        xbd_scr[...] = jnp.zeros_like(xbd_scr)

    x = xbc_ref[...]
    xbuf[SUBLANES:SUBLANES + lc, :] = x
    act = _silu(_causal_conv_from_buf(xbuf, x, cw_ref, cb_ref, lc))
    xbuf[0:SUBLANES, :] = x[lc - SUBLANES:, :]
    xs = act[:, :W_SSD]
    bm = act[:, W_SSD:W_SSD + SSD_GROUPS * SSD_STATE]
    cm = act[:, W_SSD + SSD_GROUPS * SSD_STATE:]

    dt = _softplus(dt_ref[...] + dtb_ref[...])
    d_a = dt * (-jnp.exp(alog_ref[...]))
    row_i = lax.broadcasted_iota(jnp.int32, (lc, lc), 0)
    col_i = lax.broadcasted_iota(jnp.int32, (lc, lc), 1)
    causal = row_i >= col_i
    tril = jnp.where(causal, 1.0, 0.0).astype(F32)
    cs = jnp.dot(tril, d_a, preferred_element_type=F32, precision=lax.Precision.HIGHEST)
    cs_t = cs.T
    dt_t = dt.T
    cs_last = cs[lc - 1:lc, :]

    def per_head_lanes(v):
        rows = v.shape[0]
        return jnp.concatenate(
            [jnp.broadcast_to(v[:, h:h + 1], (rows, SSD_HEAD_DIM)) for h in range(SSD_HEADS)],
            axis=1)

    w_exp = per_head_lanes(jnp.exp(cs_last - cs) * dt)
    ecs_exp = per_head_lanes(jnp.exp(cs))
    cd_exp = per_head_lanes(jnp.exp(cs_last))
    gw = SSD_HPG * SSD_HEAD_DIM
    low_half = col_i < SSD_HEAD_DIM

    for g in range(SSD_GROUPS):
        ncols = slice(g * SSD_STATE, (g + 1) * SSD_STATE)
        gcols = slice(g * gw, (g + 1) * gw)
        b_g = bm[:, ncols].astype(BF16)
        c_g = cm[:, ncols].astype(BF16)
        cb_mat = lax.dot_general(c_g, b_g, (((1,), (1,)), ((), ())),
                                 preferred_element_type=F32)
        for e in range(SSD_HPG):
            h = g * SSD_HPG + e
            cs_col = jnp.broadcast_to(cs[:, h:h + 1], (lc, lc))
            l_mat = jnp.exp(jnp.where(causal, cs_col - cs_t[h:h + 1, :], -jnp.inf))
            m_scr[g, :, e * lc:(e + 1) * lc] = (cb_mat * l_mat * dt_t[h:h + 1, :]).astype(BF16)
        for q in range(SSD_HPG // 2):
            lanes = slice(q * LANES, (q + 1) * LANES)
            slab = xs[:, g * gw + q * LANES:g * gw + (q + 1) * LANES]
            xbd_scr[g, (2 * q) * lc:(2 * q + 1) * lc, lanes] = jnp.where(
                low_half, slab, 0.0).astype(BF16)
            xbd_scr[g, (2 * q + 1) * lc:(2 * q + 2) * lc, lanes] = jnp.where(
                low_half, 0.0, slab).astype(BF16)
        st_g = st_scr[:, gcols]
        y_off = _bdot(c_g, st_g.astype(BF16)) * ecs_exp[:, gcols]
        y_scr[:, gcols] = (_bdot(m_scr[g], xbd_scr[g]) + y_off
                           + dexp_ref[:, gcols] * xs[:, gcols])
        xw = (xs[:, gcols] * w_exp[:, gcols]).astype(BF16)
        st_scr[:, gcols] = cd_exp[:, gcols] * st_g + lax.dot_general(
            b_g, xw, (((0,), (0,)), ((), ())), preferred_element_type=F32)

    yg = y_scr[...] * _silu(z_ref[...])
    ms = jnp.mean(yg * yg, axis=-1, keepdims=True)
    y_ref[...] = (yg * lax.rsqrt(ms + EPS) * ng_ref[...]).astype(y_ref.dtype)

    @pl.when(c == pl.num_programs(1) - 1)
    def _():
        st_ref[...] = st_scr[...].T


def _ssd_prompt(proj, dt_raw, batch, seq, cw, cb, dtb, alog, dexp, ng):
    lc = SSD_CHUNK
    nc = seq // lc
    full = lambda shape: pl.BlockSpec(shape, lambda b, c: (0,) * len(shape))
    z_blk = (2 * W_LRU) // W_SSD
    xbc_blk = (2 * W_LRU + W_SSD) // SSD_CONV_DIM
    y, st = pl.pallas_call(
        _ssd_prompt_kernel,
        grid=(batch, nc),
        in_specs=[pl.BlockSpec((lc, W_SSD), lambda b, c: (b * nc + c, z_blk)),
                  pl.BlockSpec((lc, SSD_CONV_DIM), lambda b, c: (b * nc + c, xbc_blk)),
                  pl.BlockSpec((lc, LANES), lambda b, c: (b * nc + c, 0)),
                  full((CONV_W, SSD_CONV_DIM)), full((1, SSD_CONV_DIM)),
                  full((1, LANES)), full((1, LANES)), full((1, W_SSD)), full((1, W_SSD))],
        out_specs=[pl.BlockSpec((lc, W_SSD), lambda b, c: (b * nc + c, 0)),
                   pl.BlockSpec((None, W_SSD, SSD_STATE), lambda b, c: (b, 0, 0))],
        out_shape=[jax.ShapeDtypeStruct((batch * seq, W_SSD), BF16),
                   jax.ShapeDtypeStruct((batch, W_SSD, SSD_STATE), F32)],
        scratch_shapes=[pltpu.VMEM((lc + SUBLANES, SSD_CONV_DIM), F32),
                        pltpu.VMEM((SSD_STATE, W_SSD), F32),
                        pltpu.VMEM((lc, W_SSD), F32),
                        pltpu.VMEM((SSD_GROUPS, lc, SSD_HPG * lc), BF16),
                        pltpu.VMEM((SSD_GROUPS, SSD_HPG * lc, SSD_HPG * SSD_HEAD_DIM), BF16)],
        compiler_params=_params(("parallel", "arbitrary")),
        name="ssd_prompt",
    )(proj, proj, dt_raw, cw, cb, dtb, alog, dexp, ng)
    return y, st.reshape(batch, SSD_HEADS, SSD_HEAD_DIM, SSD_STATE)


def _sample_pre_kernel(proj_ref, dt_ref, h0_ref, lconv_ref, sconv_ref,
                       lcw_ref, lcb_ref, wg_ref, ba_ref, bi_ref, lam_ref,
                       scw_ref, scb_ref, dtb_ref, alog_ref,
                       outl_ref, hnew_ref, lconv_new_ref, sconv_new_ref,
                       xs_ref, xdt_ref, bc_ref, dec_ref):
    nb = proj_ref.shape[0]
    xl = proj_ref[:, 0:W_LRU]
    gl = proj_ref[:, W_LRU:2 * W_LRU]
    xbc = proj_ref[:, 2 * W_LRU + W_SSD:IN_MAIN]

    def conv1(state_ref, width, x_new, w_ref, b_ref):
        y = b_ref[...] + w_ref[0:1, :] * state_ref[:, 0:width]
        y = y + w_ref[1:2, :] * state_ref[:, width:2 * width]
        y = y + w_ref[2:3, :] * state_ref[:, 2 * width:3 * width]
        return y + w_ref[3:4, :] * x_new

    xc = conv1(lconv_ref, W_LRU, xl, lcw_ref, lcb_ref)
    a, bt = _lru_gates(xc, wg_ref, ba_ref[...], bi_ref[...], _softplus(-lam_ref[...]))
    h_new = a * h0_ref[...] + bt
    hnew_ref[...] = h_new
    outl_ref[...] = (h_new * _gelu_tanh(gl)).astype(outl_ref.dtype)
    lconv_new_ref[:, 0:2 * W_LRU] = lconv_ref[:, W_LRU:3 * W_LRU]
    lconv_new_ref[:, 2 * W_LRU:3 * W_LRU] = xl

    act = _silu(conv1(sconv_ref, SSD_CONV_DIM, xbc, scw_ref, scb_ref))
    sconv_new_ref[:, 0:2 * SSD_CONV_DIM] = sconv_ref[:, SSD_CONV_DIM:3 * SSD_CONV_DIM]
    sconv_new_ref[:, 2 * SSD_CONV_DIM:3 * SSD_CONV_DIM] = xbc
    xs = act[:, :W_SSD]
    xs_ref[...] = xs
    bc_ref[...] = act[:, W_SSD:]
    dt = _softplus(dt_ref[...] + dtb_ref[...])
    dec = jnp.exp(dt * (-jnp.exp(alog_ref[...])))
    for h in range(SSD_HEADS):
        pcols = slice(h * SSD_HEAD_DIM, (h + 1) * SSD_HEAD_DIM)
        xdt_ref[:, pcols] = xs[:, pcols] * jnp.broadcast_to(dt[:, h:h + 1], (nb, SSD_HEAD_DIM))
        dec_ref[h] = jnp.broadcast_to(dec[:, h:h + 1], (nb, SSD_STATE))


def _sample_pre(proj, dt_raw, h0, lconv, sconv, p):
    nb = proj.shape[0]
    out_shape = [jax.ShapeDtypeStruct((nb, W_LRU), BF16),
                 jax.ShapeDtypeStruct((nb, W_LRU), F32),
                 jax.ShapeDtypeStruct((nb, 3 * W_LRU), F32),
                 jax.ShapeDtypeStruct((nb, 3 * SSD_CONV_DIM), F32),
                 jax.ShapeDtypeStruct((nb, W_SSD), F32),
                 jax.ShapeDtypeStruct((nb, W_SSD), F32),
                 jax.ShapeDtypeStruct((nb, 2 * SSD_GROUPS * SSD_STATE), F32),
                 jax.ShapeDtypeStruct((SSD_HEADS, nb, SSD_STATE), F32)]
    return pl.pallas_call(
        _sample_pre_kernel,
        out_shape=out_shape,
        compiler_params=pltpu.CompilerParams(vmem_limit_bytes=VMEM_LIMIT_BYTES),
        name="sample_pre",
    )(proj, dt_raw, h0, lconv, sconv,
      p["lru_cw"], p["lru_cb"], p["lru_wg"], p["lru_ba"], p["lru_bi"], p["lru_lam"],
      p["ssd_cw"], p["ssd_cb"], p["ssd_dtb"], p["ssd_alog"])


def _sample_state_kernel(s_ref, xdt_ref, bc_ref, dec_ref, o_ref, y_ref):
    bb = s_ref.shape[0]
    half = SSD_HPG * SSD_HEAD_DIM
    rid = lax.broadcasted_iota(jnp.int32, (bb, W_SSD), 0)
    xdt = xdt_ref[...]
    bcb = bc_ref[...].astype(BF16)
    for k in range(bb):
        xk = jnp.where(rid == k, xdt, 0.0).astype(BF16)
        for g in range(SSD_GROUPS):
            rows = slice(g * half, (g + 1) * half)
            b_g = bcb[:, g * SSD_STATE:(g + 1) * SSD_STATE]
            c_g = bcb[:, (SSD_GROUPS + g) * SSD_STATE:(SSD_GROUPS + g + 1) * SSD_STATE]
            outer = lax.dot_general(xk[:, rows], b_g, (((0,), (0,)), ((), ())),
                                    preferred_element_type=F32)
            dec = jnp.concatenate(
                [jnp.broadcast_to(dec_ref[g * SSD_HPG + e, k:k + 1, :], (SSD_HEAD_DIM, SSD_STATE))
                 for e in range(SSD_HPG)], axis=0)
            s_new = dec * s_ref[k, rows, :] + outer
            o_ref[k, rows, :] = s_new
            yk = lax.dot_general(c_g, s_new.astype(BF16), (((1,), (1,)), ((), ())),
                                 preferred_element_type=F32)
            y_ref[k:k + 1, rows] = yk[k:k + 1, :]


def _sample_state(ssm, xdt, bc, dec, bb=8):
    nb = ssm.shape[0]
    return pl.pallas_call(
        _sample_state_kernel,
        grid=(nb // bb,),
        in_specs=[pl.BlockSpec((bb, W_SSD, SSD_STATE), lambda i: (i, 0, 0)),
                  pl.BlockSpec((bb, W_SSD), lambda i: (i, 0)),
                  pl.BlockSpec((bb, 2 * SSD_GROUPS * SSD_STATE), lambda i: (i, 0)),
                  pl.BlockSpec((SSD_HEADS, bb, SSD_STATE), lambda i: (0, i, 0))],
        out_specs=[pl.BlockSpec((bb, W_SSD, SSD_STATE), lambda i: (i, 0, 0)),
                   pl.BlockSpec((bb, W_SSD), lambda i: (i, 0))],
        out_shape=[jax.ShapeDtypeStruct(ssm.shape, F32),
                   jax.ShapeDtypeStruct((nb, W_SSD), F32)],
        compiler_params=_params(("parallel",)),
        name="sample_state",
    )(ssm, xdt, bc, dec)


def _sample_post_kernel(y_ref, xs_ref, proj_ref, dexp_ref, ng_ref, o_ref):
    z = proj_ref[:, 2 * W_LRU:2 * W_LRU + W_SSD]
    yg = (y_ref[...] + dexp_ref[...] * xs_ref[...]) * _silu(z)
    ms = jnp.mean(yg * yg, axis=-1, keepdims=True)
    o_ref[...] = (yg * lax.rsqrt(ms + EPS) * ng_ref[...]).astype(o_ref.dtype)


def _sample_post(y_raw, xs, proj, dexp, ng):
    return pl.pallas_call(
        _sample_post_kernel,
        out_shape=jax.ShapeDtypeStruct(y_raw.shape, BF16),
        compiler_params=pltpu.CompilerParams(vmem_limit_bytes=VMEM_LIMIT_BYTES),
        name="sample_post",
    )(y_raw, xs, proj, dexp, ng)


def _block_diag_groups(w):
    per = LRU_GATE_GROUP // LRU_BLOCK
    w4 = w.reshape(LRU_HEADS // per, per, LRU_BLOCK, LRU_BLOCK)
    bd = jnp.einsum("ghij,hk->ghikj", w4, jnp.eye(per, dtype=w.dtype))
    return bd.reshape(LRU_HEADS // per, LRU_GATE_GROUP, LRU_GATE_GROUP)


def _pad_lanes(v):
    v = v.reshape(1, -1)
    return jnp.pad(v, ((0, 0), (0, LANES - v.shape[1])))


def kernel(x_prompt, x_sample, c_prompt, c_sample, state_lru_h, state_lru_conv, state_ssm, state_ssd_conv, w_ada, b_ada, g_ffn1, w_up1, w_down1, g_mix, w_in, lru_conv_w, lru_conv_b, lru_wa, lru_ba, lru_wi, lru_bi, lru_lambda, ssd_conv_w, ssd_conv_b, ssd_dt_bias, ssd_A_log, ssd_D, ssd_norm_g, w_out, g_ffn2, w_up2, w_down2, w_ada_f, b_ada_f, g_final):
    bp, seq, d = x_prompt.shape
    bs = x_sample.shape[0]
    depth = w_ada.shape[0]
    assert depth == 1 and x_sample.shape[1] == 1 and d == D_MODEL

    pad_rows = (-(bs + bp)) % (2 * SUBLANES)
    c_rows = bs + bp + pad_rows
    c_all = jnp.concatenate([c_sample, c_prompt, jnp.zeros((pad_rows, d), F32)], axis=0)

    def split_rows(mod_all):
        width = mod_all.shape[1]
        return (mod_all[bs:bs + bp].reshape(bp, 1, width),
                mod_all.reshape(1, c_rows, width)[:, :bs])

    w_in_t = jnp.swapaxes(w_in[0], 0, 1)
    w_dt_t = jnp.pad(w_in_t[IN_MAIN:], ((0, LANES - SSD_HEADS), (0, 0)))
    up_blocks = D_FF // 512
    p = {
        "lru_cw": lru_conv_w[0], "lru_cb": lru_conv_b[0].reshape(1, W_LRU),
        "lru_wg": jnp.concatenate([_block_diag_groups(lru_wa[0]), _block_diag_groups(lru_wi[0])],
                                  axis=-1),
        "lru_ba": lru_ba[0].reshape(1, W_LRU), "lru_bi": lru_bi[0].reshape(1, W_LRU),
        "lru_lam": lru_lambda[0].reshape(1, W_LRU),
        "ssd_cw": ssd_conv_w[0], "ssd_cb": ssd_conv_b[0].reshape(1, SSD_CONV_DIM),
        "ssd_dtb": _pad_lanes(ssd_dt_bias[0]), "ssd_alog": _pad_lanes(ssd_A_log[0]),
        "ssd_dexp": jnp.repeat(ssd_D[0], SSD_HEAD_DIM).reshape(1, W_SSD),
        "ssd_ng": ssd_norm_g[0].reshape(1, W_SSD),
    }

    xp = x_prompt.reshape(bp * seq, d)
    xs = x_sample.reshape(bs, d)
    tm = 1024
    up_kw = dict(n_out=D_FF, swiglu=True, out_dtype=BF16)

    mod_a_p, mod_a_s = split_rows(_ada(c_all, w_ada[0], b_ada[0], 2 * d))

    hp, hs = _norm_rows(xp, xs, g_ffn1[0], mod_a_p, mod_a_s, 0, tm=512)
    first, (hmid_s,), wb = _proj_first_tile(
        hp, [(w_up1[0], 0), (w_up1[0], up_blocks)], tm=tm, tn=512, side=hs, **up_kw)
    (hmid,), ((w_down_b,), (mod_b_all,)) = _proj_other_tiles(
        hp, wb, first, tm=tm, tn=512,
        streams=[_cast_stream(w_down1[0], 32),
                 _ada_stream(c_all, w_ada[0], b_ada[0], 2 * d, (N_MOD - 2) * d, 256)],
        **up_kw)
    mod_b_p, mod_b_s = split_rows(mod_b_all)
    (xp, hp), (xs, hs) = _resid([hmid], [hmid_s], [(w_down_b, 0)], xp, xs, mod_b_p, mod_b_s, 0,
                                g_mix[0], mod_b_p, mod_b_s, 1, factor=0.5, tm=256, emit_x=True,
                                h_dtype=BF16)

    in_kw = dict(n_out=IN_MAIN, swiglu=False, out_dtype=F32, trans_w=True, w_extra=w_dt_t)
    first, (proj_s, dt_raw_s), wb = _proj_first_tile(hp, [(w_in_t, 0)], tm=tm, tn=512, side=hs,
                                                     **in_kw)
    lconv = state_lru_conv[0].reshape(bs, (CONV_W - 1) * W_LRU)
    sconv = state_ssd_conv[0].reshape(bs, (CONV_W - 1) * SSD_CONV_DIM)
    out_l_s, lru_h_s, lconv_new, sconv_new, xs_act, xdt, bc, dec = _sample_pre(
        proj_s, dt_raw_s, state_lru_h[0], lconv, sconv, p)
    (proj, dt_raw), ((w_out_b,),) = _proj_other_tiles(
        hp, wb, first, tm=tm, tn=IN_MAIN // 3, streams=[_cast_stream(w_out[0], 16)], **in_kw)
    ssm_s, y_raw = _sample_state(state_ssm[0].reshape(bs, W_SSD, SSD_STATE), xdt, bc, dec)
    y_ssd_s = _sample_post(y_raw, xs_act, proj_s, p["ssd_dexp"], p["ssd_ng"])

    out_l, lru_h_p = _lru_prompt(proj, bp, seq, p["lru_cw"], p["lru_cb"], p["lru_wg"],
                                 p["lru_ba"], p["lru_bi"], p["lru_lam"])
    y_ssd, ssm_p = _ssd_prompt(proj, dt_raw, bp, seq, p["ssd_cw"], p["ssd_cb"], p["ssd_dtb"],
                               p["ssd_alog"], p["ssd_dexp"], p["ssd_ng"])
    proj3 = proj.reshape(bp, seq, IN_MAIN)
    lru_buf_p = proj3[:, seq - (CONV_W - 1):, :W_LRU]
    ssd_buf_p = proj3[:, seq - (CONV_W - 1):, 2 * W_LRU + W_SSD:]

    (xp, hp), (xs, hs) = _resid([out_l, y_ssd], [out_l_s, y_ssd_s], [(w_out_b, 0), (w_out_b, 1)],
                                xp, xs, mod_b_p, mod_b_s, 3, g_ffn2[0], mod_b_p, mod_b_s, 4,
                                factor=1.0, tm=512, emit_x=True, h_dtype=BF16)

    first, (hmid_s,), wb = _proj_first_tile(
        hp, [(w_up2[0], 0), (w_up2[0], up_blocks)], tm=tm, tn=512, side=hs, **up_kw)
    (hmid,), ((w_down_b,), (modf_all,)) = _proj_other_tiles(
        hp, wb, first, tm=tm, tn=512,
        streams=[_cast_stream(w_down2[0], 32),
                 _ada_stream(c_all, w_ada_f, b_ada_f, 0, 2 * d, 256)],
        **up_kw)
    modf_p, modf_s = split_rows(modf_all)
    (yp,), (ys,) = _resid([hmid], [hmid_s], [(w_down_b, 0)], xp, xs, mod_b_p, mod_b_s, 6, g_final,
                          modf_p, modf_s, 0, factor=0.5, tm=256, emit_x=False, h_dtype=F32)

    stack = lambda v: v[None]
    return (yp.reshape(bp, seq, d), ys.reshape(bs, 1, d),
            stack(lru_h_p), stack(lru_buf_p), stack(ssm_p), stack(ssd_buf_p),
            stack(lru_h_s), stack(lconv_new.reshape(bs, CONV_W - 1, W_LRU)),
            stack(ssm_s.reshape(bs, SSD_HEADS, SSD_HEAD_DIM, SSD_STATE)),
            stack(sconv_new.reshape(bs, CONV_W - 1, SSD_CONV_DIM)))
```

```python
import functools
from typing import Callable, NamedTuple

import jax
import jax.numpy as jnp
from jax import lax
from jax.experimental import pallas as pl
from jax.experimental.pallas import tpu as pltpu

F32 = jnp.float32
BF16 = jnp.bfloat16

D_MODEL = 2048
D_FF = 5632
W_LRU = 1024
W_SSD = 1024
LRU_HEADS = 16
LRU_BLOCK = 64
LRU_C = 8.0
SSD_HEADS = 16
SSD_HEAD_DIM = 64
SSD_GROUPS = 2
SSD_HPG = 8
SSD_STATE = 128
SSD_CHUNK = 128
CONV_W = 4
SSD_CONV_DIM = W_SSD + 2 * SSD_GROUPS * SSD_STATE
IN_MAIN = 2 * W_LRU + W_SSD + SSD_CONV_DIM
N_MOD = 9
EPS = 1e-6

LANES = 128
SUBLANES = 8
VMEM_LIMIT_BYTES = 56 * 1024 * 1024

LRU_GATE_GROUP = 256
LRU_TIME_TILE = 256
SCAN_ROWS = 2 * SUBLANES


def _sigmoid(v):
    return 0.5 * (jnp.tanh(0.5 * v) + 1.0)


def _silu(v):
    return v * _sigmoid(v)


def _softplus(v):
    return jnp.maximum(v, 0.0) + jnp.log1p(jnp.exp(-jnp.abs(v)))


def _gelu_tanh(v):
    return 0.5 * v * (1.0 + jnp.tanh(0.7978845608028654 * (v + 0.044715 * (v * v * v))))


def _bdot(a, b):
    return jnp.dot(a, b, preferred_element_type=F32)


def _params(sem):
    return pltpu.CompilerParams(dimension_semantics=sem, vmem_limit_bytes=VMEM_LIMIT_BYTES)


def _ada_kernel(c_ref, w_ref, b_ref, o_ref):
    s = _silu(c_ref[...]).astype(BF16)
    o_ref[...] = _bdot(s, w_ref[...].astype(BF16)) + b_ref[...]


def _ada(c, w, b, cols, tn=1024):
    m, k = c.shape
    n = cols
    return pl.pallas_call(
        _ada_kernel,
        grid=(n // tn,),
        in_specs=[pl.BlockSpec((m, k), lambda j: (0, 0)),
                  pl.BlockSpec((k, tn), lambda j: (0, j)),
                  pl.BlockSpec((1, tn), lambda j: (0, j))],
        out_specs=pl.BlockSpec((m, tn), lambda j: (0, j)),
        out_shape=jax.ShapeDtypeStruct((m, n), F32),
        compiler_params=_params(("parallel",)),
        name="ada_proj",
    )(c, w, b.reshape(1, -1))


def _norm_modulate(x, gain, shift, scale):
    ms = jnp.mean(x * x, axis=-1, keepdims=True)
    y = x * lax.rsqrt(ms + EPS) * gain
    return y * (1.0 + scale) + shift


def _norm_rows_kernel(x_ref, gain_ref, sh_ref, sc_ref, xs_ref, shs_ref, scs_ref, o_ref, os_ref):
    o_ref[...] = _norm_modulate(x_ref[...], gain_ref[...], sh_ref[...],
                                sc_ref[...]).astype(o_ref.dtype)

    @pl.when(pl.program_id(0) == 0)
    def _():
        os_ref[...] = _norm_modulate(xs_ref[...], gain_ref[...], shs_ref[...],
                                     scs_ref[...]).astype(os_ref.dtype)


def _norm_rows(x, x_s, gain, mod, mod_s, shift_chunk, *, tm):
    m, d = x.shape
    ns = x_s.shape[0]
    tiles_per_group = (m // tm) // mod.shape[0]
    once = dict(pipeline_mode=pl.Buffered(1))
    mod_spec = lambda c: pl.BlockSpec((None, 1, d), lambda i: (i // tiles_per_group, 0, c))
    mod_s_spec = lambda c: pl.BlockSpec((None, ns, d), lambda i: (0, 0, c), **once)
    return pl.pallas_call(
        _norm_rows_kernel,
        grid=(m // tm,),
        in_specs=[pl.BlockSpec((tm, d), lambda i: (i, 0)), pl.BlockSpec((1, d), lambda i: (0, 0)),
                  mod_spec(shift_chunk), mod_spec(shift_chunk + 1),
                  pl.BlockSpec((ns, d), lambda i: (0, 0), **once),
                  mod_s_spec(shift_chunk), mod_s_spec(shift_chunk + 1)],
        out_specs=[pl.BlockSpec((tm, d), lambda i: (i, 0)), pl.BlockSpec((ns, d), lambda i: (0, 0))],
        out_shape=[jax.ShapeDtypeStruct((m, d), BF16), jax.ShapeDtypeStruct((ns, d), BF16)],
        compiler_params=_params(("arbitrary",)),
        name="norm_rows",
    )(x, gain.reshape(1, d), mod, mod, x_s, mod_s, mod_s)


class _Stream(NamedTuple):
    ins: list
    outs: list
    body: Callable
    start: int
    steps: int


def _stream_io(streams, n_inner):
    in_specs, args, out_specs, out_shape = [], [], [], []

    def spec(st, block, index_fn):
        return pl.BlockSpec(
            block, lambda i, j: index_fn(jnp.clip(i * n_inner + j - st.start, 0, st.steps - 1)))

    for st in streams:
        for arr, block, index_fn in st.ins:
            in_specs.append(spec(st, block, index_fn))
            args.append(arr)
        for shape, block, index_fn in st.outs:
            out_specs.append(spec(st, block, index_fn))
            out_shape.append(shape)
    return in_specs, args, out_specs, out_shape


def _run_streams(streams, in_refs, out_refs, n_inner):
    step = pl.program_id(0) * n_inner + pl.program_id(1)
    in_refs, out_refs = iter(in_refs), iter(out_refs)
    for st in streams:
        ins = [next(in_refs) for _ in st.ins]
        outs = [next(out_refs) for _ in st.outs]

        @pl.when((step >= st.start) & (step < st.start + st.steps))
        def _(st=st, ins=ins, outs=outs):
            st.body(*ins, *outs)


def _split_stream_outs(streams, flat):
    flat = list(flat)
    return [[flat.pop(0) for _ in st.outs] for st in streams]


def _cast_body(src_ref, dst_ref):
    dst_ref[...] = src_ref[...].astype(dst_ref.dtype)


def _cast_stream(w, chunks, start=0):
    block = (w.shape[0] // chunks, w.shape[1])
    rows = lambda k: (k, 0)
    return _Stream([(w, block, rows)], [(jax.ShapeDtypeStruct(w.shape, BF16), block, rows)],
                   _cast_body, start, chunks)


def _ada_stream(c, w, b, col0, cols, tn, start=0):
    m, k = c.shape
    t0 = col0 // tn
    return _Stream(
        [(c, (m, k), lambda s: (0, 0)), (w, (k, tn), lambda s: (0, s + t0)),
         (b.reshape(1, -1), (1, tn), lambda s: (0, s + t0))],
        [(jax.ShapeDtypeStruct((m, cols), F32), (m, tn), lambda s: (0, s))],
        _ada_kernel, start, cols // tn)


def _wdot(h, w, trans_w):
    if trans_w:
        return lax.dot_general(h, w, (((1,), (1,)), ((), ())), preferred_element_type=F32)
    return _bdot(h, w)


def _proj_kernel(*refs, n_w, swiglu, trans_w, has_extra, has_side, emit_bf16, streams, n_prev,
                 nj):
    it = iter(refs)
    x_ref = next(it)
    xs_ref = next(it) if has_side else None
    w_refs = [next(it) for _ in range(n_w)]
    wx_ref = next(it) if has_extra else None
    stream_ins = [next(it) for st in streams for _ in st.ins]
    for _ in range(n_prev):
        next(it)
    o_ref = next(it)
    ox_ref = next(it) if has_extra else None
    os_ref = next(it) if has_side else None
    osx_ref = next(it) if has_side and has_extra else None
    wo_refs = [next(it) for _ in range(n_w)] if emit_bf16 else []
    stream_outs = [next(it) for st in streams for _ in st.outs]

    j = pl.program_id(1)

    if has_extra:
        @pl.when(j == 0)
        def _():
            wxb = wx_ref[...].astype(BF16)
            ox_ref[...] = _wdot(x_ref[...], wxb, trans_w)
            if has_side:
                osx_ref[...] = _wdot(xs_ref[...], wxb, trans_w)

    wbs = [w_ref[...].astype(BF16) for w_ref in w_refs]
    for wo_ref, wb in zip(wo_refs, wbs):
        wo_ref[...] = wb

    def project(h, out_ref):
        if swiglu:
            g = _wdot(h, wbs[0], trans_w)
            u = _wdot(h, wbs[1], trans_w)
            out_ref[...] = (_silu(g) * u).astype(out_ref.dtype)
        else:
            out_ref[...] = _wdot(h, wbs[0], trans_w).astype(out_ref.dtype)

    project(x_ref[...], o_ref)
    if has_side:
        project(xs_ref[...], os_ref)

    _run_streams(streams, stream_ins, stream_outs, nj)


def _proj(x, ws, *, n_out, tm, tn, swiglu, out_dtype, row_tiles, side=None, trans_w=False,
          w_extra=None, emit_bf16=False, streams=(), prev=None):
    m, d = x.shape
    t0, t1 = row_tiles
    nj = n_out // tn
    has_side = side is not None
    has_extra = w_extra is not None
    single_row_tile = t1 - t0 == 1
    once = dict(pipeline_mode=pl.Buffered(1))

    def w_spec(off):
        if trans_w:
            return pl.BlockSpec((tn, d), lambda i, j: (j + off, 0))
        return pl.BlockSpec((d, tn), lambda i, j: (0, j + off))

    x_mode = once if single_row_tile else {}
    in_specs = [pl.BlockSpec((tm, d), lambda i, j: (i + t0, 0), **x_mode)]
    args = [x]
    if has_side:
        ns = side.shape[0]
        in_specs.append(pl.BlockSpec((ns, d), lambda i, j: (0, 0), **once))
        args.append(side)
    in_specs += [w_spec(off) for _, off in ws]
    args += [w for w, _ in ws]
    if has_extra:
        nx = w_extra.shape[0] if trans_w else w_extra.shape[1]
        in_specs.append(pl.BlockSpec(w_extra.shape, lambda i, j: (0, 0)))
        args.append(w_extra)
    assert all(st.start + st.steps <= (t1 - t0) * nj for st in streams)
    st_in_specs, st_args, st_out_specs, st_out_shape = _stream_io(streams, nj)
    in_specs += st_in_specs
    args += st_args
    prev = list(prev or [])
    aliases = {}
    for k, buf in enumerate(prev):
        aliases[len(args)] = k
        in_specs.append(pl.BlockSpec(memory_space=pl.ANY))
        args.append(buf)

    out_specs = [pl.BlockSpec((tm, tn), lambda i, j: (i + t0, j))]
    out_shape = [jax.ShapeDtypeStruct((m, n_out), out_dtype)]
    if has_extra:
        out_specs.append(pl.BlockSpec((tm, nx), lambda i, j: (i + t0, 0)))
        out_shape.append(jax.ShapeDtypeStruct((m, nx), F32))
    n_main = len(out_shape)
    if has_side:
        out_specs.append(pl.BlockSpec((ns, tn), lambda i, j: (0, j)))
        out_shape.append(jax.ShapeDtypeStruct((ns, n_out), out_dtype))
        if has_extra:
            out_specs.append(pl.BlockSpec((ns, nx), lambda i, j: (0, 0)))
            out_shape.append(jax.ShapeDtypeStruct((ns, nx), F32))
    n_side = len(out_shape) - n_main
    if emit_bf16 or has_side:
        assert single_row_tile, "weight copies / side outputs are written once per column tile"
    if emit_bf16:
        for _ in ws:
            out_specs.append(w_spec(0))
            out_shape.append(jax.ShapeDtypeStruct((n_out, d) if trans_w else (d, n_out), BF16))
    out_specs += st_out_specs
    out_shape += st_out_shape
    outs = pl.pallas_call(
        functools.partial(_proj_kernel, n_w=len(ws), swiglu=swiglu, trans_w=trans_w,
                          has_extra=has_extra, has_side=has_side, emit_bf16=emit_bf16,
                          streams=tuple(streams), n_prev=len(prev), nj=nj),
        grid=(t1 - t0, nj),
        in_specs=in_specs,
        out_specs=out_specs,
        out_shape=out_shape,
        input_output_aliases=aliases,
        compiler_params=_params(("arbitrary" if streams else "parallel", "arbitrary")),
        name="proj_swiglu" if swiglu else "proj",
    )(*args)
    n_wb = len(ws) if emit_bf16 else 0
    main, rest = outs[:n_main], outs[n_main:]
    side_outs, rest = rest[:n_side], rest[n_side:]
    wb, rest = rest[:n_wb], rest[n_wb:]
    return main, side_outs, wb, _split_stream_outs(streams, rest)


def _proj_first_tile(x, ws_f32, *, tm, tn, side, **kw):
    main, side_outs, wb, _ = _proj(x, ws_f32, tm=tm, tn=tn, row_tiles=(0, 1), side=side,
                                   emit_bf16=True, **kw)
    return main, side_outs, wb


def _proj_other_tiles(x, wb, prev, *, tm, tn, streams, **kw):
    main, _, _, stream_outs = _proj(x, [(w, 0) for w in wb], tm=tm, tn=tn,
                                    row_tiles=(1, x.shape[0] // tm), prev=prev, streams=streams,
                                    **kw)
    return main, stream_outs


def _resid_kernel(*refs, n_lhs, factor, emit_x):
    it = iter(refs)
    lhs_refs = [next(it) for _ in range(n_lhs)]
    lhs_s_refs = [next(it) for _ in range(n_lhs)]
    w_refs = [next(it) for _ in range(n_lhs)]
    x_ref, gate_ref, gain_ref, sh_ref, sc_ref = (next(it) for _ in range(5))
    xs_ref, gate_s_ref, sh_s_ref, sc_s_ref = (next(it) for _ in range(4))
    n_out = 2 if emit_x else 1
    outs = [next(it) for _ in range(n_out)]
    outs_s = [next(it) for _ in range(n_out)]

    def update(lhs, x_in, gate, sh, sc, out_refs):
        acc = _bdot(lhs[0][...], w_refs[0][...])
        for l_ref, w_ref in zip(lhs[1:], w_refs[1:]):
            acc = acc + _bdot(l_ref[...], w_ref[...])
        x_new = x_in[...] + (factor * gate[...]) * acc
        if emit_x:
            out_refs[0][...] = x_new
        h_ref = out_refs[-1]
        h_ref[...] = _norm_modulate(x_new, gain_ref[...], sh[...], sc[...]).astype(h_ref.dtype)

    update(lhs_refs, x_ref, gate_ref, sh_ref, sc_ref, outs)

    @pl.when(pl.program_id(0) == 0)
    def _():
        update(lhs_s_refs, xs_ref, gate_s_ref, sh_s_ref, sc_s_ref, outs_s)


def _resid(lhs_list, lhs_s_list, ws, x, x_s, mod, mod_s, gate_chunk, gain_next, mod_next,
           mod_next_s, shift_chunk_next, *, factor, tm, emit_x, h_dtype):
    m, d = x.shape
    ns = x_s.shape[0]
    groups = mod.shape[0]
    tiles_per_group = (m // tm) // groups
    kp = lhs_list[0].shape[1]
    once = dict(pipeline_mode=pl.Buffered(1))

    def mod_spec(chunk):
        return pl.BlockSpec((None, 1, d), lambda i: (i // tiles_per_group, 0, chunk))

    def mod_s_spec(chunk):
        return pl.BlockSpec((None, ns, d), lambda i: (0, 0, chunk), **once)

    in_specs = [pl.BlockSpec((tm, kp), lambda i: (i, 0)) for _ in lhs_list]
    in_specs += [pl.BlockSpec((ns, kp), lambda i: (0, 0), **once) for _ in lhs_s_list]
    in_specs += [pl.BlockSpec((kp, d), lambda i, k=k: (k, 0), **once) for _, k in ws]
    in_specs += [pl.BlockSpec((tm, d), lambda i: (i, 0)), mod_spec(gate_chunk),
                 pl.BlockSpec((1, d), lambda i: (0, 0)),
                 mod_spec(shift_chunk_next), mod_spec(shift_chunk_next + 1),
                 pl.BlockSpec((ns, d), lambda i: (0, 0), **once), mod_s_spec(gate_chunk),
                 mod_s_spec(shift_chunk_next), mod_s_spec(shift_chunk_next + 1)]
    row = pl.BlockSpec((tm, d), lambda i: (i, 0))
    row_s = pl.BlockSpec((ns, d), lambda i: (0, 0))
    dtypes = ([F32] if emit_x else []) + [h_dtype]
    out_specs = [row for _ in dtypes] + [row_s for _ in dtypes]
    out_shape = ([jax.ShapeDtypeStruct((m, d), t) for t in dtypes]
                 + [jax.ShapeDtypeStruct((ns, d), t) for t in dtypes])
    outs = pl.pallas_call(
        functools.partial(_resid_kernel, n_lhs=len(lhs_list), factor=factor, emit_x=emit_x),
        grid=(m // tm,),
        in_specs=in_specs,
        out_specs=out_specs,
        out_shape=out_shape,
        compiler_params=_params(("arbitrary",)),
        name="resid",
    )(*lhs_list, *lhs_s_list, *[w for w, _ in ws], x, mod, gain_next.reshape(1, d), mod_next,
      mod_next, x_s, mod_s, mod_next_s, mod_next_s)
    return outs[:len(dtypes)], outs[len(dtypes):]


def _lru_gates(xc, wg_ref, ba, bi, sp):
    a_parts, b_parts = [], []
    for g in range(W_LRU // LRU_GATE_GROUP):
        cols = slice(g * LRU_GATE_GROUP, (g + 1) * LRU_GATE_GROUP)
        xg = xc[:, cols]
        ri = _bdot(xg.astype(BF16), wg_ref[g].astype(BF16))
        r = _sigmoid(ri[:, :LRU_GATE_GROUP] + ba[:, cols])
        i = _sigmoid(ri[:, LRU_GATE_GROUP:] + bi[:, cols])
        log_a = (-LRU_C * r) * sp[:, cols]
        a = jnp.exp(log_a)
        a_parts.append(a)
        b_parts.append(jnp.sqrt(1.0 - a * a) * (i * xg))
    return jnp.concatenate(a_parts, axis=1), jnp.concatenate(b_parts, axis=1)


def _causal_conv_from_buf(buf_ref, x, w_ref, b_ref, rows):
    xe = buf_ref[0:SUBLANES + rows, :]
    shifted = lambda k: pltpu.roll(xe, k, 0)[SUBLANES:, :]
    y = b_ref[...] + w_ref[0:1, :] * shifted(3)
    y = y + w_ref[1:2, :] * shifted(2)
    y = y + w_ref[2:3, :] * shifted(1)
    return y + w_ref[3:4, :] * x


def _lru_prompt_kernel(*refs, streams):
    n_in = sum(len(st.ins) for st in streams)
    n_out = sum(len(st.outs) for st in streams)
    xl_ref, gl_ref, cw_ref, cb_ref, wg_ref, ba_ref, bi_ref, lam_ref = refs[:8]
    stream_ins = refs[8:8 + n_in]
    o_ref, hT_ref = refs[8 + n_in:10 + n_in]
    stream_outs = refs[10 + n_in:10 + n_in + n_out]
    xbuf, a_scr, b_scr, hcar = refs[10 + n_in + n_out:]
    t = pl.program_id(1)
    tt = xl_ref.shape[0]

    @pl.when(t == 0)
    def _():
        xbuf[0:SUBLANES, :] = jnp.zeros((SUBLANES, W_LRU), F32)
        hcar[...] = jnp.zeros_like(hcar)

    x = xl_ref[...]
    xbuf[SUBLANES:SUBLANES + tt, :] = x
    xc = _causal_conv_from_buf(xbuf, x, cw_ref, cb_ref, tt)
    xbuf[0:SUBLANES, :] = x[tt - SUBLANES:, :]

    sp = _softplus(-lam_ref[...])
    a, bt = _lru_gates(xc, wg_ref, ba_ref[...], bi_ref[...], sp)
    a_scr[...] = a
    b_scr[...] = bt

    rid = lax.broadcasted_iota(jnp.int32, (SUBLANES, W_LRU), 0)

    def scan8(a8, b8, h_in):
        for s in (1, 2, 4):
            a_sh = pltpu.roll(a8, s, 0)
            b_sh = pltpu.roll(b8, s, 0)
            m = rid >= s
            b8 = jnp.where(m, a8 * b_sh + b8, b8)
            a8 = jnp.where(m, a8 * a_sh, a8)
        h8 = a8 * h_in + b8
        return h8, jnp.broadcast_to(h8[SUBLANES - 1:SUBLANES, :], (SUBLANES, W_LRU))

    def body(g, h_in):
        r0 = pl.multiple_of(g * SCAN_ROWS, SCAN_ROWS)
        lo = pl.ds(r0, SUBLANES)
        hi = pl.ds(r0 + SUBLANES, SUBLANES)
        h_lo, h_mid = scan8(a_scr[lo, :], b_scr[lo, :], h_in)
        h_hi, h_out = scan8(a_scr[hi, :], b_scr[hi, :], h_mid)
        rows = pl.ds(r0, SCAN_ROWS)
        h16 = jnp.concatenate([h_lo, h_hi], axis=0)
        o_ref[rows, :] = (h16 * _gelu_tanh(gl_ref[rows, :])).astype(o_ref.dtype)
        return h_out

    h_last = lax.fori_loop(0, tt // SCAN_ROWS, body, hcar[...])
    hcar[...] = h_last

    @pl.when(t == pl.num_programs(1) - 1)
    def _():
        hT_ref[...] = h_last[0:1, :]

    _run_streams(streams, stream_ins, stream_outs, pl.num_programs(1))


def _lru_prompt(proj, batch, seq, cw, cb, wg, ba, bi, lam, streams=()):
    tt = LRU_TIME_TILE
    nt = seq // tt
    assert all(st.start + st.steps <= batch * nt for st in streams)
    st_in_specs, st_args, st_out_specs, st_out_shape = _stream_io(streams, nt)
    row = lambda v: v.reshape(1, W_LRU)
    full = lambda shape: pl.BlockSpec(shape, lambda b, t: (0,) * len(shape))
    out, h_t, *rest = pl.pallas_call(
        functools.partial(_lru_prompt_kernel, streams=tuple(streams)),
        grid=(batch, nt),
        in_specs=[pl.BlockSpec((tt, W_LRU), lambda b, t: (b * nt + t, 0)),
                  pl.BlockSpec((tt, W_LRU), lambda b, t: (b * nt + t, 1)),
                  full((CONV_W, W_LRU)), full((1, W_LRU)), full(wg.shape),
                  full((1, W_LRU)), full((1, W_LRU)), full((1, W_LRU))] + st_in_specs,
        out_specs=[pl.BlockSpec((tt, W_LRU), lambda b, t: (b * nt + t, 0)),
                   pl.BlockSpec((None, 1, W_LRU), lambda b, t: (b, 0, 0))] + st_out_specs,
        out_shape=[jax.ShapeDtypeStruct((batch * seq, W_LRU), BF16),
                   jax.ShapeDtypeStruct((batch, 1, W_LRU), F32)] + st_out_shape,
        scratch_shapes=[pltpu.VMEM((tt + SUBLANES, W_LRU), F32),
                        pltpu.VMEM((tt, W_LRU), F32),
                        pltpu.VMEM((tt, W_LRU), F32),
                        pltpu.VMEM((SUBLANES, W_LRU), F32)],
        compiler_params=_params(("arbitrary" if streams else "parallel", "arbitrary")),
        name="lru_prompt",
    )(proj, proj, cw, row(cb), wg, row(ba), row(bi), row(lam), *st_args)
    return out, h_t.reshape(batch, W_LRU), _split_stream_outs(streams, rest)


def _ssd_prompt_kernel(z_ref, xbc_ref, dt_ref, cw_ref, cb_ref, dtb_ref, alog_ref, dexp_ref,
                       ng_ref, y_ref, st_ref, xbuf, st_scr, y_scr, m_scr, xbd_scr):
    c = pl.program_id(1)
    lc = SSD_CHUNK

    @pl.when(c == 0)
    def _():
        xbuf[0:SUBLANES, :] = jnp.zeros((SUBLANES, SSD_CONV_DIM), F32)
        st_scr[...] = jnp.zeros_like(st_scr)
        xbd_scr[...] = jnp.zeros_like(xbd_scr)

    x = xbc_ref[...]
    xbuf[SUBLANES:SUBLANES + lc, :] = x
    act = _silu(_causal_conv_from_buf(xbuf, x, cw_ref, cb_ref, lc))
    xbuf[0:SUBLANES, :] = x[lc - SUBLANES:, :]
    xs = act[:, :W_SSD]
    bm = act[:, W_SSD:W_SSD + SSD_GROUPS * SSD_STATE]
    cm = act[:, W_SSD + SSD_GROUPS * SSD_STATE:]

    dt = _softplus(dt_ref[...] + dtb_ref[...])
    d_a = dt * (-jnp.exp(alog_ref[...]))
    row_i = lax.broadcasted_iota(jnp.int32, (lc, lc), 0)
    col_i = lax.broadcasted_iota(jnp.int32, (lc, lc), 1)
    causal = row_i >= col_i
    tril = jnp.where(causal, 1.0, 0.0).astype(F32)
    cs = jnp.dot(tril, d_a, preferred_element_type=F32, precision=lax.Precision.HIGHEST)
    cs_t = cs.T
    dt_t = dt.T
    cs_last = cs[lc - 1:lc, :]

    def per_head_lanes(v):
        rows = v.shape[0]
        return jnp.concatenate(
            [jnp.broadcast_to(v[:, h:h + 1], (rows, SSD_HEAD_DIM)) for h in range(SSD_HEADS)],
            axis=1)

    w_exp = per_head_lanes(jnp.exp(cs_last - cs) * dt)
    ecs_exp = per_head_lanes(jnp.exp(cs))
    cd_exp = per_head_lanes(jnp.exp(cs_last))
    gw = SSD_HPG * SSD_HEAD_DIM
    low_half = col_i < SSD_HEAD_DIM

    for g in range(SSD_GROUPS):
        ncols = slice(g * SSD_STATE, (g + 1) * SSD_STATE)
        gcols = slice(g * gw, (g + 1) * gw)
        b_g = bm[:, ncols].astype(BF16)
        c_g = cm[:, ncols].astype(BF16)
        cb_mat = lax.dot_general(c_g, b_g, (((1,), (1,)), ((), ())),
                                 preferred_element_type=F32)
        for e in range(SSD_HPG):
            h = g * SSD_HPG + e
            cs_col = jnp.broadcast_to(cs[:, h:h + 1], (lc, lc))
            l_mat = jnp.exp(jnp.where(causal, cs_col - cs_t[h:h + 1, :], -jnp.inf))
            m_scr[g, :, e * lc:(e + 1) * lc] = (cb_mat * l_mat * dt_t[h:h + 1, :]).astype(BF16)
        for q in range(SSD_HPG // 2):
            lanes = slice(q * LANES, (q + 1) * LANES)
            slab = xs[:, g * gw + q * LANES:g * gw + (q + 1) * LANES]
            xbd_scr[g, (2 * q) * lc:(2 * q + 1) * lc, lanes] = jnp.where(
                low_half, slab, 0.0).astype(BF16)
            xbd_scr[g, (2 * q + 1) * lc:(2 * q + 2) * lc, lanes] = jnp.where(
                low_half, 0.0, slab).astype(BF16)
        st_g = st_scr[:, gcols]
        y_off = _bdot(c_g, st_g.astype(BF16)) * ecs_exp[:, gcols]
        y_scr[:, gcols] = (_bdot(m_scr[g], xbd_scr[g]) + y_off
                           + dexp_ref[:, gcols] * xs[:, gcols])
        xw = (xs[:, gcols] * w_exp[:, gcols]).astype(BF16)
        st_scr[:, gcols] = cd_exp[:, gcols] * st_g + lax.dot_general(
            b_g, xw, (((0,), (0,)), ((), ())), preferred_element_type=F32)

    yg = y_scr[...] * _silu(z_ref[...])
    ms = jnp.mean(yg * yg, axis=-1, keepdims=True)
    y_ref[...] = (yg * lax.rsqrt(ms + EPS) * ng_ref[...]).astype(y_ref.dtype)

    @pl.when(c == pl.num_programs(1) - 1)
    def _():
        st_ref[...] = st_scr[...].T


def _ssd_prompt(proj, dt_raw, batch, seq, cw, cb, dtb, alog, dexp, ng):
    lc = SSD_CHUNK
    nc = seq // lc
    full = lambda shape: pl.BlockSpec(shape, lambda b, c: (0,) * len(shape))
    z_blk = (2 * W_LRU) // W_SSD
    xbc_blk = (2 * W_LRU + W_SSD) // SSD_CONV_DIM
    y, st = pl.pallas_call(
        _ssd_prompt_kernel,
        grid=(batch, nc),
        in_specs=[pl.BlockSpec((lc, W_SSD), lambda b, c: (b * nc + c, z_blk)),
                  pl.BlockSpec((lc, SSD_CONV_DIM), lambda b, c: (b * nc + c, xbc_blk)),
                  pl.BlockSpec((lc, LANES), lambda b, c: (b * nc + c, 0)),
                  full((CONV_W, SSD_CONV_DIM)), full((1, SSD_CONV_DIM)),
                  full((1, LANES)), full((1, LANES)), full((1, W_SSD)), full((1, W_SSD))],
        out_specs=[pl.BlockSpec((lc, W_SSD), lambda b, c: (b * nc + c, 0)),
                   pl.BlockSpec((None, W_SSD, SSD_STATE), lambda b, c: (b, 0, 0))],
        out_shape=[jax.ShapeDtypeStruct((batch * seq, W_SSD), BF16),
                   jax.ShapeDtypeStruct((batch, W_SSD, SSD_STATE), F32)],
        scratch_shapes=[pltpu.VMEM((lc + SUBLANES, SSD_CONV_DIM), F32),
                        pltpu.VMEM((SSD_STATE, W_SSD), F32),
                        pltpu.VMEM((lc, W_SSD), F32),
                        pltpu.VMEM((SSD_GROUPS, lc, SSD_HPG * lc), BF16),
                        pltpu.VMEM((SSD_GROUPS, SSD_HPG * lc, SSD_HPG * SSD_HEAD_DIM), BF16)],
        compiler_params=_params(("parallel", "arbitrary")),
        name="ssd_prompt",
    )(proj, proj, dt_raw, cw, cb, dtb, alog, dexp, ng)
    return y, st.reshape(batch, SSD_HEADS, SSD_HEAD_DIM, SSD_STATE)


def _sample_pre_kernel(proj_ref, dt_ref, h0_ref, lconv_ref, sconv_ref,
                       lcw_ref, lcb_ref, wg_ref, ba_ref, bi_ref, lam_ref,
                       scw_ref, scb_ref, dtb_ref, alog_ref,
                       outl_ref, hnew_ref, lconv_new_ref, sconv_new_ref,
                       xs_ref, xdt_ref, bc_ref, dec_ref):
    nb = proj_ref.shape[0]
    xl = proj_ref[:, 0:W_LRU]
    gl = proj_ref[:, W_LRU:2 * W_LRU]
    xbc = proj_ref[:, 2 * W_LRU + W_SSD:IN_MAIN]

    def conv1(state_ref, width, x_new, w_ref, b_ref):
        y = b_ref[...] + w_ref[0:1, :] * state_ref[:, 0:width]
        y = y + w_ref[1:2, :] * state_ref[:, width:2 * width]
        y = y + w_ref[2:3, :] * state_ref[:, 2 * width:3 * width]
        return y + w_ref[3:4, :] * x_new

    xc = conv1(lconv_ref, W_LRU, xl, lcw_ref, lcb_ref)
    a, bt = _lru_gates(xc, wg_ref, ba_ref[...], bi_ref[...], _softplus(-lam_ref[...]))
    h_new = a * h0_ref[...] + bt
    hnew_ref[...] = h_new
    outl_ref[...] = (h_new * _gelu_tanh(gl)).astype(outl_ref.dtype)
    lconv_new_ref[:, 0:2 * W_LRU] = lconv_ref[:, W_LRU:3 * W_LRU]
    lconv_new_ref[:, 2 * W_LRU:3 * W_LRU] = xl

    act = _silu(conv1(sconv_ref, SSD_CONV_DIM, xbc, scw_ref, scb_ref))
    sconv_new_ref[:, 0:2 * SSD_CONV_DIM] = sconv_ref[:, SSD_CONV_DIM:3 * SSD_CONV_DIM]
    sconv_new_ref[:, 2 * SSD_CONV_DIM:3 * SSD_CONV_DIM] = xbc
    xs = act[:, :W_SSD]
    xs_ref[...] = xs
    bc_ref[...] = act[:, W_SSD:]
    dt = _softplus(dt_ref[...] + dtb_ref[...])
    dec = jnp.exp(dt * (-jnp.exp(alog_ref[...])))
    for h in range(SSD_HEADS):
        pcols = slice(h * SSD_HEAD_DIM, (h + 1) * SSD_HEAD_DIM)
        xdt_ref[:, pcols] = xs[:, pcols] * jnp.broadcast_to(dt[:, h:h + 1], (nb, SSD_HEAD_DIM))
        dec_ref[h] = jnp.broadcast_to(dec[:, h:h + 1], (nb, SSD_STATE))


def _sample_pre(proj, dt_raw, h0, lconv, sconv, p):
    nb = proj.shape[0]
    out_shape = [jax.ShapeDtypeStruct((nb, W_LRU), BF16),
                 jax.ShapeDtypeStruct((nb, W_LRU), F32),
                 jax.ShapeDtypeStruct((nb, 3 * W_LRU), F32),
                 jax.ShapeDtypeStruct((nb, 3 * SSD_CONV_DIM), F32),
                 jax.ShapeDtypeStruct((nb, W_SSD), F32),
                 jax.ShapeDtypeStruct((nb, W_SSD), F32),
                 jax.ShapeDtypeStruct((nb, 2 * SSD_GROUPS * SSD_STATE), F32),
                 jax.ShapeDtypeStruct((SSD_HEADS, nb, SSD_STATE), F32)]
    return pl.pallas_call(
        _sample_pre_kernel,
        out_shape=out_shape,
        compiler_params=pltpu.CompilerParams(vmem_limit_bytes=VMEM_LIMIT_BYTES),
        name="sample_pre",
    )(proj, dt_raw, h0, lconv, sconv,
      p["lru_cw"], p["lru_cb"], p["lru_wg"], p["lru_ba"], p["lru_bi"], p["lru_lam"],
      p["ssd_cw"], p["ssd_cb"], p["ssd_dtb"], p["ssd_alog"])


def _sample_state_kernel(s_ref, xdt_ref, bc_ref, dec_ref, o_ref, y_ref):
    bb = s_ref.shape[0]
    half = SSD_HPG * SSD_HEAD_DIM
    rid = lax.broadcasted_iota(jnp.int32, (bb, W_SSD), 0)
    xdt = xdt_ref[...]
    bcb = bc_ref[...].astype(BF16)
    for k in range(bb):
        xk = jnp.where(rid == k, xdt, 0.0).astype(BF16)
        for g in range(SSD_GROUPS):
            rows = slice(g * half, (g + 1) * half)
            b_g = bcb[:, g * SSD_STATE:(g + 1) * SSD_STATE]
            c_g = bcb[:, (SSD_GROUPS + g) * SSD_STATE:(SSD_GROUPS + g + 1) * SSD_STATE]
            outer = lax.dot_general(xk[:, rows], b_g, (((0,), (0,)), ((), ())),
                                    preferred_element_type=F32)
            dec = jnp.concatenate(
                [jnp.broadcast_to(dec_ref[g * SSD_HPG + e, k:k + 1, :], (SSD_HEAD_DIM, SSD_STATE))
                 for e in range(SSD_HPG)], axis=0)
            s_new = dec * s_ref[k, rows, :] + outer
            o_ref[k, rows, :] = s_new
            yk = lax.dot_general(c_g, s_new.astype(BF16), (((1,), (1,)), ((), ())),
                                 preferred_element_type=F32)
            y_ref[k:k + 1, rows] = yk[k:k + 1, :]


def _state_stream(ssm, xdt, bc, dec, bb=8, start=0):
    nb = ssm.shape[0]
    state_block = (bb, W_SSD, SSD_STATE)
    return _Stream(
        [(ssm, state_block, lambda k: (k, 0, 0)),
         (xdt, (bb, W_SSD), lambda k: (k, 0)),
         (bc, (bb, 2 * SSD_GROUPS * SSD_STATE), lambda k: (k, 0)),
         (dec, (SSD_HEADS, bb, SSD_STATE), lambda k: (0, k, 0))],
        [(jax.ShapeDtypeStruct(ssm.shape, F32), state_block, lambda k: (k, 0, 0)),
         (jax.ShapeDtypeStruct((nb, W_SSD), F32), (bb, W_SSD), lambda k: (k, 0))],
        _sample_state_kernel, start, nb // bb)


def _sample_post_kernel(y_ref, xs_ref, proj_ref, dexp_ref, ng_ref, o_ref):
    z = proj_ref[:, 2 * W_LRU:2 * W_LRU + W_SSD]
    yg = (y_ref[...] + dexp_ref[...] * xs_ref[...]) * _silu(z)
    ms = jnp.mean(yg * yg, axis=-1, keepdims=True)
    o_ref[...] = (yg * lax.rsqrt(ms + EPS) * ng_ref[...]).astype(o_ref.dtype)


def _sample_post(y_raw, xs, proj, dexp, ng):
    return pl.pallas_call(
        _sample_post_kernel,
        out_shape=jax.ShapeDtypeStruct(y_raw.shape, BF16),
        compiler_params=pltpu.CompilerParams(vmem_limit_bytes=VMEM_LIMIT_BYTES),
        name="sample_post",
    )(y_raw, xs, proj, dexp, ng)


def _block_diag_groups(w):
    per = LRU_GATE_GROUP // LRU_BLOCK
    w4 = w.reshape(LRU_HEADS // per, per, LRU_BLOCK, LRU_BLOCK)
    bd = jnp.einsum("ghij,hk->ghikj", w4, jnp.eye(per, dtype=w.dtype))
    return bd.reshape(LRU_HEADS // per, LRU_GATE_GROUP, LRU_GATE_GROUP)


def _pad_lanes(v):
    v = v.reshape(1, -1)
    return jnp.pad(v, ((0, 0), (0, LANES - v.shape[1])))


def kernel(x_prompt, x_sample, c_prompt, c_sample, state_lru_h, state_lru_conv, state_ssm, state_ssd_conv, w_ada, b_ada, g_ffn1, w_up1, w_down1, g_mix, w_in, lru_conv_w, lru_conv_b, lru_wa, lru_ba, lru_wi, lru_bi, lru_lambda, ssd_conv_w, ssd_conv_b, ssd_dt_bias, ssd_A_log, ssd_D, ssd_norm_g, w_out, g_ffn2, w_up2, w_down2, w_ada_f, b_ada_f, g_final):
    bp, seq, d = x_prompt.shape
    bs = x_sample.shape[0]
    depth = w_ada.shape[0]
    assert depth == 1 and x_sample.shape[1] == 1 and d == D_MODEL

    pad_rows = (-(bs + bp)) % (2 * SUBLANES)
    c_rows = bs + bp + pad_rows
    c_all = jnp.concatenate([c_sample, c_prompt, jnp.zeros((pad_rows, d), F32)], axis=0)

    def split_rows(mod_all):
        width = mod_all.shape[1]
        return (mod_all[bs:bs + bp].reshape(bp, 1, width),
                mod_all.reshape(1, c_rows, width)[:, :bs])

    w_in_t = jnp.swapaxes(w_in[0], 0, 1)
    w_dt_t = jnp.pad(w_in_t[IN_MAIN:], ((0, LANES - SSD_HEADS), (0, 0)))
    up_blocks = D_FF // 512
    p = {
        "lru_cw": lru_conv_w[0], "lru_cb": lru_conv_b[0].reshape(1, W_LRU),
        "lru_wg": jnp.concatenate([_block_diag_groups(lru_wa[0]), _block_diag_groups(lru_wi[0])],
                                  axis=-1),
        "lru_ba": lru_ba[0].reshape(1, W_LRU), "lru_bi": lru_bi[0].reshape(1, W_LRU),
        "lru_lam": lru_lambda[0].reshape(1, W_LRU),
        "ssd_cw": ssd_conv_w[0], "ssd_cb": ssd_conv_b[0].reshape(1, SSD_CONV_DIM),
        "ssd_dtb": _pad_lanes(ssd_dt_bias[0]), "ssd_alog": _pad_lanes(ssd_A_log[0]),
        "ssd_dexp": jnp.repeat(ssd_D[0], SSD_HEAD_DIM).reshape(1, W_SSD),
        "ssd_ng": ssd_norm_g[0].reshape(1, W_SSD),
    }

    xp = x_prompt.reshape(bp * seq, d)
    xs = x_sample.reshape(bs, d)
    tm = 1024
    up_kw = dict(n_out=D_FF, swiglu=True, out_dtype=BF16)

    mod_a_p, mod_a_s = split_rows(_ada(c_all, w_ada[0], b_ada[0], 2 * d))

    hp, hs = _norm_rows(xp, xs, g_ffn1[0], mod_a_p, mod_a_s, 0, tm=512)
    first, (hmid_s,), wb = _proj_first_tile(
        hp, [(w_up1[0], 0), (w_up1[0], up_blocks)], tm=tm, tn=512, side=hs, **up_kw)
    (hmid,), ((w_down_b,), (mod_b_all,)) = _proj_other_tiles(
        hp, wb, first, tm=tm, tn=512,
        streams=[_cast_stream(w_down1[0], 32),
                 _ada_stream(c_all, w_ada[0], b_ada[0], 2 * d, (N_MOD - 2) * d, 256)],
        **up_kw)
    mod_b_p, mod_b_s = split_rows(mod_b_all)
    (xp, hp), (xs, hs) = _resid([hmid], [hmid_s], [(w_down_b, 0)], xp, xs, mod_b_p, mod_b_s, 0,
                                g_mix[0], mod_b_p, mod_b_s, 1, factor=0.5, tm=256, emit_x=True,
                                h_dtype=BF16)

    in_kw = dict(n_out=IN_MAIN, swiglu=False, out_dtype=F32, trans_w=True, w_extra=w_dt_t)
    first, (proj_s, dt_raw_s), wb = _proj_first_tile(hp, [(w_in_t, 0)], tm=tm, tn=512, side=hs,
                                                     **in_kw)
    lconv = state_lru_conv[0].reshape(bs, (CONV_W - 1) * W_LRU)
    sconv = state_ssd_conv[0].reshape(bs, (CONV_W - 1) * SSD_CONV_DIM)
    out_l_s, lru_h_s, lconv_new, sconv_new, xs_act, xdt, bc, dec = _sample_pre(
        proj_s, dt_raw_s, state_lru_h[0], lconv, sconv, p)
    (proj, dt_raw), ((w_out_b,),) = _proj_other_tiles(
        hp, wb, first, tm=tm, tn=IN_MAIN // 3, streams=[_cast_stream(w_out[0], 16)], **in_kw)
    out_l, lru_h_p, ((ssm_s, y_raw),) = _lru_prompt(
        proj, bp, seq, p["lru_cw"], p["lru_cb"], p["lru_wg"], p["lru_ba"], p["lru_bi"],
        p["lru_lam"],
        streams=[_state_stream(state_ssm[0].reshape(bs, W_SSD, SSD_STATE), xdt, bc, dec)])
    y_ssd_s = _sample_post(y_raw, xs_act, proj_s, p["ssd_dexp"], p["ssd_ng"])
    y_ssd, ssm_p = _ssd_prompt(proj, dt_raw, bp, seq, p["ssd_cw"], p["ssd_cb"], p["ssd_dtb"],
                               p["ssd_alog"], p["ssd_dexp"], p["ssd_ng"])
    proj3 = proj.reshape(bp, seq, IN_MAIN)
    lru_buf_p = proj3[:, seq - (CONV_W - 1):, :W_LRU]
    ssd_buf_p = proj3[:, seq - (CONV_W - 1):, 2 * W_LRU + W_SSD:]

    (xp, hp), (xs, hs) = _resid([out_l, y_ssd], [out_l_s, y_ssd_s], [(w_out_b, 0), (w_out_b, 1)],
                                xp, xs, mod_b_p, mod_b_s, 3, g_ffn2[0], mod_b_p, mod_b_s, 4,
                                factor=1.0, tm=512, emit_x=True, h_dtype=BF16)

    first, (hmid_s,), wb = _proj_first_tile(
        hp, [(w_up2[0], 0), (w_up2[0], up_blocks)], tm=tm, tn=512, side=hs, **up_kw)
    (hmid,), ((w_down_b,), (modf_all,)) = _proj_other_tiles(
        hp, wb, first, tm=tm, tn=512,
        streams=[_cast_stream(w_down2[0], 32),
                 _ada_stream(c_all, w_ada_f, b_ada_f, 0, 2 * d, 256)],
        **up_kw)
    modf_p, modf_s = split_rows(modf_all)
    (yp,), (ys,) = _resid([hmid], [hmid_s], [(w_down_b, 0)], xp, xs, mod_b_p, mod_b_s, 6, g_final,
                          modf_p, modf_s, 0, factor=0.5, tm=256, emit_x=False, h_dtype=F32)

    stack = lambda v: v[None]
    return (yp.reshape(bp, seq, d), ys.reshape(bs, 1, d),
            stack(lru_h_p), stack(lru_buf_p), stack(ssm_p), stack(ssd_buf_p),
            stack(lru_h_s), stack(lconv_new.reshape(bs, CONV_W - 1, W_LRU)),
            stack(ssm_s.reshape(bs, SSD_HEADS, SSD_HEAD_DIM, SSD_STATE)),
            stack(sconv_new.reshape(bs, CONV_W - 1, SSD_CONV_DIM)))
```

```python
import functools
from typing import Callable, NamedTuple

import jax
import jax.numpy as jnp
from jax import lax
from jax.experimental import pallas as pl
from jax.experimental.pallas import tpu as pltpu

F32 = jnp.float32
BF16 = jnp.bfloat16

D_MODEL = 2048
D_FF = 5632
W_LRU = 1024
W_SSD = 1024
LRU_HEADS = 16
LRU_BLOCK = 64
LRU_C = 8.0
SSD_HEADS = 16
SSD_HEAD_DIM = 64
SSD_GROUPS = 2
SSD_HPG = 8
SSD_STATE = 128
SSD_CHUNK = 128
CONV_W = 4
SSD_CONV_DIM = W_SSD + 2 * SSD_GROUPS * SSD_STATE
IN_MAIN = 2 * W_LRU + W_SSD + SSD_CONV_DIM
N_MOD = 9
EPS = 1e-6

LANES = 128
SUBLANES = 8
VMEM_LIMIT_BYTES = 56 * 1024 * 1024

LRU_GATE_GROUP = 256
LRU_TIME_TILE = 256
SCAN_ROWS = 2 * SUBLANES


def _sigmoid(v):
    return 0.5 * (jnp.tanh(0.5 * v) + 1.0)


def _silu(v):
    return v * _sigmoid(v)


def _softplus(v):
    return jnp.maximum(v, 0.0) + jnp.log1p(jnp.exp(-jnp.abs(v)))


def _gelu_tanh(v):
    return 0.5 * v * (1.0 + jnp.tanh(0.7978845608028654 * (v + 0.044715 * (v * v * v))))


def _bdot(a, b):
    return jnp.dot(a, b, preferred_element_type=F32)


def _params(sem):
    return pltpu.CompilerParams(dimension_semantics=sem, vmem_limit_bytes=VMEM_LIMIT_BYTES)


def _ada_kernel(c_ref, w_ref, b_ref, o_ref):
    s = _silu(c_ref[...]).astype(BF16)
    o_ref[...] = _bdot(s, w_ref[...].astype(BF16)) + b_ref[...]


def _ada(c, w, b, cols, tn=1024):
    m, k = c.shape
    n = cols
    return pl.pallas_call(
        _ada_kernel,
        grid=(n // tn,),
        in_specs=[pl.BlockSpec((m, k), lambda j: (0, 0)),
                  pl.BlockSpec((k, tn), lambda j: (0, j)),
                  pl.BlockSpec((1, tn), lambda j: (0, j))],
        out_specs=pl.BlockSpec((m, tn), lambda j: (0, j)),
        out_shape=jax.ShapeDtypeStruct((m, n), F32),
        compiler_params=_params(("parallel",)),
        name="ada_proj",
    )(c, w, b.reshape(1, -1))


def _norm_modulate(x, gain, shift, scale):
    ms = jnp.mean(x * x, axis=-1, keepdims=True)
    y = x * lax.rsqrt(ms + EPS) * gain
    return y * (1.0 + scale) + shift


def _norm_rows_kernel(x_ref, gain_ref, sh_ref, sc_ref, xs_ref, shs_ref, scs_ref, o_ref, os_ref):
    o_ref[...] = _norm_modulate(x_ref[...], gain_ref[...], sh_ref[...],
                                sc_ref[...]).astype(o_ref.dtype)

    @pl.when(pl.program_id(0) == 0)
    def _():
        os_ref[...] = _norm_modulate(xs_ref[...], gain_ref[...], shs_ref[...],
                                     scs_ref[...]).astype(os_ref.dtype)


def _norm_rows(x, x_s, gain, mod, mod_s, shift_chunk, *, tm):
    m, d = x.shape
    ns = x_s.shape[0]
    tiles_per_group = (m // tm) // mod.shape[0]
    once = dict(pipeline_mode=pl.Buffered(1))
    mod_spec = lambda c: pl.BlockSpec((None, 1, d), lambda i: (i // tiles_per_group, 0, c))
    mod_s_spec = lambda c: pl.BlockSpec((None, ns, d), lambda i: (0, 0, c), **once)
    return pl.pallas_call(
        _norm_rows_kernel,
        grid=(m // tm,),
        in_specs=[pl.BlockSpec((tm, d), lambda i: (i, 0)), pl.BlockSpec((1, d), lambda i: (0, 0)),
                  mod_spec(shift_chunk), mod_spec(shift_chunk + 1),
                  pl.BlockSpec((ns, d), lambda i: (0, 0), **once),
                  mod_s_spec(shift_chunk), mod_s_spec(shift_chunk + 1)],
        out_specs=[pl.BlockSpec((tm, d), lambda i: (i, 0)), pl.BlockSpec((ns, d), lambda i: (0, 0))],
        out_shape=[jax.ShapeDtypeStruct((m, d), BF16), jax.ShapeDtypeStruct((ns, d), BF16)],
        compiler_params=_params(("arbitrary",)),
        name="norm_rows",
    )(x, gain.reshape(1, d), mod, mod, x_s, mod_s, mod_s)


class _Stream(NamedTuple):
    ins: list
    outs: list
    body: Callable
    start: int
    steps: int


def _stream_io(streams, n_inner):
    in_specs, args, out_specs, out_shape = [], [], [], []

    def spec(st, block, index_fn):
        return pl.BlockSpec(
            block, lambda i, j: index_fn(jnp.clip(i * n_inner + j - st.start, 0, st.steps - 1)))

    for st in streams:
        for arr, block, index_fn in st.ins:
            in_specs.append(spec(st, block, index_fn))
            args.append(arr)
        for shape, block, index_fn in st.outs:
            out_specs.append(spec(st, block, index_fn))
            out_shape.append(shape)
    return in_specs, args, out_specs, out_shape


def _run_streams(streams, in_refs, out_refs, n_inner):
    step = pl.program_id(0) * n_inner + pl.program_id(1)
    in_refs, out_refs = iter(in_refs), iter(out_refs)
    for st in streams:
        ins = [next(in_refs) for _ in st.ins]
        outs = [next(out_refs) for _ in st.outs]

        @pl.when((step >= st.start) & (step < st.start + st.steps))
        def _(st=st, ins=ins, outs=outs):
            st.body(*ins, *outs)


def _split_stream_outs(streams, flat):
    flat = list(flat)
    return [[flat.pop(0) for _ in st.outs] for st in streams]


def _cast_body(src_ref, dst_ref):
    dst_ref[...] = src_ref[...].astype(dst_ref.dtype)


def _cast_stream(w, chunks, start=0, rows=None):
    rows = w.shape[0] if rows is None else rows
    block = (rows // chunks, w.shape[1])
    index = lambda k: (k, 0)
    return _Stream([(w, block, index)],
                   [(jax.ShapeDtypeStruct((rows, w.shape[1]), BF16), block, index)],
                   _cast_body, start, chunks)


def _ada_stream(c, w, b, col0, cols, tn, start=0):
    m, k = c.shape
    t0 = col0 // tn
    return _Stream(
        [(c, (m, k), lambda s: (0, 0)), (w, (k, tn), lambda s: (0, s + t0)),
         (b.reshape(1, -1), (1, tn), lambda s: (0, s + t0))],
        [(jax.ShapeDtypeStruct((m, cols), F32), (m, tn), lambda s: (0, s))],
        _ada_kernel, start, cols // tn)


def _wdot(h, w, trans_w):
    if trans_w:
        return lax.dot_general(h, w, (((1,), (1,)), ((), ())), preferred_element_type=F32)
    return _bdot(h, w)


def _proj_kernel(*refs, n_w, swiglu, trans_w, has_extra, has_side, emit_bf16, streams, n_prev,
                 nj):
    it = iter(refs)
    x_ref = next(it)
    xs_ref = next(it) if has_side else None
    w_refs = [next(it) for _ in range(n_w)]
    wx_ref = next(it) if has_extra else None
    stream_ins = [next(it) for st in streams for _ in st.ins]
    for _ in range(n_prev):
        next(it)
    o_ref = next(it)
    ox_ref = next(it) if has_extra else None
    os_ref = next(it) if has_side else None
    osx_ref = next(it) if has_side and has_extra else None
    wo_refs = [next(it) for _ in range(n_w)] if emit_bf16 else []
    stream_outs = [next(it) for st in streams for _ in st.outs]

    j = pl.program_id(1)
    first_tile = pl.program_id(0) == 0

    if has_extra:
        @pl.when(j == 0)
        def _():
            ox_ref[...] = _wdot(x_ref[...], wx_ref[...].astype(BF16), trans_w)

        if has_side:
            @pl.when((j == 0) & first_tile)
            def _():
                osx_ref[...] = _wdot(xs_ref[...], wx_ref[...].astype(BF16), trans_w)

    wbs = [w_ref[...].astype(BF16) for w_ref in w_refs]
    for wo_ref, wb in zip(wo_refs, wbs):
        wo_ref[...] = wb

    def project(h, out_ref):
        if swiglu:
            g = _wdot(h, wbs[0], trans_w)
            u = _wdot(h, wbs[1], trans_w)
            out_ref[...] = (_silu(g) * u).astype(out_ref.dtype)
        else:
            out_ref[...] = _wdot(h, wbs[0], trans_w).astype(out_ref.dtype)

    project(x_ref[...], o_ref)
    if has_side:
        @pl.when(first_tile)
        def _():
            project(xs_ref[...], os_ref)

    _run_streams(streams, stream_ins, stream_outs, nj)


def _proj(x, ws, *, n_out, tm, tn, swiglu, out_dtype, row_tiles, side=None, trans_w=False,
          w_extra=None, emit_bf16=False, streams=(), prev=None):
    m, d = x.shape
    t0, t1 = row_tiles
    nj = n_out // tn
    has_side = side is not None
    has_extra = w_extra is not None
    single_row_tile = t1 - t0 == 1
    once = dict(pipeline_mode=pl.Buffered(1))

    def w_spec(off):
        if trans_w:
            return pl.BlockSpec((tn, d), lambda i, j: (j + off, 0))
        return pl.BlockSpec((d, tn), lambda i, j: (0, j + off))

    x_mode = once if single_row_tile else {}
    in_specs = [pl.BlockSpec((tm, d), lambda i, j: (i + t0, 0), **x_mode)]
    args = [x]
    if has_side:
        ns = side.shape[0]
        in_specs.append(pl.BlockSpec((ns, d), lambda i, j: (0, 0), **once))
        args.append(side)
    in_specs += [w_spec(off) for _, off in ws]
    args += [w for w, _ in ws]
    if has_extra:
        nx = w_extra.shape[0] if trans_w else w_extra.shape[1]
        in_specs.append(pl.BlockSpec(w_extra.shape, lambda i, j: (0, 0)))
        args.append(w_extra)
    assert all(st.start + st.steps <= (t1 - t0) * nj for st in streams)
    st_in_specs, st_args, st_out_specs, st_out_shape = _stream_io(streams, nj)
    in_specs += st_in_specs
    args += st_args
    prev = list(prev or [])
    aliases = {}
    for k, buf in enumerate(prev):
        aliases[len(args)] = k
        in_specs.append(pl.BlockSpec(memory_space=pl.ANY))
        args.append(buf)

    out_specs = [pl.BlockSpec((tm, tn), lambda i, j: (i + t0, j))]
    out_shape = [jax.ShapeDtypeStruct((m, n_out), out_dtype)]
    if has_extra:
        out_specs.append(pl.BlockSpec((tm, nx), lambda i, j: (i + t0, 0)))
        out_shape.append(jax.ShapeDtypeStruct((m, nx), F32))
    n_main = len(out_shape)
    if has_side:
        out_specs.append(pl.BlockSpec((ns, tn), lambda i, j: (0, jnp.where(i == 0, j, nj - 1))))
        out_shape.append(jax.ShapeDtypeStruct((ns, n_out), out_dtype))
        if has_extra:
            out_specs.append(pl.BlockSpec((ns, nx), lambda i, j: (0, 0)))
            out_shape.append(jax.ShapeDtypeStruct((ns, nx), F32))
    n_side = len(out_shape) - n_main
    if emit_bf16:
        assert single_row_tile, "weight copies are written once per column tile"
        for _ in ws:
            out_specs.append(w_spec(0))
            out_shape.append(jax.ShapeDtypeStruct((n_out, d) if trans_w else (d, n_out), BF16))
    out_specs += st_out_specs
    out_shape += st_out_shape
    outs = pl.pallas_call(
        functools.partial(_proj_kernel, n_w=len(ws), swiglu=swiglu, trans_w=trans_w,
                          has_extra=has_extra, has_side=has_side, emit_bf16=emit_bf16,
                          streams=tuple(streams), n_prev=len(prev), nj=nj),
        grid=(t1 - t0, nj),
        in_specs=in_specs,
        out_specs=out_specs,
        out_shape=out_shape,
        input_output_aliases=aliases,
        compiler_params=_params(("arbitrary" if streams or has_side else "parallel", "arbitrary")),
        name="proj_swiglu" if swiglu else "proj",
    )(*args)
    n_wb = len(ws) if emit_bf16 else 0
    main, rest = outs[:n_main], outs[n_main:]
    side_outs, rest = rest[:n_side], rest[n_side:]
    wb, rest = rest[:n_wb], rest[n_wb:]
    return main, side_outs, wb, _split_stream_outs(streams, rest)


def _proj_first_tile(x, ws_f32, *, tm, tn, side, **kw):
    main, side_outs, wb, _ = _proj(x, ws_f32, tm=tm, tn=tn, row_tiles=(0, 1), side=side,
                                   emit_bf16=True, **kw)
    return main, side_outs, wb


def _proj_other_tiles(x, wb, prev, *, tm, tn, streams, **kw):
    main, _, _, stream_outs = _proj(x, [(w, 0) for w in wb], tm=tm, tn=tn,
                                    row_tiles=(1, x.shape[0] // tm), prev=prev, streams=streams,
                                    **kw)
    return main, stream_outs


def _resid_kernel(*refs, n_lhs, factor, emit_x):
    it = iter(refs)
    lhs_refs = [next(it) for _ in range(n_lhs)]
    lhs_s_refs = [next(it) for _ in range(n_lhs)]
    w_refs = [next(it) for _ in range(n_lhs)]
    x_ref, gate_ref, gain_ref, sh_ref, sc_ref = (next(it) for _ in range(5))
    xs_ref, gate_s_ref, sh_s_ref, sc_s_ref = (next(it) for _ in range(4))
    n_out = 2 if emit_x else 1
    outs = [next(it) for _ in range(n_out)]
    outs_s = [next(it) for _ in range(n_out)]

    def update(lhs, x_in, gate, sh, sc, out_refs):
        acc = _bdot(lhs[0][...], w_refs[0][...])
        for l_ref, w_ref in zip(lhs[1:], w_refs[1:]):
            acc = acc + _bdot(l_ref[...], w_ref[...])
        x_new = x_in[...] + (factor * gate[...]) * acc
        if emit_x:
            out_refs[0][...] = x_new
        h_ref = out_refs[-1]
        h_ref[...] = _norm_modulate(x_new, gain_ref[...], sh[...], sc[...]).astype(h_ref.dtype)

    update(lhs_refs, x_ref, gate_ref, sh_ref, sc_ref, outs)

    @pl.when(pl.program_id(0) == 0)
    def _():
        update(lhs_s_refs, xs_ref, gate_s_ref, sh_s_ref, sc_s_ref, outs_s)


def _resid(lhs_list, lhs_s_list, ws, x, x_s, mod, mod_s, gate_chunk, gain_next, mod_next,
           mod_next_s, shift_chunk_next, *, factor, tm, emit_x, h_dtype):
    m, d = x.shape
    ns = x_s.shape[0]
    groups = mod.shape[0]
    tiles_per_group = (m // tm) // groups
    kp = lhs_list[0].shape[1]
    once = dict(pipeline_mode=pl.Buffered(1))

    def mod_spec(chunk):
        return pl.BlockSpec((None, 1, d), lambda i: (i // tiles_per_group, 0, chunk))

    def mod_s_spec(chunk):
        return pl.BlockSpec((None, ns, d), lambda i: (0, 0, chunk), **once)

    in_specs = [pl.BlockSpec((tm, kp), lambda i: (i, 0)) for _ in lhs_list]
    in_specs += [pl.BlockSpec((ns, kp), lambda i: (0, 0), **once) for _ in lhs_s_list]
    in_specs += [pl.BlockSpec((kp, d), lambda i, k=k: (k, 0), **once) for _, k in ws]
    in_specs += [pl.BlockSpec((tm, d), lambda i: (i, 0)), mod_spec(gate_chunk),
                 pl.BlockSpec((1, d), lambda i: (0, 0)),
                 mod_spec(shift_chunk_next), mod_spec(shift_chunk_next + 1),
                 pl.BlockSpec((ns, d), lambda i: (0, 0), **once), mod_s_spec(gate_chunk),
                 mod_s_spec(shift_chunk_next), mod_s_spec(shift_chunk_next + 1)]
    row = pl.BlockSpec((tm, d), lambda i: (i, 0))
    row_s = pl.BlockSpec((ns, d), lambda i: (0, 0))
    dtypes = ([F32] if emit_x else []) + [h_dtype]
    out_specs = [row for _ in dtypes] + [row_s for _ in dtypes]
    out_shape = ([jax.ShapeDtypeStruct((m, d), t) for t in dtypes]
                 + [jax.ShapeDtypeStruct((ns, d), t) for t in dtypes])
    outs = pl.pallas_call(
        functools.partial(_resid_kernel, n_lhs=len(lhs_list), factor=factor, emit_x=emit_x),
        grid=(m // tm,),
        in_specs=in_specs,
        out_specs=out_specs,
        out_shape=out_shape,
        compiler_params=_params(("arbitrary",)),
        name="resid",
    )(*lhs_list, *lhs_s_list, *[w for w, _ in ws], x, mod, gain_next.reshape(1, d), mod_next,
      mod_next, x_s, mod_s, mod_next_s, mod_next_s)
    return outs[:len(dtypes)], outs[len(dtypes):]


def _lru_gates(xc, wg_ref, ba, bi, sp):
    a_parts, b_parts = [], []
    for g in range(W_LRU // LRU_GATE_GROUP):
        cols = slice(g * LRU_GATE_GROUP, (g + 1) * LRU_GATE_GROUP)
        xg = xc[:, cols]
        ri = _bdot(xg.astype(BF16), wg_ref[g].astype(BF16))
        r = _sigmoid(ri[:, :LRU_GATE_GROUP] + ba[:, cols])
        i = _sigmoid(ri[:, LRU_GATE_GROUP:] + bi[:, cols])
        log_a = (-LRU_C * r) * sp[:, cols]
        a = jnp.exp(log_a)
        a_parts.append(a)
        b_parts.append(jnp.sqrt(1.0 - a * a) * (i * xg))
    return jnp.concatenate(a_parts, axis=1), jnp.concatenate(b_parts, axis=1)


def _causal_conv_from_buf(buf_ref, x, w_ref, b_ref, rows):
    xe = buf_ref[0:SUBLANES + rows, :]
    shifted = lambda k: pltpu.roll(xe, k, 0)[SUBLANES:, :]
    y = b_ref[...] + w_ref[0:1, :] * shifted(3)
    y = y + w_ref[1:2, :] * shifted(2)
    y = y + w_ref[2:3, :] * shifted(1)
    return y + w_ref[3:4, :] * x


def _lru_prompt_kernel(*refs, streams):
    n_in = sum(len(st.ins) for st in streams)
    n_out = sum(len(st.outs) for st in streams)
    xl_ref, gl_ref, cw_ref, cb_ref, wg_ref, ba_ref, bi_ref, lam_ref = refs[:8]
    stream_ins = refs[8:8 + n_in]
    o_ref, hT_ref = refs[8 + n_in:10 + n_in]
    stream_outs = refs[10 + n_in:10 + n_in + n_out]
    xbuf, a_scr, b_scr, hcar = refs[10 + n_in + n_out:]
    t = pl.program_id(1)
    tt = xl_ref.shape[0]

    @pl.when(t == 0)
    def _():
        xbuf[0:SUBLANES, :] = jnp.zeros((SUBLANES, W_LRU), F32)
        hcar[...] = jnp.zeros_like(hcar)

    x = xl_ref[...]
    xbuf[SUBLANES:SUBLANES + tt, :] = x
    xc = _causal_conv_from_buf(xbuf, x, cw_ref, cb_ref, tt)
    xbuf[0:SUBLANES, :] = x[tt - SUBLANES:, :]

    sp = _softplus(-lam_ref[...])
    a, bt = _lru_gates(xc, wg_ref, ba_ref[...], bi_ref[...], sp)
    a_scr[...] = a
    b_scr[...] = bt

    rid = lax.broadcasted_iota(jnp.int32, (SUBLANES, W_LRU), 0)

    def scan8(a8, b8, h_in):
        for s in (1, 2, 4):
            a_sh = pltpu.roll(a8, s, 0)
            b_sh = pltpu.roll(b8, s, 0)
            m = rid >= s
            b8 = jnp.where(m, a8 * b_sh + b8, b8)
            a8 = jnp.where(m, a8 * a_sh, a8)
        h8 = a8 * h_in + b8
        return h8, jnp.broadcast_to(h8[SUBLANES - 1:SUBLANES, :], (SUBLANES, W_LRU))

    def body(g, h_in):
        r0 = pl.multiple_of(g * SCAN_ROWS, SCAN_ROWS)
        lo = pl.ds(r0, SUBLANES)
        hi = pl.ds(r0 + SUBLANES, SUBLANES)
        h_lo, h_mid = scan8(a_scr[lo, :], b_scr[lo, :], h_in)
        h_hi, h_out = scan8(a_scr[hi, :], b_scr[hi, :], h_mid)
        rows = pl.ds(r0, SCAN_ROWS)
        h16 = jnp.concatenate([h_lo, h_hi], axis=0)
        o_ref[rows, :] = (h16 * _gelu_tanh(gl_ref[rows, :])).astype(o_ref.dtype)
        return h_out

    h_last = lax.fori_loop(0, tt // SCAN_ROWS, body, hcar[...])
    hcar[...] = h_last

    @pl.when(t == pl.num_programs(1) - 1)
    def _():
        hT_ref[...] = h_last[0:1, :]

    _run_streams(streams, stream_ins, stream_outs, pl.num_programs(1))


def _lru_prompt(proj, batch, seq, cw, cb, wg, ba, bi, lam, streams=()):
    tt = LRU_TIME_TILE
    nt = seq // tt
    assert all(st.start + st.steps <= batch * nt for st in streams)
    st_in_specs, st_args, st_out_specs, st_out_shape = _stream_io(streams, nt)
    row = lambda v: v.reshape(1, W_LRU)
    full = lambda shape: pl.BlockSpec(shape, lambda b, t: (0,) * len(shape))
    out, h_t, *rest = pl.pallas_call(
        functools.partial(_lru_prompt_kernel, streams=tuple(streams)),
        grid=(batch, nt),
        in_specs=[pl.BlockSpec((tt, W_LRU), lambda b, t: (b * nt + t, 0)),
                  pl.BlockSpec((tt, W_LRU), lambda b, t: (b * nt + t, 1)),
                  full((CONV_W, W_LRU)), full((1, W_LRU)), full(wg.shape),
                  full((1, W_LRU)), full((1, W_LRU)), full((1, W_LRU))] + st_in_specs,
        out_specs=[pl.BlockSpec((tt, W_LRU), lambda b, t: (b * nt + t, 0)),
                   pl.BlockSpec((None, 1, W_LRU), lambda b, t: (b, 0, 0))] + st_out_specs,
        out_shape=[jax.ShapeDtypeStruct((batch * seq, W_LRU), BF16),
                   jax.ShapeDtypeStruct((batch, 1, W_LRU), F32)] + st_out_shape,
        scratch_shapes=[pltpu.VMEM((tt + SUBLANES, W_LRU), F32),
                        pltpu.VMEM((tt, W_LRU), F32),
                        pltpu.VMEM((tt, W_LRU), F32),
                        pltpu.VMEM((SUBLANES, W_LRU), F32)],
        compiler_params=_params(("arbitrary" if streams else "parallel", "arbitrary")),
        name="lru_prompt",
    )(proj, proj, cw, row(cb), wg, row(ba), row(bi), row(lam), *st_args)
    return out, h_t.reshape(batch, W_LRU), _split_stream_outs(streams, rest)


def _ssd_prompt_kernel(z_ref, xbc_ref, dt_ref, cw_ref, cb_ref, dtb_ref, alog_ref, dexp_ref,
                       ng_ref, y_ref, st_ref, xbuf, st_scr, y_scr, m_scr, xbd_scr):
    c = pl.program_id(1)
    lc = SSD_CHUNK

    @pl.when(c == 0)
    def _():
        xbuf[0:SUBLANES, :] = jnp.zeros((SUBLANES, SSD_CONV_DIM), F32)
        st_scr[...] = jnp.zeros_like(st_scr)
        xbd_scr[...] = jnp.zeros_like(xbd_scr)

    x = xbc_ref[...]
    xbuf[SUBLANES:SUBLANES + lc, :] = x
    act = _silu(_causal_conv_from_buf(xbuf, x, cw_ref, cb_ref, lc))
    xbuf[0:SUBLANES, :] = x[lc - SUBLANES:, :]
    xs = act[:, :W_SSD]
    bm = act[:, W_SSD:W_SSD + SSD_GROUPS * SSD_STATE]
    cm = act[:, W_SSD + SSD_GROUPS * SSD_STATE:]

    dt = _softplus(dt_ref[...] + dtb_ref[...])
    d_a = dt * (-jnp.exp(alog_ref[...]))
    row_i = lax.broadcasted_iota(jnp.int32, (lc, lc), 0)
    col_i = lax.broadcasted_iota(jnp.int32, (lc, lc), 1)
    causal = row_i >= col_i
    tril = jnp.where(causal, 1.0, 0.0).astype(F32)
    cs = jnp.dot(tril, d_a, preferred_element_type=F32, precision=lax.Precision.HIGHEST)
    cs_t = cs.T
    dt_t = dt.T
    cs_last = cs[lc - 1:lc, :]

    def per_head_lanes(v):
        rows = v.shape[0]
        return jnp.concatenate(
            [jnp.broadcast_to(v[:, h:h + 1], (rows, SSD_HEAD_DIM)) for h in range(SSD_HEADS)],
            axis=1)

    w_exp = per_head_lanes(jnp.exp(cs_last - cs) * dt)
    ecs_exp = per_head_lanes(jnp.exp(cs))
    cd_exp = per_head_lanes(jnp.exp(cs_last))
    gw = SSD_HPG * SSD_HEAD_DIM
    low_half = col_i < SSD_HEAD_DIM

    for g in range(SSD_GROUPS):
        ncols = slice(g * SSD_STATE, (g + 1) * SSD_STATE)
        gcols = slice(g * gw, (g + 1) * gw)
        b_g = bm[:, ncols].astype(BF16)
        c_g = cm[:, ncols].astype(BF16)
        cb_mat = lax.dot_general(c_g, b_g, (((1,), (1,)), ((), ())),
                                 preferred_element_type=F32)
        for e in range(SSD_HPG):
            h = g * SSD_HPG + e
            cs_col = jnp.broadcast_to(cs[:, h:h + 1], (lc, lc))
            l_mat = jnp.exp(jnp.where(causal, cs_col - cs_t[h:h + 1, :], -jnp.inf))
            m_scr[g, :, e * lc:(e + 1) * lc] = (cb_mat * l_mat * dt_t[h:h + 1, :]).astype(BF16)
        for q in range(SSD_HPG // 2):
            lanes = slice(q * LANES, (q + 1) * LANES)
            slab = xs[:, g * gw + q * LANES:g * gw + (q + 1) * LANES]
            xbd_scr[g, (2 * q) * lc:(2 * q + 1) * lc, lanes] = jnp.where(
                low_half, slab, 0.0).astype(BF16)
            xbd_scr[g, (2 * q + 1) * lc:(2 * q + 2) * lc, lanes] = jnp.where(
                low_half, 0.0, slab).astype(BF16)
        st_g = st_scr[:, gcols]
        y_off = _bdot(c_g, st_g.astype(BF16)) * ecs_exp[:, gcols]
        y_scr[:, gcols] = (_bdot(m_scr[g], xbd_scr[g]) + y_off
                           + dexp_ref[:, gcols] * xs[:, gcols])
        xw = (xs[:, gcols] * w_exp[:, gcols]).astype(BF16)
        st_scr[:, gcols] = cd_exp[:, gcols] * st_g + lax.dot_general(
            b_g, xw, (((0,), (0,)), ((), ())), preferred_element_type=F32)

    yg = y_scr[...] * _silu(z_ref[...])
    ms = jnp.mean(yg * yg, axis=-1, keepdims=True)
    y_ref[...] = (yg * lax.rsqrt(ms + EPS) * ng_ref[...]).astype(y_ref.dtype)

    @pl.when(c == pl.num_programs(1) - 1)
    def _():
        st_ref[...] = st_scr[...].T


def _ssd_prompt(proj, dt_raw, batch, seq, cw, cb, dtb, alog, dexp, ng):
    lc = SSD_CHUNK
    nc = seq // lc
    full = lambda shape: pl.BlockSpec(shape, lambda b, c: (0,) * len(shape))
    z_blk = (2 * W_LRU) // W_SSD
    xbc_blk = (2 * W_LRU + W_SSD) // SSD_CONV_DIM
    y, st = pl.pallas_call(
        _ssd_prompt_kernel,
        grid=(batch, nc),
        in_specs=[pl.BlockSpec((lc, W_SSD), lambda b, c: (b * nc + c, z_blk)),
                  pl.BlockSpec((lc, SSD_CONV_DIM), lambda b, c: (b * nc + c, xbc_blk)),
                  pl.BlockSpec((lc, LANES), lambda b, c: (b * nc + c, 0)),
                  full((CONV_W, SSD_CONV_DIM)), full((1, SSD_CONV_DIM)),
                  full((1, LANES)), full((1, LANES)), full((1, W_SSD)), full((1, W_SSD))],
        out_specs=[pl.BlockSpec((lc, W_SSD), lambda b, c: (b * nc + c, 0)),
                   pl.BlockSpec((None, W_SSD, SSD_STATE), lambda b, c: (b, 0, 0))],
        out_shape=[jax.ShapeDtypeStruct((batch * seq, W_SSD), BF16),
                   jax.ShapeDtypeStruct((batch, W_SSD, SSD_STATE), F32)],
        scratch_shapes=[pltpu.VMEM((lc + SUBLANES, SSD_CONV_DIM), F32),
                        pltpu.VMEM((SSD_STATE, W_SSD), F32),
                        pltpu.VMEM((lc, W_SSD), F32),
                        pltpu.VMEM((SSD_GROUPS, lc, SSD_HPG * lc), BF16),
                        pltpu.VMEM((SSD_GROUPS, SSD_HPG * lc, SSD_HPG * SSD_HEAD_DIM), BF16)],
        compiler_params=_params(("parallel", "arbitrary")),
        name="ssd_prompt",
    )(proj, proj, dt_raw, cw, cb, dtb, alog, dexp, ng)
    return y, st.reshape(batch, SSD_HEADS, SSD_HEAD_DIM, SSD_STATE)


def _sample_pre_kernel(proj_ref, dt_ref, h0_ref, lconv_ref, sconv_ref,
                       lcw_ref, lcb_ref, wg_ref, ba_ref, bi_ref, lam_ref,
                       scw_ref, scb_ref, dtb_ref, alog_ref,
                       outl_ref, hnew_ref, lconv_new_ref, sconv_new_ref,
                       xs_ref, xdt_ref, bc_ref, dec_ref):
    nb = proj_ref.shape[0]
    xl = proj_ref[:, 0:W_LRU]
    gl = proj_ref[:, W_LRU:2 * W_LRU]
    xbc = proj_ref[:, 2 * W_LRU + W_SSD:IN_MAIN]

    def conv1(state_ref, width, x_new, w_ref, b_ref):
        y = b_ref[...] + w_ref[0:1, :] * state_ref[:, 0:width]
        y = y + w_ref[1:2, :] * state_ref[:, width:2 * width]
        y = y + w_ref[2:3, :] * state_ref[:, 2 * width:3 * width]
        return y + w_ref[3:4, :] * x_new

    xc = conv1(lconv_ref, W_LRU, xl, lcw_ref, lcb_ref)
    a, bt = _lru_gates(xc, wg_ref, ba_ref[...], bi_ref[...], _softplus(-lam_ref[...]))
    h_new = a * h0_ref[...] + bt
    hnew_ref[...] = h_new
    outl_ref[...] = (h_new * _gelu_tanh(gl)).astype(outl_ref.dtype)
    lconv_new_ref[:, 0:2 * W_LRU] = lconv_ref[:, W_LRU:3 * W_LRU]
    lconv_new_ref[:, 2 * W_LRU:3 * W_LRU] = xl

    act = _silu(conv1(sconv_ref, SSD_CONV_DIM, xbc, scw_ref, scb_ref))
    sconv_new_ref[:, 0:2 * SSD_CONV_DIM] = sconv_ref[:, SSD_CONV_DIM:3 * SSD_CONV_DIM]
    sconv_new_ref[:, 2 * SSD_CONV_DIM:3 * SSD_CONV_DIM] = xbc
    xs = act[:, :W_SSD]
    xs_ref[...] = xs
    bc_ref[...] = act[:, W_SSD:]
    dt = _softplus(dt_ref[...] + dtb_ref[...])
    dec = jnp.exp(dt * (-jnp.exp(alog_ref[...])))
    for h in range(SSD_HEADS):
        pcols = slice(h * SSD_HEAD_DIM, (h + 1) * SSD_HEAD_DIM)
        xdt_ref[:, pcols] = xs[:, pcols] * jnp.broadcast_to(dt[:, h:h + 1], (nb, SSD_HEAD_DIM))
        dec_ref[h] = jnp.broadcast_to(dec[:, h:h + 1], (nb, SSD_STATE))


def _sample_pre(proj, dt_raw, h0, lconv, sconv, p):
    nb = proj.shape[0]
    out_shape = [jax.ShapeDtypeStruct((nb, W_LRU), BF16),
                 jax.ShapeDtypeStruct((nb, W_LRU), F32),
                 jax.ShapeDtypeStruct((nb, 3 * W_LRU), F32),
                 jax.ShapeDtypeStruct((nb, 3 * SSD_CONV_DIM), F32),
                 jax.ShapeDtypeStruct((nb, W_SSD), F32),
                 jax.ShapeDtypeStruct((nb, W_SSD), F32),
                 jax.ShapeDtypeStruct((nb, 2 * SSD_GROUPS * SSD_STATE), F32),
                 jax.ShapeDtypeStruct((SSD_HEADS, nb, SSD_STATE), F32)]
    return pl.pallas_call(
        _sample_pre_kernel,
        out_shape=out_shape,
        compiler_params=pltpu.CompilerParams(vmem_limit_bytes=VMEM_LIMIT_BYTES),
        name="sample_pre",
    )(proj, dt_raw, h0, lconv, sconv,
      p["lru_cw"], p["lru_cb"], p["lru_wg"], p["lru_ba"], p["lru_bi"], p["lru_lam"],
      p["ssd_cw"], p["ssd_cb"], p["ssd_dtb"], p["ssd_alog"])


def _sample_state_kernel(s_ref, xdt_ref, bc_ref, dec_ref, o_ref, y_ref):
    bb = s_ref.shape[0]
    half = SSD_HPG * SSD_HEAD_DIM
    rid = lax.broadcasted_iota(jnp.int32, (bb, W_SSD), 0)
    xdt = xdt_ref[...]
    bcb = bc_ref[...].astype(BF16)
    for k in range(bb):
        xk = jnp.where(rid == k, xdt, 0.0).astype(BF16)
        for g in range(SSD_GROUPS):
            rows = slice(g * half, (g + 1) * half)
            b_g = bcb[:, g * SSD_STATE:(g + 1) * SSD_STATE]
            c_g = bcb[:, (SSD_GROUPS + g) * SSD_STATE:(SSD_GROUPS + g + 1) * SSD_STATE]
            outer = lax.dot_general(xk[:, rows], b_g, (((0,), (0,)), ((), ())),
                                    preferred_element_type=F32)
            dec = jnp.concatenate(
                [jnp.broadcast_to(dec_ref[g * SSD_HPG + e, k:k + 1, :], (SSD_HEAD_DIM, SSD_STATE))
                 for e in range(SSD_HPG)], axis=0)
            s_new = dec * s_ref[k, rows, :] + outer
            o_ref[k, rows, :] = s_new
            yk = lax.dot_general(c_g, s_new.astype(BF16), (((1,), (1,)), ((), ())),
                                 preferred_element_type=F32)
            y_ref[k:k + 1, rows] = yk[k:k + 1, :]


def _state_stream(ssm, xdt, bc, dec, bb=8, start=0):
    nb = ssm.shape[0]
    state_block = (bb, W_SSD, SSD_STATE)
    return _Stream(
        [(ssm, state_block, lambda k: (k, 0, 0)),
         (xdt, (bb, W_SSD), lambda k: (k, 0)),
         (bc, (bb, 2 * SSD_GROUPS * SSD_STATE), lambda k: (k, 0)),
         (dec, (SSD_HEADS, bb, SSD_STATE), lambda k: (0, k, 0))],
        [(jax.ShapeDtypeStruct(ssm.shape, F32), state_block, lambda k: (k, 0, 0)),
         (jax.ShapeDtypeStruct((nb, W_SSD), F32), (bb, W_SSD), lambda k: (k, 0))],
        _sample_state_kernel, start, nb // bb)


def _sample_post_kernel(y_ref, xs_ref, proj_ref, dexp_ref, ng_ref, o_ref):
    z = proj_ref[:, 2 * W_LRU:2 * W_LRU + W_SSD]
    yg = (y_ref[...] + dexp_ref[...] * xs_ref[...]) * _silu(z)
    ms = jnp.mean(yg * yg, axis=-1, keepdims=True)
    o_ref[...] = (yg * lax.rsqrt(ms + EPS) * ng_ref[...]).astype(o_ref.dtype)


def _sample_post(y_raw, xs, proj, dexp, ng):
    return pl.pallas_call(
        _sample_post_kernel,
        out_shape=jax.ShapeDtypeStruct(y_raw.shape, BF16),
        compiler_params=pltpu.CompilerParams(vmem_limit_bytes=VMEM_LIMIT_BYTES),
        name="sample_post",
    )(y_raw, xs, proj, dexp, ng)


def _block_diag_groups(w):
    per = LRU_GATE_GROUP // LRU_BLOCK
    w4 = w.reshape(LRU_HEADS // per, per, LRU_BLOCK, LRU_BLOCK)
    bd = jnp.einsum("ghij,hk->ghikj", w4, jnp.eye(per, dtype=w.dtype))
    return bd.reshape(LRU_HEADS // per, LRU_GATE_GROUP, LRU_GATE_GROUP)


def _pad_lanes(v):
    v = v.reshape(1, -1)
    return jnp.pad(v, ((0, 0), (0, LANES - v.shape[1])))


def kernel(x_prompt, x_sample, c_prompt, c_sample, state_lru_h, state_lru_conv, state_ssm, state_ssd_conv, w_ada, b_ada, g_ffn1, w_up1, w_down1, g_mix, w_in, lru_conv_w, lru_conv_b, lru_wa, lru_ba, lru_wi, lru_bi, lru_lambda, ssd_conv_w, ssd_conv_b, ssd_dt_bias, ssd_A_log, ssd_D, ssd_norm_g, w_out, g_ffn2, w_up2, w_down2, w_ada_f, b_ada_f, g_final):
    bp, seq, d = x_prompt.shape
    bs = x_sample.shape[0]
    depth = w_ada.shape[0]
    assert depth == 1 and x_sample.shape[1] == 1 and d == D_MODEL

    pad_rows = (-(bs + bp)) % (2 * SUBLANES)
    c_rows = bs + bp + pad_rows
    c_all = jnp.concatenate([c_sample, c_prompt, jnp.zeros((pad_rows, d), F32)], axis=0)

    def split_rows(mod_all):
        width = mod_all.shape[1]
        return mod_all[bs:bs + bp].reshape(bp, 1, width), mod_all.reshape(1, c_rows, width)

    w_in_t = jnp.swapaxes(w_in[0], 0, 1)
    w_dt_t = jnp.pad(w_in_t[IN_MAIN:], ((0, LANES - SSD_HEADS), (0, 0)))
    up_blocks = D_FF // 512
    p = {
        "lru_cw": lru_conv_w[0], "lru_cb": lru_conv_b[0].reshape(1, W_LRU),
        "lru_wg": jnp.concatenate([_block_diag_groups(lru_wa[0]), _block_diag_groups(lru_wi[0])],
                                  axis=-1),
        "lru_ba": lru_ba[0].reshape(1, W_LRU), "lru_bi": lru_bi[0].reshape(1, W_LRU),
        "lru_lam": lru_lambda[0].reshape(1, W_LRU),
        "ssd_cw": ssd_conv_w[0], "ssd_cb": ssd_conv_b[0].reshape(1, SSD_CONV_DIM),
        "ssd_dtb": _pad_lanes(ssd_dt_bias[0]), "ssd_alog": _pad_lanes(ssd_A_log[0]),
        "ssd_dexp": jnp.repeat(ssd_D[0], SSD_HEAD_DIM).reshape(1, W_SSD),
        "ssd_ng": ssd_norm_g[0].reshape(1, W_SSD),
    }

    xp = x_prompt.reshape(bp * seq, d)
    xs = x_sample.reshape(bs, d)
    tm = 1024
    up_kw = dict(n_out=D_FF, swiglu=True, out_dtype=BF16)

    mod_a_p, mod_a_s = split_rows(_ada(c_all, w_ada[0], b_ada[0], 2 * d))

    hp, hs = _norm_rows(xp, xs, g_ffn1[0], mod_a_p, mod_a_s, 0, tm=tm)
    first, (hmid_s,), wb = _proj_first_tile(
        hp, [(w_up1[0], 0), (w_up1[0], up_blocks)], tm=tm, tn=512, side=hs, **up_kw)
    (hmid,), ((w_down_b,), (mod_b_all,), (w_in_b,)) = _proj_other_tiles(
        hp, wb, first, tm=tm, tn=512,
        streams=[_cast_stream(w_down1[0], 32),
                 _ada_stream(c_all, w_ada[0], b_ada[0], 2 * d, (N_MOD - 2) * d, 256),
                 _cast_stream(w_in_t, 32, start=32, rows=IN_MAIN)],
        **up_kw)
    mod_b_p, mod_b_s = split_rows(mod_b_all)
    (xp, hp), (xs, hs) = _resid([hmid], [hmid_s], [(w_down_b, 0)], xp, xs, mod_b_p, mod_b_s, 0,
                                g_mix[0], mod_b_p, mod_b_s, 1, factor=0.5, tm=256, emit_x=True,
                                h_dtype=BF16)

    (proj, dt_raw), (proj_s, dt_raw_s), _, ((w_out_b,),) = _proj(
        hp, [(w_in_b, 0)], n_out=IN_MAIN, tm=tm, tn=IN_MAIN // 3, swiglu=False, out_dtype=F32,
        row_tiles=(0, bp * seq // tm), side=hs, trans_w=True, w_extra=w_dt_t,
        streams=[_cast_stream(w_out[0], 16)])
    lconv = state_lru_conv[0].reshape(bs, (CONV_W - 1) * W_LRU)
    sconv = state_ssd_conv[0].reshape(bs, (CONV_W - 1) * SSD_CONV_DIM)
    out_l_s, lru_h_s, lconv_new, sconv_new, xs_act, xdt, bc, dec = _sample_pre(
        proj_s, dt_raw_s, state_lru_h[0], lconv, sconv, p)
    out_l, lru_h_p, ((ssm_s, y_raw),) = _lru_prompt(
        proj, bp, seq, p["lru_cw"], p["lru_cb"], p["lru_wg"], p["lru_ba"], p["lru_bi"],
        p["lru_lam"],
        streams=[_state_stream(state_ssm[0].reshape(bs, W_SSD, SSD_STATE), xdt, bc, dec)])
    y_ssd_s = _sample_post(y_raw, xs_act, proj_s, p["ssd_dexp"], p["ssd_ng"])
    y_ssd, ssm_p = _ssd_prompt(proj, dt_raw, bp, seq, p["ssd_cw"], p["ssd_cb"], p["ssd_dtb"],
                               p["ssd_alog"], p["ssd_dexp"], p["ssd_ng"])
    proj3 = proj.reshape(bp, seq, IN_MAIN)
    lru_buf_p = proj3[:, seq - (CONV_W - 1):, :W_LRU]
    ssd_buf_p = proj3[:, seq - (CONV_W - 1):, 2 * W_LRU + W_SSD:]

    (xp, hp), (xs, hs) = _resid([out_l, y_ssd], [out_l_s, y_ssd_s], [(w_out_b, 0), (w_out_b, 1)],
                                xp, xs, mod_b_p, mod_b_s, 3, g_ffn2[0], mod_b_p, mod_b_s, 4,
                                factor=1.0, tm=512, emit_x=True, h_dtype=BF16)

    first, (hmid_s,), wb = _proj_first_tile(
        hp, [(w_up2[0], 0), (w_up2[0], up_blocks)], tm=tm, tn=512, side=hs, **up_kw)
    (hmid,), ((w_down_b,), (modf_all,)) = _proj_other_tiles(
        hp, wb, first, tm=tm, tn=512,
        streams=[_cast_stream(w_down2[0], 32),
                 _ada_stream(c_all, w_ada_f, b_ada_f, 0, 2 * d, 256)],
        **up_kw)
    modf_p, modf_s = split_rows(modf_all)
    (yp,), (ys,) = _resid([hmid], [hmid_s], [(w_down_b, 0)], xp, xs, mod_b_p, mod_b_s, 6, g_final,
                          modf_p, modf_s, 0, factor=0.5, tm=256, emit_x=False, h_dtype=F32)

    stack = lambda v: v[None]
    return (yp.reshape(bp, seq, d), ys.reshape(bs, 1, d),
            stack(lru_h_p), stack(lru_buf_p), stack(ssm_p), stack(ssd_buf_p),
            stack(lru_h_s), stack(lconv_new.reshape(bs, CONV_W - 1, W_LRU)),
            stack(ssm_s.reshape(bs, SSD_HEADS, SSD_HEAD_DIM, SSD_STATE)),
            stack(sconv_new.reshape(bs, CONV_W - 1, SSD_CONV_DIM)))
```

```python
import functools
from typing import Callable, NamedTuple

import jax
import jax.numpy as jnp
from jax import lax
from jax.experimental import pallas as pl
from jax.experimental.pallas import tpu as pltpu

F32 = jnp.float32
BF16 = jnp.bfloat16

D_MODEL = 2048
D_FF = 5632
W_LRU = 1024
W_SSD = 1024
LRU_HEADS = 16
LRU_BLOCK = 64
LRU_C = 8.0
SSD_HEADS = 16
SSD_HEAD_DIM = 64
SSD_GROUPS = 2
SSD_HPG = 8
SSD_STATE = 128
SSD_CHUNK = 128
CONV_W = 4
SSD_CONV_DIM = W_SSD + 2 * SSD_GROUPS * SSD_STATE
IN_MAIN = 2 * W_LRU + W_SSD + SSD_CONV_DIM
N_MOD = 9
EPS = 1e-6

LANES = 128
SUBLANES = 8
VMEM_LIMIT_BYTES = 56 * 1024 * 1024

LRU_GATE_GROUP = 256
LRU_TIME_TILE = 256
SCAN_ROWS = 2 * SUBLANES


def _sigmoid(v):
    return 0.5 * (jnp.tanh(0.5 * v) + 1.0)


def _silu(v):
    return v * _sigmoid(v)


def _softplus(v):
    return jnp.maximum(v, 0.0) + jnp.log1p(jnp.exp(-jnp.abs(v)))


def _gelu_tanh(v):
    return 0.5 * v * (1.0 + jnp.tanh(0.7978845608028654 * (v + 0.044715 * (v * v * v))))


def _bdot(a, b):
    return jnp.dot(a, b, preferred_element_type=F32)


def _params(sem):
    return pltpu.CompilerParams(dimension_semantics=sem, vmem_limit_bytes=VMEM_LIMIT_BYTES)


def _ada_mm_kernel(s_ref, w_ref, b_ref, o_ref):
    o_ref[...] = _bdot(s_ref[...], w_ref[...].astype(BF16)) + b_ref[...]


def _ada_kernel(c_ref, w_ref, b_ref, o_ref, s_ref):
    s_ref[...] = _silu(c_ref[...]).astype(BF16)
    _ada_mm_kernel(s_ref, w_ref, b_ref, o_ref)


def _ada(c, w, b, cols, tn=1024):
    m, k = c.shape
    n = cols
    return pl.pallas_call(
        _ada_kernel,
        grid=(n // tn,),
        in_specs=[pl.BlockSpec((m, k), lambda j: (0, 0)),
                  pl.BlockSpec((k, tn), lambda j: (0, j)),
                  pl.BlockSpec((1, tn), lambda j: (0, j))],
        out_specs=[pl.BlockSpec((m, tn), lambda j: (0, j)), pl.BlockSpec((m, k), lambda j: (0, 0))],
        out_shape=[jax.ShapeDtypeStruct((m, n), F32), jax.ShapeDtypeStruct((m, k), BF16)],
        compiler_params=_params(("arbitrary",)),
        name="ada_proj",
    )(c, w, b.reshape(1, -1))


def _norm_modulate(x, gain, shift, scale):
    ms = jnp.mean(x * x, axis=-1, keepdims=True)
    y = x * lax.rsqrt(ms + EPS) * gain
    return y * (1.0 + scale) + shift


def _norm_rows_kernel(x_ref, gain_ref, sh_ref, sc_ref, xs_ref, shs_ref, scs_ref, o_ref, os_ref):
    o_ref[...] = _norm_modulate(x_ref[...], gain_ref[...], sh_ref[...],
                                sc_ref[...]).astype(o_ref.dtype)

    @pl.when(pl.program_id(0) == 0)
    def _():
        os_ref[...] = _norm_modulate(xs_ref[...], gain_ref[...], shs_ref[...],
                                     scs_ref[...]).astype(os_ref.dtype)


def _norm_rows(x, x_s, gain, mod, mod_s, shift_chunk, *, tm):
    m, d = x.shape
    ns = x_s.shape[0]
    tiles_per_group = (m // tm) // mod.shape[0]
    once = dict(pipeline_mode=pl.Buffered(1))
    mod_spec = lambda c: pl.BlockSpec((None, 1, d), lambda i: (i // tiles_per_group, 0, c))
    mod_s_spec = lambda c: pl.BlockSpec((None, ns, d), lambda i: (0, 0, c), **once)
    return pl.pallas_call(
        _norm_rows_kernel,
        grid=(m // tm,),
        in_specs=[pl.BlockSpec((tm, d), lambda i: (i, 0)), pl.BlockSpec((1, d), lambda i: (0, 0)),
                  mod_spec(shift_chunk), mod_spec(shift_chunk + 1),
                  pl.BlockSpec((ns, d), lambda i: (0, 0), **once),
                  mod_s_spec(shift_chunk), mod_s_spec(shift_chunk + 1)],
        out_specs=[pl.BlockSpec((tm, d), lambda i: (i, 0)), pl.BlockSpec((ns, d), lambda i: (0, 0))],
        out_shape=[jax.ShapeDtypeStruct((m, d), BF16), jax.ShapeDtypeStruct((ns, d), BF16)],
        compiler_params=_params(("arbitrary",)),
        name="norm_rows",
    )(x, gain.reshape(1, d), mod, mod, x_s, mod_s, mod_s)


class _Stream(NamedTuple):
    ins: list
    outs: list
    body: Callable
    start: int
    steps: int


def _stream_io(streams, n_inner):
    in_specs, args, out_specs, out_shape = [], [], [], []

    def spec(st, block, index_fn):
        return pl.BlockSpec(
            block, lambda i, j: index_fn(jnp.clip(i * n_inner + j - st.start, 0, st.steps - 1)))

    for st in streams:
        for arr, block, index_fn in st.ins:
            in_specs.append(spec(st, block, index_fn))
            args.append(arr)
        for shape, block, index_fn in st.outs:
            out_specs.append(spec(st, block, index_fn))
            out_shape.append(shape)
    return in_specs, args, out_specs, out_shape


def _run_streams(streams, in_refs, out_refs, n_inner, n_steps):
    step = pl.program_id(0) * n_inner + pl.program_id(1)
    in_refs, out_refs = iter(in_refs), iter(out_refs)
    for st in streams:
        ins = [next(in_refs) for _ in st.ins]
        outs = [next(out_refs) for _ in st.outs]
        if st.start == 0 and st.steps == n_steps:
            st.body(*ins, *outs)
            continue

        @pl.when((step >= st.start) & (step < st.start + st.steps))
        def _(st=st, ins=ins, outs=outs):
            st.body(*ins, *outs)


def _split_stream_outs(streams, flat):
    flat = list(flat)
    return [[flat.pop(0) for _ in st.outs] for st in streams]


def _cast_body(src_ref, dst_ref):
    dst_ref[...] = src_ref[...].astype(dst_ref.dtype)


def _cast_stream(w, chunks, start=0, rows=None):
    rows = w.shape[0] if rows is None else rows
    block = (rows // chunks, w.shape[1])
    index = lambda k: (k, 0)
    return _Stream([(w, block, index)],
                   [(jax.ShapeDtypeStruct((rows, w.shape[1]), BF16), block, index)],
                   _cast_body, start, chunks)


def _ada_stream(sc, w, b, col0, cols, tn, start=0):
    m, k = sc.shape
    t0 = col0 // tn
    return _Stream(
        [(sc, (m, k), lambda s: (0, 0)), (w, (k, tn), lambda s: (0, s + t0)),
         (b.reshape(1, -1), (1, tn), lambda s: (0, s + t0))],
        [(jax.ShapeDtypeStruct((m, cols), F32), (m, tn), lambda s: (0, s))],
        _ada_mm_kernel, start, cols // tn)


def _wdot(h, w, trans_w):
    if trans_w:
        return lax.dot_general(h, w, (((1,), (1,)), ((), ())), preferred_element_type=F32)
    return _bdot(h, w)


def _proj_kernel(*refs, n_w, swiglu, trans_w, has_extra, has_side, emit_bf16, streams, n_prev,
                 nj, n_steps):
    it = iter(refs)
    x_ref = next(it)
    xs_ref = next(it) if has_side else None
    w_refs = [next(it) for _ in range(n_w)]
    wx_ref = next(it) if has_extra else None
    stream_ins = [next(it) for st in streams for _ in st.ins]
    for _ in range(n_prev):
        next(it)
    o_ref = next(it)
    ox_ref = next(it) if has_extra else None
    os_ref = next(it) if has_side else None
    osx_ref = next(it) if has_side and has_extra else None
    wo_refs = [next(it) for _ in range(n_w)] if emit_bf16 else []
    stream_outs = [next(it) for st in streams for _ in st.outs]

    j = pl.program_id(1)
    first_tile = pl.program_id(0) == 0

    if has_extra:
        @pl.when(j == 0)
        def _():
            ox_ref[...] = _wdot(x_ref[...], wx_ref[...].astype(BF16), trans_w)

        if has_side:
            @pl.when((j == 0) & first_tile)
            def _():
                osx_ref[...] = _wdot(xs_ref[...], wx_ref[...].astype(BF16), trans_w)

    wbs = [w_ref[...].astype(BF16) for w_ref in w_refs]
    for wo_ref, wb in zip(wo_refs, wbs):
        wo_ref[...] = wb

    def project(h, out_ref):
        if swiglu:
            g = _wdot(h, wbs[0], trans_w)
            u = _wdot(h, wbs[1], trans_w)
            out_ref[...] = (_silu(g) * u).astype(out_ref.dtype)
        else:
            out_ref[...] = _wdot(h, wbs[0], trans_w).astype(out_ref.dtype)

    project(x_ref[...], o_ref)
    if has_side:
        @pl.when(first_tile)
        def _():
            project(xs_ref[...], os_ref)

    _run_streams(streams, stream_ins, stream_outs, nj, n_steps)


def _proj(x, ws, *, n_out, tm, tn, swiglu, out_dtype, row_tiles, side=None, trans_w=False,
          w_extra=None, emit_bf16=False, streams=(), prev=None):
    m, d = x.shape
    t0, t1 = row_tiles
    nj = n_out // tn
    has_side = side is not None
    has_extra = w_extra is not None
    single_row_tile = t1 - t0 == 1
    once = dict(pipeline_mode=pl.Buffered(1))

    def w_spec(off):
        if trans_w:
            return pl.BlockSpec((tn, d), lambda i, j: (j + off, 0))
        return pl.BlockSpec((d, tn), lambda i, j: (0, j + off))

    x_mode = once if single_row_tile else {}
    in_specs = [pl.BlockSpec((tm, d), lambda i, j: (i + t0, 0), **x_mode)]
    args = [x]
    if has_side:
        ns = side.shape[0]
        in_specs.append(pl.BlockSpec((ns, d), lambda i, j: (0, 0), **once))
        args.append(side)
    in_specs += [w_spec(off) for _, off in ws]
    args += [w for w, _ in ws]
    if has_extra:
        nx = w_extra.shape[0] if trans_w else w_extra.shape[1]
        in_specs.append(pl.BlockSpec(w_extra.shape, lambda i, j: (0, 0)))
        args.append(w_extra)
    assert all(st.start + st.steps <= (t1 - t0) * nj for st in streams)
    st_in_specs, st_args, st_out_specs, st_out_shape = _stream_io(streams, nj)
    in_specs += st_in_specs
    args += st_args
    prev = list(prev or [])
    aliases = {}
    for k, buf in enumerate(prev):
        aliases[len(args)] = k
        in_specs.append(pl.BlockSpec(memory_space=pl.ANY))
        args.append(buf)

    out_specs = [pl.BlockSpec((tm, tn), lambda i, j: (i + t0, j))]
    out_shape = [jax.ShapeDtypeStruct((m, n_out), out_dtype)]
    if has_extra:
        out_specs.append(pl.BlockSpec((tm, nx), lambda i, j: (i + t0, 0)))
        out_shape.append(jax.ShapeDtypeStruct((m, nx), F32))
    n_main = len(out_shape)
    if has_side:
        out_specs.append(pl.BlockSpec((ns, tn), lambda i, j: (0, jnp.where(i == 0, j, nj - 1))))
        out_shape.append(jax.ShapeDtypeStruct((ns, n_out), out_dtype))
        if has_extra:
            out_specs.append(pl.BlockSpec((ns, nx), lambda i, j: (0, 0)))
            out_shape.append(jax.ShapeDtypeStruct((ns, nx), F32))
    n_side = len(out_shape) - n_main
    if emit_bf16:
        assert single_row_tile, "weight copies are written once per column tile"
        for _ in ws:
            out_specs.append(w_spec(0))
            out_shape.append(jax.ShapeDtypeStruct((n_out, d) if trans_w else (d, n_out), BF16))
    out_specs += st_out_specs
    out_shape += st_out_shape
    outs = pl.pallas_call(
        functools.partial(_proj_kernel, n_w=len(ws), swiglu=swiglu, trans_w=trans_w,
                          has_extra=has_extra, has_side=has_side, emit_bf16=emit_bf16,
                          streams=tuple(streams), n_prev=len(prev), nj=nj,
                          n_steps=(t1 - t0) * nj),
        grid=(t1 - t0, nj),
        in_specs=in_specs,
        out_specs=out_specs,
        out_shape=out_shape,
        input_output_aliases=aliases,
        compiler_params=_params(("arbitrary" if streams or has_side else "parallel", "arbitrary")),
        name="proj_swiglu" if swiglu else "proj",
    )(*args)
    n_wb = len(ws) if emit_bf16 else 0
    main, rest = outs[:n_main], outs[n_main:]
    side_outs, rest = rest[:n_side], rest[n_side:]
    wb, rest = rest[:n_wb], rest[n_wb:]
    return main, side_outs, wb, _split_stream_outs(streams, rest)


def _proj_first_tile(x, ws_f32, *, tm, tn, side, **kw):
    main, side_outs, wb, _ = _proj(x, ws_f32, tm=tm, tn=tn, row_tiles=(0, 1), side=side,
                                   emit_bf16=True, **kw)
    return main, side_outs, wb


def _proj_other_tiles(x, wb, prev, *, tm, tn, streams, **kw):
    main, _, _, stream_outs = _proj(x, [(w, 0) for w in wb], tm=tm, tn=tn,
                                    row_tiles=(1, x.shape[0] // tm), prev=prev, streams=streams,
                                    **kw)
    return main, stream_outs


def _resid_kernel(*refs, n_lhs, factor, emit_x):
    it = iter(refs)
    lhs_refs = [next(it) for _ in range(n_lhs)]
    lhs_s_refs = [next(it) for _ in range(n_lhs)]
    w_refs = [next(it) for _ in range(n_lhs)]
    x_ref, gate_ref, gain_ref, sh_ref, sc_ref = (next(it) for _ in range(5))
    xs_ref, gate_s_ref, sh_s_ref, sc_s_ref = (next(it) for _ in range(4))
    n_out = 2 if emit_x else 1
    outs = [next(it) for _ in range(n_out)]
    outs_s = [next(it) for _ in range(n_out)]

    def update(lhs, x_in, gate, sh, sc, out_refs):
        acc = _bdot(lhs[0][...], w_refs[0][...])
        for l_ref, w_ref in zip(lhs[1:], w_refs[1:]):
            acc = acc + _bdot(l_ref[...], w_ref[...])
        x_new = x_in[...] + (factor * gate[...]) * acc
        if emit_x:
            out_refs[0][...] = x_new
        h_ref = out_refs[-1]
        h_ref[...] = _norm_modulate(x_new, gain_ref[...], sh[...], sc[...]).astype(h_ref.dtype)

    update(lhs_refs, x_ref, gate_ref, sh_ref, sc_ref, outs)

    @pl.when(pl.program_id(0) == 0)
    def _():
        update(lhs_s_refs, xs_ref, gate_s_ref, sh_s_ref, sc_s_ref, outs_s)


def _resid(lhs_list, lhs_s_list, ws, x, x_s, mod, mod_s, gate_chunk, gain_next, mod_next,
           mod_next_s, shift_chunk_next, *, factor, tm, emit_x, h_dtype):
    m, d = x.shape
    ns = x_s.shape[0]
    groups = mod.shape[0]
    tiles_per_group = (m // tm) // groups
    kp = lhs_list[0].shape[1]
    once = dict(pipeline_mode=pl.Buffered(1))

    def mod_spec(chunk):
        return pl.BlockSpec((None, 1, d), lambda i: (i // tiles_per_group, 0, chunk))

    def mod_s_spec(chunk):
        return pl.BlockSpec((None, ns, d), lambda i: (0, 0, chunk), **once)

    in_specs = [pl.BlockSpec((tm, kp), lambda i: (i, 0)) for _ in lhs_list]
    in_specs += [pl.BlockSpec((ns, kp), lambda i: (0, 0), **once) for _ in lhs_s_list]
    in_specs += [pl.BlockSpec((kp, d), lambda i, k=k: (k, 0), **once) for _, k in ws]
    in_specs += [pl.BlockSpec((tm, d), lambda i: (i, 0)), mod_spec(gate_chunk),
                 pl.BlockSpec((1, d), lambda i: (0, 0)),
                 mod_spec(shift_chunk_next), mod_spec(shift_chunk_next + 1),
                 pl.BlockSpec((ns, d), lambda i: (0, 0), **once), mod_s_spec(gate_chunk),
                 mod_s_spec(shift_chunk_next), mod_s_spec(shift_chunk_next + 1)]
    row = pl.BlockSpec((tm, d), lambda i: (i, 0))
    row_s = pl.BlockSpec((ns, d), lambda i: (0, 0))
    dtypes = ([F32] if emit_x else []) + [h_dtype]
    out_specs = [row for _ in dtypes] + [row_s for _ in dtypes]
    out_shape = ([jax.ShapeDtypeStruct((m, d), t) for t in dtypes]
                 + [jax.ShapeDtypeStruct((ns, d), t) for t in dtypes])
    outs = pl.pallas_call(
        functools.partial(_resid_kernel, n_lhs=len(lhs_list), factor=factor, emit_x=emit_x),
        grid=(m // tm,),
        in_specs=in_specs,
        out_specs=out_specs,
        out_shape=out_shape,
        compiler_params=_params(("arbitrary",)),
        name="resid",
    )(*lhs_list, *lhs_s_list, *[w for w, _ in ws], x, mod, gain_next.reshape(1, d), mod_next,
      mod_next, x_s, mod_s, mod_next_s, mod_next_s)
    return outs[:len(dtypes)], outs[len(dtypes):]


def _lru_gates(xc, wg_ref, ba, bi, sp):
    a_parts, b_parts = [], []
    for g in range(W_LRU // LRU_GATE_GROUP):
        cols = slice(g * LRU_GATE_GROUP, (g + 1) * LRU_GATE_GROUP)
        xg = xc[:, cols]
        ri = _bdot(xg.astype(BF16), wg_ref[g].astype(BF16))
        r = _sigmoid(ri[:, :LRU_GATE_GROUP] + ba[:, cols])
        i = _sigmoid(ri[:, LRU_GATE_GROUP:] + bi[:, cols])
        log_a = (-LRU_C * r) * sp[:, cols]
        a = jnp.exp(log_a)
        a_parts.append(a)
        b_parts.append(jnp.sqrt(1.0 - a * a) * (i * xg))
    return jnp.concatenate(a_parts, axis=1), jnp.concatenate(b_parts, axis=1)


def _causal_conv(prev8, x, w_ref, b_ref):
    rows, width = x.shape
    rid = lax.broadcasted_iota(jnp.int32, (SUBLANES, width), 0)
    shifts = (1, 2, 3)
    taps = [w_ref[k:k + 1, :] for k in range(CONV_W)]
    bias = b_ref[...]
    prev_rot = [pltpu.roll(prev8, k, 0) for k in shifts]
    out = []
    for r in range(rows // SUBLANES):
        cur = x[r * SUBLANES:(r + 1) * SUBLANES, :]
        cur_rot = [pltpu.roll(cur, k, 0) for k in shifts]
        s1, s2, s3 = [jnp.where(rid < k, p, c) for k, p, c in zip(shifts, prev_rot, cur_rot)]
        out.append(bias + taps[0] * s3 + taps[1] * s2 + taps[2] * s1 + taps[3] * cur)
        prev_rot = cur_rot
    return jnp.concatenate(out, axis=0)


def _lru_prompt_kernel(*refs, streams, nt, n_steps):
    n_in = sum(len(st.ins) for st in streams)
    n_out = sum(len(st.outs) for st in streams)
    xl_ref, gl_ref, cw_ref, cb_ref, wg_ref, ba_ref, bi_ref, lam_ref = refs[:8]
    stream_ins = refs[8:8 + n_in]
    o_ref, hT_ref = refs[8 + n_in:10 + n_in]
    stream_outs = refs[10 + n_in:10 + n_in + n_out]
    xbuf, a_scr, b_scr, hcar = refs[10 + n_in + n_out:]
    t = pl.program_id(1)
    tt = xl_ref.shape[0]

    @pl.when(t == 0)
    def _():
        xbuf[...] = jnp.zeros_like(xbuf)
        hcar[...] = jnp.zeros_like(hcar)

    _run_streams(streams, stream_ins, stream_outs, nt, n_steps)

    x = xl_ref[...]
    xc = _causal_conv(xbuf[...], x, cw_ref, cb_ref)
    xbuf[...] = x[tt - SUBLANES:, :]

    sp = _softplus(-lam_ref[...])
    a, bt = _lru_gates(xc, wg_ref, ba_ref[...], bi_ref[...], sp)
    a_scr[...] = a
    b_scr[...] = bt

    rid = lax.broadcasted_iota(jnp.int32, (SUBLANES, W_LRU), 0)

    def scan8(a8, b8, h_in):
        for s in (1, 2, 4):
            a_sh = pltpu.roll(a8, s, 0)
            b_sh = pltpu.roll(b8, s, 0)
            m = rid >= s
            b8 = jnp.where(m, a8 * b_sh + b8, b8)
            a8 = jnp.where(m, a8 * a_sh, a8)
        h8 = a8 * h_in + b8
        return h8, jnp.broadcast_to(h8[SUBLANES - 1:SUBLANES, :], (SUBLANES, W_LRU))

    def body(g, h_in):
        r0 = pl.multiple_of(g * SCAN_ROWS, SCAN_ROWS)
        lo = pl.ds(r0, SUBLANES)
        hi = pl.ds(r0 + SUBLANES, SUBLANES)
        h_lo, h_mid = scan8(a_scr[lo, :], b_scr[lo, :], h_in)
        h_hi, h_out = scan8(a_scr[hi, :], b_scr[hi, :], h_mid)
        rows = pl.ds(r0, SCAN_ROWS)
        h16 = jnp.concatenate([h_lo, h_hi], axis=0)
        o_ref[rows, :] = (h16 * _gelu_tanh(gl_ref[rows, :])).astype(o_ref.dtype)
        return h_out

    h_last = lax.fori_loop(0, tt // SCAN_ROWS, body, hcar[...])
    hcar[...] = h_last

    @pl.when(t == pl.num_programs(1) - 1)
    def _():
        hT_ref[...] = h_last[0:1, :]


def _lru_prompt(proj, batch, seq, cw, cb, wg, ba, bi, lam, streams=()):
    tt = LRU_TIME_TILE
    nt = seq // tt
    assert all(st.start + st.steps <= batch * nt for st in streams)
    st_in_specs, st_args, st_out_specs, st_out_shape = _stream_io(streams, nt)
    row = lambda v: v.reshape(1, W_LRU)
    full = lambda shape: pl.BlockSpec(shape, lambda b, t: (0,) * len(shape))
    out, h_t, *rest = pl.pallas_call(
        functools.partial(_lru_prompt_kernel, streams=tuple(streams), nt=nt, n_steps=batch * nt),
        grid=(batch, nt),
        in_specs=[pl.BlockSpec((tt, W_LRU), lambda b, t: (b * nt + t, 0)),
                  pl.BlockSpec((tt, W_LRU), lambda b, t: (b * nt + t, 1)),
                  full((CONV_W, W_LRU)), full((1, W_LRU)), full(wg.shape),
                  full((1, W_LRU)), full((1, W_LRU)), full((1, W_LRU))] + st_in_specs,
        out_specs=[pl.BlockSpec((tt, W_LRU), lambda b, t: (b * nt + t, 0)),
                   pl.BlockSpec((None, 1, W_LRU), lambda b, t: (b, 0, 0))] + st_out_specs,
        out_shape=[jax.ShapeDtypeStruct((batch * seq, W_LRU), BF16),
                   jax.ShapeDtypeStruct((batch, 1, W_LRU), F32)] + st_out_shape,
        scratch_shapes=[pltpu.VMEM((SUBLANES, W_LRU), F32),
                        pltpu.VMEM((tt, W_LRU), F32),
                        pltpu.VMEM((tt, W_LRU), F32),
                        pltpu.VMEM((SUBLANES, W_LRU), F32)],
        compiler_params=_params(("arbitrary" if streams else "parallel", "arbitrary")),
        name="lru_prompt",
    )(proj, proj, cw, row(cb), wg, row(ba), row(bi), row(lam), *st_args)
    return out, h_t.reshape(batch, W_LRU), _split_stream_outs(streams, rest)


def _ssd_prompt_kernel(*refs, streams, nc, n_steps):
    n_in = sum(len(st.ins) for st in streams)
    n_out = sum(len(st.outs) for st in streams)
    z_ref, xbc_ref, dt_ref, cw_ref, cb_ref, dtb_ref, alog_ref, dexp_ref, ng_ref = refs[:9]
    stream_ins = refs[9:9 + n_in]
    y_ref, st_ref = refs[9 + n_in:11 + n_in]
    stream_outs = refs[11 + n_in:11 + n_in + n_out]
    xbuf, st_scr, y_scr, m_scr, xbd_scr = refs[11 + n_in + n_out:]
    c = pl.program_id(1)
    lc = SSD_CHUNK

    @pl.when(c == 0)
    def _():
        xbuf[...] = jnp.zeros_like(xbuf)
        st_scr[...] = jnp.zeros_like(st_scr)
        xbd_scr[...] = jnp.zeros_like(xbd_scr)

    _run_streams(streams, stream_ins, stream_outs, nc, n_steps)

    x = xbc_ref[...]
    act = _silu(_causal_conv(xbuf[...], x, cw_ref, cb_ref))
    xbuf[...] = x[lc - SUBLANES:, :]
    xs = act[:, :W_SSD]
    bm = act[:, W_SSD:W_SSD + SSD_GROUPS * SSD_STATE]
    cm = act[:, W_SSD + SSD_GROUPS * SSD_STATE:]

    dt = _softplus(dt_ref[...] + dtb_ref[...])
    d_a = dt * (-jnp.exp(alog_ref[...]))
    row_i = lax.broadcasted_iota(jnp.int32, (lc, lc), 0)
    col_i = lax.broadcasted_iota(jnp.int32, (lc, lc), 1)
    causal = row_i >= col_i
    tril = jnp.where(causal, 1.0, 0.0).astype(F32)
    cs = jnp.dot(tril, d_a, preferred_element_type=F32, precision=lax.Precision.HIGHEST)
    cs_t = cs.T
    dt_t = dt.T
    cs_last = cs[lc - 1:lc, :]

    def per_head_lanes(v):
        rows = v.shape[0]
        return jnp.concatenate(
            [jnp.broadcast_to(v[:, h:h + 1], (rows, SSD_HEAD_DIM)) for h in range(SSD_HEADS)],
            axis=1)

    w_exp = per_head_lanes(jnp.exp(cs_last - cs) * dt)
    ecs_exp = per_head_lanes(jnp.exp(cs))
    cd_exp = per_head_lanes(jnp.exp(cs_last))
    gw = SSD_HPG * SSD_HEAD_DIM
    low_half = col_i < SSD_HEAD_DIM

    for g in range(SSD_GROUPS):
        ncols = slice(g * SSD_STATE, (g + 1) * SSD_STATE)
        gcols = slice(g * gw, (g + 1) * gw)
        b_g = bm[:, ncols].astype(BF16)
        c_g = cm[:, ncols].astype(BF16)
        cb_mat = lax.dot_general(c_g, b_g, (((1,), (1,)), ((), ())),
                                 preferred_element_type=F32)
        for e in range(SSD_HPG):
            h = g * SSD_HPG + e
            cs_col = jnp.broadcast_to(cs[:, h:h + 1], (lc, lc))
            l_mat = jnp.exp(jnp.where(causal, cs_col - cs_t[h:h + 1, :], -jnp.inf))
            m_scr[g, :, e * lc:(e + 1) * lc] = (cb_mat * l_mat * dt_t[h:h + 1, :]).astype(BF16)
        for q in range(SSD_HPG // 2):
            lanes = slice(q * LANES, (q + 1) * LANES)
            slab = xs[:, g * gw + q * LANES:g * gw + (q + 1) * LANES]
            xbd_scr[g, (2 * q) * lc:(2 * q + 1) * lc, lanes] = jnp.where(
                low_half, slab, 0.0).astype(BF16)
            xbd_scr[g, (2 * q + 1) * lc:(2 * q + 2) * lc, lanes] = jnp.where(
                low_half, 0.0, slab).astype(BF16)
        st_g = st_scr[:, gcols]
        y_off = _bdot(c_g, st_g.astype(BF16)) * ecs_exp[:, gcols]
        y_scr[:, gcols] = (_bdot(m_scr[g], xbd_scr[g]) + y_off
                           + dexp_ref[:, gcols] * xs[:, gcols])
        xw = (xs[:, gcols] * w_exp[:, gcols]).astype(BF16)
        st_scr[:, gcols] = cd_exp[:, gcols] * st_g + lax.dot_general(
            b_g, xw, (((0,), (0,)), ((), ())), preferred_element_type=F32)

    yg = y_scr[...] * _silu(z_ref[...])
    ms = jnp.mean(yg * yg, axis=-1, keepdims=True)
    y_ref[...] = (yg * lax.rsqrt(ms + EPS) * ng_ref[...]).astype(y_ref.dtype)

    @pl.when(c == pl.num_programs(1) - 1)
    def _():
        st_ref[...] = st_scr[...].T


def _ssd_prompt(proj, dt_raw, batch, seq, cw, cb, dtb, alog, dexp, ng, streams=()):
    lc = SSD_CHUNK
    nc = seq // lc
    assert all(st.start + st.steps <= batch * nc for st in streams)
    st_in_specs, st_args, st_out_specs, st_out_shape = _stream_io(streams, nc)
    full = lambda shape: pl.BlockSpec(shape, lambda b, c: (0,) * len(shape))
    z_blk = (2 * W_LRU) // W_SSD
    xbc_blk = (2 * W_LRU + W_SSD) // SSD_CONV_DIM
    y, st, *rest = pl.pallas_call(
        functools.partial(_ssd_prompt_kernel, streams=tuple(streams), nc=nc, n_steps=batch * nc),
        grid=(batch, nc),
        in_specs=[pl.BlockSpec((lc, W_SSD), lambda b, c: (b * nc + c, z_blk)),
                  pl.BlockSpec((lc, SSD_CONV_DIM), lambda b, c: (b * nc + c, xbc_blk)),
                  pl.BlockSpec((lc, LANES), lambda b, c: (b * nc + c, 0)),
                  full((CONV_W, SSD_CONV_DIM)), full((1, SSD_CONV_DIM)),
                  full((1, LANES)), full((1, LANES)), full((1, W_SSD)), full((1, W_SSD))]
        + st_in_specs,
        out_specs=[pl.BlockSpec((lc, W_SSD), lambda b, c: (b * nc + c, 0)),
                   pl.BlockSpec((None, W_SSD, SSD_STATE), lambda b, c: (b, 0, 0))] + st_out_specs,
        out_shape=[jax.ShapeDtypeStruct((batch * seq, W_SSD), BF16),
                   jax.ShapeDtypeStruct((batch, W_SSD, SSD_STATE), F32)] + st_out_shape,
        scratch_shapes=[pltpu.VMEM((SUBLANES, SSD_CONV_DIM), F32),
                        pltpu.VMEM((SSD_STATE, W_SSD), F32),
                        pltpu.VMEM((lc, W_SSD), F32),
                        pltpu.VMEM((SSD_GROUPS, lc, SSD_HPG * lc), BF16),
                        pltpu.VMEM((SSD_GROUPS, SSD_HPG * lc, SSD_HPG * SSD_HEAD_DIM), BF16)],
        compiler_params=_params(("arbitrary" if streams else "parallel", "arbitrary")),
        name="ssd_prompt",
    )(proj, proj, dt_raw, cw, cb, dtb, alog, dexp, ng, *st_args)
    return (y, st.reshape(batch, SSD_HEADS, SSD_HEAD_DIM, SSD_STATE),
            _split_stream_outs(streams, rest))


def _sample_pre_kernel(proj_ref, dt_ref, h0_ref, lconv_ref, sconv_ref,
                       lcw_ref, lcb_ref, wg_ref, ba_ref, bi_ref, lam_ref,
                       scw_ref, scb_ref, dtb_ref, alog_ref,
                       outl_ref, hnew_ref, lconv_new_ref, sconv_new_ref,
                       xs_ref, xdt_ref, bc_ref, dec_ref):
    nb = proj_ref.shape[0]
    xl = proj_ref[:, 0:W_LRU]
    gl = proj_ref[:, W_LRU:2 * W_LRU]
    xbc = proj_ref[:, 2 * W_LRU + W_SSD:IN_MAIN]

    def conv1(state_ref, width, x_new, w_ref, b_ref):
        y = b_ref[...] + w_ref[0:1, :] * state_ref[:, 0:width]
        y = y + w_ref[1:2, :] * state_ref[:, width:2 * width]
        y = y + w_ref[2:3, :] * state_ref[:, 2 * width:3 * width]
        return y + w_ref[3:4, :] * x_new

    xc = conv1(lconv_ref, W_LRU, xl, lcw_ref, lcb_ref)
    a, bt = _lru_gates(xc, wg_ref, ba_ref[...], bi_ref[...], _softplus(-lam_ref[...]))
    h_new = a * h0_ref[...] + bt
    hnew_ref[...] = h_new
    outl_ref[...] = (h_new * _gelu_tanh(gl)).astype(outl_ref.dtype)
    lconv_new_ref[:, 0:2 * W_LRU] = lconv_ref[:, W_LRU:3 * W_LRU]
    lconv_new_ref[:, 2 * W_LRU:3 * W_LRU] = xl

    act = _silu(conv1(sconv_ref, SSD_CONV_DIM, xbc, scw_ref, scb_ref))
    sconv_new_ref[:, 0:2 * SSD_CONV_DIM] = sconv_ref[:, SSD_CONV_DIM:3 * SSD_CONV_DIM]
    sconv_new_ref[:, 2 * SSD_CONV_DIM:3 * SSD_CONV_DIM] = xbc
    xs = act[:, :W_SSD]
    xs_ref[...] = xs
    bc_ref[...] = act[:, W_SSD:]
    dt = _softplus(dt_ref[...] + dtb_ref[...])
    dec = jnp.exp(dt * (-jnp.exp(alog_ref[...])))
    for h in range(SSD_HEADS):
        pcols = slice(h * SSD_HEAD_DIM, (h + 1) * SSD_HEAD_DIM)
        xdt_ref[:, pcols] = xs[:, pcols] * jnp.broadcast_to(dt[:, h:h + 1], (nb, SSD_HEAD_DIM))
        dec_ref[h] = jnp.broadcast_to(dec[:, h:h + 1], (nb, SSD_STATE))


def _sample_pre(proj, dt_raw, h0, lconv, sconv, p):
    nb = proj.shape[0]
    out_shape = [jax.ShapeDtypeStruct((nb, W_LRU), BF16),
                 jax.ShapeDtypeStruct((nb, W_LRU), F32),
                 jax.ShapeDtypeStruct((nb, 3 * W_LRU), F32),
                 jax.ShapeDtypeStruct((nb, 3 * SSD_CONV_DIM), F32),
                 jax.ShapeDtypeStruct((nb, W_SSD), F32),
                 jax.ShapeDtypeStruct((nb, W_SSD), F32),
                 jax.ShapeDtypeStruct((nb, 2 * SSD_GROUPS * SSD_STATE), F32),
                 jax.ShapeDtypeStruct((SSD_HEADS, nb, SSD_STATE), F32)]
    return pl.pallas_call(
        _sample_pre_kernel,
        out_shape=out_shape,
        compiler_params=pltpu.CompilerParams(vmem_limit_bytes=VMEM_LIMIT_BYTES),
        name="sample_pre",
    )(proj, dt_raw, h0, lconv, sconv,
      p["lru_cw"], p["lru_cb"], p["lru_wg"], p["lru_ba"], p["lru_bi"], p["lru_lam"],
      p["ssd_cw"], p["ssd_cb"], p["ssd_dtb"], p["ssd_alog"])


def _sample_state_kernel(s_ref, xdt_ref, bc_ref, dec_ref, o_ref, y_ref):
    bb = s_ref.shape[0]
    half = SSD_HPG * SSD_HEAD_DIM
    rid = lax.broadcasted_iota(jnp.int32, (bb, W_SSD), 0)
    xdt = xdt_ref[...]
    bcb = bc_ref[...].astype(BF16)
    for k in range(bb):
        xk = jnp.where(rid == k, xdt, 0.0).astype(BF16)
        for g in range(SSD_GROUPS):
            rows = slice(g * half, (g + 1) * half)
            b_g = bcb[:, g * SSD_STATE:(g + 1) * SSD_STATE]
            c_g = bcb[:, (SSD_GROUPS + g) * SSD_STATE:(SSD_GROUPS + g + 1) * SSD_STATE]
            outer = lax.dot_general(xk[:, rows], b_g, (((0,), (0,)), ((), ())),
                                    preferred_element_type=F32)
            dec = jnp.concatenate(
                [jnp.broadcast_to(dec_ref[g * SSD_HPG + e, k:k + 1, :], (SSD_HEAD_DIM, SSD_STATE))
                 for e in range(SSD_HPG)], axis=0)
            s_new = dec * s_ref[k, rows, :] + outer
            o_ref[k, rows, :] = s_new
            yk = lax.dot_general(c_g, s_new.astype(BF16), (((1,), (1,)), ((), ())),
                                 preferred_element_type=F32)
            y_ref[k:k + 1, rows] = yk[k:k + 1, :]


def _state_stream(ssm, xdt, bc, dec, bb=8, start=0):
    nb = ssm.shape[0]
    state_block = (bb, W_SSD, SSD_STATE)
    return _Stream(
        [(ssm, state_block, lambda k: (k, 0, 0)),
         (xdt, (bb, W_SSD), lambda k: (k, 0)),
         (bc, (bb, 2 * SSD_GROUPS * SSD_STATE), lambda k: (k, 0)),
         (dec, (SSD_HEADS, bb, SSD_STATE), lambda k: (0, k, 0))],
        [(jax.ShapeDtypeStruct(ssm.shape, F32), state_block, lambda k: (k, 0, 0)),
         (jax.ShapeDtypeStruct((nb, W_SSD), F32), (bb, W_SSD), lambda k: (k, 0))],
        _sample_state_kernel, start, nb // bb)


def _sample_post_kernel(y_ref, xs_ref, proj_ref, dexp_ref, ng_ref, o_ref):
    z = proj_ref[:, 2 * W_LRU:2 * W_LRU + W_SSD]
    yg = (y_ref[...] + dexp_ref[...] * xs_ref[...]) * _silu(z)
    ms = jnp.mean(yg * yg, axis=-1, keepdims=True)
    o_ref[...] = (yg * lax.rsqrt(ms + EPS) * ng_ref[...]).astype(o_ref.dtype)


def _sample_post(y_raw, xs, proj, dexp, ng):
    return pl.pallas_call(
        _sample_post_kernel,
        out_shape=jax.ShapeDtypeStruct(y_raw.shape, BF16),
        compiler_params=pltpu.CompilerParams(vmem_limit_bytes=VMEM_LIMIT_BYTES),
        name="sample_post",
    )(y_raw, xs, proj, dexp, ng)


def _block_diag_groups(w):
    per = LRU_GATE_GROUP // LRU_BLOCK
    w4 = w.reshape(LRU_HEADS // per, per, LRU_BLOCK, LRU_BLOCK)
    bd = jnp.einsum("ghij,hk->ghikj", w4, jnp.eye(per, dtype=w.dtype))
    return bd.reshape(LRU_HEADS // per, LRU_GATE_GROUP, LRU_GATE_GROUP)


def _pad_lanes(v):
    v = v.reshape(1, -1)
    return jnp.pad(v, ((0, 0), (0, LANES - v.shape[1])))


def kernel(x_prompt, x_sample, c_prompt, c_sample, state_lru_h, state_lru_conv, state_ssm, state_ssd_conv, w_ada, b_ada, g_ffn1, w_up1, w_down1, g_mix, w_in, lru_conv_w, lru_conv_b, lru_wa, lru_ba, lru_wi, lru_bi, lru_lambda, ssd_conv_w, ssd_conv_b, ssd_dt_bias, ssd_A_log, ssd_D, ssd_norm_g, w_out, g_ffn2, w_up2, w_down2, w_ada_f, b_ada_f, g_final):
    bp, seq, d = x_prompt.shape
    bs = x_sample.shape[0]
    depth = w_ada.shape[0]
    assert depth == 1 and x_sample.shape[1] == 1 and d == D_MODEL

    pad_rows = (-(bs + bp)) % (2 * SUBLANES)
    c_rows = bs + bp + pad_rows
    c_all = jnp.concatenate([c_sample, c_prompt, jnp.zeros((pad_rows, d), F32)], axis=0)

    def split_rows(mod_all):
        width = mod_all.shape[1]
        return mod_all[bs:bs + bp].reshape(bp, 1, width), mod_all.reshape(1, c_rows, width)

    w_in_t = jnp.swapaxes(w_in[0], 0, 1)
    w_dt_t = jnp.pad(w_in_t[IN_MAIN:], ((0, LANES - SSD_HEADS), (0, 0)))
    up_blocks = D_FF // 512
    p = {
        "lru_cw": lru_conv_w[0], "lru_cb": lru_conv_b[0].reshape(1, W_LRU),
        "lru_wg": jnp.concatenate([_block_diag_groups(lru_wa[0]), _block_diag_groups(lru_wi[0])],
                                  axis=-1),
        "lru_ba": lru_ba[0].reshape(1, W_LRU), "lru_bi": lru_bi[0].reshape(1, W_LRU),
        "lru_lam": lru_lambda[0].reshape(1, W_LRU),
        "ssd_cw": ssd_conv_w[0], "ssd_cb": ssd_conv_b[0].reshape(1, SSD_CONV_DIM),
        "ssd_dtb": _pad_lanes(ssd_dt_bias[0]), "ssd_alog": _pad_lanes(ssd_A_log[0]),
        "ssd_dexp": jnp.repeat(ssd_D[0], SSD_HEAD_DIM).reshape(1, W_SSD),
        "ssd_ng": ssd_norm_g[0].reshape(1, W_SSD),
    }

    xp = x_prompt.reshape(bp * seq, d)
    xs = x_sample.reshape(bs, d)
    tm = 1024
    up_kw = dict(n_out=D_FF, swiglu=True, out_dtype=BF16)

    mod_a_all, silu_c = _ada(c_all, w_ada[0], b_ada[0], 2 * d)
    mod_a_p, mod_a_s = split_rows(mod_a_all)

    hp, hs = _norm_rows(xp, xs, g_ffn1[0], mod_a_p, mod_a_s, 0, tm=tm)
    first, (hmid_s,), wb = _proj_first_tile(
        hp, [(w_up1[0], 0), (w_up1[0], up_blocks)], tm=tm, tn=512, side=hs, **up_kw)
    (hmid,), ((w_down_b,), (mod_b_all,), (w_in_b,)) = _proj_other_tiles(
        hp, wb, first, tm=tm, tn=512,
        streams=[_cast_stream(w_down1[0], 32),
                 _ada_stream(silu_c, w_ada[0], b_ada[0], 2 * d, 3 * d, 256),
                 _cast_stream(w_in_t, 32, start=32, rows=IN_MAIN)],
        **up_kw)
    mod_b_p, mod_b_s = split_rows(mod_b_all)
    (xp, hp), (xs, hs) = _resid([hmid], [hmid_s], [(w_down_b, 0)], xp, xs, mod_b_p, mod_b_s, 0,
                                g_mix[0], mod_b_p, mod_b_s, 1, factor=0.5, tm=256, emit_x=True,
                                h_dtype=BF16)

    (proj, dt_raw), (proj_s, dt_raw_s), _, ((w_out_b,),) = _proj(
        hp, [(w_in_b, 0)], n_out=IN_MAIN, tm=tm, tn=IN_MAIN // 3, swiglu=False, out_dtype=F32,
        row_tiles=(0, bp * seq // tm), side=hs, trans_w=True, w_extra=w_dt_t,
        streams=[_cast_stream(w_out[0], 16)])
    lconv = state_lru_conv[0].reshape(bs, (CONV_W - 1) * W_LRU)
    sconv = state_ssd_conv[0].reshape(bs, (CONV_W - 1) * SSD_CONV_DIM)
    out_l_s, lru_h_s, lconv_new, sconv_new, xs_act, xdt, bc, dec = _sample_pre(
        proj_s, dt_raw_s, state_lru_h[0], lconv, sconv, p)
    out_l, lru_h_p, ((ssm_s, y_raw), (modf_all,)) = _lru_prompt(
        proj, bp, seq, p["lru_cw"], p["lru_cb"], p["lru_wg"], p["lru_ba"], p["lru_bi"],
        p["lru_lam"],
        streams=[_state_stream(state_ssm[0].reshape(bs, W_SSD, SSD_STATE), xdt, bc, dec),
                 _ada_stream(silu_c, w_ada_f, b_ada_f, 0, 2 * d, 2 * d // (bp * seq // LRU_TIME_TILE))])
    y_ssd_s = _sample_post(y_raw, xs_act, proj_s, p["ssd_dexp"], p["ssd_ng"])
    y_ssd, ssm_p, ((mod_c_all,),) = _ssd_prompt(
        proj, dt_raw, bp, seq, p["ssd_cw"], p["ssd_cb"], p["ssd_dtb"], p["ssd_alog"],
        p["ssd_dexp"], p["ssd_ng"],
        streams=[_ada_stream(silu_c, w_ada[0], b_ada[0], 5 * d, (N_MOD - 5) * d,
                             (N_MOD - 5) * d // (bp * seq // SSD_CHUNK))])
    mod_c_p, mod_c_s = split_rows(mod_c_all)
    modf_p, modf_s = split_rows(modf_all)
    proj3 = proj.reshape(bp, seq, IN_MAIN)
    lru_buf_p = proj3[:, seq - (CONV_W - 1):, :W_LRU]
    ssd_buf_p = proj3[:, seq - (CONV_W - 1):, 2 * W_LRU + W_SSD:]

    (xp, hp), (xs, hs) = _resid([out_l, y_ssd], [out_l_s, y_ssd_s], [(w_out_b, 0), (w_out_b, 1)],
                                xp, xs, mod_c_p, mod_c_s, 0, g_ffn2[0], mod_c_p, mod_c_s, 1,
                                factor=1.0, tm=512, emit_x=True, h_dtype=BF16)

    first, (hmid_s,), wb = _proj_first_tile(
        hp, [(w_up2[0], 0), (w_up2[0], up_blocks)], tm=tm, tn=512, side=hs, **up_kw)
    (hmid,), ((w_down_b,),) = _proj_other_tiles(
        hp, wb, first, tm=tm, tn=512, streams=[_cast_stream(w_down2[0], 32)], **up_kw)
    (yp,), (ys,) = _resid([hmid], [hmid_s], [(w_down_b, 0)], xp, xs, mod_c_p, mod_c_s, 3, g_final,
                          modf_p, modf_s, 0, factor=0.5, tm=256, emit_x=False, h_dtype=F32)

    stack = lambda v: v[None]
    return (yp.reshape(bp, seq, d), ys.reshape(bs, 1, d),
            stack(lru_h_p), stack(lru_buf_p), stack(ssm_p), stack(ssd_buf_p),
            stack(lru_h_s), stack(lconv_new.reshape(bs, CONV_W - 1, W_LRU)),
            stack(ssm_s.reshape(bs, SSD_HEADS, SSD_HEAD_DIM, SSD_STATE)),
            stack(sconv_new.reshape(bs, CONV_W - 1, SSD_CONV_DIM)))
```

```python
import functools
from typing import Callable, NamedTuple

import jax
import jax.numpy as jnp
from jax import lax
from jax.experimental import pallas as pl
from jax.experimental.pallas import tpu as pltpu

F32 = jnp.float32
BF16 = jnp.bfloat16

D_MODEL = 2048
D_FF = 5632
W_LRU = 1024
W_SSD = 1024
LRU_HEADS = 16
LRU_BLOCK = 64
LRU_C = 8.0
SSD_HEADS = 16
SSD_HEAD_DIM = 64
SSD_GROUPS = 2
SSD_HPG = 8
SSD_STATE = 128
SSD_CHUNK = 128
CONV_W = 4
SSD_CONV_DIM = W_SSD + 2 * SSD_GROUPS * SSD_STATE
IN_MAIN = 2 * W_LRU + W_SSD + SSD_CONV_DIM
N_MOD = 9
EPS = 1e-6

LANES = 128
SUBLANES = 8
VMEM_LIMIT_BYTES = 56 * 1024 * 1024

LRU_GATE_GROUP = 256
LRU_TIME_TILE = 512
SCAN_ROWS = 2 * SUBLANES


def _sigmoid(v):
    return 0.5 * (jnp.tanh(0.5 * v) + 1.0)


def _silu(v):
    return v * _sigmoid(v)


def _softplus(v):
    return jnp.maximum(v, 0.0) + jnp.log1p(jnp.exp(-jnp.abs(v)))


def _gelu_tanh(v):
    return 0.5 * v * (1.0 + jnp.tanh(0.7978845608028654 * (v + 0.044715 * (v * v * v))))


def _bdot(a, b):
    return jnp.dot(a, b, preferred_element_type=F32)


def _params(sem):
    return pltpu.CompilerParams(dimension_semantics=sem, vmem_limit_bytes=VMEM_LIMIT_BYTES)


def _ada_mm_kernel(s_ref, w_ref, b_ref, o_ref):
    o_ref[...] = _bdot(s_ref[...], w_ref[...].astype(BF16)) + b_ref[...]


def _ada_kernel(c_ref, w_ref, b_ref, o_ref, s_ref):
    s_ref[...] = _silu(c_ref[...]).astype(BF16)
    _ada_mm_kernel(s_ref, w_ref, b_ref, o_ref)


def _ada(c, w, b, cols, tn=1024):
    m, k = c.shape
    n = cols
    return pl.pallas_call(
        _ada_kernel,
        grid=(n // tn,),
        in_specs=[pl.BlockSpec((m, k), lambda j: (0, 0)),
                  pl.BlockSpec((k, tn), lambda j: (0, j)),
                  pl.BlockSpec((1, tn), lambda j: (0, j))],
        out_specs=[pl.BlockSpec((m, tn), lambda j: (0, j)), pl.BlockSpec((m, k), lambda j: (0, 0))],
        out_shape=[jax.ShapeDtypeStruct((m, n), F32), jax.ShapeDtypeStruct((m, k), BF16)],
        compiler_params=_params(("arbitrary",)),
        name="ada_proj",
    )(c, w, b.reshape(1, -1))


def _norm_modulate(x, gain, shift, scale):
    ms = jnp.mean(x * x, axis=-1, keepdims=True)
    y = x * lax.rsqrt(ms + EPS) * gain
    return y * (1.0 + scale) + shift


def _norm_rows_kernel(x_ref, gain_ref, sh_ref, sc_ref, xs_ref, shs_ref, scs_ref, o_ref, os_ref):
    o_ref[...] = _norm_modulate(x_ref[...], gain_ref[...], sh_ref[...],
                                sc_ref[...]).astype(o_ref.dtype)

    @pl.when(pl.program_id(0) == 0)
    def _():
        os_ref[...] = _norm_modulate(xs_ref[...], gain_ref[...], shs_ref[...],
                                     scs_ref[...]).astype(os_ref.dtype)


def _norm_rows(x, x_s, gain, mod, mod_s, shift_chunk, *, tm):
    m, d = x.shape
    ns = x_s.shape[0]
    tiles_per_group = (m // tm) // mod.shape[0]
    once = dict(pipeline_mode=pl.Buffered(1))
    mod_spec = lambda c: pl.BlockSpec((None, 1, d), lambda i: (i // tiles_per_group, 0, c))
    mod_s_spec = lambda c: pl.BlockSpec((None, ns, d), lambda i: (0, 0, c), **once)
    return pl.pallas_call(
        _norm_rows_kernel,
        grid=(m // tm,),
        in_specs=[pl.BlockSpec((tm, d), lambda i: (i, 0)), pl.BlockSpec((1, d), lambda i: (0, 0)),
                  mod_spec(shift_chunk), mod_spec(shift_chunk + 1),
                  pl.BlockSpec((ns, d), lambda i: (0, 0), **once),
                  mod_s_spec(shift_chunk), mod_s_spec(shift_chunk + 1)],
        out_specs=[pl.BlockSpec((tm, d), lambda i: (i, 0)), pl.BlockSpec((ns, d), lambda i: (0, 0))],
        out_shape=[jax.ShapeDtypeStruct((m, d), BF16), jax.ShapeDtypeStruct((ns, d), BF16)],
        compiler_params=_params(("arbitrary",)),
        name="norm_rows",
    )(x, gain.reshape(1, d), mod, mod, x_s, mod_s, mod_s)


class _Stream(NamedTuple):
    ins: list
    outs: list
    body: Callable
    start: int
    steps: int


def _stream_io(streams, n_inner):
    in_specs, args, out_specs, out_shape = [], [], [], []

    def spec(st, block, index_fn):
        return pl.BlockSpec(
            block, lambda i, j: index_fn(jnp.clip(i * n_inner + j - st.start, 0, st.steps - 1)))

    for st in streams:
        for arr, block, index_fn in st.ins:
            in_specs.append(spec(st, block, index_fn))
            args.append(arr)
        for shape, block, index_fn in st.outs:
            out_specs.append(spec(st, block, index_fn))
            out_shape.append(shape)
    return in_specs, args, out_specs, out_shape


def _run_streams(streams, in_refs, out_refs, n_inner, n_steps):
    step = pl.program_id(0) * n_inner + pl.program_id(1)
    in_refs, out_refs = iter(in_refs), iter(out_refs)
    for st in streams:
        ins = [next(in_refs) for _ in st.ins]
        outs = [next(out_refs) for _ in st.outs]
        if st.start == 0 and st.steps == n_steps:
            st.body(*ins, *outs)
            continue

        @pl.when((step >= st.start) & (step < st.start + st.steps))
        def _(st=st, ins=ins, outs=outs):
            st.body(*ins, *outs)


def _split_stream_outs(streams, flat):
    flat = list(flat)
    return [[flat.pop(0) for _ in st.outs] for st in streams]


def _cast_body(src_ref, dst_ref):
    dst_ref[...] = src_ref[...].astype(dst_ref.dtype)


def _cast_stream(w, chunks, start=0, rows=None):
    rows = w.shape[0] if rows is None else rows
    block = (rows // chunks, w.shape[1])
    index = lambda k: (k, 0)
    return _Stream([(w, block, index)],
                   [(jax.ShapeDtypeStruct((rows, w.shape[1]), BF16), block, index)],
                   _cast_body, start, chunks)


def _ada_stream(sc, w, b, col0, cols, tn, start=0):
    m, k = sc.shape
    t0 = col0 // tn
    return _Stream(
        [(sc, (m, k), lambda s: (0, 0)), (w, (k, tn), lambda s: (0, s + t0)),
         (b.reshape(1, -1), (1, tn), lambda s: (0, s + t0))],
        [(jax.ShapeDtypeStruct((m, cols), F32), (m, tn), lambda s: (0, s))],
        _ada_mm_kernel, start, cols // tn)


def _wdot(h, w, trans_w):
    if trans_w:
        return lax.dot_general(h, w, (((1,), (1,)), ((), ())), preferred_element_type=F32)
    return _bdot(h, w)


def _proj_kernel(*refs, n_w, swiglu, trans_w, has_extra, has_side, emit_bf16, streams, n_prev,
                 nj, n_steps):
    it = iter(refs)
    x_ref = next(it)
    xs_ref = next(it) if has_side else None
    w_refs = [next(it) for _ in range(n_w)]
    wx_ref = next(it) if has_extra else None
    stream_ins = [next(it) for st in streams for _ in st.ins]
    for _ in range(n_prev):
        next(it)
    o_ref = next(it)
    ox_ref = next(it) if has_extra else None
    os_ref = next(it) if has_side else None
    osx_ref = next(it) if has_side and has_extra else None
    wo_refs = [next(it) for _ in range(n_w)] if emit_bf16 else []
    stream_outs = [next(it) for st in streams for _ in st.outs]

    j = pl.program_id(1)
    first_tile = pl.program_id(0) == 0

    if has_extra:
        @pl.when(j == 0)
        def _():
            ox_ref[...] = _wdot(x_ref[...], wx_ref[...].astype(BF16), trans_w)

        if has_side:
            @pl.when((j == 0) & first_tile)
            def _():
                osx_ref[...] = _wdot(xs_ref[...], wx_ref[...].astype(BF16), trans_w)

    wbs = [w_ref[...].astype(BF16) for w_ref in w_refs]
    for wo_ref, wb in zip(wo_refs, wbs):
        wo_ref[...] = wb

    def project(h, out_ref):
        if swiglu:
            g = _wdot(h, wbs[0], trans_w)
            u = _wdot(h, wbs[1], trans_w)
            out_ref[...] = (_silu(g) * u).astype(out_ref.dtype)
        else:
            out_ref[...] = _wdot(h, wbs[0], trans_w).astype(out_ref.dtype)

    project(x_ref[...], o_ref)
    if has_side:
        @pl.when(first_tile)
        def _():
            project(xs_ref[...], os_ref)

    _run_streams(streams, stream_ins, stream_outs, nj, n_steps)


def _proj(x, ws, *, n_out, tm, tn, swiglu, out_dtype, row_tiles, side=None, trans_w=False,
          w_extra=None, emit_bf16=False, streams=(), prev=None):
    m, d = x.shape
    t0, t1 = row_tiles
    nj = n_out // tn
    has_side = side is not None
    has_extra = w_extra is not None
    single_row_tile = t1 - t0 == 1
    once = dict(pipeline_mode=pl.Buffered(1))

    def w_spec(off):
        if trans_w:
            return pl.BlockSpec((tn, d), lambda i, j: (j + off, 0))
        return pl.BlockSpec((d, tn), lambda i, j: (0, j + off))

    x_mode = once if single_row_tile else {}
    in_specs = [pl.BlockSpec((tm, d), lambda i, j: (i + t0, 0), **x_mode)]
    args = [x]
    if has_side:
        ns = side.shape[0]
        in_specs.append(pl.BlockSpec((ns, d), lambda i, j: (0, 0), **once))
        args.append(side)
    in_specs += [w_spec(off) for _, off in ws]
    args += [w for w, _ in ws]
    if has_extra:
        nx = w_extra.shape[0] if trans_w else w_extra.shape[1]
        in_specs.append(pl.BlockSpec(w_extra.shape, lambda i, j: (0, 0)))
        args.append(w_extra)
    assert all(st.start + st.steps <= (t1 - t0) * nj for st in streams)
    st_in_specs, st_args, st_out_specs, st_out_shape = _stream_io(streams, nj)
    in_specs += st_in_specs
    args += st_args
    prev = list(prev or [])
    aliases = {}
    for k, buf in enumerate(prev):
        aliases[len(args)] = k
        in_specs.append(pl.BlockSpec(memory_space=pl.ANY))
        args.append(buf)

    out_specs = [pl.BlockSpec((tm, tn), lambda i, j: (i + t0, j))]
    out_shape = [jax.ShapeDtypeStruct((m, n_out), out_dtype)]
    if has_extra:
        out_specs.append(pl.BlockSpec((tm, nx), lambda i, j: (i + t0, 0)))
        out_shape.append(jax.ShapeDtypeStruct((m, nx), F32))
    n_main = len(out_shape)
    if has_side:
        out_specs.append(pl.BlockSpec((ns, tn), lambda i, j: (0, jnp.where(i == 0, j, nj - 1))))
        out_shape.append(jax.ShapeDtypeStruct((ns, n_out), out_dtype))
        if has_extra:
            out_specs.append(pl.BlockSpec((ns, nx), lambda i, j: (0, 0)))
            out_shape.append(jax.ShapeDtypeStruct((ns, nx), F32))
    n_side = len(out_shape) - n_main
    if emit_bf16:
        assert single_row_tile, "weight copies are written once per column tile"
        for _ in ws:
            out_specs.append(w_spec(0))
            out_shape.append(jax.ShapeDtypeStruct((n_out, d) if trans_w else (d, n_out), BF16))
    out_specs += st_out_specs
    out_shape += st_out_shape
    outs = pl.pallas_call(
        functools.partial(_proj_kernel, n_w=len(ws), swiglu=swiglu, trans_w=trans_w,
                          has_extra=has_extra, has_side=has_side, emit_bf16=emit_bf16,
                          streams=tuple(streams), n_prev=len(prev), nj=nj,
                          n_steps=(t1 - t0) * nj),
        grid=(t1 - t0, nj),
        in_specs=in_specs,
        out_specs=out_specs,
        out_shape=out_shape,
        input_output_aliases=aliases,
        compiler_params=_params(("arbitrary" if streams or has_side else "parallel", "arbitrary")),
        name="proj_swiglu" if swiglu else "proj",
    )(*args)
    n_wb = len(ws) if emit_bf16 else 0
    main, rest = outs[:n_main], outs[n_main:]
    side_outs, rest = rest[:n_side], rest[n_side:]
    wb, rest = rest[:n_wb], rest[n_wb:]
    return main, side_outs, wb, _split_stream_outs(streams, rest)


def _proj_first_tile(x, ws_f32, *, tm, tn, side, **kw):
    main, side_outs, wb, _ = _proj(x, ws_f32, tm=tm, tn=tn, row_tiles=(0, 1), side=side,
                                   emit_bf16=True, **kw)
    return main, side_outs, wb


def _proj_other_tiles(x, wb, prev, *, tm, tn, streams, **kw):
    main, _, _, stream_outs = _proj(x, [(w, 0) for w in wb], tm=tm, tn=tn,
                                    row_tiles=(1, x.shape[0] // tm), prev=prev, streams=streams,
                                    **kw)
    return main, stream_outs


def _resid_kernel(*refs, n_lhs, factor, emit_x):
    it = iter(refs)
    lhs_refs = [next(it) for _ in range(n_lhs)]
    lhs_s_refs = [next(it) for _ in range(n_lhs)]
    w_refs = [next(it) for _ in range(n_lhs)]
    x_ref, gate_ref, gain_ref, sh_ref, sc_ref = (next(it) for _ in range(5))
    xs_ref, gate_s_ref, sh_s_ref, sc_s_ref = (next(it) for _ in range(4))
    n_out = 2 if emit_x else 1
    outs = [next(it) for _ in range(n_out)]
    outs_s = [next(it) for _ in range(n_out)]

    def update(lhs, x_in, gate, sh, sc, out_refs):
        acc = _bdot(lhs[0][...], w_refs[0][...])
        for l_ref, w_ref in zip(lhs[1:], w_refs[1:]):
            acc = acc + _bdot(l_ref[...], w_ref[...])
        x_new = x_in[...] + (factor * gate[...]) * acc
        if emit_x:
            out_refs[0][...] = x_new
        h_ref = out_refs[-1]
        h_ref[...] = _norm_modulate(x_new, gain_ref[...], sh[...], sc[...]).astype(h_ref.dtype)

    update(lhs_refs, x_ref, gate_ref, sh_ref, sc_ref, outs)

    @pl.when(pl.program_id(0) == 0)
    def _():
        update(lhs_s_refs, xs_ref, gate_s_ref, sh_s_ref, sc_s_ref, outs_s)


def _resid(lhs_list, lhs_s_list, ws, x, x_s, mod, mod_s, gate_chunk, gain_next, mod_next,
           mod_next_s, shift_chunk_next, *, factor, tm, emit_x, h_dtype):
    m, d = x.shape
    ns = x_s.shape[0]
    groups = mod.shape[0]
    tiles_per_group = (m // tm) // groups
    kp = lhs_list[0].shape[1]
    once = dict(pipeline_mode=pl.Buffered(1))

    def mod_spec(chunk):
        return pl.BlockSpec((None, 1, d), lambda i: (i // tiles_per_group, 0, chunk))

    def mod_s_spec(chunk):
        return pl.BlockSpec((None, ns, d), lambda i: (0, 0, chunk), **once)

    in_specs = [pl.BlockSpec((tm, kp), lambda i: (i, 0)) for _ in lhs_list]
    in_specs += [pl.BlockSpec((ns, kp), lambda i: (0, 0), **once) for _ in lhs_s_list]
    in_specs += [pl.BlockSpec((kp, d), lambda i, k=k: (k, 0), **once) for _, k in ws]
    in_specs += [pl.BlockSpec((tm, d), lambda i: (i, 0)), mod_spec(gate_chunk),
                 pl.BlockSpec((1, d), lambda i: (0, 0)),
                 mod_spec(shift_chunk_next), mod_spec(shift_chunk_next + 1),
                 pl.BlockSpec((ns, d), lambda i: (0, 0), **once), mod_s_spec(gate_chunk),
                 mod_s_spec(shift_chunk_next), mod_s_spec(shift_chunk_next + 1)]
    row = pl.BlockSpec((tm, d), lambda i: (i, 0))
    row_s = pl.BlockSpec((ns, d), lambda i: (0, 0))
    dtypes = ([F32] if emit_x else []) + [h_dtype]
    out_specs = [row for _ in dtypes] + [row_s for _ in dtypes]
    out_shape = ([jax.ShapeDtypeStruct((m, d), t) for t in dtypes]
                 + [jax.ShapeDtypeStruct((ns, d), t) for t in dtypes])
    outs = pl.pallas_call(
        functools.partial(_resid_kernel, n_lhs=len(lhs_list), factor=factor, emit_x=emit_x),
        grid=(m // tm,),
        in_specs=in_specs,
        out_specs=out_specs,
        out_shape=out_shape,
        compiler_params=_params(("arbitrary",)),
        name="resid",
    )(*lhs_list, *lhs_s_list, *[w for w, _ in ws], x, mod, gain_next.reshape(1, d), mod_next,
      mod_next, x_s, mod_s, mod_next_s, mod_next_s)
    return outs[:len(dtypes)], outs[len(dtypes):]


def _lru_gates(xc, wg_ref, ba, bi, sp):
    a_parts, b_parts = [], []
    for g in range(W_LRU // LRU_GATE_GROUP):
        cols = slice(g * LRU_GATE_GROUP, (g + 1) * LRU_GATE_GROUP)
        xg = xc[:, cols]
        ri = _bdot(xg.astype(BF16), wg_ref[g].astype(BF16))
        r = _sigmoid(ri[:, :LRU_GATE_GROUP] + ba[:, cols])
        i = _sigmoid(ri[:, LRU_GATE_GROUP:] + bi[:, cols])
        log_a = (-LRU_C * r) * sp[:, cols]
        a = jnp.exp(log_a)
        a_parts.append(a)
        b_parts.append(jnp.sqrt(1.0 - a * a) * (i * xg))
    return jnp.concatenate(a_parts, axis=1), jnp.concatenate(b_parts, axis=1)


def _causal_conv(prev8, x, w_ref, b_ref):
    rows, width = x.shape
    rid = lax.broadcasted_iota(jnp.int32, (SUBLANES, width), 0)
    shifts = (1, 2, 3)
    taps = [w_ref[k:k + 1, :] for k in range(CONV_W)]
    bias = b_ref[...]
    prev_rot = [pltpu.roll(prev8, k, 0) for k in shifts]
    out = []
    for r in range(rows // SUBLANES):
        cur = x[r * SUBLANES:(r + 1) * SUBLANES, :]
        cur_rot = [pltpu.roll(cur, k, 0) for k in shifts]
        s1, s2, s3 = [jnp.where(rid < k, p, c) for k, p, c in zip(shifts, prev_rot, cur_rot)]
        out.append(bias + taps[0] * s3 + taps[1] * s2 + taps[2] * s1 + taps[3] * cur)
        prev_rot = cur_rot
    return jnp.concatenate(out, axis=0)


def _lru_prompt_kernel(*refs, streams, nt, n_steps):
    n_in = sum(len(st.ins) for st in streams)
    n_out = sum(len(st.outs) for st in streams)
    xl_ref, gl_ref, cw_ref, cb_ref, wg_ref, ba_ref, bi_ref, lam_ref = refs[:8]
    stream_ins = refs[8:8 + n_in]
    o_ref, hT_ref = refs[8 + n_in:10 + n_in]
    stream_outs = refs[10 + n_in:10 + n_in + n_out]
    xbuf, a_scr, b_scr, hcar = refs[10 + n_in + n_out:]
    t = pl.program_id(1)
    tt = xl_ref.shape[0]

    @pl.when(t == 0)
    def _():
        xbuf[...] = jnp.zeros_like(xbuf)
        hcar[...] = jnp.zeros_like(hcar)

    _run_streams(streams, stream_ins, stream_outs, nt, n_steps)

    x = xl_ref[...]
    xc = _causal_conv(xbuf[...], x, cw_ref, cb_ref)
    xbuf[...] = x[tt - SUBLANES:, :]

    sp = _softplus(-lam_ref[...])
    a, bt = _lru_gates(xc, wg_ref, ba_ref[...], bi_ref[...], sp)
    a_scr[...] = a
    b_scr[...] = bt

    rid = lax.broadcasted_iota(jnp.int32, (SUBLANES, W_LRU), 0)

    def scan8(a8, b8, h_in):
        for s in (1, 2, 4):
            a_sh = pltpu.roll(a8, s, 0)
            b_sh = pltpu.roll(b8, s, 0)
            m = rid >= s
            b8 = jnp.where(m, a8 * b_sh + b8, b8)
            a8 = jnp.where(m, a8 * a_sh, a8)
        h8 = a8 * h_in + b8
        return h8, jnp.broadcast_to(h8[SUBLANES - 1:SUBLANES, :], (SUBLANES, W_LRU))

    def body(g, h_in):
        r0 = pl.multiple_of(g * SCAN_ROWS, SCAN_ROWS)
        lo = pl.ds(r0, SUBLANES)
        hi = pl.ds(r0 + SUBLANES, SUBLANES)
        h_lo, h_mid = scan8(a_scr[lo, :], b_scr[lo, :], h_in)
        h_hi, h_out = scan8(a_scr[hi, :], b_scr[hi, :], h_mid)
        rows = pl.ds(r0, SCAN_ROWS)
        h16 = jnp.concatenate([h_lo, h_hi], axis=0)
        o_ref[rows, :] = (h16 * _gelu_tanh(gl_ref[rows, :])).astype(o_ref.dtype)
        return h_out

    h_last = lax.fori_loop(0, tt // SCAN_ROWS, body, hcar[...])
    hcar[...] = h_last

    @pl.when(t == pl.num_programs(1) - 1)
    def _():
        hT_ref[...] = h_last[0:1, :]


def _lru_prompt(proj, batch, seq, cw, cb, wg, ba, bi, lam, streams=()):
    tt = LRU_TIME_TILE
    nt = seq // tt
    assert all(st.start + st.steps <= batch * nt for st in streams)
    st_in_specs, st_args, st_out_specs, st_out_shape = _stream_io(streams, nt)
    row = lambda v: v.reshape(1, W_LRU)
    full = lambda shape: pl.BlockSpec(shape, lambda b, t: (0,) * len(shape))
    out, h_t, *rest = pl.pallas_call(
        functools.partial(_lru_prompt_kernel, streams=tuple(streams), nt=nt, n_steps=batch * nt),
        grid=(batch, nt),
        in_specs=[pl.BlockSpec((tt, W_LRU), lambda b, t: (b * nt + t, 0)),
                  pl.BlockSpec((tt, W_LRU), lambda b, t: (b * nt + t, 1)),
                  full((CONV_W, W_LRU)), full((1, W_LRU)), full(wg.shape),
                  full((1, W_LRU)), full((1, W_LRU)), full((1, W_LRU))] + st_in_specs,
        out_specs=[pl.BlockSpec((tt, W_LRU), lambda b, t: (b * nt + t, 0)),
                   pl.BlockSpec((None, 1, W_LRU), lambda b, t: (b, 0, 0))] + st_out_specs,
        out_shape=[jax.ShapeDtypeStruct((batch * seq, W_LRU), BF16),
                   jax.ShapeDtypeStruct((batch, 1, W_LRU), F32)] + st_out_shape,
        scratch_shapes=[pltpu.VMEM((SUBLANES, W_LRU), F32),
                        pltpu.VMEM((tt, W_LRU), F32),
                        pltpu.VMEM((tt, W_LRU), F32),
                        pltpu.VMEM((SUBLANES, W_LRU), F32)],
        compiler_params=_params(("arbitrary" if streams else "parallel", "arbitrary")),
        name="lru_prompt",
    )(proj, proj, cw, row(cb), wg, row(ba), row(bi), row(lam), *st_args)
    return out, h_t.reshape(batch, W_LRU), _split_stream_outs(streams, rest)


def _ssd_prompt_kernel(*refs, streams, nc, n_steps):
    n_in = sum(len(st.ins) for st in streams)
    n_out = sum(len(st.outs) for st in streams)
    z_ref, xbc_ref, dt_ref, cw_ref, cb_ref, dtb_ref, alog_ref, dexp_ref, ng_ref = refs[:9]
    stream_ins = refs[9:9 + n_in]
    y_ref, st_ref = refs[9 + n_in:11 + n_in]
    stream_outs = refs[11 + n_in:11 + n_in + n_out]
    xbuf, st_scr, y_scr, m_scr, xbd_scr = refs[11 + n_in + n_out:]
    c = pl.program_id(1)
    lc = SSD_CHUNK

    @pl.when(c == 0)
    def _():
        xbuf[...] = jnp.zeros_like(xbuf)
        st_scr[...] = jnp.zeros_like(st_scr)
        xbd_scr[...] = jnp.zeros_like(xbd_scr)

    _run_streams(streams, stream_ins, stream_outs, nc, n_steps)

    x = xbc_ref[...]
    act = _silu(_causal_conv(xbuf[...], x, cw_ref, cb_ref))
    xbuf[...] = x[lc - SUBLANES:, :]
    xs = act[:, :W_SSD]
    bm = act[:, W_SSD:W_SSD + SSD_GROUPS * SSD_STATE]
    cm = act[:, W_SSD + SSD_GROUPS * SSD_STATE:]

    dt = _softplus(dt_ref[...] + dtb_ref[...])
    d_a = dt * (-jnp.exp(alog_ref[...]))
    row_i = lax.broadcasted_iota(jnp.int32, (lc, lc), 0)
    col_i = lax.broadcasted_iota(jnp.int32, (lc, lc), 1)
    causal = row_i >= col_i
    tril = jnp.where(causal, 1.0, 0.0).astype(F32)
    cs = jnp.dot(tril, d_a, preferred_element_type=F32, precision=lax.Precision.HIGHEST)
    cs_t = cs.T
    dt_t = dt.T
    cs_last = cs[lc - 1:lc, :]

    def per_head_lanes(v):
        rows = v.shape[0]
        return jnp.concatenate(
            [jnp.broadcast_to(v[:, h:h + 1], (rows, SSD_HEAD_DIM)) for h in range(SSD_HEADS)],
            axis=1)

    w_exp = per_head_lanes(jnp.exp(cs_last - cs) * dt)
    ecs_exp = per_head_lanes(jnp.exp(cs))
    cd_exp = per_head_lanes(jnp.exp(cs_last))
    gw = SSD_HPG * SSD_HEAD_DIM
    low_half = col_i < SSD_HEAD_DIM

    for g in range(SSD_GROUPS):
        ncols = slice(g * SSD_STATE, (g + 1) * SSD_STATE)
        gcols = slice(g * gw, (g + 1) * gw)
        b_g = bm[:, ncols].astype(BF16)
        c_g = cm[:, ncols].astype(BF16)
        cb_mat = lax.dot_general(c_g, b_g, (((1,), (1,)), ((), ())),
                                 preferred_element_type=F32)
        for e in range(SSD_HPG):
            h = g * SSD_HPG + e
            cs_col = jnp.broadcast_to(cs[:, h:h + 1], (lc, lc))
            l_mat = jnp.exp(jnp.where(causal, cs_col - cs_t[h:h + 1, :], -jnp.inf))
            m_scr[g, :, e * lc:(e + 1) * lc] = (cb_mat * l_mat * dt_t[h:h + 1, :]).astype(BF16)
        for q in range(SSD_HPG // 2):
            lanes = slice(q * LANES, (q + 1) * LANES)
            slab = xs[:, g * gw + q * LANES:g * gw + (q + 1) * LANES]
            xbd_scr[g, (2 * q) * lc:(2 * q + 1) * lc, lanes] = jnp.where(
                low_half, slab, 0.0).astype(BF16)
            xbd_scr[g, (2 * q + 1) * lc:(2 * q + 2) * lc, lanes] = jnp.where(
                low_half, 0.0, slab).astype(BF16)
        st_g = st_scr[:, gcols]
        y_off = _bdot(c_g, st_g.astype(BF16)) * ecs_exp[:, gcols]
        y_scr[:, gcols] = (_bdot(m_scr[g], xbd_scr[g]) + y_off
                           + dexp_ref[:, gcols] * xs[:, gcols])
        xw = (xs[:, gcols] * w_exp[:, gcols]).astype(BF16)
        st_scr[:, gcols] = cd_exp[:, gcols] * st_g + lax.dot_general(
            b_g, xw, (((0,), (0,)), ((), ())), preferred_element_type=F32)

    yg = y_scr[...] * _silu(z_ref[...])
    ms = jnp.mean(yg * yg, axis=-1, keepdims=True)
    y_ref[...] = (yg * lax.rsqrt(ms + EPS) * ng_ref[...]).astype(y_ref.dtype)

    @pl.when(c == pl.num_programs(1) - 1)
    def _():
        st_ref[...] = st_scr[...].T


def _ssd_prompt(proj, dt_raw, batch, seq, cw, cb, dtb, alog, dexp, ng, streams=()):
    lc = SSD_CHUNK
    nc = seq // lc
    assert all(st.start + st.steps <= batch * nc for st in streams)
    st_in_specs, st_args, st_out_specs, st_out_shape = _stream_io(streams, nc)
    full = lambda shape: pl.BlockSpec(shape, lambda b, c: (0,) * len(shape))
    z_blk = (2 * W_LRU) // W_SSD
    xbc_blk = (2 * W_LRU + W_SSD) // SSD_CONV_DIM
    y, st, *rest = pl.pallas_call(
        functools.partial(_ssd_prompt_kernel, streams=tuple(streams), nc=nc, n_steps=batch * nc),
        grid=(batch, nc),
        in_specs=[pl.BlockSpec((lc, W_SSD), lambda b, c: (b * nc + c, z_blk)),
                  pl.BlockSpec((lc, SSD_CONV_DIM), lambda b, c: (b * nc + c, xbc_blk)),
                  pl.BlockSpec((lc, LANES), lambda b, c: (b * nc + c, 0)),
                  full((CONV_W, SSD_CONV_DIM)), full((1, SSD_CONV_DIM)),
                  full((1, LANES)), full((1, LANES)), full((1, W_SSD)), full((1, W_SSD))]
        + st_in_specs,
        out_specs=[pl.BlockSpec((lc, W_SSD), lambda b, c: (b * nc + c, 0)),
                   pl.BlockSpec((None, W_SSD, SSD_STATE), lambda b, c: (b, 0, 0))] + st_out_specs,
        out_shape=[jax.ShapeDtypeStruct((batch * seq, W_SSD), BF16),
                   jax.ShapeDtypeStruct((batch, W_SSD, SSD_STATE), F32)] + st_out_shape,
        scratch_shapes=[pltpu.VMEM((SUBLANES, SSD_CONV_DIM), F32),
                        pltpu.VMEM((SSD_STATE, W_SSD), F32),
                        pltpu.VMEM((lc, W_SSD), F32),
                        pltpu.VMEM((SSD_GROUPS, lc, SSD_HPG * lc), BF16),
                        pltpu.VMEM((SSD_GROUPS, SSD_HPG * lc, SSD_HPG * SSD_HEAD_DIM), BF16)],
        compiler_params=_params(("arbitrary" if streams else "parallel", "arbitrary")),
        name="ssd_prompt",
    )(proj, proj, dt_raw, cw, cb, dtb, alog, dexp, ng, *st_args)
    return (y, st.reshape(batch, SSD_HEADS, SSD_HEAD_DIM, SSD_STATE),
            _split_stream_outs(streams, rest))


def _sample_pre_kernel(proj_ref, dt_ref, h0_ref, lconv_ref, sconv_ref,
                       lcw_ref, lcb_ref, wg_ref, ba_ref, bi_ref, lam_ref,
                       scw_ref, scb_ref, dtb_ref, alog_ref,
                       outl_ref, hnew_ref, lconv_new_ref, sconv_new_ref,
                       xs_ref, xdt_ref, bc_ref, dec_ref):
    nb = proj_ref.shape[0]
    xl = proj_ref[:, 0:W_LRU]
    gl = proj_ref[:, W_LRU:2 * W_LRU]
    xbc = proj_ref[:, 2 * W_LRU + W_SSD:IN_MAIN]

    def conv1(state_ref, width, x_new, w_ref, b_ref):
        y = b_ref[...] + w_ref[0:1, :] * state_ref[:, 0:width]
        y = y + w_ref[1:2, :] * state_ref[:, width:2 * width]
        y = y + w_ref[2:3, :] * state_ref[:, 2 * width:3 * width]
        return y + w_ref[3:4, :] * x_new

    xc = conv1(lconv_ref, W_LRU, xl, lcw_ref, lcb_ref)
    a, bt = _lru_gates(xc, wg_ref, ba_ref[...], bi_ref[...], _softplus(-lam_ref[...]))
    h_new = a * h0_ref[...] + bt
    hnew_ref[...] = h_new
    outl_ref[...] = (h_new * _gelu_tanh(gl)).astype(outl_ref.dtype)
    lconv_new_ref[:, 0:2 * W_LRU] = lconv_ref[:, W_LRU:3 * W_LRU]
    lconv_new_ref[:, 2 * W_LRU:3 * W_LRU] = xl

    act = _silu(conv1(sconv_ref, SSD_CONV_DIM, xbc, scw_ref, scb_ref))
    sconv_new_ref[:, 0:2 * SSD_CONV_DIM] = sconv_ref[:, SSD_CONV_DIM:3 * SSD_CONV_DIM]
    sconv_new_ref[:, 2 * SSD_CONV_DIM:3 * SSD_CONV_DIM] = xbc
    xs = act[:, :W_SSD]
    xs_ref[...] = xs
    bc_ref[...] = act[:, W_SSD:]
    dt = _softplus(dt_ref[...] + dtb_ref[...])
    dec = jnp.exp(dt * (-jnp.exp(alog_ref[...])))
    for h in range(SSD_HEADS):
        pcols = slice(h * SSD_HEAD_DIM, (h + 1) * SSD_HEAD_DIM)
        xdt_ref[:, pcols] = xs[:, pcols] * jnp.broadcast_to(dt[:, h:h + 1], (nb, SSD_HEAD_DIM))
        dec_ref[h] = jnp.broadcast_to(dec[:, h:h + 1], (nb, SSD_STATE))


def _sample_pre(proj, dt_raw, h0, lconv, sconv, p):
    nb = proj.shape[0]
    out_shape = [jax.ShapeDtypeStruct((nb, W_LRU), BF16),
                 jax.ShapeDtypeStruct((nb, W_LRU), F32),
                 jax.ShapeDtypeStruct((nb, 3 * W_LRU), F32),
                 jax.ShapeDtypeStruct((nb, 3 * SSD_CONV_DIM), F32),
                 jax.ShapeDtypeStruct((nb, W_SSD), F32),
                 jax.ShapeDtypeStruct((nb, W_SSD), F32),
                 jax.ShapeDtypeStruct((nb, 2 * SSD_GROUPS * SSD_STATE), F32),
                 jax.ShapeDtypeStruct((SSD_HEADS, nb, SSD_STATE), F32)]
    return pl.pallas_call(
        _sample_pre_kernel,
        out_shape=out_shape,
        compiler_params=pltpu.CompilerParams(vmem_limit_bytes=VMEM_LIMIT_BYTES),
        name="sample_pre",
    )(proj, dt_raw, h0, lconv, sconv,
      p["lru_cw"], p["lru_cb"], p["lru_wg"], p["lru_ba"], p["lru_bi"], p["lru_lam"],
      p["ssd_cw"], p["ssd_cb"], p["ssd_dtb"], p["ssd_alog"])


def _sample_state_kernel(s_ref, xdt_ref, bc_ref, dec_ref, o_ref, y_ref):
    bb = s_ref.shape[0]
    half = SSD_HPG * SSD_HEAD_DIM
    rid = lax.broadcasted_iota(jnp.int32, (bb, W_SSD), 0)
    xdt = xdt_ref[...]
    bcb = bc_ref[...].astype(BF16)
    for k in range(bb):
        xk = jnp.where(rid == k, xdt, 0.0).astype(BF16)
        for g in range(SSD_GROUPS):
            rows = slice(g * half, (g + 1) * half)
            b_g = bcb[:, g * SSD_STATE:(g + 1) * SSD_STATE]
            c_g = bcb[:, (SSD_GROUPS + g) * SSD_STATE:(SSD_GROUPS + g + 1) * SSD_STATE]
            outer = lax.dot_general(xk[:, rows], b_g, (((0,), (0,)), ((), ())),
                                    preferred_element_type=F32)
            dec = jnp.concatenate(
                [jnp.broadcast_to(dec_ref[g * SSD_HPG + e, k:k + 1, :], (SSD_HEAD_DIM, SSD_STATE))
                 for e in range(SSD_HPG)], axis=0)
            s_new = dec * s_ref[k, rows, :] + outer
            o_ref[k, rows, :] = s_new
            yk = lax.dot_general(c_g, s_new.astype(BF16), (((1,), (1,)), ((), ())),
                                 preferred_element_type=F32)
            y_ref[k:k + 1, rows] = yk[k:k + 1, :]


def _state_stream(ssm, xdt, bc, dec, bb=8, start=0):
    nb = ssm.shape[0]
    state_block = (bb, W_SSD, SSD_STATE)
    return _Stream(
        [(ssm, state_block, lambda k: (k, 0, 0)),
         (xdt, (bb, W_SSD), lambda k: (k, 0)),
         (bc, (bb, 2 * SSD_GROUPS * SSD_STATE), lambda k: (k, 0)),
         (dec, (SSD_HEADS, bb, SSD_STATE), lambda k: (0, k, 0))],
        [(jax.ShapeDtypeStruct(ssm.shape, F32), state_block, lambda k: (k, 0, 0)),
         (jax.ShapeDtypeStruct((nb, W_SSD), F32), (bb, W_SSD), lambda k: (k, 0))],
        _sample_state_kernel, start, nb // bb)


def _sample_post_kernel(y_ref, xs_ref, proj_ref, dexp_ref, ng_ref, o_ref):
    z = proj_ref[:, 2 * W_LRU:2 * W_LRU + W_SSD]
    yg = (y_ref[...] + dexp_ref[...] * xs_ref[...]) * _silu(z)
    ms = jnp.mean(yg * yg, axis=-1, keepdims=True)
    o_ref[...] = (yg * lax.rsqrt(ms + EPS) * ng_ref[...]).astype(o_ref.dtype)


def _sample_post(y_raw, xs, proj, dexp, ng):
    return pl.pallas_call(
        _sample_post_kernel,
        out_shape=jax.ShapeDtypeStruct(y_raw.shape, BF16),
        compiler_params=pltpu.CompilerParams(vmem_limit_bytes=VMEM_LIMIT_BYTES),
        name="sample_post",
    )(y_raw, xs, proj, dexp, ng)


def _block_diag_groups(w):
    per = LRU_GATE_GROUP // LRU_BLOCK
    w4 = w.reshape(LRU_HEADS // per, per, LRU_BLOCK, LRU_BLOCK)
    bd = jnp.einsum("ghij,hk->ghikj", w4, jnp.eye(per, dtype=w.dtype))
    return bd.reshape(LRU_HEADS // per, LRU_GATE_GROUP, LRU_GATE_GROUP)


def _pad_lanes(v):
    v = v.reshape(1, -1)
    return jnp.pad(v, ((0, 0), (0, LANES - v.shape[1])))


def kernel(x_prompt, x_sample, c_prompt, c_sample, state_lru_h, state_lru_conv, state_ssm, state_ssd_conv, w_ada, b_ada, g_ffn1, w_up1, w_down1, g_mix, w_in, lru_conv_w, lru_conv_b, lru_wa, lru_ba, lru_wi, lru_bi, lru_lambda, ssd_conv_w, ssd_conv_b, ssd_dt_bias, ssd_A_log, ssd_D, ssd_norm_g, w_out, g_ffn2, w_up2, w_down2, w_ada_f, b_ada_f, g_final):
    bp, seq, d = x_prompt.shape
    bs = x_sample.shape[0]
    depth = w_ada.shape[0]
    assert depth == 1 and x_sample.shape[1] == 1 and d == D_MODEL

    pad_rows = (-(bs + bp)) % (2 * SUBLANES)
    c_rows = bs + bp + pad_rows
    c_all = jnp.concatenate([c_sample, c_prompt, jnp.zeros((pad_rows, d), F32)], axis=0)

    def split_rows(mod_all):
        width = mod_all.shape[1]
        return mod_all[bs:bs + bp].reshape(bp, 1, width), mod_all.reshape(1, c_rows, width)

    w_in_t = jnp.swapaxes(w_in[0], 0, 1)
    w_dt_t = jnp.pad(w_in_t[IN_MAIN:], ((0, LANES - SSD_HEADS), (0, 0)))
    up_blocks = D_FF // 512
    p = {
        "lru_cw": lru_conv_w[0], "lru_cb": lru_conv_b[0].reshape(1, W_LRU),
        "lru_wg": jnp.concatenate([_block_diag_groups(lru_wa[0]), _block_diag_groups(lru_wi[0])],
                                  axis=-1),
        "lru_ba": lru_ba[0].reshape(1, W_LRU), "lru_bi": lru_bi[0].reshape(1, W_LRU),
        "lru_lam": lru_lambda[0].reshape(1, W_LRU),
        "ssd_cw": ssd_conv_w[0], "ssd_cb": ssd_conv_b[0].reshape(1, SSD_CONV_DIM),
        "ssd_dtb": _pad_lanes(ssd_dt_bias[0]), "ssd_alog": _pad_lanes(ssd_A_log[0]),
        "ssd_dexp": jnp.repeat(ssd_D[0], SSD_HEAD_DIM).reshape(1, W_SSD),
        "ssd_ng": ssd_norm_g[0].reshape(1, W_SSD),
    }

    xp = x_prompt.reshape(bp * seq, d)
    xs = x_sample.reshape(bs, d)
    tm = 1024
    up_kw = dict(n_out=D_FF, swiglu=True, out_dtype=BF16)

    mod_a_all, silu_c = _ada(c_all, w_ada[0], b_ada[0], 2 * d)
    mod_a_p, mod_a_s = split_rows(mod_a_all)

    hp, hs = _norm_rows(xp, xs, g_ffn1[0], mod_a_p, mod_a_s, 0, tm=tm)
    first, (hmid_s,), wb = _proj_first_tile(
        hp, [(w_up1[0], 0), (w_up1[0], up_blocks)], tm=tm, tn=512, side=hs, **up_kw)
    (hmid,), ((w_down_b,), (mod_b_all,), (w_in_b,)) = _proj_other_tiles(
        hp, wb, first, tm=tm, tn=512,
        streams=[_cast_stream(w_down1[0], 32),
                 _ada_stream(silu_c, w_ada[0], b_ada[0], 2 * d, 3 * d, 256),
                 _cast_stream(w_in_t, 32, start=32, rows=IN_MAIN)],
        **up_kw)
    mod_b_p, mod_b_s = split_rows(mod_b_all)
    (xp, hp), (xs, hs) = _resid([hmid], [hmid_s], [(w_down_b, 0)], xp, xs, mod_b_p, mod_b_s, 0,
                                g_mix[0], mod_b_p, mod_b_s, 1, factor=0.5, tm=256, emit_x=True,
                                h_dtype=BF16)

    (proj, dt_raw), (proj_s, dt_raw_s), _, ((w_out_b,),) = _proj(
        hp, [(w_in_b, 0)], n_out=IN_MAIN, tm=tm, tn=IN_MAIN // 3, swiglu=False, out_dtype=F32,
        row_tiles=(0, bp * seq // tm), side=hs, trans_w=True, w_extra=w_dt_t,
        streams=[_cast_stream(w_out[0], 16)])
    lconv = state_lru_conv[0].reshape(bs, (CONV_W - 1) * W_LRU)
    sconv = state_ssd_conv[0].reshape(bs, (CONV_W - 1) * SSD_CONV_DIM)
    out_l_s, lru_h_s, lconv_new, sconv_new, xs_act, xdt, bc, dec = _sample_pre(
        proj_s, dt_raw_s, state_lru_h[0], lconv, sconv, p)
    out_l, lru_h_p, ((ssm_s, y_raw), (modf_all,)) = _lru_prompt(
        proj, bp, seq, p["lru_cw"], p["lru_cb"], p["lru_wg"], p["lru_ba"], p["lru_bi"],
        p["lru_lam"],
        streams=[_state_stream(state_ssm[0].reshape(bs, W_SSD, SSD_STATE), xdt, bc, dec),
                 _ada_stream(silu_c, w_ada_f, b_ada_f, 0, 2 * d, 2 * d // (bp * seq // LRU_TIME_TILE))])
    y_ssd_s = _sample_post(y_raw, xs_act, proj_s, p["ssd_dexp"], p["ssd_ng"])
    y_ssd, ssm_p, ((mod_c_all,),) = _ssd_prompt(
        proj, dt_raw, bp, seq, p["ssd_cw"], p["ssd_cb"], p["ssd_dtb"], p["ssd_alog"],
        p["ssd_dexp"], p["ssd_ng"],
        streams=[_ada_stream(silu_c, w_ada[0], b_ada[0], 5 * d, (N_MOD - 5) * d,
                             (N_MOD - 5) * d // (bp * seq // SSD_CHUNK))])
    mod_c_p, mod_c_s = split_rows(mod_c_all)
    modf_p, modf_s = split_rows(modf_all)
    proj3 = proj.reshape(bp, seq, IN_MAIN)
    lru_buf_p = proj3[:, seq - (CONV_W - 1):, :W_LRU]
    ssd_buf_p = proj3[:, seq - (CONV_W - 1):, 2 * W_LRU + W_SSD:]

    (xp, hp), (xs, hs) = _resid([out_l, y_ssd], [out_l_s, y_ssd_s], [(w_out_b, 0), (w_out_b, 1)],
                                xp, xs, mod_c_p, mod_c_s, 0, g_ffn2[0], mod_c_p, mod_c_s, 1,
                                factor=1.0, tm=512, emit_x=True, h_dtype=BF16)

    first, (hmid_s,), wb = _proj_first_tile(
        hp, [(w_up2[0], 0), (w_up2[0], up_blocks)], tm=tm, tn=512, side=hs, **up_kw)
    (hmid,), ((w_down_b,),) = _proj_other_tiles(
        hp, wb, first, tm=tm, tn=512, streams=[_cast_stream(w_down2[0], 32)], **up_kw)
    (yp,), (ys,) = _resid([hmid], [hmid_s], [(w_down_b, 0)], xp, xs, mod_c_p, mod_c_s, 3, g_final,
                          modf_p, modf_s, 0, factor=0.5, tm=256, emit_x=False, h_dtype=F32)

    stack = lambda v: v[None]
    return (yp.reshape(bp, seq, d), ys.reshape(bs, 1, d),
            stack(lru_h_p), stack(lru_buf_p), stack(ssm_p), stack(ssd_buf_p),
            stack(lru_h_s), stack(lconv_new.reshape(bs, CONV_W - 1, W_LRU)),
            stack(ssm_s.reshape(bs, SSD_HEADS, SSD_HEAD_DIM, SSD_STATE)),
            stack(sconv_new.reshape(bs, CONV_W - 1, SSD_CONV_DIM)))
```

```python
import functools
from typing import Callable, NamedTuple

import jax
import jax.numpy as jnp
from jax import lax
from jax.experimental import pallas as pl
from jax.experimental.pallas import tpu as pltpu

F32 = jnp.float32
BF16 = jnp.bfloat16

D_MODEL = 2048
D_FF = 5632
W_LRU = 1024
W_SSD = 1024
LRU_HEADS = 16
LRU_BLOCK = 64
LRU_C = 8.0
SSD_HEADS = 16
SSD_HEAD_DIM = 64
SSD_GROUPS = 2
SSD_HPG = 8
SSD_STATE = 128
SSD_CHUNK = 128
SSD_CHUNKS_PER_STEP = 2
CONV_W = 4
SSD_CONV_DIM = W_SSD + 2 * SSD_GROUPS * SSD_STATE
IN_MAIN = 2 * W_LRU + W_SSD + SSD_CONV_DIM
N_MOD = 9
EPS = 1e-6

LANES = 128
SUBLANES = 8
VMEM_LIMIT_BYTES = 56 * 1024 * 1024

LRU_GATE_GROUP = 256
LRU_TIME_TILE = 512
SCAN_ROWS = 2 * SUBLANES


def _sigmoid(v):
    return 0.5 * (jnp.tanh(0.5 * v) + 1.0)


def _silu(v):
    return v * _sigmoid(v)


def _softplus(v):
    return jnp.maximum(v, 0.0) + jnp.log1p(jnp.exp(-jnp.abs(v)))


def _gelu_tanh(v):
    return 0.5 * v * (1.0 + jnp.tanh(0.7978845608028654 * (v + 0.044715 * (v * v * v))))


def _bdot(a, b):
    return jnp.dot(a, b, preferred_element_type=F32)


def _params(sem):
    return pltpu.CompilerParams(dimension_semantics=sem, vmem_limit_bytes=VMEM_LIMIT_BYTES)


def _ada_mm_kernel(s_ref, w_ref, b_ref, o_ref):
    o_ref[...] = _bdot(s_ref[...], w_ref[...].astype(BF16)) + b_ref[...]


def _ada_kernel(c_ref, w_ref, b_ref, o_ref, s_ref):
    s_ref[...] = _silu(c_ref[...]).astype(BF16)
    _ada_mm_kernel(s_ref, w_ref, b_ref, o_ref)


def _ada(c, w, b, cols, tn=1024):
    m, k = c.shape
    n = cols
    return pl.pallas_call(
        _ada_kernel,
        grid=(n // tn,),
        in_specs=[pl.BlockSpec((m, k), lambda j: (0, 0)),
                  pl.BlockSpec((k, tn), lambda j: (0, j)),
                  pl.BlockSpec((1, tn), lambda j: (0, j))],
        out_specs=[pl.BlockSpec((m, tn), lambda j: (0, j)), pl.BlockSpec((m, k), lambda j: (0, 0))],
        out_shape=[jax.ShapeDtypeStruct((m, n), F32), jax.ShapeDtypeStruct((m, k), BF16)],
        compiler_params=_params(("arbitrary",)),
        name="ada_proj",
    )(c, w, b.reshape(1, -1))


def _norm_modulate(x, gain, shift, scale):
    ms = jnp.mean(x * x, axis=-1, keepdims=True)
    y = x * lax.rsqrt(ms + EPS) * gain
    return y * (1.0 + scale) + shift


def _norm_rows_kernel(x_ref, gain_ref, sh_ref, sc_ref, xs_ref, shs_ref, scs_ref, o_ref, os_ref):
    o_ref[...] = _norm_modulate(x_ref[...], gain_ref[...], sh_ref[...],
                                sc_ref[...]).astype(o_ref.dtype)

    @pl.when(pl.program_id(0) == 0)
    def _():
        os_ref[...] = _norm_modulate(xs_ref[...], gain_ref[...], shs_ref[...],
                                     scs_ref[...]).astype(os_ref.dtype)


def _norm_rows(x, x_s, gain, mod, mod_s, shift_chunk, *, tm):
    m, d = x.shape
    ns = x_s.shape[0]
    tiles_per_group = (m // tm) // mod.shape[0]
    once = dict(pipeline_mode=pl.Buffered(1))
    mod_spec = lambda c: pl.BlockSpec((None, 1, d), lambda i: (i // tiles_per_group, 0, c))
    mod_s_spec = lambda c: pl.BlockSpec((None, ns, d), lambda i: (0, 0, c), **once)
    return pl.pallas_call(
        _norm_rows_kernel,
        grid=(m // tm,),
        in_specs=[pl.BlockSpec((tm, d), lambda i: (i, 0)), pl.BlockSpec((1, d), lambda i: (0, 0)),
                  mod_spec(shift_chunk), mod_spec(shift_chunk + 1),
                  pl.BlockSpec((ns, d), lambda i: (0, 0), **once),
                  mod_s_spec(shift_chunk), mod_s_spec(shift_chunk + 1)],
        out_specs=[pl.BlockSpec((tm, d), lambda i: (i, 0)), pl.BlockSpec((ns, d), lambda i: (0, 0))],
        out_shape=[jax.ShapeDtypeStruct((m, d), BF16), jax.ShapeDtypeStruct((ns, d), BF16)],
        compiler_params=_params(("arbitrary",)),
        name="norm_rows",
    )(x, gain.reshape(1, d), mod, mod, x_s, mod_s, mod_s)


class _Stream(NamedTuple):
    ins: list
    outs: list
    body: Callable
    start: int
    steps: int


def _stream_io(streams, n_inner):
    in_specs, args, out_specs, out_shape = [], [], [], []

    def spec(st, block, index_fn):
        return pl.BlockSpec(
            block, lambda i, j: index_fn(jnp.clip(i * n_inner + j - st.start, 0, st.steps - 1)))

    for st in streams:
        for arr, block, index_fn in st.ins:
            in_specs.append(spec(st, block, index_fn))
            args.append(arr)
        for shape, block, index_fn in st.outs:
            out_specs.append(spec(st, block, index_fn))
            out_shape.append(shape)
    return in_specs, args, out_specs, out_shape


def _run_streams(streams, in_refs, out_refs, n_inner, n_steps):
    step = pl.program_id(0) * n_inner + pl.program_id(1)
    in_refs, out_refs = iter(in_refs), iter(out_refs)
    for st in streams:
        ins = [next(in_refs) for _ in st.ins]
        outs = [next(out_refs) for _ in st.outs]
        if st.start == 0 and st.steps == n_steps:
            st.body(*ins, *outs)
            continue

        @pl.when((step >= st.start) & (step < st.start + st.steps))
        def _(st=st, ins=ins, outs=outs):
            st.body(*ins, *outs)


def _split_stream_outs(streams, flat):
    flat = list(flat)
    return [[flat.pop(0) for _ in st.outs] for st in streams]


def _cast_body(src_ref, dst_ref):
    dst_ref[...] = src_ref[...].astype(dst_ref.dtype)


def _cast_stream(w, chunks, start=0, rows=None):
    rows = w.shape[0] if rows is None else rows
    block = (rows // chunks, w.shape[1])
    index = lambda k: (k, 0)
    return _Stream([(w, block, index)],
                   [(jax.ShapeDtypeStruct((rows, w.shape[1]), BF16), block, index)],
                   _cast_body, start, chunks)


def _ada_stream(sc, w, b, col0, cols, tn, start=0):
    m, k = sc.shape
    t0 = col0 // tn
    return _Stream(
        [(sc, (m, k), lambda s: (0, 0)), (w, (k, tn), lambda s: (0, s + t0)),
         (b.reshape(1, -1), (1, tn), lambda s: (0, s + t0))],
        [(jax.ShapeDtypeStruct((m, cols), F32), (m, tn), lambda s: (0, s))],
        _ada_mm_kernel, start, cols // tn)


def _wdot(h, w, trans_w):
    if trans_w:
        return lax.dot_general(h, w, (((1,), (1,)), ((), ())), preferred_element_type=F32)
    return _bdot(h, w)


def _proj_kernel(*refs, n_w, swiglu, trans_w, has_extra, has_side, emit_bf16, streams, n_prev,
                 nj, n_steps):
    it = iter(refs)
    x_ref = next(it)
    xs_ref = next(it) if has_side else None
    w_refs = [next(it) for _ in range(n_w)]
    wx_ref = next(it) if has_extra else None
    stream_ins = [next(it) for st in streams for _ in st.ins]
    for _ in range(n_prev):
        next(it)
    o_ref = next(it)
    ox_ref = next(it) if has_extra else None
    os_ref = next(it) if has_side else None
    osx_ref = next(it) if has_side and has_extra else None
    wo_refs = [next(it) for _ in range(n_w)] if emit_bf16 else []
    stream_outs = [next(it) for st in streams for _ in st.outs]

    j = pl.program_id(1)
    first_tile = pl.program_id(0) == 0

    if has_extra:
        @pl.when(j == 0)
        def _():
            ox_ref[...] = _wdot(x_ref[...], wx_ref[...].astype(BF16), trans_w)

        if has_side:
            @pl.when((j == 0) & first_tile)
            def _():
                osx_ref[...] = _wdot(xs_ref[...], wx_ref[...].astype(BF16), trans_w)

    wbs = [w_ref[...].astype(BF16) for w_ref in w_refs]
    for wo_ref, wb in zip(wo_refs, wbs):
        wo_ref[...] = wb

    def project(h, out_ref):
        if swiglu:
            g = _wdot(h, wbs[0], trans_w)
            u = _wdot(h, wbs[1], trans_w)
            out_ref[...] = (_silu(g) * u).astype(out_ref.dtype)
        else:
            out_ref[...] = _wdot(h, wbs[0], trans_w).astype(out_ref.dtype)

    project(x_ref[...], o_ref)
    if has_side:
        @pl.when(first_tile)
        def _():
            project(xs_ref[...], os_ref)

    _run_streams(streams, stream_ins, stream_outs, nj, n_steps)


def _proj(x, ws, *, n_out, tm, tn, swiglu, out_dtype, row_tiles, side=None, trans_w=False,
          w_extra=None, emit_bf16=False, streams=(), prev=None):
    m, d = x.shape
    t0, t1 = row_tiles
    nj = n_out // tn
    has_side = side is not None
    has_extra = w_extra is not None
    single_row_tile = t1 - t0 == 1
    once = dict(pipeline_mode=pl.Buffered(1))

    def w_spec(off):
        if trans_w:
            return pl.BlockSpec((tn, d), lambda i, j: (j + off, 0))
        return pl.BlockSpec((d, tn), lambda i, j: (0, j + off))

    x_mode = once if single_row_tile else {}
    in_specs = [pl.BlockSpec((tm, d), lambda i, j: (i + t0, 0), **x_mode)]
    args = [x]
    if has_side:
        ns = side.shape[0]
        in_specs.append(pl.BlockSpec((ns, d), lambda i, j: (0, 0), **once))
        args.append(side)
    in_specs += [w_spec(off) for _, off in ws]
    args += [w for w, _ in ws]
    if has_extra:
        nx = w_extra.shape[0] if trans_w else w_extra.shape[1]
        in_specs.append(pl.BlockSpec(w_extra.shape, lambda i, j: (0, 0)))
        args.append(w_extra)
    assert all(st.start + st.steps <= (t1 - t0) * nj for st in streams)
    st_in_specs, st_args, st_out_specs, st_out_shape = _stream_io(streams, nj)
    in_specs += st_in_specs
    args += st_args
    prev = list(prev or [])
    aliases = {}
    for k, buf in enumerate(prev):
        aliases[len(args)] = k
        in_specs.append(pl.BlockSpec(memory_space=pl.ANY))
        args.append(buf)

    out_specs = [pl.BlockSpec((tm, tn), lambda i, j: (i + t0, j))]
    out_shape = [jax.ShapeDtypeStruct((m, n_out), out_dtype)]
    if has_extra:
        out_specs.append(pl.BlockSpec((tm, nx), lambda i, j: (i + t0, 0)))
        out_shape.append(jax.ShapeDtypeStruct((m, nx), F32))
    n_main = len(out_shape)
    if has_side:
        out_specs.append(pl.BlockSpec((ns, tn), lambda i, j: (0, jnp.where(i == 0, j, nj - 1))))
        out_shape.append(jax.ShapeDtypeStruct((ns, n_out), out_dtype))
        if has_extra:
            out_specs.append(pl.BlockSpec((ns, nx), lambda i, j: (0, 0)))
            out_shape.append(jax.ShapeDtypeStruct((ns, nx), F32))
    n_side = len(out_shape) - n_main
    if emit_bf16:
        assert single_row_tile, "weight copies are written once per column tile"
        for _ in ws:
            out_specs.append(w_spec(0))
            out_shape.append(jax.ShapeDtypeStruct((n_out, d) if trans_w else (d, n_out), BF16))
    out_specs += st_out_specs
    out_shape += st_out_shape
    outs = pl.pallas_call(
        functools.partial(_proj_kernel, n_w=len(ws), swiglu=swiglu, trans_w=trans_w,
                          has_extra=has_extra, has_side=has_side, emit_bf16=emit_bf16,
                          streams=tuple(streams), n_prev=len(prev), nj=nj,
                          n_steps=(t1 - t0) * nj),
        grid=(t1 - t0, nj),
        in_specs=in_specs,
        out_specs=out_specs,
        out_shape=out_shape,
        input_output_aliases=aliases,
        compiler_params=_params(("arbitrary" if streams or has_side else "parallel", "arbitrary")),
        name="proj_swiglu" if swiglu else "proj",
    )(*args)
    n_wb = len(ws) if emit_bf16 else 0
    main, rest = outs[:n_main], outs[n_main:]
    side_outs, rest = rest[:n_side], rest[n_side:]
    wb, rest = rest[:n_wb], rest[n_wb:]
    return main, side_outs, wb, _split_stream_outs(streams, rest)


def _proj_first_tile(x, ws_f32, *, tm, tn, side, **kw):
    main, side_outs, wb, _ = _proj(x, ws_f32, tm=tm, tn=tn, row_tiles=(0, 1), side=side,
                                   emit_bf16=True, **kw)
    return main, side_outs, wb


def _proj_other_tiles(x, wb, prev, *, tm, tn, streams, **kw):
    main, _, _, stream_outs = _proj(x, [(w, 0) for w in wb], tm=tm, tn=tn,
                                    row_tiles=(1, x.shape[0] // tm), prev=prev, streams=streams,
                                    **kw)
    return main, stream_outs


def _resid_kernel(*refs, n_lhs, factor, emit_x):
    it = iter(refs)
    lhs_refs = [next(it) for _ in range(n_lhs)]
    lhs_s_refs = [next(it) for _ in range(n_lhs)]
    w_refs = [next(it) for _ in range(n_lhs)]
    x_ref, gate_ref, gain_ref, sh_ref, sc_ref = (next(it) for _ in range(5))
    xs_ref, gate_s_ref, sh_s_ref, sc_s_ref = (next(it) for _ in range(4))
    n_out = 2 if emit_x else 1
    outs = [next(it) for _ in range(n_out)]
    outs_s = [next(it) for _ in range(n_out)]

    def update(lhs, x_in, gate, sh, sc, out_refs):
        acc = _bdot(lhs[0][...], w_refs[0][...])
        for l_ref, w_ref in zip(lhs[1:], w_refs[1:]):
            acc = acc + _bdot(l_ref[...], w_ref[...])
        x_new = x_in[...] + (factor * gate[...]) * acc
        if emit_x:
            out_refs[0][...] = x_new
        h_ref = out_refs[-1]
        h_ref[...] = _norm_modulate(x_new, gain_ref[...], sh[...], sc[...]).astype(h_ref.dtype)

    update(lhs_refs, x_ref, gate_ref, sh_ref, sc_ref, outs)

    @pl.when(pl.program_id(0) == 0)
    def _():
        update(lhs_s_refs, xs_ref, gate_s_ref, sh_s_ref, sc_s_ref, outs_s)


def _resid(lhs_list, lhs_s_list, ws, x, x_s, mod, mod_s, gate_chunk, gain_next, mod_next,
           mod_next_s, shift_chunk_next, *, factor, tm, emit_x, h_dtype):
    m, d = x.shape
    ns = x_s.shape[0]
    groups = mod.shape[0]
    tiles_per_group = (m // tm) // groups
    kp = lhs_list[0].shape[1]
    once = dict(pipeline_mode=pl.Buffered(1))

    def mod_spec(chunk):
        return pl.BlockSpec((None, 1, d), lambda i: (i // tiles_per_group, 0, chunk))

    def mod_s_spec(chunk):
        return pl.BlockSpec((None, ns, d), lambda i: (0, 0, chunk), **once)

    in_specs = [pl.BlockSpec((tm, kp), lambda i: (i, 0)) for _ in lhs_list]
    in_specs += [pl.BlockSpec((ns, kp), lambda i: (0, 0), **once) for _ in lhs_s_list]
    in_specs += [pl.BlockSpec((kp, d), lambda i, k=k: (k, 0), **once) for _, k in ws]
    in_specs += [pl.BlockSpec((tm, d), lambda i: (i, 0)), mod_spec(gate_chunk),
                 pl.BlockSpec((1, d), lambda i: (0, 0)),
                 mod_spec(shift_chunk_next), mod_spec(shift_chunk_next + 1),
                 pl.BlockSpec((ns, d), lambda i: (0, 0), **once), mod_s_spec(gate_chunk),
                 mod_s_spec(shift_chunk_next), mod_s_spec(shift_chunk_next + 1)]
    row = pl.BlockSpec((tm, d), lambda i: (i, 0))
    row_s = pl.BlockSpec((ns, d), lambda i: (0, 0))
    dtypes = ([F32] if emit_x else []) + [h_dtype]
    out_specs = [row for _ in dtypes] + [row_s for _ in dtypes]
    out_shape = ([jax.ShapeDtypeStruct((m, d), t) for t in dtypes]
                 + [jax.ShapeDtypeStruct((ns, d), t) for t in dtypes])
    outs = pl.pallas_call(
        functools.partial(_resid_kernel, n_lhs=len(lhs_list), factor=factor, emit_x=emit_x),
        grid=(m // tm,),
        in_specs=in_specs,
        out_specs=out_specs,
        out_shape=out_shape,
        compiler_params=_params(("arbitrary",)),
        name="resid",
    )(*lhs_list, *lhs_s_list, *[w for w, _ in ws], x, mod, gain_next.reshape(1, d), mod_next,
      mod_next, x_s, mod_s, mod_next_s, mod_next_s)
    return outs[:len(dtypes)], outs[len(dtypes):]


def _lru_gates(xc, wg_ref, ba, bi, sp):
    a_parts, b_parts = [], []
    for g in range(W_LRU // LRU_GATE_GROUP):
        cols = slice(g * LRU_GATE_GROUP, (g + 1) * LRU_GATE_GROUP)
        xg = xc[:, cols]
        ri = _bdot(xg.astype(BF16), wg_ref[g].astype(BF16))
        r = _sigmoid(ri[:, :LRU_GATE_GROUP] + ba[:, cols])
        i = _sigmoid(ri[:, LRU_GATE_GROUP:] + bi[:, cols])
        log_a = (-LRU_C * r) * sp[:, cols]
        a = jnp.exp(log_a)
        a_parts.append(a)
        b_parts.append(jnp.sqrt(1.0 - a * a) * (i * xg))
    return jnp.concatenate(a_parts, axis=1), jnp.concatenate(b_parts, axis=1)


def _causal_conv(prev8, x, w_ref, b_ref):
    rows, width = x.shape
    rid = lax.broadcasted_iota(jnp.int32, (SUBLANES, width), 0)
    shifts = (1, 2, 3)
    taps = [w_ref[k:k + 1, :] for k in range(CONV_W)]
    bias = b_ref[...]
    prev_rot = [pltpu.roll(prev8, k, 0) for k in shifts]
    out = []
    for r in range(rows // SUBLANES):
        cur = x[r * SUBLANES:(r + 1) * SUBLANES, :]
        cur_rot = [pltpu.roll(cur, k, 0) for k in shifts]
        s1, s2, s3 = [jnp.where(rid < k, p, c) for k, p, c in zip(shifts, prev_rot, cur_rot)]
        out.append(bias + taps[0] * s3 + taps[1] * s2 + taps[2] * s1 + taps[3] * cur)
        prev_rot = cur_rot
    return jnp.concatenate(out, axis=0)


def _lru_prompt_kernel(*refs, streams, nt, n_steps):
    n_in = sum(len(st.ins) for st in streams)
    n_out = sum(len(st.outs) for st in streams)
    xl_ref, gl_ref, cw_ref, cb_ref, wg_ref, ba_ref, bi_ref, lam_ref = refs[:8]
    stream_ins = refs[8:8 + n_in]
    o_ref, hT_ref = refs[8 + n_in:10 + n_in]
    stream_outs = refs[10 + n_in:10 + n_in + n_out]
    xbuf, a_scr, b_scr, hcar = refs[10 + n_in + n_out:]
    t = pl.program_id(1)
    tt = xl_ref.shape[0]

    @pl.when(t == 0)
    def _():
        xbuf[...] = jnp.zeros_like(xbuf)
        hcar[...] = jnp.zeros_like(hcar)

    _run_streams(streams, stream_ins, stream_outs, nt, n_steps)

    x = xl_ref[...]
    xc = _causal_conv(xbuf[...], x, cw_ref, cb_ref)
    xbuf[...] = x[tt - SUBLANES:, :]

    sp = _softplus(-lam_ref[...])
    a, bt = _lru_gates(xc, wg_ref, ba_ref[...], bi_ref[...], sp)
    a_scr[...] = a
    b_scr[...] = bt

    rid = lax.broadcasted_iota(jnp.int32, (SUBLANES, W_LRU), 0)

    def scan8(a8, b8, h_in):
        for s in (1, 2, 4):
            a_sh = pltpu.roll(a8, s, 0)
            b_sh = pltpu.roll(b8, s, 0)
            m = rid >= s
            b8 = jnp.where(m, a8 * b_sh + b8, b8)
            a8 = jnp.where(m, a8 * a_sh, a8)
        h8 = a8 * h_in + b8
        return h8, jnp.broadcast_to(h8[SUBLANES - 1:SUBLANES, :], (SUBLANES, W_LRU))

    def body(g, h_in):
        r0 = pl.multiple_of(g * SCAN_ROWS, SCAN_ROWS)
        lo = pl.ds(r0, SUBLANES)
        hi = pl.ds(r0 + SUBLANES, SUBLANES)
        h_lo, h_mid = scan8(a_scr[lo, :], b_scr[lo, :], h_in)
        h_hi, h_out = scan8(a_scr[hi, :], b_scr[hi, :], h_mid)
        rows = pl.ds(r0, SCAN_ROWS)
        h16 = jnp.concatenate([h_lo, h_hi], axis=0)
        o_ref[rows, :] = (h16 * _gelu_tanh(gl_ref[rows, :])).astype(o_ref.dtype)
        return h_out

    h_last = lax.fori_loop(0, tt // SCAN_ROWS, body, hcar[...])
    hcar[...] = h_last

    @pl.when(t == pl.num_programs(1) - 1)
    def _():
        hT_ref[...] = h_last[0:1, :]


def _lru_prompt(proj, batch, seq, cw, cb, wg, ba, bi, lam, streams=()):
    tt = LRU_TIME_TILE
    nt = seq // tt
    assert all(st.start + st.steps <= batch * nt for st in streams)
    st_in_specs, st_args, st_out_specs, st_out_shape = _stream_io(streams, nt)
    row = lambda v: v.reshape(1, W_LRU)
    full = lambda shape: pl.BlockSpec(shape, lambda b, t: (0,) * len(shape))
    out, h_t, *rest = pl.pallas_call(
        functools.partial(_lru_prompt_kernel, streams=tuple(streams), nt=nt, n_steps=batch * nt),
        grid=(batch, nt),
        in_specs=[pl.BlockSpec((tt, W_LRU), lambda b, t: (b * nt + t, 0)),
                  pl.BlockSpec((tt, W_LRU), lambda b, t: (b * nt + t, 1)),
                  full((CONV_W, W_LRU)), full((1, W_LRU)), full(wg.shape),
                  full((1, W_LRU)), full((1, W_LRU)), full((1, W_LRU))] + st_in_specs,
        out_specs=[pl.BlockSpec((tt, W_LRU), lambda b, t: (b * nt + t, 0)),
                   pl.BlockSpec((None, 1, W_LRU), lambda b, t: (b, 0, 0))] + st_out_specs,
        out_shape=[jax.ShapeDtypeStruct((batch * seq, W_LRU), BF16),
                   jax.ShapeDtypeStruct((batch, 1, W_LRU), F32)] + st_out_shape,
        scratch_shapes=[pltpu.VMEM((SUBLANES, W_LRU), F32),
                        pltpu.VMEM((tt, W_LRU), F32),
                        pltpu.VMEM((tt, W_LRU), F32),
                        pltpu.VMEM((SUBLANES, W_LRU), F32)],
        compiler_params=_params(("arbitrary" if streams else "parallel", "arbitrary")),
        name="lru_prompt",
    )(proj, proj, cw, row(cb), wg, row(ba), row(bi), row(lam), *st_args)
    return out, h_t.reshape(batch, W_LRU), _split_stream_outs(streams, rest)


def _ssd_chunk(z_ref, xbc_ref, dt_ref, cw_ref, cb_ref, dtb_ref, alog_ref, dexp_ref, ng_ref, y_ref,
               xbuf, st_scr, y_scr, m_scr, xbd_scr):
    lc = SSD_CHUNK
    x = xbc_ref[...]
    act = _silu(_causal_conv(xbuf[...], x, cw_ref, cb_ref))
    xbuf[...] = x[lc - SUBLANES:, :]
    xs = act[:, :W_SSD]
    bm = act[:, W_SSD:W_SSD + SSD_GROUPS * SSD_STATE]
    cm = act[:, W_SSD + SSD_GROUPS * SSD_STATE:]

    dt = _softplus(dt_ref[...] + dtb_ref[...])
    d_a = dt * (-jnp.exp(alog_ref[...]))
    row_i = lax.broadcasted_iota(jnp.int32, (lc, lc), 0)
    col_i = lax.broadcasted_iota(jnp.int32, (lc, lc), 1)
    causal = row_i >= col_i
    tril = jnp.where(causal, 1.0, 0.0).astype(F32)
    cs = jnp.dot(tril, d_a, preferred_element_type=F32, precision=lax.Precision.HIGHEST)
    cs_t = cs.T
    dt_t = dt.T
    cs_last = cs[lc - 1:lc, :]

    def per_head_lanes(v):
        rows = v.shape[0]
        return jnp.concatenate(
            [jnp.broadcast_to(v[:, h:h + 1], (rows, SSD_HEAD_DIM)) for h in range(SSD_HEADS)],
            axis=1)

    w_exp = per_head_lanes(jnp.exp(cs_last - cs) * dt)
    ecs_exp = per_head_lanes(jnp.exp(cs))
    cd_exp = per_head_lanes(jnp.exp(cs_last))
    gw = SSD_HPG * SSD_HEAD_DIM
    low_half = col_i < SSD_HEAD_DIM

    for g in range(SSD_GROUPS):
        ncols = slice(g * SSD_STATE, (g + 1) * SSD_STATE)
        gcols = slice(g * gw, (g + 1) * gw)
        b_g = bm[:, ncols].astype(BF16)
        c_g = cm[:, ncols].astype(BF16)
        cb_mat = lax.dot_general(c_g, b_g, (((1,), (1,)), ((), ())),
                                 preferred_element_type=F32)
        for e in range(SSD_HPG):
            h = g * SSD_HPG + e
            cs_col = jnp.broadcast_to(cs[:, h:h + 1], (lc, lc))
            l_mat = jnp.exp(jnp.where(causal, cs_col - cs_t[h:h + 1, :], -jnp.inf))
            m_scr[g, :, e * lc:(e + 1) * lc] = (cb_mat * l_mat * dt_t[h:h + 1, :]).astype(BF16)
        for q in range(SSD_HPG // 2):
            lanes = slice(q * LANES, (q + 1) * LANES)
            slab = xs[:, g * gw + q * LANES:g * gw + (q + 1) * LANES]
            xbd_scr[g, (2 * q) * lc:(2 * q + 1) * lc, lanes] = jnp.where(
                low_half, slab, 0.0).astype(BF16)
            xbd_scr[g, (2 * q + 1) * lc:(2 * q + 2) * lc, lanes] = jnp.where(
                low_half, 0.0, slab).astype(BF16)
        st_g = st_scr[:, gcols]
        y_off = _bdot(c_g, st_g.astype(BF16)) * ecs_exp[:, gcols]
        y_scr[:, gcols] = (_bdot(m_scr[g], xbd_scr[g]) + y_off
                           + dexp_ref[:, gcols] * xs[:, gcols])
        xw = (xs[:, gcols] * w_exp[:, gcols]).astype(BF16)
        st_scr[:, gcols] = cd_exp[:, gcols] * st_g + lax.dot_general(
            b_g, xw, (((0,), (0,)), ((), ())), preferred_element_type=F32)

    yg = y_scr[...] * _silu(z_ref[...])
    ms = jnp.mean(yg * yg, axis=-1, keepdims=True)
    y_ref[...] = (yg * lax.rsqrt(ms + EPS) * ng_ref[...]).astype(y_ref.dtype)


def _ssd_prompt_kernel(*refs, streams, nc, n_steps):
    n_in = sum(len(st.ins) for st in streams)
    n_out = sum(len(st.outs) for st in streams)
    z_ref, xbc_ref, dt_ref, cw_ref, cb_ref, dtb_ref, alog_ref, dexp_ref, ng_ref = refs[:9]
    stream_ins = refs[9:9 + n_in]
    y_ref, st_ref = refs[9 + n_in:11 + n_in]
    stream_outs = refs[11 + n_in:11 + n_in + n_out]
    xbuf, st_scr, y_scr, m_scr, xbd_scr = refs[11 + n_in + n_out:]
    c = pl.program_id(1)

    @pl.when(c == 0)
    def _():
        xbuf[...] = jnp.zeros_like(xbuf)
        st_scr[...] = jnp.zeros_like(st_scr)
        xbd_scr[...] = jnp.zeros_like(xbd_scr)

    _run_streams(streams, stream_ins, stream_outs, nc, n_steps)

    for cc in range(SSD_CHUNKS_PER_STEP):
        rows = pl.ds(cc * SSD_CHUNK, SSD_CHUNK)
        _ssd_chunk(z_ref.at[rows], xbc_ref.at[rows], dt_ref.at[rows], cw_ref, cb_ref, dtb_ref,
                   alog_ref, dexp_ref, ng_ref, y_ref.at[rows], xbuf, st_scr, y_scr.at[cc],
                   m_scr.at[cc], xbd_scr.at[cc])

    @pl.when(c == pl.num_programs(1) - 1)
    def _():
        st_ref[...] = st_scr[...].T


def _ssd_prompt(proj, dt_raw, batch, seq, cw, cb, dtb, alog, dexp, ng, streams=()):
    lc = SSD_CHUNK
    cps = SSD_CHUNKS_PER_STEP
    tl = cps * lc
    nc = seq // tl
    assert all(st.start + st.steps <= batch * nc for st in streams)
    st_in_specs, st_args, st_out_specs, st_out_shape = _stream_io(streams, nc)
    full = lambda shape: pl.BlockSpec(shape, lambda b, c: (0,) * len(shape))
    z_blk = (2 * W_LRU) // W_SSD
    xbc_blk = (2 * W_LRU + W_SSD) // SSD_CONV_DIM
    y, st, *rest = pl.pallas_call(
        functools.partial(_ssd_prompt_kernel, streams=tuple(streams), nc=nc, n_steps=batch * nc),
        grid=(batch, nc),
        in_specs=[pl.BlockSpec((tl, W_SSD), lambda b, c: (b * nc + c, z_blk)),
                  pl.BlockSpec((tl, SSD_CONV_DIM), lambda b, c: (b * nc + c, xbc_blk)),
                  pl.BlockSpec((tl, LANES), lambda b, c: (b * nc + c, 0)),
                  full((CONV_W, SSD_CONV_DIM)), full((1, SSD_CONV_DIM)),
                  full((1, LANES)), full((1, LANES)), full((1, W_SSD)), full((1, W_SSD))]
        + st_in_specs,
        out_specs=[pl.BlockSpec((tl, W_SSD), lambda b, c: (b * nc + c, 0)),
                   pl.BlockSpec((None, W_SSD, SSD_STATE), lambda b, c: (b, 0, 0))] + st_out_specs,
        out_shape=[jax.ShapeDtypeStruct((batch * seq, W_SSD), BF16),
                   jax.ShapeDtypeStruct((batch, W_SSD, SSD_STATE), F32)] + st_out_shape,
        scratch_shapes=[pltpu.VMEM((SUBLANES, SSD_CONV_DIM), F32),
                        pltpu.VMEM((SSD_STATE, W_SSD), F32),
                        pltpu.VMEM((cps, lc, W_SSD), F32),
                        pltpu.VMEM((cps, SSD_GROUPS, lc, SSD_HPG * lc), BF16),
                        pltpu.VMEM((cps, SSD_GROUPS, SSD_HPG * lc, SSD_HPG * SSD_HEAD_DIM), BF16)],
        compiler_params=_params(("arbitrary" if streams else "parallel", "arbitrary")),
        name="ssd_prompt",
    )(proj, proj, dt_raw, cw, cb, dtb, alog, dexp, ng, *st_args)
    return (y, st.reshape(batch, SSD_HEADS, SSD_HEAD_DIM, SSD_STATE),
            _split_stream_outs(streams, rest))


def _sample_pre_kernel(proj_ref, dt_ref, h0_ref, lconv_ref, sconv_ref,
                       lcw_ref, lcb_ref, wg_ref, ba_ref, bi_ref, lam_ref,
                       scw_ref, scb_ref, dtb_ref, alog_ref,
                       outl_ref, hnew_ref, lconv_new_ref, sconv_new_ref,
                       xs_ref, xdt_ref, bc_ref, dec_ref):
    nb = proj_ref.shape[0]
    xl = proj_ref[:, 0:W_LRU]
    gl = proj_ref[:, W_LRU:2 * W_LRU]
    xbc = proj_ref[:, 2 * W_LRU + W_SSD:IN_MAIN]

    def conv1(state_ref, width, x_new, w_ref, b_ref):
        y = b_ref[...] + w_ref[0:1, :] * state_ref[:, 0:width]
        y = y + w_ref[1:2, :] * state_ref[:, width:2 * width]
        y = y + w_ref[2:3, :] * state_ref[:, 2 * width:3 * width]
        return y + w_ref[3:4, :] * x_new

    xc = conv1(lconv_ref, W_LRU, xl, lcw_ref, lcb_ref)
    a, bt = _lru_gates(xc, wg_ref, ba_ref[...], bi_ref[...], _softplus(-lam_ref[...]))
    h_new = a * h0_ref[...] + bt
    hnew_ref[...] = h_new
    outl_ref[...] = (h_new * _gelu_tanh(gl)).astype(outl_ref.dtype)
    lconv_new_ref[:, 0:2 * W_LRU] = lconv_ref[:, W_LRU:3 * W_LRU]
    lconv_new_ref[:, 2 * W_LRU:3 * W_LRU] = xl

    act = _silu(conv1(sconv_ref, SSD_CONV_DIM, xbc, scw_ref, scb_ref))
    sconv_new_ref[:, 0:2 * SSD_CONV_DIM] = sconv_ref[:, SSD_CONV_DIM:3 * SSD_CONV_DIM]
    sconv_new_ref[:, 2 * SSD_CONV_DIM:3 * SSD_CONV_DIM] = xbc
    xs = act[:, :W_SSD]
    xs_ref[...] = xs
    bc_ref[...] = act[:, W_SSD:]
    dt = _softplus(dt_ref[...] + dtb_ref[...])
    dec = jnp.exp(dt * (-jnp.exp(alog_ref[...])))
    for h in range(SSD_HEADS):
        pcols = slice(h * SSD_HEAD_DIM, (h + 1) * SSD_HEAD_DIM)
        xdt_ref[:, pcols] = xs[:, pcols] * jnp.broadcast_to(dt[:, h:h + 1], (nb, SSD_HEAD_DIM))
        dec_ref[h] = jnp.broadcast_to(dec[:, h:h + 1], (nb, SSD_STATE))


def _sample_pre(proj, dt_raw, h0, lconv, sconv, p):
    nb = proj.shape[0]
    out_shape = [jax.ShapeDtypeStruct((nb, W_LRU), BF16),
                 jax.ShapeDtypeStruct((nb, W_LRU), F32),
                 jax.ShapeDtypeStruct((nb, 3 * W_LRU), F32),
                 jax.ShapeDtypeStruct((nb, 3 * SSD_CONV_DIM), F32),
                 jax.ShapeDtypeStruct((nb, W_SSD), F32),
                 jax.ShapeDtypeStruct((nb, W_SSD), F32),
                 jax.ShapeDtypeStruct((nb, 2 * SSD_GROUPS * SSD_STATE), F32),
                 jax.ShapeDtypeStruct((SSD_HEADS, nb, SSD_STATE), F32)]
    return pl.pallas_call(
        _sample_pre_kernel,
        out_shape=out_shape,
        compiler_params=pltpu.CompilerParams(vmem_limit_bytes=VMEM_LIMIT_BYTES),
        name="sample_pre",
    )(proj, dt_raw, h0, lconv, sconv,
      p["lru_cw"], p["lru_cb"], p["lru_wg"], p["lru_ba"], p["lru_bi"], p["lru_lam"],
      p["ssd_cw"], p["ssd_cb"], p["ssd_dtb"], p["ssd_alog"])


def _sample_state_kernel(s_ref, xdt_ref, bc_ref, dec_ref, o_ref, y_ref):
    bb = s_ref.shape[0]
    half = SSD_HPG * SSD_HEAD_DIM
    rid = lax.broadcasted_iota(jnp.int32, (bb, W_SSD), 0)
    xdt = xdt_ref[...]
    bcb = bc_ref[...].astype(BF16)
    for k in range(bb):
        xk = jnp.where(rid == k, xdt, 0.0).astype(BF16)
        for g in range(SSD_GROUPS):
            rows = slice(g * half, (g + 1) * half)
            b_g = bcb[:, g * SSD_STATE:(g + 1) * SSD_STATE]
            c_g = bcb[:, (SSD_GROUPS + g) * SSD_STATE:(SSD_GROUPS + g + 1) * SSD_STATE]
            outer = lax.dot_general(xk[:, rows], b_g, (((0,), (0,)), ((), ())),
                                    preferred_element_type=F32)
            dec = jnp.concatenate(
                [jnp.broadcast_to(dec_ref[g * SSD_HPG + e, k:k + 1, :], (SSD_HEAD_DIM, SSD_STATE))
                 for e in range(SSD_HPG)], axis=0)
            s_new = dec * s_ref[k, rows, :] + outer
            o_ref[k, rows, :] = s_new
            yk = lax.dot_general(c_g, s_new.astype(BF16), (((1,), (1,)), ((), ())),
                                 preferred_element_type=F32)
            y_ref[k:k + 1, rows] = yk[k:k + 1, :]


def _state_stream(ssm, xdt, bc, dec, bb=8, start=0):
    nb = ssm.shape[0]
    state_block = (bb, W_SSD, SSD_STATE)
    return _Stream(
        [(ssm, state_block, lambda k: (k, 0, 0)),
         (xdt, (bb, W_SSD), lambda k: (k, 0)),
         (bc, (bb, 2 * SSD_GROUPS * SSD_STATE), lambda k: (k, 0)),
         (dec, (SSD_HEADS, bb, SSD_STATE), lambda k: (0, k, 0))],
        [(jax.ShapeDtypeStruct(ssm.shape, F32), state_block, lambda k: (k, 0, 0)),
         (jax.ShapeDtypeStruct((nb, W_SSD), F32), (bb, W_SSD), lambda k: (k, 0))],
        _sample_state_kernel, start, nb // bb)


def _sample_post_kernel(y_ref, xs_ref, proj_ref, dexp_ref, ng_ref, o_ref):
    z = proj_ref[:, 2 * W_LRU:2 * W_LRU + W_SSD]
    yg = (y_ref[...] + dexp_ref[...] * xs_ref[...]) * _silu(z)
    ms = jnp.mean(yg * yg, axis=-1, keepdims=True)
    o_ref[...] = (yg * lax.rsqrt(ms + EPS) * ng_ref[...]).astype(o_ref.dtype)


def _sample_post(y_raw, xs, proj, dexp, ng):
    return pl.pallas_call(
        _sample_post_kernel,
        out_shape=jax.ShapeDtypeStruct(y_raw.shape, BF16),
        compiler_params=pltpu.CompilerParams(vmem_limit_bytes=VMEM_LIMIT_BYTES),
        name="sample_post",
    )(y_raw, xs, proj, dexp, ng)


def _block_diag_groups(w):
    per = LRU_GATE_GROUP // LRU_BLOCK
    w4 = w.reshape(LRU_HEADS // per, per, LRU_BLOCK, LRU_BLOCK)
    bd = jnp.einsum("ghij,hk->ghikj", w4, jnp.eye(per, dtype=w.dtype))
    return bd.reshape(LRU_HEADS // per, LRU_GATE_GROUP, LRU_GATE_GROUP)


def _pad_lanes(v):
    v = v.reshape(1, -1)
    return jnp.pad(v, ((0, 0), (0, LANES - v.shape[1])))


def kernel(x_prompt, x_sample, c_prompt, c_sample, state_lru_h, state_lru_conv, state_ssm, state_ssd_conv, w_ada, b_ada, g_ffn1, w_up1, w_down1, g_mix, w_in, lru_conv_w, lru_conv_b, lru_wa, lru_ba, lru_wi, lru_bi, lru_lambda, ssd_conv_w, ssd_conv_b, ssd_dt_bias, ssd_A_log, ssd_D, ssd_norm_g, w_out, g_ffn2, w_up2, w_down2, w_ada_f, b_ada_f, g_final):
    bp, seq, d = x_prompt.shape
    bs = x_sample.shape[0]
    depth = w_ada.shape[0]
    assert depth == 1 and x_sample.shape[1] == 1 and d == D_MODEL

    pad_rows = (-(bs + bp)) % (2 * SUBLANES)
    c_rows = bs + bp + pad_rows
    c_all = jnp.concatenate([c_sample, c_prompt, jnp.zeros((pad_rows, d), F32)], axis=0)

    def split_rows(mod_all):
        width = mod_all.shape[1]
        return mod_all[bs:bs + bp].reshape(bp, 1, width), mod_all.reshape(1, c_rows, width)

    w_in_t = jnp.swapaxes(w_in[0], 0, 1)
    w_dt_t = jnp.pad(w_in_t[IN_MAIN:], ((0, LANES - SSD_HEADS), (0, 0)))
    up_blocks = D_FF // 512
    p = {
        "lru_cw": lru_conv_w[0], "lru_cb": lru_conv_b[0].reshape(1, W_LRU),
        "lru_wg": jnp.concatenate([_block_diag_groups(lru_wa[0]), _block_diag_groups(lru_wi[0])],
                                  axis=-1),
        "lru_ba": lru_ba[0].reshape(1, W_LRU), "lru_bi": lru_bi[0].reshape(1, W_LRU),
        "lru_lam": lru_lambda[0].reshape(1, W_LRU),
        "ssd_cw": ssd_conv_w[0], "ssd_cb": ssd_conv_b[0].reshape(1, SSD_CONV_DIM),
        "ssd_dtb": _pad_lanes(ssd_dt_bias[0]), "ssd_alog": _pad_lanes(ssd_A_log[0]),
        "ssd_dexp": jnp.repeat(ssd_D[0], SSD_HEAD_DIM).reshape(1, W_SSD),
        "ssd_ng": ssd_norm_g[0].reshape(1, W_SSD),
    }

    xp = x_prompt.reshape(bp * seq, d)
    xs = x_sample.reshape(bs, d)
    tm = 1024
    up_kw = dict(n_out=D_FF, swiglu=True, out_dtype=BF16)

    mod_a_all, silu_c = _ada(c_all, w_ada[0], b_ada[0], 2 * d)
    mod_a_p, mod_a_s = split_rows(mod_a_all)

    hp, hs = _norm_rows(xp, xs, g_ffn1[0], mod_a_p, mod_a_s, 0, tm=tm)
    first, (hmid_s,), wb = _proj_first_tile(
        hp, [(w_up1[0], 0), (w_up1[0], up_blocks)], tm=tm, tn=512, side=hs, **up_kw)
    (hmid,), ((w_down_b,), (mod_b_all,), (w_in_b,)) = _proj_other_tiles(
        hp, wb, first, tm=tm, tn=512,
        streams=[_cast_stream(w_down1[0], 32),
                 _ada_stream(silu_c, w_ada[0], b_ada[0], 2 * d, 3 * d, 256),
                 _cast_stream(w_in_t, 32, start=32, rows=IN_MAIN)],
        **up_kw)
    mod_b_p, mod_b_s = split_rows(mod_b_all)
    (xp, hp), (xs, hs) = _resid([hmid], [hmid_s], [(w_down_b, 0)], xp, xs, mod_b_p, mod_b_s, 0,
                                g_mix[0], mod_b_p, mod_b_s, 1, factor=0.5, tm=256, emit_x=True,
                                h_dtype=BF16)

    (proj, dt_raw), (proj_s, dt_raw_s), _, ((w_out_b,),) = _proj(
        hp, [(w_in_b, 0)], n_out=IN_MAIN, tm=tm, tn=IN_MAIN // 3, swiglu=False, out_dtype=F32,
        row_tiles=(0, bp * seq // tm), side=hs, trans_w=True, w_extra=w_dt_t,
        streams=[_cast_stream(w_out[0], 16)])
    lconv = state_lru_conv[0].reshape(bs, (CONV_W - 1) * W_LRU)
    sconv = state_ssd_conv[0].reshape(bs, (CONV_W - 1) * SSD_CONV_DIM)
    out_l_s, lru_h_s, lconv_new, sconv_new, xs_act, xdt, bc, dec = _sample_pre(
        proj_s, dt_raw_s, state_lru_h[0], lconv, sconv, p)
    out_l, lru_h_p, ((ssm_s, y_raw), (modf_all,)) = _lru_prompt(
        proj, bp, seq, p["lru_cw"], p["lru_cb"], p["lru_wg"], p["lru_ba"], p["lru_bi"],
        p["lru_lam"],
        streams=[_state_stream(state_ssm[0].reshape(bs, W_SSD, SSD_STATE), xdt, bc, dec),
                 _ada_stream(silu_c, w_ada_f, b_ada_f, 0, 2 * d, 2 * d // (bp * seq // LRU_TIME_TILE))])
    y_ssd_s = _sample_post(y_raw, xs_act, proj_s, p["ssd_dexp"], p["ssd_ng"])
    y_ssd, ssm_p, ((mod_c_all,),) = _ssd_prompt(
        proj, dt_raw, bp, seq, p["ssd_cw"], p["ssd_cb"], p["ssd_dtb"], p["ssd_alog"],
        p["ssd_dexp"], p["ssd_ng"],
        streams=[_ada_stream(silu_c, w_ada[0], b_ada[0], 5 * d, (N_MOD - 5) * d,
                             (N_MOD - 5) * d // (bp * seq // (SSD_CHUNK * SSD_CHUNKS_PER_STEP)))])
    mod_c_p, mod_c_s = split_rows(mod_c_all)
    modf_p, modf_s = split_rows(modf_all)
    proj3 = proj.reshape(bp, seq, IN_MAIN)
    lru_buf_p = proj3[:, seq - (CONV_W - 1):, :W_LRU]
    ssd_buf_p = proj3[:, seq - (CONV_W - 1):, 2 * W_LRU + W_SSD:]

    (xp, hp), (xs, hs) = _resid([out_l, y_ssd], [out_l_s, y_ssd_s], [(w_out_b, 0), (w_out_b, 1)],
                                xp, xs, mod_c_p, mod_c_s, 0, g_ffn2[0], mod_c_p, mod_c_s, 1,
                                factor=1.0, tm=512, emit_x=True, h_dtype=BF16)

    first, (hmid_s,), wb = _proj_first_tile(
        hp, [(w_up2[0], 0), (w_up2[0], up_blocks)], tm=tm, tn=512, side=hs, **up_kw)
    (hmid,), ((w_down_b,),) = _proj_other_tiles(
        hp, wb, first, tm=tm, tn=512, streams=[_cast_stream(w_down2[0], 32)], **up_kw)
    (yp,), (ys,) = _resid([hmid], [hmid_s], [(w_down_b, 0)], xp, xs, mod_c_p, mod_c_s, 3, g_final,
                          modf_p, modf_s, 0, factor=0.5, tm=256, emit_x=False, h_dtype=F32)

    stack = lambda v: v[None]
    return (yp.reshape(bp, seq, d), ys.reshape(bs, 1, d),
            stack(lru_h_p), stack(lru_buf_p), stack(ssm_p), stack(ssd_buf_p),
            stack(lru_h_s), stack(lconv_new.reshape(bs, CONV_W - 1, W_LRU)),
            stack(ssm_s.reshape(bs, SSD_HEADS, SSD_HEAD_DIM, SSD_STATE)),
            stack(sconv_new.reshape(bs, CONV_W - 1, SSD_CONV_DIM)))
```

```python
import functools
from typing import Callable, NamedTuple

import jax
import jax.numpy as jnp
from jax import lax
from jax.experimental import pallas as pl
from jax.experimental.pallas import tpu as pltpu

F32 = jnp.float32
BF16 = jnp.bfloat16

D_MODEL = 2048
D_FF = 5632
W_LRU = 1024
W_SSD = 1024
LRU_HEADS = 16
LRU_BLOCK = 64
LRU_C = 8.0
SSD_HEADS = 16
SSD_HEAD_DIM = 64
SSD_GROUPS = 2
SSD_HPG = 8
SSD_STATE = 128
SSD_CHUNK = 128
SSD_CHUNKS_PER_STEP = 4
CONV_W = 4
SSD_CONV_DIM = W_SSD + 2 * SSD_GROUPS * SSD_STATE
IN_MAIN = 2 * W_LRU + W_SSD + SSD_CONV_DIM
N_MOD = 9
EPS = 1e-6

LANES = 128
SUBLANES = 8
VMEM_LIMIT_BYTES = 56 * 1024 * 1024

LRU_GATE_GROUP = 256
LRU_TIME_TILE = 512
SCAN_ROWS = 2 * SUBLANES


def _sigmoid(v):
    return 0.5 * (jnp.tanh(0.5 * v) + 1.0)


def _silu(v):
    return v * _sigmoid(v)


def _softplus(v):
    return jnp.maximum(v, 0.0) + jnp.log1p(jnp.exp(-jnp.abs(v)))


def _gelu_tanh(v):
    return 0.5 * v * (1.0 + jnp.tanh(0.7978845608028654 * (v + 0.044715 * (v * v * v))))


def _bdot(a, b):
    return jnp.dot(a, b, preferred_element_type=F32)


def _params(sem):
    return pltpu.CompilerParams(dimension_semantics=sem, vmem_limit_bytes=VMEM_LIMIT_BYTES)


def _ada_mm_kernel(s_ref, w_ref, b_ref, o_ref):
    o_ref[...] = _bdot(s_ref[...], w_ref[...].astype(BF16)) + b_ref[...]


def _ada_kernel(c_ref, w_ref, b_ref, o_ref, s_ref):
    s_ref[...] = _silu(c_ref[...]).astype(BF16)
    _ada_mm_kernel(s_ref, w_ref, b_ref, o_ref)


def _ada(c, w, b, cols, tn=1024):
    m, k = c.shape
    n = cols
    return pl.pallas_call(
        _ada_kernel,
        grid=(n // tn,),
        in_specs=[pl.BlockSpec((m, k), lambda j: (0, 0)),
                  pl.BlockSpec((k, tn), lambda j: (0, j)),
                  pl.BlockSpec((1, tn), lambda j: (0, j))],
        out_specs=[pl.BlockSpec((m, tn), lambda j: (0, j)), pl.BlockSpec((m, k), lambda j: (0, 0))],
        out_shape=[jax.ShapeDtypeStruct((m, n), F32), jax.ShapeDtypeStruct((m, k), BF16)],
        compiler_params=_params(("arbitrary",)),
        name="ada_proj",
    )(c, w, b.reshape(1, -1))


def _norm_modulate(x, gain, shift, scale):
    ms = jnp.mean(x * x, axis=-1, keepdims=True)
    y = x * lax.rsqrt(ms + EPS) * gain
    return y * (1.0 + scale) + shift


def _norm_rows_kernel(x_ref, gain_ref, sh_ref, sc_ref, xs_ref, shs_ref, scs_ref, o_ref, os_ref):
    o_ref[...] = _norm_modulate(x_ref[...], gain_ref[...], sh_ref[...],
                                sc_ref[...]).astype(o_ref.dtype)

    @pl.when(pl.program_id(0) == 0)
    def _():
        os_ref[...] = _norm_modulate(xs_ref[...], gain_ref[...], shs_ref[...],
                                     scs_ref[...]).astype(os_ref.dtype)


def _norm_rows(x, x_s, gain, mod, mod_s, shift_chunk, *, tm):
    m, d = x.shape
    ns = x_s.shape[0]
    tiles_per_group = (m // tm) // mod.shape[0]
    once = dict(pipeline_mode=pl.Buffered(1))
    mod_spec = lambda c: pl.BlockSpec((None, 1, d), lambda i: (i // tiles_per_group, 0, c))
    mod_s_spec = lambda c: pl.BlockSpec((None, ns, d), lambda i: (0, 0, c), **once)
    return pl.pallas_call(
        _norm_rows_kernel,
        grid=(m // tm,),
        in_specs=[pl.BlockSpec((tm, d), lambda i: (i, 0)), pl.BlockSpec((1, d), lambda i: (0, 0)),
                  mod_spec(shift_chunk), mod_spec(shift_chunk + 1),
                  pl.BlockSpec((ns, d), lambda i: (0, 0), **once),
                  mod_s_spec(shift_chunk), mod_s_spec(shift_chunk + 1)],
        out_specs=[pl.BlockSpec((tm, d), lambda i: (i, 0)), pl.BlockSpec((ns, d), lambda i: (0, 0))],
        out_shape=[jax.ShapeDtypeStruct((m, d), BF16), jax.ShapeDtypeStruct((ns, d), BF16)],
        compiler_params=_params(("arbitrary",)),
        name="norm_rows",
    )(x, gain.reshape(1, d), mod, mod, x_s, mod_s, mod_s)


class _Stream(NamedTuple):
    ins: list
    outs: list
    body: Callable
    start: int
    steps: int


def _stream_io(streams, n_inner):
    in_specs, args, out_specs, out_shape = [], [], [], []

    def spec(st, block, index_fn):
        return pl.BlockSpec(
            block, lambda i, j: index_fn(jnp.clip(i * n_inner + j - st.start, 0, st.steps - 1)))

    for st in streams:
        for arr, block, index_fn in st.ins:
            in_specs.append(spec(st, block, index_fn))
            args.append(arr)
        for shape, block, index_fn in st.outs:
            out_specs.append(spec(st, block, index_fn))
            out_shape.append(shape)
    return in_specs, args, out_specs, out_shape


def _run_streams(streams, in_refs, out_refs, n_inner, n_steps):
    step = pl.program_id(0) * n_inner + pl.program_id(1)
    in_refs, out_refs = iter(in_refs), iter(out_refs)
    for st in streams:
        ins = [next(in_refs) for _ in st.ins]
        outs = [next(out_refs) for _ in st.outs]
        if st.start == 0 and st.steps == n_steps:
            st.body(*ins, *outs)
            continue

        @pl.when((step >= st.start) & (step < st.start + st.steps))
        def _(st=st, ins=ins, outs=outs):
            st.body(*ins, *outs)


def _split_stream_outs(streams, flat):
    flat = list(flat)
    return [[flat.pop(0) for _ in st.outs] for st in streams]


def _cast_body(src_ref, dst_ref):
    dst_ref[...] = src_ref[...].astype(dst_ref.dtype)


def _cast_stream(w, chunks, start=0, rows=None):
    rows = w.shape[0] if rows is None else rows
    block = (rows // chunks, w.shape[1])
    index = lambda k: (k, 0)
    return _Stream([(w, block, index)],
                   [(jax.ShapeDtypeStruct((rows, w.shape[1]), BF16), block, index)],
                   _cast_body, start, chunks)


def _ada_stream(sc, w, b, col0, cols, tn, start=0):
    m, k = sc.shape
    t0 = col0 // tn
    return _Stream(
        [(sc, (m, k), lambda s: (0, 0)), (w, (k, tn), lambda s: (0, s + t0)),
         (b.reshape(1, -1), (1, tn), lambda s: (0, s + t0))],
        [(jax.ShapeDtypeStruct((m, cols), F32), (m, tn), lambda s: (0, s))],
        _ada_mm_kernel, start, cols // tn)


def _wdot(h, w, trans_w):
    if trans_w:
        return lax.dot_general(h, w, (((1,), (1,)), ((), ())), preferred_element_type=F32)
    return _bdot(h, w)


def _proj_kernel(*refs, n_w, swiglu, trans_w, has_extra, has_side, emit_bf16, streams, n_prev,
                 nj, n_steps):
    it = iter(refs)
    x_ref = next(it)
    xs_ref = next(it) if has_side else None
    w_refs = [next(it) for _ in range(n_w)]
    wx_ref = next(it) if has_extra else None
    stream_ins = [next(it) for st in streams for _ in st.ins]
    for _ in range(n_prev):
        next(it)
    o_ref = next(it)
    ox_ref = next(it) if has_extra else None
    os_ref = next(it) if has_side else None
    osx_ref = next(it) if has_side and has_extra else None
    wo_refs = [next(it) for _ in range(n_w)] if emit_bf16 else []
    stream_outs = [next(it) for st in streams for _ in st.outs]

    j = pl.program_id(1)
    first_tile = pl.program_id(0) == 0

    if has_extra:
        @pl.when(j == 0)
        def _():
            ox_ref[...] = _wdot(x_ref[...], wx_ref[...].astype(BF16), trans_w)

        if has_side:
            @pl.when((j == 0) & first_tile)
            def _():
                osx_ref[...] = _wdot(xs_ref[...], wx_ref[...].astype(BF16), trans_w)

    wbs = [w_ref[...].astype(BF16) for w_ref in w_refs]
    for wo_ref, wb in zip(wo_refs, wbs):
        wo_ref[...] = wb

    def project(h, out_ref):
        if swiglu:
            g = _wdot(h, wbs[0], trans_w)
            u = _wdot(h, wbs[1], trans_w)
            out_ref[...] = (_silu(g) * u).astype(out_ref.dtype)
        else:
            out_ref[...] = _wdot(h, wbs[0], trans_w).astype(out_ref.dtype)

    project(x_ref[...], o_ref)
    if has_side:
        @pl.when(first_tile)
        def _():
            project(xs_ref[...], os_ref)

    _run_streams(streams, stream_ins, stream_outs, nj, n_steps)


def _proj(x, ws, *, n_out, tm, tn, swiglu, out_dtype, row_tiles, side=None, trans_w=False,
          w_extra=None, emit_bf16=False, streams=(), prev=None):
    m, d = x.shape
    t0, t1 = row_tiles
    nj = n_out // tn
    has_side = side is not None
    has_extra = w_extra is not None
    single_row_tile = t1 - t0 == 1
    once = dict(pipeline_mode=pl.Buffered(1))

    def w_spec(off):
        if trans_w:
            return pl.BlockSpec((tn, d), lambda i, j: (j + off, 0))
        return pl.BlockSpec((d, tn), lambda i, j: (0, j + off))

    x_mode = once if single_row_tile else {}
    in_specs = [pl.BlockSpec((tm, d), lambda i, j: (i + t0, 0), **x_mode)]
    args = [x]
    if has_side:
        ns = side.shape[0]
        in_specs.append(pl.BlockSpec((ns, d), lambda i, j: (0, 0), **once))
        args.append(side)
    in_specs += [w_spec(off) for _, off in ws]
    args += [w for w, _ in ws]
    if has_extra:
        nx = w_extra.shape[0] if trans_w else w_extra.shape[1]
        in_specs.append(pl.BlockSpec(w_extra.shape, lambda i, j: (0, 0)))
        args.append(w_extra)
    assert all(st.start + st.steps <= (t1 - t0) * nj for st in streams)
    st_in_specs, st_args, st_out_specs, st_out_shape = _stream_io(streams, nj)
    in_specs += st_in_specs
    args += st_args
    prev = list(prev or [])
    aliases = {}
    for k, buf in enumerate(prev):
        aliases[len(args)] = k
        in_specs.append(pl.BlockSpec(memory_space=pl.ANY))
        args.append(buf)

    out_specs = [pl.BlockSpec((tm, tn), lambda i, j: (i + t0, j))]
    out_shape = [jax.ShapeDtypeStruct((m, n_out), out_dtype)]
    if has_extra:
        out_specs.append(pl.BlockSpec((tm, nx), lambda i, j: (i + t0, 0)))
        out_shape.append(jax.ShapeDtypeStruct((m, nx), F32))
    n_main = len(out_shape)
    if has_side:
        out_specs.append(pl.BlockSpec((ns, tn), lambda i, j: (0, jnp.where(i == 0, j, nj - 1))))
        out_shape.append(jax.ShapeDtypeStruct((ns, n_out), out_dtype))
        if has_extra:
            out_specs.append(pl.BlockSpec((ns, nx), lambda i, j: (0, 0)))
            out_shape.append(jax.ShapeDtypeStruct((ns, nx), F32))
    n_side = len(out_shape) - n_main
    if emit_bf16:
        assert single_row_tile, "weight copies are written once per column tile"
        for _ in ws:
            out_specs.append(w_spec(0))
            out_shape.append(jax.ShapeDtypeStruct((n_out, d) if trans_w else (d, n_out), BF16))
    out_specs += st_out_specs
    out_shape += st_out_shape
    outs = pl.pallas_call(
        functools.partial(_proj_kernel, n_w=len(ws), swiglu=swiglu, trans_w=trans_w,
                          has_extra=has_extra, has_side=has_side, emit_bf16=emit_bf16,
                          streams=tuple(streams), n_prev=len(prev), nj=nj,
                          n_steps=(t1 - t0) * nj),
        grid=(t1 - t0, nj),
        in_specs=in_specs,
        out_specs=out_specs,
        out_shape=out_shape,
        input_output_aliases=aliases,
        compiler_params=_params(("arbitrary" if streams or has_side else "parallel", "arbitrary")),
        name="proj_swiglu" if swiglu else "proj",
    )(*args)
    n_wb = len(ws) if emit_bf16 else 0
    main, rest = outs[:n_main], outs[n_main:]
    side_outs, rest = rest[:n_side], rest[n_side:]
    wb, rest = rest[:n_wb], rest[n_wb:]
    return main, side_outs, wb, _split_stream_outs(streams, rest)


def _proj_first_tile(x, ws_f32, *, tm, tn, side, **kw):
    main, side_outs, wb, _ = _proj(x, ws_f32, tm=tm, tn=tn, row_tiles=(0, 1), side=side,
                                   emit_bf16=True, **kw)
    return main, side_outs, wb


def _proj_other_tiles(x, wb, prev, *, tm, tn, streams, **kw):
    main, _, _, stream_outs = _proj(x, [(w, 0) for w in wb], tm=tm, tn=tn,
                                    row_tiles=(1, x.shape[0] // tm), prev=prev, streams=streams,
                                    **kw)
    return main, stream_outs


def _resid_kernel(*refs, n_lhs, factor, emit_x):
    it = iter(refs)
    lhs_refs = [next(it) for _ in range(n_lhs)]
    lhs_s_refs = [next(it) for _ in range(n_lhs)]
    w_refs = [next(it) for _ in range(n_lhs)]
    x_ref, gate_ref, gain_ref, sh_ref, sc_ref = (next(it) for _ in range(5))
    xs_ref, gate_s_ref, sh_s_ref, sc_s_ref = (next(it) for _ in range(4))
    n_out = 2 if emit_x else 1
    outs = [next(it) for _ in range(n_out)]
    outs_s = [next(it) for _ in range(n_out)]

    def update(lhs, x_in, gate, sh, sc, out_refs):
        acc = _bdot(lhs[0][...], w_refs[0][...])
        for l_ref, w_ref in zip(lhs[1:], w_refs[1:]):
            acc = acc + _bdot(l_ref[...], w_ref[...])
        x_new = x_in[...] + (factor * gate[...]) * acc
        if emit_x:
            out_refs[0][...] = x_new
        h_ref = out_refs[-1]
        h_ref[...] = _norm_modulate(x_new, gain_ref[...], sh[...], sc[...]).astype(h_ref.dtype)

    update(lhs_refs, x_ref, gate_ref, sh_ref, sc_ref, outs)

    @pl.when(pl.program_id(0) == 0)
    def _():
        update(lhs_s_refs, xs_ref, gate_s_ref, sh_s_ref, sc_s_ref, outs_s)


def _resid(lhs_list, lhs_s_list, ws, x, x_s, mod, mod_s, gate_chunk, gain_next, mod_next,
           mod_next_s, shift_chunk_next, *, factor, tm, emit_x, h_dtype):
    m, d = x.shape
    ns = x_s.shape[0]
    groups = mod.shape[0]
    tiles_per_group = (m // tm) // groups
    kp = lhs_list[0].shape[1]
    once = dict(pipeline_mode=pl.Buffered(1))

    def mod_spec(chunk):
        return pl.BlockSpec((None, 1, d), lambda i: (i // tiles_per_group, 0, chunk))

    def mod_s_spec(chunk):
        return pl.BlockSpec((None, ns, d), lambda i: (0, 0, chunk), **once)

    in_specs = [pl.BlockSpec((tm, kp), lambda i: (i, 0)) for _ in lhs_list]
    in_specs += [pl.BlockSpec((ns, kp), lambda i: (0, 0), **once) for _ in lhs_s_list]
    in_specs += [pl.BlockSpec((kp, d), lambda i, k=k: (k, 0), **once) for _, k in ws]
    in_specs += [pl.BlockSpec((tm, d), lambda i: (i, 0)), mod_spec(gate_chunk),
                 pl.BlockSpec((1, d), lambda i: (0, 0)),
                 mod_spec(shift_chunk_next), mod_spec(shift_chunk_next + 1),
                 pl.BlockSpec((ns, d), lambda i: (0, 0), **once), mod_s_spec(gate_chunk),
                 mod_s_spec(shift_chunk_next), mod_s_spec(shift_chunk_next + 1)]
    row = pl.BlockSpec((tm, d), lambda i: (i, 0))
    row_s = pl.BlockSpec((ns, d), lambda i: (0, 0))
    dtypes = ([F32] if emit_x else []) + [h_dtype]
    out_specs = [row for _ in dtypes] + [row_s for _ in dtypes]
    out_shape = ([jax.ShapeDtypeStruct((m, d), t) for t in dtypes]
                 + [jax.ShapeDtypeStruct((ns, d), t) for t in dtypes])
    outs = pl.pallas_call(
        functools.partial(_resid_kernel, n_lhs=len(lhs_list), factor=factor, emit_x=emit_x),
        grid=(m // tm,),
        in_specs=in_specs,
        out_specs=out_specs,
        out_shape=out_shape,
        compiler_params=_params(("arbitrary",)),
        name="resid",
    )(*lhs_list, *lhs_s_list, *[w for w, _ in ws], x, mod, gain_next.reshape(1, d), mod_next,
      mod_next, x_s, mod_s, mod_next_s, mod_next_s)
    return outs[:len(dtypes)], outs[len(dtypes):]


def _lru_gates(xc, wg_ref, ba, bi, sp):
    a_parts, b_parts = [], []
    for g in range(W_LRU // LRU_GATE_GROUP):
        cols = slice(g * LRU_GATE_GROUP, (g + 1) * LRU_GATE_GROUP)
        xg = xc[:, cols]
        ri = _bdot(xg.astype(BF16), wg_ref[g].astype(BF16))
        r = _sigmoid(ri[:, :LRU_GATE_GROUP] + ba[:, cols])
        i = _sigmoid(ri[:, LRU_GATE_GROUP:] + bi[:, cols])
        log_a = (-LRU_C * r) * sp[:, cols]
        a = jnp.exp(log_a)
        a_parts.append(a)
        v = 1.0 - a * a
        root = jnp.where(v > 0.0, v * lax.rsqrt(v), 0.0)
        b_parts.append(root * (i * xg))
    return jnp.concatenate(a_parts, axis=1), jnp.concatenate(b_parts, axis=1)


def _causal_conv(prev8, x, w_ref, b_ref):
    rows, width = x.shape
    rid = lax.broadcasted_iota(jnp.int32, (SUBLANES, width), 0)
    shifts = (1, 2, 3)
    taps = [w_ref[k:k + 1, :] for k in range(CONV_W)]
    bias = b_ref[...]
    prev_rot = [pltpu.roll(prev8, k, 0) for k in shifts]
    out = []
    for r in range(rows // SUBLANES):
        cur = x[r * SUBLANES:(r + 1) * SUBLANES, :]
        cur_rot = [pltpu.roll(cur, k, 0) for k in shifts]
        s1, s2, s3 = [jnp.where(rid < k, p, c) for k, p, c in zip(shifts, prev_rot, cur_rot)]
        out.append(bias + taps[0] * s3 + taps[1] * s2 + taps[2] * s1 + taps[3] * cur)
        prev_rot = cur_rot
    return jnp.concatenate(out, axis=0)


def _lru_prompt_kernel(*refs, streams, nt, n_steps):
    n_in = sum(len(st.ins) for st in streams)
    n_out = sum(len(st.outs) for st in streams)
    xl_ref, gl_ref, cw_ref, cb_ref, wg_ref, ba_ref, bi_ref, lam_ref = refs[:8]
    stream_ins = refs[8:8 + n_in]
    o_ref, hT_ref = refs[8 + n_in:10 + n_in]
    stream_outs = refs[10 + n_in:10 + n_in + n_out]
    xbuf, a_scr, b_scr, hcar = refs[10 + n_in + n_out:]
    t = pl.program_id(1)
    tt = xl_ref.shape[0]

    @pl.when(t == 0)
    def _():
        xbuf[...] = jnp.zeros_like(xbuf)
        hcar[...] = jnp.zeros_like(hcar)

    _run_streams(streams, stream_ins, stream_outs, nt, n_steps)

    x = xl_ref[...]
    xc = _causal_conv(xbuf[...], x, cw_ref, cb_ref)
    xbuf[...] = x[tt - SUBLANES:, :]

    sp = _softplus(-lam_ref[...])
    a, bt = _lru_gates(xc, wg_ref, ba_ref[...], bi_ref[...], sp)
    a_scr[...] = a
    b_scr[...] = bt

    rid = lax.broadcasted_iota(jnp.int32, (SUBLANES, W_LRU), 0)

    def scan8(a8, b8, h_in):
        for s in (1, 2, 4):
            a_sh = pltpu.roll(a8, s, 0)
            b_sh = pltpu.roll(b8, s, 0)
            m = rid >= s
            b8 = jnp.where(m, a8 * b_sh + b8, b8)
            a8 = jnp.where(m, a8 * a_sh, a8)
        h8 = a8 * h_in + b8
        return h8, jnp.broadcast_to(h8[SUBLANES - 1:SUBLANES, :], (SUBLANES, W_LRU))

    def body(g, h_in):
        r0 = pl.multiple_of(g * SCAN_ROWS, SCAN_ROWS)
        lo = pl.ds(r0, SUBLANES)
        hi = pl.ds(r0 + SUBLANES, SUBLANES)
        h_lo, h_mid = scan8(a_scr[lo, :], b_scr[lo, :], h_in)
        h_hi, h_out = scan8(a_scr[hi, :], b_scr[hi, :], h_mid)
        rows = pl.ds(r0, SCAN_ROWS)
        h16 = jnp.concatenate([h_lo, h_hi], axis=0)
        o_ref[rows, :] = (h16 * _gelu_tanh(gl_ref[rows, :])).astype(o_ref.dtype)
        return h_out

    h_last = lax.fori_loop(0, tt // SCAN_ROWS, body, hcar[...])
    hcar[...] = h_last

    @pl.when(t == pl.num_programs(1) - 1)
    def _():
        hT_ref[...] = h_last[0:1, :]


def _lru_prompt(proj, batch, seq, cw, cb, wg, ba, bi, lam, streams=()):
    tt = LRU_TIME_TILE
    nt = seq // tt
    assert all(st.start + st.steps <= batch * nt for st in streams)
    st_in_specs, st_args, st_out_specs, st_out_shape = _stream_io(streams, nt)
    row = lambda v: v.reshape(1, W_LRU)
    full = lambda shape: pl.BlockSpec(shape, lambda b, t: (0,) * len(shape))
    out, h_t, *rest = pl.pallas_call(
        functools.partial(_lru_prompt_kernel, streams=tuple(streams), nt=nt, n_steps=batch * nt),
        grid=(batch, nt),
        in_specs=[pl.BlockSpec((tt, W_LRU), lambda b, t: (b * nt + t, 0)),
                  pl.BlockSpec((tt, W_LRU), lambda b, t: (b * nt + t, 1)),
                  full((CONV_W, W_LRU)), full((1, W_LRU)), full(wg.shape),
                  full((1, W_LRU)), full((1, W_LRU)), full((1, W_LRU))] + st_in_specs,
        out_specs=[pl.BlockSpec((tt, W_LRU), lambda b, t: (b * nt + t, 0)),
                   pl.BlockSpec((None, 1, W_LRU), lambda b, t: (b, 0, 0))] + st_out_specs,
        out_shape=[jax.ShapeDtypeStruct((batch * seq, W_LRU), BF16),
                   jax.ShapeDtypeStruct((batch, 1, W_LRU), F32)] + st_out_shape,
        scratch_shapes=[pltpu.VMEM((SUBLANES, W_LRU), F32),
                        pltpu.VMEM((tt, W_LRU), F32),
                        pltpu.VMEM((tt, W_LRU), F32),
                        pltpu.VMEM((SUBLANES, W_LRU), F32)],
        compiler_params=_params(("arbitrary" if streams else "parallel", "arbitrary")),
        name="lru_prompt",
    )(proj, proj, cw, row(cb), wg, row(ba), row(bi), row(lam), *st_args)
    return out, h_t.reshape(batch, W_LRU), _split_stream_outs(streams, rest)


def _ssd_chunk(z_ref, xbc_ref, dt_ref, cw_ref, cb_ref, dtb_ref, alog_ref, dexp_ref, ng_ref, y_ref,
               xbuf, st_scr, y_scr, m_scr, xbd_scr):
    lc = SSD_CHUNK
    x = xbc_ref[...]
    act = _silu(_causal_conv(xbuf[...], x, cw_ref, cb_ref))
    xbuf[...] = x[lc - SUBLANES:, :]
    xs = act[:, :W_SSD]
    bm = act[:, W_SSD:W_SSD + SSD_GROUPS * SSD_STATE]
    cm = act[:, W_SSD + SSD_GROUPS * SSD_STATE:]

    dt = _softplus(dt_ref[...] + dtb_ref[...])
    d_a = dt * (-jnp.exp(alog_ref[...]))
    row_i = lax.broadcasted_iota(jnp.int32, (lc, lc), 0)
    col_i = lax.broadcasted_iota(jnp.int32, (lc, lc), 1)
    causal = row_i >= col_i
    tril = jnp.where(causal, 1.0, 0.0).astype(F32)
    cs = jnp.dot(tril, d_a, preferred_element_type=F32, precision=lax.Precision.HIGHEST)
    cs_t = cs.T
    dt_t = dt.T
    cs_last = cs[lc - 1:lc, :]

    def per_head_lanes(v):
        rows = v.shape[0]
        return jnp.concatenate(
            [jnp.broadcast_to(v[:, h:h + 1], (rows, SSD_HEAD_DIM)) for h in range(SSD_HEADS)],
            axis=1)

    w_exp = per_head_lanes(jnp.exp(cs_last - cs) * dt)
    ecs_exp = per_head_lanes(jnp.exp(cs))
    cd_exp = per_head_lanes(jnp.exp(cs_last))
    gw = SSD_HPG * SSD_HEAD_DIM
    low_half = col_i < SSD_HEAD_DIM

    for g in range(SSD_GROUPS):
        ncols = slice(g * SSD_STATE, (g + 1) * SSD_STATE)
        gcols = slice(g * gw, (g + 1) * gw)
        b_g = bm[:, ncols].astype(BF16)
        c_g = cm[:, ncols].astype(BF16)
        cb_mat = lax.dot_general(c_g, b_g, (((1,), (1,)), ((), ())),
                                 preferred_element_type=F32)
        for e in range(SSD_HPG):
            h = g * SSD_HPG + e
            cs_col = jnp.broadcast_to(cs[:, h:h + 1], (lc, lc))
            l_mat = jnp.exp(jnp.where(causal, cs_col - cs_t[h:h + 1, :], -jnp.inf))
            m_scr[g, :, e * lc:(e + 1) * lc] = (cb_mat * l_mat * dt_t[h:h + 1, :]).astype(BF16)
        for q in range(SSD_HPG // 2):
            lanes = slice(q * LANES, (q + 1) * LANES)
            slab = xs[:, g * gw + q * LANES:g * gw + (q + 1) * LANES]
            xbd_scr[g, (2 * q) * lc:(2 * q + 1) * lc, lanes] = jnp.where(
                low_half, slab, 0.0).astype(BF16)
            xbd_scr[g, (2 * q + 1) * lc:(2 * q + 2) * lc, lanes] = jnp.where(
                low_half, 0.0, slab).astype(BF16)
        st_g = st_scr[:, gcols]
        y_off = _bdot(c_g, st_g.astype(BF16)) * ecs_exp[:, gcols]
        y_scr[:, gcols] = (_bdot(m_scr[g], xbd_scr[g]) + y_off
                           + dexp_ref[:, gcols] * xs[:, gcols])
        xw = (xs[:, gcols] * w_exp[:, gcols]).astype(BF16)
        st_scr[:, gcols] = cd_exp[:, gcols] * st_g + lax.dot_general(
            b_g, xw, (((0,), (0,)), ((), ())), preferred_element_type=F32)

    yg = y_scr[...] * _silu(z_ref[...])
    ms = jnp.mean(yg * yg, axis=-1, keepdims=True)
    y_ref[...] = (yg * lax.rsqrt(ms + EPS) * ng_ref[...]).astype(y_ref.dtype)


def _ssd_prompt_kernel(*refs, streams, nc, n_steps):
    n_in = sum(len(st.ins) for st in streams)
    n_out = sum(len(st.outs) for st in streams)
    z_ref, xbc_ref, dt_ref, cw_ref, cb_ref, dtb_ref, alog_ref, dexp_ref, ng_ref = refs[:9]
    stream_ins = refs[9:9 + n_in]
    y_ref, st_ref = refs[9 + n_in:11 + n_in]
    stream_outs = refs[11 + n_in:11 + n_in + n_out]
    xbuf, st_scr, y_scr, m_scr, xbd_scr = refs[11 + n_in + n_out:]
    c = pl.program_id(1)

    @pl.when(c == 0)
    def _():
        xbuf[...] = jnp.zeros_like(xbuf)
        st_scr[...] = jnp.zeros_like(st_scr)
        xbd_scr[...] = jnp.zeros_like(xbd_scr)

    _run_streams(streams, stream_ins, stream_outs, nc, n_steps)

    for cc in range(SSD_CHUNKS_PER_STEP):
        rows = pl.ds(cc * SSD_CHUNK, SSD_CHUNK)
        _ssd_chunk(z_ref.at[rows], xbc_ref.at[rows], dt_ref.at[rows], cw_ref, cb_ref, dtb_ref,
                   alog_ref, dexp_ref, ng_ref, y_ref.at[rows], xbuf, st_scr, y_scr.at[cc],
                   m_scr.at[cc], xbd_scr.at[cc])

    @pl.when(c == pl.num_programs(1) - 1)
    def _():
        st_ref[...] = st_scr[...].T


def _ssd_prompt(proj, dt_raw, batch, seq, cw, cb, dtb, alog, dexp, ng, streams=()):
    lc = SSD_CHUNK
    cps = SSD_CHUNKS_PER_STEP
    tl = cps * lc
    nc = seq // tl
    assert all(st.start + st.steps <= batch * nc for st in streams)
    st_in_specs, st_args, st_out_specs, st_out_shape = _stream_io(streams, nc)
    full = lambda shape: pl.BlockSpec(shape, lambda b, c: (0,) * len(shape))
    z_blk = (2 * W_LRU) // W_SSD
    xbc_blk = (2 * W_LRU + W_SSD) // SSD_CONV_DIM
    y, st, *rest = pl.pallas_call(
        functools.partial(_ssd_prompt_kernel, streams=tuple(streams), nc=nc, n_steps=batch * nc),
        grid=(batch, nc),
        in_specs=[pl.BlockSpec((tl, W_SSD), lambda b, c: (b * nc + c, z_blk)),
                  pl.BlockSpec((tl, SSD_CONV_DIM), lambda b, c: (b * nc + c, xbc_blk)),
                  pl.BlockSpec((tl, LANES), lambda b, c: (b * nc + c, 0)),
                  full((CONV_W, SSD_CONV_DIM)), full((1, SSD_CONV_DIM)),
                  full((1, LANES)), full((1, LANES)), full((1, W_SSD)), full((1, W_SSD))]
        + st_in_specs,
        out_specs=[pl.BlockSpec((tl, W_SSD), lambda b, c: (b * nc + c, 0)),
                   pl.BlockSpec((None, W_SSD, SSD_STATE), lambda b, c: (b, 0, 0))] + st_out_specs,
        out_shape=[jax.ShapeDtypeStruct((batch * seq, W_SSD), BF16),
                   jax.ShapeDtypeStruct((batch, W_SSD, SSD_STATE), F32)] + st_out_shape,
        scratch_shapes=[pltpu.VMEM((SUBLANES, SSD_CONV_DIM), F32),
                        pltpu.VMEM((SSD_STATE, W_SSD), F32),
                        pltpu.VMEM((cps, lc, W_SSD), F32),
                        pltpu.VMEM((cps, SSD_GROUPS, lc, SSD_HPG * lc), BF16),
                        pltpu.VMEM((cps, SSD_GROUPS, SSD_HPG * lc, SSD_HPG * SSD_HEAD_DIM), BF16)],
        compiler_params=_params(("arbitrary" if streams else "parallel", "arbitrary")),
        name="ssd_prompt",
    )(proj, proj, dt_raw, cw, cb, dtb, alog, dexp, ng, *st_args)
    return (y, st.reshape(batch, SSD_HEADS, SSD_HEAD_DIM, SSD_STATE),
            _split_stream_outs(streams, rest))


def _sample_pre_kernel(proj_ref, dt_ref, h0_ref, lconv_ref, sconv_ref,
                       lcw_ref, lcb_ref, wg_ref, ba_ref, bi_ref, lam_ref,
                       scw_ref, scb_ref, dtb_ref, alog_ref,
                       outl_ref, hnew_ref, lconv_new_ref, sconv_new_ref,
                       xs_ref, xdt_ref, bc_ref, dec_ref):
    nb = proj_ref.shape[0]
    xl = proj_ref[:, 0:W_LRU]
    gl = proj_ref[:, W_LRU:2 * W_LRU]
    xbc = proj_ref[:, 2 * W_LRU + W_SSD:IN_MAIN]

    def conv1(state_ref, width, x_new, w_ref, b_ref):
        y = b_ref[...] + w_ref[0:1, :] * state_ref[:, 0:width]
        y = y + w_ref[1:2, :] * state_ref[:, width:2 * width]
        y = y + w_ref[2:3, :] * state_ref[:, 2 * width:3 * width]
        return y + w_ref[3:4, :] * x_new

    xc = conv1(lconv_ref, W_LRU, xl, lcw_ref, lcb_ref)
    a, bt = _lru_gates(xc, wg_ref, ba_ref[...], bi_ref[...], _softplus(-lam_ref[...]))
    h_new = a * h0_ref[...] + bt
    hnew_ref[...] = h_new
    outl_ref[...] = (h_new * _gelu_tanh(gl)).astype(outl_ref.dtype)
    lconv_new_ref[:, 0:2 * W_LRU] = lconv_ref[:, W_LRU:3 * W_LRU]
    lconv_new_ref[:, 2 * W_LRU:3 * W_LRU] = xl

    act = _silu(conv1(sconv_ref, SSD_CONV_DIM, xbc, scw_ref, scb_ref))
    sconv_new_ref[:, 0:2 * SSD_CONV_DIM] = sconv_ref[:, SSD_CONV_DIM:3 * SSD_CONV_DIM]
    sconv_new_ref[:, 2 * SSD_CONV_DIM:3 * SSD_CONV_DIM] = xbc
    xs = act[:, :W_SSD]
    xs_ref[...] = xs
    bc_ref[...] = act[:, W_SSD:]
    dt = _softplus(dt_ref[...] + dtb_ref[...])
    dec = jnp.exp(dt * (-jnp.exp(alog_ref[...])))
    for h in range(SSD_HEADS):
        pcols = slice(h * SSD_HEAD_DIM, (h + 1) * SSD_HEAD_DIM)
        xdt_ref[:, pcols] = xs[:, pcols] * jnp.broadcast_to(dt[:, h:h + 1], (nb, SSD_HEAD_DIM))
        dec_ref[h] = jnp.broadcast_to(dec[:, h:h + 1], (nb, SSD_STATE))


def _sample_pre(proj, dt_raw, h0, lconv, sconv, p):
    nb = proj.shape[0]
    out_shape = [jax.ShapeDtypeStruct((nb, W_LRU), BF16),
                 jax.ShapeDtypeStruct((nb, W_LRU), F32),
                 jax.ShapeDtypeStruct((nb, 3 * W_LRU), F32),
                 jax.ShapeDtypeStruct((nb, 3 * SSD_CONV_DIM), F32),
                 jax.ShapeDtypeStruct((nb, W_SSD), F32),
                 jax.ShapeDtypeStruct((nb, W_SSD), F32),
                 jax.ShapeDtypeStruct((nb, 2 * SSD_GROUPS * SSD_STATE), F32),
                 jax.ShapeDtypeStruct((SSD_HEADS, nb, SSD_STATE), F32)]
    return pl.pallas_call(
        _sample_pre_kernel,
        out_shape=out_shape,
        compiler_params=pltpu.CompilerParams(vmem_limit_bytes=VMEM_LIMIT_BYTES),
        name="sample_pre",
    )(proj, dt_raw, h0, lconv, sconv,
      p["lru_cw"], p["lru_cb"], p["lru_wg"], p["lru_ba"], p["lru_bi"], p["lru_lam"],
      p["ssd_cw"], p["ssd_cb"], p["ssd_dtb"], p["ssd_alog"])


def _sample_state_kernel(s_ref, xdt_ref, bc_ref, dec_ref, o_ref, y_ref):
    bb = s_ref.shape[0]
    half = SSD_HPG * SSD_HEAD_DIM
    rid = lax.broadcasted_iota(jnp.int32, (bb, W_SSD), 0)
    xdt = xdt_ref[...]
    bcb = bc_ref[...].astype(BF16)
    for k in range(bb):
        xk = jnp.where(rid == k, xdt, 0.0).astype(BF16)
        for g in range(SSD_GROUPS):
            rows = slice(g * half, (g + 1) * half)
            b_g = bcb[:, g * SSD_STATE:(g + 1) * SSD_STATE]
            c_g = bcb[:, (SSD_GROUPS + g) * SSD_STATE:(SSD_GROUPS + g + 1) * SSD_STATE]
            outer = lax.dot_general(xk[:, rows], b_g, (((0,), (0,)), ((), ())),
                                    preferred_element_type=F32)
            dec = jnp.concatenate(
                [jnp.broadcast_to(dec_ref[g * SSD_HPG + e, k:k + 1, :], (SSD_HEAD_DIM, SSD_STATE))
                 for e in range(SSD_HPG)], axis=0)
            s_new = dec * s_ref[k, rows, :] + outer
            o_ref[k, rows, :] = s_new
            yk = lax.dot_general(c_g, s_new.astype(BF16), (((1,), (1,)), ((), ())),
                                 preferred_element_type=F32)
            y_ref[k:k + 1, rows] = yk[k:k + 1, :]


def _state_stream(ssm, xdt, bc, dec, bb=8, start=0):
    nb = ssm.shape[0]
    state_block = (bb, W_SSD, SSD_STATE)
    return _Stream(
        [(ssm, state_block, lambda k: (k, 0, 0)),
         (xdt, (bb, W_SSD), lambda k: (k, 0)),
         (bc, (bb, 2 * SSD_GROUPS * SSD_STATE), lambda k: (k, 0)),
         (dec, (SSD_HEADS, bb, SSD_STATE), lambda k: (0, k, 0))],
        [(jax.ShapeDtypeStruct(ssm.shape, F32), state_block, lambda k: (k, 0, 0)),
         (jax.ShapeDtypeStruct((nb, W_SSD), F32), (bb, W_SSD), lambda k: (k, 0))],
        _sample_state_kernel, start, nb // bb)


def _sample_post_kernel(y_ref, xs_ref, proj_ref, dexp_ref, ng_ref, o_ref):
    z = proj_ref[:, 2 * W_LRU:2 * W_LRU + W_SSD]
    yg = (y_ref[...] + dexp_ref[...] * xs_ref[...]) * _silu(z)
    ms = jnp.mean(yg * yg, axis=-1, keepdims=True)
    o_ref[...] = (yg * lax.rsqrt(ms + EPS) * ng_ref[...]).astype(o_ref.dtype)


def _sample_post(y_raw, xs, proj, dexp, ng):
    return pl.pallas_call(
        _sample_post_kernel,
        out_shape=jax.ShapeDtypeStruct(y_raw.shape, BF16),
        compiler_params=pltpu.CompilerParams(vmem_limit_bytes=VMEM_LIMIT_BYTES),
        name="sample_post",
    )(y_raw, xs, proj, dexp, ng)


def _block_diag_groups(w):
    per = LRU_GATE_GROUP // LRU_BLOCK
    w4 = w.reshape(LRU_HEADS // per, per, LRU_BLOCK, LRU_BLOCK)
    bd = jnp.einsum("ghij,hk->ghikj", w4, jnp.eye(per, dtype=w.dtype))
    return bd.reshape(LRU_HEADS // per, LRU_GATE_GROUP, LRU_GATE_GROUP)


def _pad_lanes(v):
    v = v.reshape(1, -1)
    return jnp.pad(v, ((0, 0), (0, LANES - v.shape[1])))


def kernel(x_prompt, x_sample, c_prompt, c_sample, state_lru_h, state_lru_conv, state_ssm, state_ssd_conv, w_ada, b_ada, g_ffn1, w_up1, w_down1, g_mix, w_in, lru_conv_w, lru_conv_b, lru_wa, lru_ba, lru_wi, lru_bi, lru_lambda, ssd_conv_w, ssd_conv_b, ssd_dt_bias, ssd_A_log, ssd_D, ssd_norm_g, w_out, g_ffn2, w_up2, w_down2, w_ada_f, b_ada_f, g_final):
    bp, seq, d = x_prompt.shape
    bs = x_sample.shape[0]
    depth = w_ada.shape[0]
    assert depth == 1 and x_sample.shape[1] == 1 and d == D_MODEL

    pad_rows = (-(bs + bp)) % (2 * SUBLANES)
    c_rows = bs + bp + pad_rows
    c_all = jnp.concatenate([c_sample, c_prompt, jnp.zeros((pad_rows, d), F32)], axis=0)

    def split_rows(mod_all):
        width = mod_all.shape[1]
        return mod_all[bs:bs + bp].reshape(bp, 1, width), mod_all.reshape(1, c_rows, width)

    w_in_t = jnp.swapaxes(w_in[0], 0, 1)
    w_dt_t = jnp.pad(w_in_t[IN_MAIN:], ((0, LANES - SSD_HEADS), (0, 0)))
    up_blocks = D_FF // 512
    p = {
        "lru_cw": lru_conv_w[0], "lru_cb": lru_conv_b[0].reshape(1, W_LRU),
        "lru_wg": jnp.concatenate([_block_diag_groups(lru_wa[0]), _block_diag_groups(lru_wi[0])],
                                  axis=-1),
        "lru_ba": lru_ba[0].reshape(1, W_LRU), "lru_bi": lru_bi[0].reshape(1, W_LRU),
        "lru_lam": lru_lambda[0].reshape(1, W_LRU),
        "ssd_cw": ssd_conv_w[0], "ssd_cb": ssd_conv_b[0].reshape(1, SSD_CONV_DIM),
        "ssd_dtb": _pad_lanes(ssd_dt_bias[0]), "ssd_alog": _pad_lanes(ssd_A_log[0]),
        "ssd_dexp": jnp.repeat(ssd_D[0], SSD_HEAD_DIM).reshape(1, W_SSD),
        "ssd_ng": ssd_norm_g[0].reshape(1, W_SSD),
    }

    xp = x_prompt.reshape(bp * seq, d)
    xs = x_sample.reshape(bs, d)
    tm = 1024
    up_kw = dict(n_out=D_FF, swiglu=True, out_dtype=BF16)

    mod_a_all, silu_c = _ada(c_all, w_ada[0], b_ada[0], 2 * d)
    mod_a_p, mod_a_s = split_rows(mod_a_all)

    hp, hs = _norm_rows(xp, xs, g_ffn1[0], mod_a_p, mod_a_s, 0, tm=tm)
    first, (hmid_s,), wb = _proj_first_tile(
        hp, [(w_up1[0], 0), (w_up1[0], up_blocks)], tm=tm, tn=512, side=hs, **up_kw)
    (hmid,), ((w_down_b,), (mod_b_all,), (w_in_b,)) = _proj_other_tiles(
        hp, wb, first, tm=tm, tn=512,
        streams=[_cast_stream(w_down1[0], 32),
                 _ada_stream(silu_c, w_ada[0], b_ada[0], 2 * d, 3 * d, 256),
                 _cast_stream(w_in_t, 32, start=32, rows=IN_MAIN)],
        **up_kw)
    mod_b_p, mod_b_s = split_rows(mod_b_all)
    (xp, hp), (xs, hs) = _resid([hmid], [hmid_s], [(w_down_b, 0)], xp, xs, mod_b_p, mod_b_s, 0,
                                g_mix[0], mod_b_p, mod_b_s, 1, factor=0.5, tm=256, emit_x=True,
                                h_dtype=BF16)

    (proj, dt_raw), (proj_s, dt_raw_s), _, ((w_out_b,),) = _proj(
        hp, [(w_in_b, 0)], n_out=IN_MAIN, tm=tm, tn=IN_MAIN // 3, swiglu=False, out_dtype=F32,
        row_tiles=(0, bp * seq // tm), side=hs, trans_w=True, w_extra=w_dt_t,
        streams=[_cast_stream(w_out[0], 16)])
    lconv = state_lru_conv[0].reshape(bs, (CONV_W - 1) * W_LRU)
    sconv = state_ssd_conv[0].reshape(bs, (CONV_W - 1) * SSD_CONV_DIM)
    out_l_s, lru_h_s, lconv_new, sconv_new, xs_act, xdt, bc, dec = _sample_pre(
        proj_s, dt_raw_s, state_lru_h[0], lconv, sconv, p)
    out_l, lru_h_p, ((ssm_s, y_raw), (modf_all,)) = _lru_prompt(
        proj, bp, seq, p["lru_cw"], p["lru_cb"], p["lru_wg"], p["lru_ba"], p["lru_bi"],
        p["lru_lam"],
        streams=[_state_stream(state_ssm[0].reshape(bs, W_SSD, SSD_STATE), xdt, bc, dec),
                 _ada_stream(silu_c, w_ada_f, b_ada_f, 0, 2 * d, 2 * d // (bp * seq // LRU_TIME_TILE))])
    y_ssd_s = _sample_post(y_raw, xs_act, proj_s, p["ssd_dexp"], p["ssd_ng"])
    y_ssd, ssm_p, ((mod_c_all,),) = _ssd_prompt(
        proj, dt_raw, bp, seq, p["ssd_cw"], p["ssd_cb"], p["ssd_dtb"], p["ssd_alog"],
        p["ssd_dexp"], p["ssd_ng"],
        streams=[_ada_stream(silu_c, w_ada[0], b_ada[0], 5 * d, (N_MOD - 5) * d,
                             (N_MOD - 5) * d // (bp * seq // (SSD_CHUNK * SSD_CHUNKS_PER_STEP)))])
    mod_c_p, mod_c_s = split_rows(mod_c_all)
    modf_p, modf_s = split_rows(modf_all)
    proj3 = proj.reshape(bp, seq, IN_MAIN)
    lru_buf_p = proj3[:, seq - (CONV_W - 1):, :W_LRU]
    ssd_buf_p = proj3[:, seq - (CONV_W - 1):, 2 * W_LRU + W_SSD:]

    (xp, hp), (xs, hs) = _resid([out_l, y_ssd], [out_l_s, y_ssd_s], [(w_out_b, 0), (w_out_b, 1)],
                                xp, xs, mod_c_p, mod_c_s, 0, g_ffn2[0], mod_c_p, mod_c_s, 1,
                                factor=1.0, tm=512, emit_x=True, h_dtype=BF16)

    first, (hmid_s,), wb = _proj_first_tile(
        hp, [(w_up2[0], 0), (w_up2[0], up_blocks)], tm=tm, tn=512, side=hs, **up_kw)
    (hmid,), ((w_down_b,),) = _proj_other_tiles(
        hp, wb, first, tm=tm, tn=512, streams=[_cast_stream(w_down2[0], 32)], **up_kw)
    (yp,), (ys,) = _resid([hmid], [hmid_s], [(w_down_b, 0)], xp, xs, mod_c_p, mod_c_s, 3, g_final,
                          modf_p, modf_s, 0, factor=0.5, tm=256, emit_x=False, h_dtype=F32)

    stack = lambda v: v[None]
    return (yp.reshape(bp, seq, d), ys.reshape(bs, 1, d),
            stack(lru_h_p), stack(lru_buf_p), stack(ssm_p), stack(ssd_buf_p),
            stack(lru_h_s), stack(lconv_new.reshape(bs, CONV_W - 1, W_LRU)),
            stack(ssm_s.reshape(bs, SSD_HEADS, SSD_HEAD_DIM, SSD_STATE)),
            stack(sconv_new.reshape(bs, CONV_W - 1, SSD_CONV_DIM)))
```

```python
import functools
from typing import Callable, NamedTuple

import jax
import jax.numpy as jnp
from jax import lax
from jax.experimental import pallas as pl
from jax.experimental.pallas import tpu as pltpu

F32 = jnp.float32
BF16 = jnp.bfloat16

D_MODEL = 2048
D_FF = 5632
W_LRU = 1024
W_SSD = 1024
LRU_HEADS = 16
LRU_BLOCK = 64
LRU_C = 8.0
SSD_HEADS = 16
SSD_HEAD_DIM = 64
SSD_GROUPS = 2
SSD_HPG = 8
SSD_STATE = 128
SSD_CHUNK = 128
SSD_CHUNKS_PER_STEP = 2
CONV_W = 4
SSD_CONV_DIM = W_SSD + 2 * SSD_GROUPS * SSD_STATE
IN_MAIN = 2 * W_LRU + W_SSD + SSD_CONV_DIM
N_MOD = 9
EPS = 1e-6

LANES = 128
SUBLANES = 8
VMEM_LIMIT_BYTES = 56 * 1024 * 1024

LRU_GATE_GROUP = 256
LRU_TIME_TILE = 512
SCAN_ROWS = 2 * SUBLANES

ROW_TILE = 1024
UP_COL_TILE = 512
IN_COL_TILE = IN_MAIN // 3
RESID_ROWS_FFN = 256
RESID_ROWS_MIX = 512
CAST_CHUNKS = 32
CAST_CHUNKS_OUT = 16
ADA_STREAM_TILE = 256


def _sigmoid(v):
    return 0.5 * (jnp.tanh(0.5 * v) + 1.0)


def _silu(v):
    return v * _sigmoid(v)


def _softplus(v):
    return jnp.maximum(v, 0.0) + jnp.log1p(jnp.exp(-jnp.abs(v)))


def _gelu_tanh(v):
    return 0.5 * v * (1.0 + jnp.tanh(0.7978845608028654 * (v + 0.044715 * (v * v * v))))


def _bdot(a, b):
    return jnp.dot(a, b, preferred_element_type=F32)


def _params(sem):
    return pltpu.CompilerParams(dimension_semantics=sem, vmem_limit_bytes=VMEM_LIMIT_BYTES)


def _ada_mm_kernel(s_ref, w_ref, b_ref, o_ref):
    o_ref[...] = _bdot(s_ref[...], w_ref[...].astype(BF16)) + b_ref[...]


def _ada_kernel(c_ref, w_ref, b_ref, o_ref, s_ref):
    s_ref[...] = _silu(c_ref[...]).astype(BF16)
    _ada_mm_kernel(s_ref, w_ref, b_ref, o_ref)


def _ada(c, w, b, cols, tn=1024):
    m, k = c.shape
    n = cols
    return pl.pallas_call(
        _ada_kernel,
        grid=(n // tn,),
        in_specs=[pl.BlockSpec((m, k), lambda j: (0, 0)),
                  pl.BlockSpec((k, tn), lambda j: (0, j)),
                  pl.BlockSpec((1, tn), lambda j: (0, j))],
        out_specs=[pl.BlockSpec((m, tn), lambda j: (0, j)), pl.BlockSpec((m, k), lambda j: (0, 0))],
        out_shape=[jax.ShapeDtypeStruct((m, n), F32), jax.ShapeDtypeStruct((m, k), BF16)],
        compiler_params=_params(("arbitrary",)),
        name="ada_proj",
    )(c, w, b.reshape(1, -1))


def _norm_modulate(x, gain, shift, scale):
    ms = jnp.mean(x * x, axis=-1, keepdims=True)
    y = x * lax.rsqrt(ms + EPS) * gain
    return y * (1.0 + scale) + shift


def _norm_rows_kernel(x_ref, gain_ref, sh_ref, sc_ref, xs_ref, shs_ref, scs_ref, o_ref, os_ref):
    o_ref[...] = _norm_modulate(x_ref[...], gain_ref[...], sh_ref[...],
                                sc_ref[...]).astype(o_ref.dtype)

    @pl.when(pl.program_id(0) == 0)
    def _():
        os_ref[...] = _norm_modulate(xs_ref[...], gain_ref[...], shs_ref[...],
                                     scs_ref[...]).astype(os_ref.dtype)


def _norm_rows(x, x_s, gain, mod, mod_s, shift_chunk, *, tm):
    m, d = x.shape
    ns = x_s.shape[0]
    tiles_per_group = (m // tm) // mod.shape[0]
    once = dict(pipeline_mode=pl.Buffered(1))
    mod_spec = lambda c: pl.BlockSpec((None, 1, d), lambda i: (i // tiles_per_group, 0, c))
    mod_s_spec = lambda c: pl.BlockSpec((None, ns, d), lambda i: (0, 0, c), **once)
    return pl.pallas_call(
        _norm_rows_kernel,
        grid=(m // tm,),
        in_specs=[pl.BlockSpec((tm, d), lambda i: (i, 0)), pl.BlockSpec((1, d), lambda i: (0, 0)),
                  mod_spec(shift_chunk), mod_spec(shift_chunk + 1),
                  pl.BlockSpec((ns, d), lambda i: (0, 0), **once),
                  mod_s_spec(shift_chunk), mod_s_spec(shift_chunk + 1)],
        out_specs=[pl.BlockSpec((tm, d), lambda i: (i, 0)), pl.BlockSpec((ns, d), lambda i: (0, 0))],
        out_shape=[jax.ShapeDtypeStruct((m, d), BF16), jax.ShapeDtypeStruct((ns, d), BF16)],
        compiler_params=_params(("arbitrary",)),
        name="norm_rows",
    )(x, gain.reshape(1, d), mod, mod, x_s, mod_s, mod_s)


class _Stream(NamedTuple):
    ins: list
    outs: list
    body: Callable
    start: int
    steps: int


def _stream_io(streams, n_inner):
    in_specs, args, out_specs, out_shape = [], [], [], []

    def spec(st, block, index_fn):
        return pl.BlockSpec(
            block, lambda i, j: index_fn(jnp.clip(i * n_inner + j - st.start, 0, st.steps - 1)))

    for st in streams:
        for arr, block, index_fn in st.ins:
            in_specs.append(spec(st, block, index_fn))
            args.append(arr)
        for shape, block, index_fn in st.outs:
            out_specs.append(spec(st, block, index_fn))
            out_shape.append(shape)
    return in_specs, args, out_specs, out_shape


def _run_streams(streams, in_refs, out_refs, n_inner, n_steps):
    step = pl.program_id(0) * n_inner + pl.program_id(1)
    in_refs, out_refs = iter(in_refs), iter(out_refs)
    for st in streams:
        ins = [next(in_refs) for _ in st.ins]
        outs = [next(out_refs) for _ in st.outs]
        if st.start == 0 and st.steps == n_steps:
            st.body(*ins, *outs)
            continue

        @pl.when((step >= st.start) & (step < st.start + st.steps))
        def _(st=st, ins=ins, outs=outs):
            st.body(*ins, *outs)


def _split_stream_outs(streams, flat):
    flat = list(flat)
    return [[flat.pop(0) for _ in st.outs] for st in streams]


def _cast_body(src_ref, dst_ref):
    dst_ref[...] = src_ref[...].astype(dst_ref.dtype)


def _cast_stream(w, chunks, start=0, rows=None):
    rows = w.shape[0] if rows is None else rows
    block = (rows // chunks, w.shape[1])
    index = lambda k: (k, 0)
    return _Stream([(w, block, index)],
                   [(jax.ShapeDtypeStruct((rows, w.shape[1]), BF16), block, index)],
                   _cast_body, start, chunks)


def _ada_stream(sc, w, b, col0, cols, tn, start=0):
    m, k = sc.shape
    t0 = col0 // tn
    return _Stream(
        [(sc, (m, k), lambda s: (0, 0)), (w, (k, tn), lambda s: (0, s + t0)),
         (b.reshape(1, -1), (1, tn), lambda s: (0, s + t0))],
        [(jax.ShapeDtypeStruct((m, cols), F32), (m, tn), lambda s: (0, s))],
        _ada_mm_kernel, start, cols // tn)


def _wdot(h, w, trans_w):
    if trans_w:
        return lax.dot_general(h, w, (((1,), (1,)), ((), ())), preferred_element_type=F32)
    return _bdot(h, w)


def _proj_kernel(*refs, n_w, swiglu, trans_w, has_extra, has_side, emit_bf16, streams, n_prev,
                 nj, n_steps):
    it = iter(refs)
    x_ref = next(it)
    xs_ref = next(it) if has_side else None
    w_refs = [next(it) for _ in range(n_w)]
    wx_ref = next(it) if has_extra else None
    stream_ins = [next(it) for st in streams for _ in st.ins]
    for _ in range(n_prev):
        next(it)
    o_ref = next(it)
    ox_ref = next(it) if has_extra else None
    os_ref = next(it) if has_side else None
    osx_ref = next(it) if has_side and has_extra else None
    wo_refs = [next(it) for _ in range(n_w)] if emit_bf16 else []
    stream_outs = [next(it) for st in streams for _ in st.outs]

    j = pl.program_id(1)
    first_tile = pl.program_id(0) == 0

    if has_extra:
        @pl.when(j == 0)
        def _():
            ox_ref[...] = _wdot(x_ref[...], wx_ref[...].astype(BF16), trans_w)

        if has_side:
            @pl.when((j == 0) & first_tile)
            def _():
                osx_ref[...] = _wdot(xs_ref[...], wx_ref[...].astype(BF16), trans_w)

    wbs = [w_ref[...].astype(BF16) for w_ref in w_refs]
    for wo_ref, wb in zip(wo_refs, wbs):
        wo_ref[...] = wb

    def project(h, out_ref):
        if swiglu:
            g = _wdot(h, wbs[0], trans_w)
            u = _wdot(h, wbs[1], trans_w)
            out_ref[...] = (_silu(g) * u).astype(out_ref.dtype)
        else:
            out_ref[...] = _wdot(h, wbs[0], trans_w).astype(out_ref.dtype)

    project(x_ref[...], o_ref)
    if has_side:
        @pl.when(first_tile)
        def _():
            project(xs_ref[...], os_ref)

    _run_streams(streams, stream_ins, stream_outs, nj, n_steps)


def _proj(x, ws, *, n_out, tm, tn, swiglu, out_dtype, row_tiles, side=None, trans_w=False,
          w_extra=None, emit_bf16=False, streams=(), prev=None):
    m, d = x.shape
    t0, t1 = row_tiles
    nj = n_out // tn
    has_side = side is not None
    has_extra = w_extra is not None
    single_row_tile = t1 - t0 == 1
    once = dict(pipeline_mode=pl.Buffered(1))

    def w_spec(off):
        if trans_w:
            return pl.BlockSpec((tn, d), lambda i, j: (j + off, 0))
        return pl.BlockSpec((d, tn), lambda i, j: (0, j + off))

    x_mode = once if single_row_tile else {}
    in_specs = [pl.BlockSpec((tm, d), lambda i, j: (i + t0, 0), **x_mode)]
    args = [x]
    if has_side:
        ns = side.shape[0]
        in_specs.append(pl.BlockSpec((ns, d), lambda i, j: (0, 0), **once))
        args.append(side)
    in_specs += [w_spec(off) for _, off in ws]
    args += [w for w, _ in ws]
    if has_extra:
        nx = w_extra.shape[0] if trans_w else w_extra.shape[1]
        in_specs.append(pl.BlockSpec(w_extra.shape, lambda i, j: (0, 0)))
        args.append(w_extra)
    assert all(st.start + st.steps <= (t1 - t0) * nj for st in streams)
    st_in_specs, st_args, st_out_specs, st_out_shape = _stream_io(streams, nj)
    in_specs += st_in_specs
    args += st_args
    prev = list(prev or [])
    aliases = {}
    for k, buf in enumerate(prev):
        aliases[len(args)] = k
        in_specs.append(pl.BlockSpec(memory_space=pl.ANY))
        args.append(buf)

    out_specs = [pl.BlockSpec((tm, tn), lambda i, j: (i + t0, j))]
    out_shape = [jax.ShapeDtypeStruct((m, n_out), out_dtype)]
    if has_extra:
        out_specs.append(pl.BlockSpec((tm, nx), lambda i, j: (i + t0, 0)))
        out_shape.append(jax.ShapeDtypeStruct((m, nx), F32))
    n_main = len(out_shape)
    if has_side:
        out_specs.append(pl.BlockSpec((ns, tn), lambda i, j: (0, jnp.where(i == 0, j, nj - 1))))
        out_shape.append(jax.ShapeDtypeStruct((ns, n_out), out_dtype))
        if has_extra:
            out_specs.append(pl.BlockSpec((ns, nx), lambda i, j: (0, 0)))
            out_shape.append(jax.ShapeDtypeStruct((ns, nx), F32))
    n_side = len(out_shape) - n_main
    if emit_bf16:
        assert single_row_tile, "weight copies are written once per column tile"
        for _ in ws:
            out_specs.append(w_spec(0))
            out_shape.append(jax.ShapeDtypeStruct((n_out, d) if trans_w else (d, n_out), BF16))
    out_specs += st_out_specs
    out_shape += st_out_shape
    outs = pl.pallas_call(
        functools.partial(_proj_kernel, n_w=len(ws), swiglu=swiglu, trans_w=trans_w,
                          has_extra=has_extra, has_side=has_side, emit_bf16=emit_bf16,
                          streams=tuple(streams), n_prev=len(prev), nj=nj,
                          n_steps=(t1 - t0) * nj),
        grid=(t1 - t0, nj),
        in_specs=in_specs,
        out_specs=out_specs,
        out_shape=out_shape,
        input_output_aliases=aliases,
        compiler_params=_params(("arbitrary" if streams or has_side else "parallel", "arbitrary")),
        name="proj_swiglu" if swiglu else "proj",
    )(*args)
    n_wb = len(ws) if emit_bf16 else 0
    main, rest = outs[:n_main], outs[n_main:]
    side_outs, rest = rest[:n_side], rest[n_side:]
    wb, rest = rest[:n_wb], rest[n_wb:]
    return main, side_outs, wb, _split_stream_outs(streams, rest)


def _proj_first_tile(x, ws_f32, *, tm, tn, side, **kw):
    main, side_outs, wb, _ = _proj(x, ws_f32, tm=tm, tn=tn, row_tiles=(0, 1), side=side,
                                   emit_bf16=True, **kw)
    return main, side_outs, wb


def _proj_other_tiles(x, wb, prev, *, tm, tn, streams, **kw):
    main, _, _, stream_outs = _proj(x, [(w, 0) for w in wb], tm=tm, tn=tn,
                                    row_tiles=(1, x.shape[0] // tm), prev=prev, streams=streams,
                                    **kw)
    return main, stream_outs


def _resid_kernel(*refs, n_lhs, factor, emit_x):
    it = iter(refs)
    lhs_refs = [next(it) for _ in range(n_lhs)]
    lhs_s_refs = [next(it) for _ in range(n_lhs)]
    w_refs = [next(it) for _ in range(n_lhs)]
    x_ref, gate_ref, gain_ref, sh_ref, sc_ref = (next(it) for _ in range(5))
    xs_ref, gate_s_ref, sh_s_ref, sc_s_ref = (next(it) for _ in range(4))
    n_out = 2 if emit_x else 1
    outs = [next(it) for _ in range(n_out)]
    outs_s = [next(it) for _ in range(n_out)]

    def update(lhs, x_in, gate, sh, sc, out_refs):
        acc = _bdot(lhs[0][...], w_refs[0][...])
        for l_ref, w_ref in zip(lhs[1:], w_refs[1:]):
            acc = acc + _bdot(l_ref[...], w_ref[...])
        x_new = x_in[...] + (factor * gate[...]) * acc
        if emit_x:
            out_refs[0][...] = x_new
        h_ref = out_refs[-1]
        h_ref[...] = _norm_modulate(x_new, gain_ref[...], sh[...], sc[...]).astype(h_ref.dtype)

    update(lhs_refs, x_ref, gate_ref, sh_ref, sc_ref, outs)

    @pl.when(pl.program_id(0) == 0)
    def _():
        update(lhs_s_refs, xs_ref, gate_s_ref, sh_s_ref, sc_s_ref, outs_s)


def _resid(lhs_list, lhs_s_list, ws, x, x_s, mod, mod_s, gate_chunk, gain_next, mod_next,
           mod_next_s, shift_chunk_next, *, factor, tm, emit_x, h_dtype):
    m, d = x.shape
    ns = x_s.shape[0]
    groups = mod.shape[0]
    tiles_per_group = (m // tm) // groups
    kp = lhs_list[0].shape[1]
    once = dict(pipeline_mode=pl.Buffered(1))

    def mod_spec(chunk):
        return pl.BlockSpec((None, 1, d), lambda i: (i // tiles_per_group, 0, chunk))

    def mod_s_spec(chunk):
        return pl.BlockSpec((None, ns, d), lambda i: (0, 0, chunk), **once)

    in_specs = [pl.BlockSpec((tm, kp), lambda i: (i, 0)) for _ in lhs_list]
    in_specs += [pl.BlockSpec((ns, kp), lambda i: (0, 0), **once) for _ in lhs_s_list]
    in_specs += [pl.BlockSpec((kp, d), lambda i, k=k: (k, 0), **once) for _, k in ws]
    in_specs += [pl.BlockSpec((tm, d), lambda i: (i, 0)), mod_spec(gate_chunk),
                 pl.BlockSpec((1, d), lambda i: (0, 0)),
                 mod_spec(shift_chunk_next), mod_spec(shift_chunk_next + 1),
                 pl.BlockSpec((ns, d), lambda i: (0, 0), **once), mod_s_spec(gate_chunk),
                 mod_s_spec(shift_chunk_next), mod_s_spec(shift_chunk_next + 1)]
    row = pl.BlockSpec((tm, d), lambda i: (i, 0))
    row_s = pl.BlockSpec((ns, d), lambda i: (0, 0))
    dtypes = ([F32] if emit_x else []) + [h_dtype]
    out_specs = [row for _ in dtypes] + [row_s for _ in dtypes]
    out_shape = ([jax.ShapeDtypeStruct((m, d), t) for t in dtypes]
                 + [jax.ShapeDtypeStruct((ns, d), t) for t in dtypes])
    outs = pl.pallas_call(
        functools.partial(_resid_kernel, n_lhs=len(lhs_list), factor=factor, emit_x=emit_x),
        grid=(m // tm,),
        in_specs=in_specs,
        out_specs=out_specs,
        out_shape=out_shape,
        compiler_params=_params(("arbitrary",)),
        name="resid",
    )(*lhs_list, *lhs_s_list, *[w for w, _ in ws], x, mod, gain_next.reshape(1, d), mod_next,
      mod_next, x_s, mod_s, mod_next_s, mod_next_s)
    return outs[:len(dtypes)], outs[len(dtypes):]


def _lru_gates(xc, wg_ref, ba, bi, sp):
    a_parts, b_parts = [], []
    for g in range(W_LRU // LRU_GATE_GROUP):
        cols = slice(g * LRU_GATE_GROUP, (g + 1) * LRU_GATE_GROUP)
        xg = xc[:, cols]
        ri = _bdot(xg.astype(BF16), wg_ref[g].astype(BF16))
        r = _sigmoid(ri[:, :LRU_GATE_GROUP] + ba[:, cols])
        i = _sigmoid(ri[:, LRU_GATE_GROUP:] + bi[:, cols])
        log_a = (-LRU_C * r) * sp[:, cols]
        a = jnp.exp(log_a)
        a_parts.append(a)
        v = 1.0 - a * a
        root = jnp.where(v > 0.0, v * lax.rsqrt(v), 0.0)
        b_parts.append(root * (i * xg))
    return jnp.concatenate(a_parts, axis=1), jnp.concatenate(b_parts, axis=1)


def _causal_conv(prev8, x, w_ref, b_ref):
    rows, width = x.shape
    rid = lax.broadcasted_iota(jnp.int32, (SUBLANES, width), 0)
    shifts = (1, 2, 3)
    taps = [w_ref[k:k + 1, :] for k in range(CONV_W)]
    bias = b_ref[...]
    prev_rot = [pltpu.roll(prev8, k, 0) for k in shifts]
    out = []
    for r in range(rows // SUBLANES):
        cur = x[r * SUBLANES:(r + 1) * SUBLANES, :]
        cur_rot = [pltpu.roll(cur, k, 0) for k in shifts]
        s1, s2, s3 = [jnp.where(rid < k, p, c) for k, p, c in zip(shifts, prev_rot, cur_rot)]
        out.append(bias + taps[0] * s3 + taps[1] * s2 + taps[2] * s1 + taps[3] * cur)
        prev_rot = cur_rot
    return jnp.concatenate(out, axis=0)


def _lru_prompt_kernel(*refs, streams, nt, n_steps):
    n_in = sum(len(st.ins) for st in streams)
    n_out = sum(len(st.outs) for st in streams)
    xl_ref, gl_ref, cw_ref, cb_ref, wg_ref, ba_ref, bi_ref, lam_ref = refs[:8]
    stream_ins = refs[8:8 + n_in]
    o_ref, hT_ref = refs[8 + n_in:10 + n_in]
    stream_outs = refs[10 + n_in:10 + n_in + n_out]
    xbuf, a_scr, b_scr, hcar = refs[10 + n_in + n_out:]
    t = pl.program_id(1)
    tt = xl_ref.shape[0]

    @pl.when(t == 0)
    def _():
        xbuf[...] = jnp.zeros_like(xbuf)
        hcar[...] = jnp.zeros_like(hcar)

    _run_streams(streams, stream_ins, stream_outs, nt, n_steps)

    x = xl_ref[...]
    xc = _causal_conv(xbuf[...], x, cw_ref, cb_ref)
    xbuf[...] = x[tt - SUBLANES:, :]

    sp = _softplus(-lam_ref[...])
    a, bt = _lru_gates(xc, wg_ref, ba_ref[...], bi_ref[...], sp)
    a_scr[...] = a
    b_scr[...] = bt

    rid = lax.broadcasted_iota(jnp.int32, (SUBLANES, W_LRU), 0)

    def scan8(a8, b8, h_in):
        for s in (1, 2, 4):
            a_sh = pltpu.roll(a8, s, 0)
            b_sh = pltpu.roll(b8, s, 0)
            m = rid >= s
            b8 = jnp.where(m, a8 * b_sh + b8, b8)
            a8 = jnp.where(m, a8 * a_sh, a8)
        h8 = a8 * h_in + b8
        return h8, jnp.broadcast_to(h8[SUBLANES - 1:SUBLANES, :], (SUBLANES, W_LRU))

    def body(g, h_in):
        r0 = pl.multiple_of(g * SCAN_ROWS, SCAN_ROWS)
        lo = pl.ds(r0, SUBLANES)
        hi = pl.ds(r0 + SUBLANES, SUBLANES)
        h_lo, h_mid = scan8(a_scr[lo, :], b_scr[lo, :], h_in)
        h_hi, h_out = scan8(a_scr[hi, :], b_scr[hi, :], h_mid)
        rows = pl.ds(r0, SCAN_ROWS)
        h16 = jnp.concatenate([h_lo, h_hi], axis=0)
        o_ref[rows, :] = (h16 * _gelu_tanh(gl_ref[rows, :])).astype(o_ref.dtype)
        return h_out

    h_last = lax.fori_loop(0, tt // SCAN_ROWS, body, hcar[...])
    hcar[...] = h_last

    @pl.when(t == pl.num_programs(1) - 1)
    def _():
        hT_ref[...] = h_last[0:1, :]


def _lru_prompt(proj, batch, seq, cw, cb, wg, ba, bi, lam, streams=()):
    tt = LRU_TIME_TILE
    nt = seq // tt
    assert all(st.start + st.steps <= batch * nt for st in streams)
    st_in_specs, st_args, st_out_specs, st_out_shape = _stream_io(streams, nt)
    row = lambda v: v.reshape(1, W_LRU)
    full = lambda shape: pl.BlockSpec(shape, lambda b, t: (0,) * len(shape))
    out, h_t, *rest = pl.pallas_call(
        functools.partial(_lru_prompt_kernel, streams=tuple(streams), nt=nt, n_steps=batch * nt),
        grid=(batch, nt),
        in_specs=[pl.BlockSpec((tt, W_LRU), lambda b, t: (b * nt + t, 0)),
                  pl.BlockSpec((tt, W_LRU), lambda b, t: (b * nt + t, 1)),
                  full((CONV_W, W_LRU)), full((1, W_LRU)), full(wg.shape),
                  full((1, W_LRU)), full((1, W_LRU)), full((1, W_LRU))] + st_in_specs,
        out_specs=[pl.BlockSpec((tt, W_LRU), lambda b, t: (b * nt + t, 0)),
                   pl.BlockSpec((None, 1, W_LRU), lambda b, t: (b, 0, 0))] + st_out_specs,
        out_shape=[jax.ShapeDtypeStruct((batch * seq, W_LRU), BF16),
                   jax.ShapeDtypeStruct((batch, 1, W_LRU), F32)] + st_out_shape,
        scratch_shapes=[pltpu.VMEM((SUBLANES, W_LRU), F32),
                        pltpu.VMEM((tt, W_LRU), F32),
                        pltpu.VMEM((tt, W_LRU), F32),
                        pltpu.VMEM((SUBLANES, W_LRU), F32)],
        compiler_params=_params(("arbitrary" if streams else "parallel", "arbitrary")),
        name="lru_prompt",
    )(proj, proj, cw, row(cb), wg, row(ba), row(bi), row(lam), *st_args)
    return out, h_t.reshape(batch, W_LRU), _split_stream_outs(streams, rest)


def _ssd_chunk(z_ref, xbc_ref, dt_ref, cw_ref, cb_ref, dtb_ref, alog_ref, dexp_ref, ng_ref, y_ref,
               xbuf, st_scr, y_scr, m_scr, xbd_scr):
    lc = SSD_CHUNK
    x = xbc_ref[...]
    act = _silu(_causal_conv(xbuf[...], x, cw_ref, cb_ref))
    xbuf[...] = x[lc - SUBLANES:, :]
    xs = act[:, :W_SSD]
    bm = act[:, W_SSD:W_SSD + SSD_GROUPS * SSD_STATE]
    cm = act[:, W_SSD + SSD_GROUPS * SSD_STATE:]

    dt = _softplus(dt_ref[...] + dtb_ref[...])
    d_a = dt * (-jnp.exp(alog_ref[...]))
    row_i = lax.broadcasted_iota(jnp.int32, (lc, lc), 0)
    col_i = lax.broadcasted_iota(jnp.int32, (lc, lc), 1)
    causal = row_i >= col_i
    tril = jnp.where(causal, 1.0, 0.0).astype(F32)
    cs = jnp.dot(tril, d_a, preferred_element_type=F32, precision=lax.Precision.HIGHEST)
    cs_t = cs.T
    dt_t = dt.T
    cs_last = cs[lc - 1:lc, :]

    def per_head_lanes(v):
        rows = v.shape[0]
        return jnp.concatenate(
            [jnp.broadcast_to(v[:, h:h + 1], (rows, SSD_HEAD_DIM)) for h in range(SSD_HEADS)],
            axis=1)

    w_exp = per_head_lanes(jnp.exp(cs_last - cs) * dt)
    ecs_exp = per_head_lanes(jnp.exp(cs))
    cd_exp = per_head_lanes(jnp.exp(cs_last))
    gw = SSD_HPG * SSD_HEAD_DIM
    low_half = col_i < SSD_HEAD_DIM

    for g in range(SSD_GROUPS):
        ncols = slice(g * SSD_STATE, (g + 1) * SSD_STATE)
        gcols = slice(g * gw, (g + 1) * gw)
        b_g = bm[:, ncols].astype(BF16)
        c_g = cm[:, ncols].astype(BF16)
        cb_mat = lax.dot_general(c_g, b_g, (((1,), (1,)), ((), ())),
                                 preferred_element_type=F32)
        for e in range(SSD_HPG):
            h = g * SSD_HPG + e
            cs_col = jnp.broadcast_to(cs[:, h:h + 1], (lc, lc))
            l_mat = jnp.exp(jnp.where(causal, cs_col - cs_t[h:h + 1, :], -jnp.inf))
            m_scr[g, :, e * lc:(e + 1) * lc] = (cb_mat * l_mat * dt_t[h:h + 1, :]).astype(BF16)
        for q in range(SSD_HPG // 2):
            lanes = slice(q * LANES, (q + 1) * LANES)
            slab = xs[:, g * gw + q * LANES:g * gw + (q + 1) * LANES]
            xbd_scr[g, (2 * q) * lc:(2 * q + 1) * lc, lanes] = jnp.where(
                low_half, slab, 0.0).astype(BF16)
            xbd_scr[g, (2 * q + 1) * lc:(2 * q + 2) * lc, lanes] = jnp.where(
                low_half, 0.0, slab).astype(BF16)
        st_g = st_scr[:, gcols]
        y_off = _bdot(c_g, st_g.astype(BF16)) * ecs_exp[:, gcols]
        y_scr[:, gcols] = (_bdot(m_scr[g], xbd_scr[g]) + y_off
                           + dexp_ref[:, gcols] * xs[:, gcols])
        xw = (xs[:, gcols] * w_exp[:, gcols]).astype(BF16)
        st_scr[:, gcols] = cd_exp[:, gcols] * st_g + lax.dot_general(
            b_g, xw, (((0,), (0,)), ((), ())), preferred_element_type=F32)

    yg = y_scr[...] * _silu(z_ref[...])
    ms = jnp.mean(yg * yg, axis=-1, keepdims=True)
    y_ref[...] = (yg * lax.rsqrt(ms + EPS) * ng_ref[...]).astype(y_ref.dtype)


def _ssd_prompt_kernel(*refs, streams, nc, n_steps):
    n_in = sum(len(st.ins) for st in streams)
    n_out = sum(len(st.outs) for st in streams)
    z_ref, xbc_ref, dt_ref, cw_ref, cb_ref, dtb_ref, alog_ref, dexp_ref, ng_ref = refs[:9]
    stream_ins = refs[9:9 + n_in]
    y_ref, st_ref = refs[9 + n_in:11 + n_in]
    stream_outs = refs[11 + n_in:11 + n_in + n_out]
    xbuf, st_scr, y_scr, m_scr, xbd_scr = refs[11 + n_in + n_out:]
    c = pl.program_id(1)

    @pl.when(c == 0)
    def _():
        xbuf[...] = jnp.zeros_like(xbuf)
        st_scr[...] = jnp.zeros_like(st_scr)
        xbd_scr[...] = jnp.zeros_like(xbd_scr)

    _run_streams(streams, stream_ins, stream_outs, nc, n_steps)

    for cc in range(SSD_CHUNKS_PER_STEP):
        rows = pl.ds(cc * SSD_CHUNK, SSD_CHUNK)
        _ssd_chunk(z_ref.at[rows], xbc_ref.at[rows], dt_ref.at[rows], cw_ref, cb_ref, dtb_ref,
                   alog_ref, dexp_ref, ng_ref, y_ref.at[rows], xbuf, st_scr, y_scr.at[cc],
                   m_scr.at[cc], xbd_scr.at[cc])

    @pl.when(c == pl.num_programs(1) - 1)
    def _():
        st_ref[...] = st_scr[...].T


def _ssd_prompt(proj, dt_raw, batch, seq, cw, cb, dtb, alog, dexp, ng, streams=()):
    lc = SSD_CHUNK
    cps = SSD_CHUNKS_PER_STEP
    tl = cps * lc
    nc = seq // tl
    assert all(st.start + st.steps <= batch * nc for st in streams)
    st_in_specs, st_args, st_out_specs, st_out_shape = _stream_io(streams, nc)
    full = lambda shape: pl.BlockSpec(shape, lambda b, c: (0,) * len(shape))
    z_blk = (2 * W_LRU) // W_SSD
    xbc_blk = (2 * W_LRU + W_SSD) // SSD_CONV_DIM
    y, st, *rest = pl.pallas_call(
        functools.partial(_ssd_prompt_kernel, streams=tuple(streams), nc=nc, n_steps=batch * nc),
        grid=(batch, nc),
        in_specs=[pl.BlockSpec((tl, W_SSD), lambda b, c: (b * nc + c, z_blk)),
                  pl.BlockSpec((tl, SSD_CONV_DIM), lambda b, c: (b * nc + c, xbc_blk)),
                  pl.BlockSpec((tl, LANES), lambda b, c: (b * nc + c, 0)),
                  full((CONV_W, SSD_CONV_DIM)), full((1, SSD_CONV_DIM)),
                  full((1, LANES)), full((1, LANES)), full((1, W_SSD)), full((1, W_SSD))]
        + st_in_specs,
        out_specs=[pl.BlockSpec((tl, W_SSD), lambda b, c: (b * nc + c, 0)),
                   pl.BlockSpec((None, W_SSD, SSD_STATE), lambda b, c: (b, 0, 0))] + st_out_specs,
        out_shape=[jax.ShapeDtypeStruct((batch * seq, W_SSD), BF16),
                   jax.ShapeDtypeStruct((batch, W_SSD, SSD_STATE), F32)] + st_out_shape,
        scratch_shapes=[pltpu.VMEM((SUBLANES, SSD_CONV_DIM), F32),
                        pltpu.VMEM((SSD_STATE, W_SSD), F32),
                        pltpu.VMEM((cps, lc, W_SSD), F32),
                        pltpu.VMEM((cps, SSD_GROUPS, lc, SSD_HPG * lc), BF16),
                        pltpu.VMEM((cps, SSD_GROUPS, SSD_HPG * lc, SSD_HPG * SSD_HEAD_DIM), BF16)],
        compiler_params=_params(("arbitrary" if streams else "parallel", "arbitrary")),
        name="ssd_prompt",
    )(proj, proj, dt_raw, cw, cb, dtb, alog, dexp, ng, *st_args)
    return (y, st.reshape(batch, SSD_HEADS, SSD_HEAD_DIM, SSD_STATE),
            _split_stream_outs(streams, rest))


def _sample_pre_kernel(proj_ref, dt_ref, h0_ref, lconv_ref, sconv_ref,
                       lcw_ref, lcb_ref, wg_ref, ba_ref, bi_ref, lam_ref,
                       scw_ref, scb_ref, dtb_ref, alog_ref,
                       outl_ref, hnew_ref, lconv_new_ref, sconv_new_ref,
                       xs_ref, xdt_ref, bc_ref, dec_ref):
    nb = proj_ref.shape[0]
    xl = proj_ref[:, 0:W_LRU]
    gl = proj_ref[:, W_LRU:2 * W_LRU]
    xbc = proj_ref[:, 2 * W_LRU + W_SSD:IN_MAIN]

    def conv1(state_ref, width, x_new, w_ref, b_ref):
        y = b_ref[...] + w_ref[0:1, :] * state_ref[:, 0:width]
        y = y + w_ref[1:2, :] * state_ref[:, width:2 * width]
        y = y + w_ref[2:3, :] * state_ref[:, 2 * width:3 * width]
        return y + w_ref[3:4, :] * x_new

    xc = conv1(lconv_ref, W_LRU, xl, lcw_ref, lcb_ref)
    a, bt = _lru_gates(xc, wg_ref, ba_ref[...], bi_ref[...], _softplus(-lam_ref[...]))
    h_new = a * h0_ref[...] + bt
    hnew_ref[...] = h_new
    outl_ref[...] = (h_new * _gelu_tanh(gl)).astype(outl_ref.dtype)
    lconv_new_ref[:, 0:2 * W_LRU] = lconv_ref[:, W_LRU:3 * W_LRU]
    lconv_new_ref[:, 2 * W_LRU:3 * W_LRU] = xl

    act = _silu(conv1(sconv_ref, SSD_CONV_DIM, xbc, scw_ref, scb_ref))
    sconv_new_ref[:, 0:2 * SSD_CONV_DIM] = sconv_ref[:, SSD_CONV_DIM:3 * SSD_CONV_DIM]
    sconv_new_ref[:, 2 * SSD_CONV_DIM:3 * SSD_CONV_DIM] = xbc
    xs = act[:, :W_SSD]
    xs_ref[...] = xs
    bc_ref[...] = act[:, W_SSD:]
    dt = _softplus(dt_ref[...] + dtb_ref[...])
    dec = jnp.exp(dt * (-jnp.exp(alog_ref[...])))
    for h in range(SSD_HEADS):
        pcols = slice(h * SSD_HEAD_DIM, (h + 1) * SSD_HEAD_DIM)
        xdt_ref[:, pcols] = xs[:, pcols] * jnp.broadcast_to(dt[:, h:h + 1], (nb, SSD_HEAD_DIM))
        dec_ref[h] = jnp.broadcast_to(dec[:, h:h + 1], (nb, SSD_STATE))


def _sample_pre(proj, dt_raw, h0, lconv, sconv, p):
    nb = proj.shape[0]
    out_shape = [jax.ShapeDtypeStruct((nb, W_LRU), BF16),
                 jax.ShapeDtypeStruct((nb, W_LRU), F32),
                 jax.ShapeDtypeStruct((nb, 3 * W_LRU), F32),
                 jax.ShapeDtypeStruct((nb, 3 * SSD_CONV_DIM), F32),
                 jax.ShapeDtypeStruct((nb, W_SSD), F32),
                 jax.ShapeDtypeStruct((nb, W_SSD), F32),
                 jax.ShapeDtypeStruct((nb, 2 * SSD_GROUPS * SSD_STATE), F32),
                 jax.ShapeDtypeStruct((SSD_HEADS, nb, SSD_STATE), F32)]
    return pl.pallas_call(
        _sample_pre_kernel,
        out_shape=out_shape,
        compiler_params=pltpu.CompilerParams(vmem_limit_bytes=VMEM_LIMIT_BYTES),
        name="sample_pre",
    )(proj, dt_raw, h0, lconv, sconv,
      p["lru_cw"], p["lru_cb"], p["lru_wg"], p["lru_ba"], p["lru_bi"], p["lru_lam"],
      p["ssd_cw"], p["ssd_cb"], p["ssd_dtb"], p["ssd_alog"])


def _sample_state_kernel(s_ref, xdt_ref, bc_ref, dec_ref, o_ref, y_ref):
    bb = s_ref.shape[0]
    half = SSD_HPG * SSD_HEAD_DIM
    rid = lax.broadcasted_iota(jnp.int32, (bb, W_SSD), 0)
    xdt = xdt_ref[...]
    bcb = bc_ref[...].astype(BF16)
    for k in range(bb):
        xk = jnp.where(rid == k, xdt, 0.0).astype(BF16)
        for g in range(SSD_GROUPS):
            rows = slice(g * half, (g + 1) * half)
            b_g = bcb[:, g * SSD_STATE:(g + 1) * SSD_STATE]
            c_g = bcb[:, (SSD_GROUPS + g) * SSD_STATE:(SSD_GROUPS + g + 1) * SSD_STATE]
            outer = lax.dot_general(xk[:, rows], b_g, (((0,), (0,)), ((), ())),
                                    preferred_element_type=F32)
            dec = jnp.concatenate(
                [jnp.broadcast_to(dec_ref[g * SSD_HPG + e, k:k + 1, :], (SSD_HEAD_DIM, SSD_STATE))
                 for e in range(SSD_HPG)], axis=0)
            s_new = dec * s_ref[k, rows, :] + outer
            o_ref[k, rows, :] = s_new
            yk = lax.dot_general(c_g, s_new.astype(BF16), (((1,), (1,)), ((), ())),
                                 preferred_element_type=F32)
            y_ref[k:k + 1, rows] = yk[k:k + 1, :]


def _state_stream(ssm, xdt, bc, dec, bb=8, start=0):
    nb = ssm.shape[0]
    state_block = (bb, W_SSD, SSD_STATE)
    return _Stream(
        [(ssm, state_block, lambda k: (k, 0, 0)),
         (xdt, (bb, W_SSD), lambda k: (k, 0)),
         (bc, (bb, 2 * SSD_GROUPS * SSD_STATE), lambda k: (k, 0)),
         (dec, (SSD_HEADS, bb, SSD_STATE), lambda k: (0, k, 0))],
        [(jax.ShapeDtypeStruct(ssm.shape, F32), state_block, lambda k: (k, 0, 0)),
         (jax.ShapeDtypeStruct((nb, W_SSD), F32), (bb, W_SSD), lambda k: (k, 0))],
        _sample_state_kernel, start, nb // bb)


def _sample_post_kernel(y_ref, xs_ref, proj_ref, dexp_ref, ng_ref, o_ref):
    z = proj_ref[:, 2 * W_LRU:2 * W_LRU + W_SSD]
    yg = (y_ref[...] + dexp_ref[...] * xs_ref[...]) * _silu(z)
    ms = jnp.mean(yg * yg, axis=-1, keepdims=True)
    o_ref[...] = (yg * lax.rsqrt(ms + EPS) * ng_ref[...]).astype(o_ref.dtype)


def _sample_post(y_raw, xs, proj, dexp, ng):
    return pl.pallas_call(
        _sample_post_kernel,
        out_shape=jax.ShapeDtypeStruct(y_raw.shape, BF16),
        compiler_params=pltpu.CompilerParams(vmem_limit_bytes=VMEM_LIMIT_BYTES),
        name="sample_post",
    )(y_raw, xs, proj, dexp, ng)


def _block_diag_groups(w):
    per = LRU_GATE_GROUP // LRU_BLOCK
    w4 = w.reshape(LRU_HEADS // per, per, LRU_BLOCK, LRU_BLOCK)
    bd = jnp.einsum("ghij,hk->ghikj", w4, jnp.eye(per, dtype=w.dtype))
    return bd.reshape(LRU_HEADS // per, LRU_GATE_GROUP, LRU_GATE_GROUP)


def _pad_lanes(v):
    v = v.reshape(1, -1)
    return jnp.pad(v, ((0, 0), (0, LANES - v.shape[1])))


def kernel(x_prompt, x_sample, c_prompt, c_sample, state_lru_h, state_lru_conv, state_ssm, state_ssd_conv, w_ada, b_ada, g_ffn1, w_up1, w_down1, g_mix, w_in, lru_conv_w, lru_conv_b, lru_wa, lru_ba, lru_wi, lru_bi, lru_lambda, ssd_conv_w, ssd_conv_b, ssd_dt_bias, ssd_A_log, ssd_D, ssd_norm_g, w_out, g_ffn2, w_up2, w_down2, w_ada_f, b_ada_f, g_final):
    bp, seq, d = x_prompt.shape
    bs = x_sample.shape[0]
    depth = w_ada.shape[0]
    assert depth == 1 and x_sample.shape[1] == 1 and d == D_MODEL

    pad_rows = (-(bs + bp)) % (2 * SUBLANES)
    c_rows = bs + bp + pad_rows
    c_all = jnp.concatenate([c_sample, c_prompt, jnp.zeros((pad_rows, d), F32)], axis=0)

    def split_rows(mod_all):
        width = mod_all.shape[1]
        return mod_all[bs:bs + bp].reshape(bp, 1, width), mod_all.reshape(1, c_rows, width)

    w_in_t = jnp.swapaxes(w_in[0], 0, 1)
    w_dt_t = jnp.pad(w_in_t[IN_MAIN:], ((0, LANES - SSD_HEADS), (0, 0)))
    up_blocks = D_FF // UP_COL_TILE
    p = {
        "lru_cw": lru_conv_w[0], "lru_cb": lru_conv_b[0].reshape(1, W_LRU),
        "lru_wg": jnp.concatenate([_block_diag_groups(lru_wa[0]), _block_diag_groups(lru_wi[0])],
                                  axis=-1),
        "lru_ba": lru_ba[0].reshape(1, W_LRU), "lru_bi": lru_bi[0].reshape(1, W_LRU),
        "lru_lam": lru_lambda[0].reshape(1, W_LRU),
        "ssd_cw": ssd_conv_w[0], "ssd_cb": ssd_conv_b[0].reshape(1, SSD_CONV_DIM),
        "ssd_dtb": _pad_lanes(ssd_dt_bias[0]), "ssd_alog": _pad_lanes(ssd_A_log[0]),
        "ssd_dexp": jnp.repeat(ssd_D[0], SSD_HEAD_DIM).reshape(1, W_SSD),
        "ssd_ng": ssd_norm_g[0].reshape(1, W_SSD),
    }

    xp = x_prompt.reshape(bp * seq, d)
    xs = x_sample.reshape(bs, d)
    tm = ROW_TILE
    up_kw = dict(n_out=D_FF, swiglu=True, out_dtype=BF16)

    mod_a_all, silu_c = _ada(c_all, w_ada[0], b_ada[0], 2 * d)
    mod_a_p, mod_a_s = split_rows(mod_a_all)

    hp, hs = _norm_rows(xp, xs, g_ffn1[0], mod_a_p, mod_a_s, 0, tm=tm)
    first, (hmid_s,), wb = _proj_first_tile(
        hp, [(w_up1[0], 0), (w_up1[0], up_blocks)], tm=tm, tn=UP_COL_TILE, side=hs, **up_kw)
    (hmid,), ((w_down_b,), (mod_b_all,), (w_in_b,)) = _proj_other_tiles(
        hp, wb, first, tm=tm, tn=UP_COL_TILE,
        streams=[_cast_stream(w_down1[0], CAST_CHUNKS),
                 _ada_stream(silu_c, w_ada[0], b_ada[0], 2 * d, 3 * d, ADA_STREAM_TILE),
                 _cast_stream(w_in_t, CAST_CHUNKS, start=CAST_CHUNKS, rows=IN_MAIN)],
        **up_kw)
    mod_b_p, mod_b_s = split_rows(mod_b_all)
    (xp, hp), (xs, hs) = _resid([hmid], [hmid_s], [(w_down_b, 0)], xp, xs, mod_b_p, mod_b_s, 0,
                                g_mix[0], mod_b_p, mod_b_s, 1, factor=0.5, tm=RESID_ROWS_FFN,
                                emit_x=True,
                                h_dtype=BF16)

    (proj, dt_raw), (proj_s, dt_raw_s), _, ((w_out_b,),) = _proj(
        hp, [(w_in_b, 0)], n_out=IN_MAIN, tm=tm, tn=IN_COL_TILE, swiglu=False, out_dtype=F32,
        row_tiles=(0, bp * seq // tm), side=hs, trans_w=True, w_extra=w_dt_t,
        streams=[_cast_stream(w_out[0], CAST_CHUNKS_OUT)])
    lconv = state_lru_conv[0].reshape(bs, (CONV_W - 1) * W_LRU)
    sconv = state_ssd_conv[0].reshape(bs, (CONV_W - 1) * SSD_CONV_DIM)
    out_l_s, lru_h_s, lconv_new, sconv_new, xs_act, xdt, bc, dec = _sample_pre(
        proj_s, dt_raw_s, state_lru_h[0], lconv, sconv, p)
    out_l, lru_h_p, ((ssm_s, y_raw), (modf_all,)) = _lru_prompt(
        proj, bp, seq, p["lru_cw"], p["lru_cb"], p["lru_wg"], p["lru_ba"], p["lru_bi"],
        p["lru_lam"],
        streams=[_state_stream(state_ssm[0].reshape(bs, W_SSD, SSD_STATE), xdt, bc, dec),
                 _ada_stream(silu_c, w_ada_f, b_ada_f, 0, 2 * d, 2 * d // (bp * seq // LRU_TIME_TILE))])
    y_ssd_s = _sample_post(y_raw, xs_act, proj_s, p["ssd_dexp"], p["ssd_ng"])
    y_ssd, ssm_p, ((mod_c_all,),) = _ssd_prompt(
        proj, dt_raw, bp, seq, p["ssd_cw"], p["ssd_cb"], p["ssd_dtb"], p["ssd_alog"],
        p["ssd_dexp"], p["ssd_ng"],
        streams=[_ada_stream(silu_c, w_ada[0], b_ada[0], 5 * d, (N_MOD - 5) * d,
                             (N_MOD - 5) * d // (bp * seq // (SSD_CHUNK * SSD_CHUNKS_PER_STEP)))])
    mod_c_p, mod_c_s = split_rows(mod_c_all)
    modf_p, modf_s = split_rows(modf_all)
    proj3 = proj.reshape(bp, seq, IN_MAIN)
    lru_buf_p = proj3[:, seq - (CONV_W - 1):, :W_LRU]
    ssd_buf_p = proj3[:, seq - (CONV_W - 1):, 2 * W_LRU + W_SSD:]

    (xp, hp), (xs, hs) = _resid([out_l, y_ssd], [out_l_s, y_ssd_s], [(w_out_b, 0), (w_out_b, 1)],
                                xp, xs, mod_c_p, mod_c_s, 0, g_ffn2[0], mod_c_p, mod_c_s, 1,
                                factor=1.0, tm=RESID_ROWS_MIX, emit_x=True, h_dtype=BF16)

    first, (hmid_s,), wb = _proj_first_tile(
        hp, [(w_up2[0], 0), (w_up2[0], up_blocks)], tm=tm, tn=UP_COL_TILE, side=hs, **up_kw)
    (hmid,), ((w_down_b,),) = _proj_other_tiles(
        hp, wb, first, tm=tm, tn=UP_COL_TILE, streams=[_cast_stream(w_down2[0], CAST_CHUNKS)],
        **up_kw)
    (yp,), (ys,) = _resid([hmid], [hmid_s], [(w_down_b, 0)], xp, xs, mod_c_p, mod_c_s, 3, g_final,
                          modf_p, modf_s, 0, factor=0.5, tm=RESID_ROWS_FFN, emit_x=False,
                          h_dtype=F32)

    stack = lambda v: v[None]
    return (yp.reshape(bp, seq, d), ys.reshape(bs, 1, d),
            stack(lru_h_p), stack(lru_buf_p), stack(ssm_p), stack(ssd_buf_p),
            stack(lru_h_s), stack(lconv_new.reshape(bs, CONV_W - 1, W_LRU)),
            stack(ssm_s.reshape(bs, SSD_HEADS, SSD_HEAD_DIM, SSD_STATE)),
            stack(sconv_new.reshape(bs, CONV_W - 1, SSD_CONV_DIM)))
```

```python
import functools
from typing import Callable, NamedTuple

import jax
import jax.numpy as jnp
from jax import lax
from jax.experimental import pallas as pl
from jax.experimental.pallas import tpu as pltpu

F32 = jnp.float32
BF16 = jnp.bfloat16

D_MODEL = 2048
D_FF = 5632
W_LRU = 1024
W_SSD = 1024
LRU_HEADS = 16
LRU_BLOCK = 64
LRU_C = 8.0
SSD_HEADS = 16
SSD_HEAD_DIM = 64
SSD_GROUPS = 2
SSD_HPG = 8
SSD_STATE = 128
SSD_CHUNK = 128
SSD_CHUNKS_PER_STEP = 2
CONV_W = 4
SSD_CONV_DIM = W_SSD + 2 * SSD_GROUPS * SSD_STATE
IN_MAIN = 2 * W_LRU + W_SSD + SSD_CONV_DIM
N_MOD = 9
EPS = 1e-6

LANES = 128
SUBLANES = 8
VMEM_LIMIT_BYTES = 56 * 1024 * 1024

LRU_GATE_GROUP = 256
LRU_TIME_TILE = 512
SCAN_ROWS = 2 * SUBLANES

ROW_TILE = 1024
UP_COL_TILE = 512
IN_COL_TILE = IN_MAIN // 3
RESID_ROWS_FFN = 256
RESID_ROWS_MIX = 512
CAST_CHUNKS = 32
CAST_CHUNKS_OUT = 16
ADA_STREAM_TILE = 256


def _sigmoid(v):
    return 0.5 * (jnp.tanh(0.5 * v) + 1.0)


def _silu(v):
    return v * _sigmoid(v)


def _softplus(v):
    return jnp.maximum(v, 0.0) + jnp.log1p(jnp.exp(-jnp.abs(v)))


def _gelu_tanh(v):
    return 0.5 * v * (1.0 + jnp.tanh(0.7978845608028654 * (v + 0.044715 * (v * v * v))))


def _bdot(a, b):
    return jnp.dot(a, b, preferred_element_type=F32)


def _params(sem):
    return pltpu.CompilerParams(dimension_semantics=sem, vmem_limit_bytes=VMEM_LIMIT_BYTES)


def _ada_mm_kernel(s_ref, w_ref, b_ref, o_ref):
    o_ref[...] = _bdot(s_ref[...], w_ref[...].astype(BF16)) + b_ref[...]


def _ada_kernel(c_ref, w_ref, b_ref, o_ref, s_ref):
    s_ref[...] = _silu(c_ref[...]).astype(BF16)
    _ada_mm_kernel(s_ref, w_ref, b_ref, o_ref)


def _ada(c, w, b, cols, tn=1024):
    m, k = c.shape
    n = cols
    return pl.pallas_call(
        _ada_kernel,
        grid=(n // tn,),
        in_specs=[pl.BlockSpec((m, k), lambda j: (0, 0)),
                  pl.BlockSpec((k, tn), lambda j: (0, j)),
                  pl.BlockSpec((1, tn), lambda j: (0, j))],
        out_specs=[pl.BlockSpec((m, tn), lambda j: (0, j)), pl.BlockSpec((m, k), lambda j: (0, 0))],
        out_shape=[jax.ShapeDtypeStruct((m, n), F32), jax.ShapeDtypeStruct((m, k), BF16)],
        compiler_params=_params(("arbitrary",)),
        name="ada_proj",
    )(c, w, b.reshape(1, -1))


def _norm_modulate(x, gain, shift, scale):
    ms = jnp.mean(x * x, axis=-1, keepdims=True)
    y = x * lax.rsqrt(ms + EPS) * gain
    return y * (1.0 + scale) + shift


def _norm_rows_kernel(x_ref, gain_ref, sh_ref, sc_ref, xs_ref, shs_ref, scs_ref, o_ref, os_ref):
    o_ref[...] = _norm_modulate(x_ref[...], gain_ref[...], sh_ref[...],
                                sc_ref[...]).astype(o_ref.dtype)

    @pl.when(pl.program_id(0) == 0)
    def _():
        os_ref[...] = _norm_modulate(xs_ref[...], gain_ref[...], shs_ref[...],
                                     scs_ref[...]).astype(os_ref.dtype)


def _norm_rows(x, x_s, gain, mod, mod_s, shift_chunk, *, tm):
    m, d = x.shape
    ns = x_s.shape[0]
    tiles_per_group = (m // tm) // mod.shape[0]
    once = dict(pipeline_mode=pl.Buffered(1))
    mod_spec = lambda c: pl.BlockSpec((None, 1, d), lambda i: (i // tiles_per_group, 0, c))
    mod_s_spec = lambda c: pl.BlockSpec((None, ns, d), lambda i: (0, 0, c), **once)
    return pl.pallas_call(
        _norm_rows_kernel,
        grid=(m // tm,),
        in_specs=[pl.BlockSpec((tm, d), lambda i: (i, 0)), pl.BlockSpec((1, d), lambda i: (0, 0)),
                  mod_spec(shift_chunk), mod_spec(shift_chunk + 1),
                  pl.BlockSpec((ns, d), lambda i: (0, 0), **once),
                  mod_s_spec(shift_chunk), mod_s_spec(shift_chunk + 1)],
        out_specs=[pl.BlockSpec((tm, d), lambda i: (i, 0)), pl.BlockSpec((ns, d), lambda i: (0, 0))],
        out_shape=[jax.ShapeDtypeStruct((m, d), BF16), jax.ShapeDtypeStruct((ns, d), BF16)],
        compiler_params=_params(("arbitrary",)),
        name="norm_rows",
    )(x, gain.reshape(1, d), mod, mod, x_s, mod_s, mod_s)


class _Stream(NamedTuple):
    ins: list
    outs: list
    body: Callable
    start: int
    steps: int


def _stream_io(streams, n_inner):
    in_specs, args, out_specs, out_shape = [], [], [], []

    def spec(st, block, index_fn):
        return pl.BlockSpec(
            block, lambda i, j: index_fn(jnp.clip(i * n_inner + j - st.start, 0, st.steps - 1)))

    for st in streams:
        for arr, block, index_fn in st.ins:
            in_specs.append(spec(st, block, index_fn))
            args.append(arr)
        for shape, block, index_fn in st.outs:
            out_specs.append(spec(st, block, index_fn))
            out_shape.append(shape)
    return in_specs, args, out_specs, out_shape


def _run_streams(streams, in_refs, out_refs, n_inner, n_steps):
    step = pl.program_id(0) * n_inner + pl.program_id(1)
    in_refs, out_refs = iter(in_refs), iter(out_refs)
    for st in streams:
        ins = [next(in_refs) for _ in st.ins]
        outs = [next(out_refs) for _ in st.outs]
        if st.start == 0 and st.steps == n_steps:
            st.body(*ins, *outs)
            continue

        @pl.when((step >= st.start) & (step < st.start + st.steps))
        def _(st=st, ins=ins, outs=outs):
            st.body(*ins, *outs)


def _split_stream_outs(streams, flat):
    flat = list(flat)
    return [[flat.pop(0) for _ in st.outs] for st in streams]


def _cast_body(src_ref, dst_ref):
    dst_ref[...] = src_ref[...].astype(dst_ref.dtype)


def _cast_stream(w, chunks, start=0, rows=None):
    rows = w.shape[0] if rows is None else rows
    block = (rows // chunks, w.shape[1])
    index = lambda k: (k, 0)
    return _Stream([(w, block, index)],
                   [(jax.ShapeDtypeStruct((rows, w.shape[1]), BF16), block, index)],
                   _cast_body, start, chunks)


def _ada_stream(sc, w, b, col0, cols, tn, start=0):
    m, k = sc.shape
    t0 = col0 // tn
    return _Stream(
        [(sc, (m, k), lambda s: (0, 0)), (w, (k, tn), lambda s: (0, s + t0)),
         (b.reshape(1, -1), (1, tn), lambda s: (0, s + t0))],
        [(jax.ShapeDtypeStruct((m, cols), F32), (m, tn), lambda s: (0, s))],
        _ada_mm_kernel, start, cols // tn)


def _wdot(h, w, trans_w):
    if trans_w:
        return lax.dot_general(h, w, (((1,), (1,)), ((), ())), preferred_element_type=F32)
    return _bdot(h, w)


def _proj_kernel(*refs, n_w, swiglu, trans_w, has_extra, has_side, emit_bf16, streams, n_prev,
                 nj, n_steps):
    it = iter(refs)
    x_ref = next(it)
    xs_ref = next(it) if has_side else None
    w_refs = [next(it) for _ in range(n_w)]
    wx_ref = next(it) if has_extra else None
    stream_ins = [next(it) for st in streams for _ in st.ins]
    for _ in range(n_prev):
        next(it)
    o_ref = next(it)
    ox_ref = next(it) if has_extra else None
    os_ref = next(it) if has_side else None
    osx_ref = next(it) if has_side and has_extra else None
    wo_refs = [next(it) for _ in range(n_w)] if emit_bf16 else []
    stream_outs = [next(it) for st in streams for _ in st.outs]

    j = pl.program_id(1)
    first_tile = pl.program_id(0) == 0

    if has_extra:
        @pl.when(j == 0)
        def _():
            ox_ref[...] = _wdot(x_ref[...], wx_ref[...].astype(BF16), trans_w)

        if has_side:
            @pl.when((j == 0) & first_tile)
            def _():
                osx_ref[...] = _wdot(xs_ref[...], wx_ref[...].astype(BF16), trans_w)

    wbs = [w_ref[...].astype(BF16) for w_ref in w_refs]
    for wo_ref, wb in zip(wo_refs, wbs):
        wo_ref[...] = wb

    def project(h, out_ref):
        if swiglu:
            g = _wdot(h, wbs[0], trans_w)
            u = _wdot(h, wbs[1], trans_w)
            out_ref[...] = (_silu(g) * u).astype(out_ref.dtype)
        else:
            out_ref[...] = _wdot(h, wbs[0], trans_w).astype(out_ref.dtype)

    project(x_ref[...], o_ref)
    if has_side:
        @pl.when(first_tile)
        def _():
            project(xs_ref[...], os_ref)

    _run_streams(streams, stream_ins, stream_outs, nj, n_steps)


def _proj(x, ws, *, n_out, tm, tn, swiglu, out_dtype, row_tiles, side=None, trans_w=False,
          w_extra=None, emit_bf16=False, streams=(), prev=None):
    m, d = x.shape
    t0, t1 = row_tiles
    nj = n_out // tn
    has_side = side is not None
    has_extra = w_extra is not None
    single_row_tile = t1 - t0 == 1
    once = dict(pipeline_mode=pl.Buffered(1))

    def w_spec(off):
        if trans_w:
            return pl.BlockSpec((tn, d), lambda i, j: (j + off, 0))
        return pl.BlockSpec((d, tn), lambda i, j: (0, j + off))

    x_mode = once if single_row_tile else {}
    in_specs = [pl.BlockSpec((tm, d), lambda i, j: (i + t0, 0), **x_mode)]
    args = [x]
    if has_side:
        ns = side.shape[0]
        in_specs.append(pl.BlockSpec((ns, d), lambda i, j: (0, 0), **once))
        args.append(side)
    in_specs += [w_spec(off) for _, off in ws]
    args += [w for w, _ in ws]
    if has_extra:
        nx = w_extra.shape[0] if trans_w else w_extra.shape[1]
        in_specs.append(pl.BlockSpec(w_extra.shape, lambda i, j: (0, 0)))
        args.append(w_extra)
    assert all(st.start + st.steps <= (t1 - t0) * nj for st in streams)
    st_in_specs, st_args, st_out_specs, st_out_shape = _stream_io(streams, nj)
    in_specs += st_in_specs
    args += st_args
    prev = list(prev or [])
    aliases = {}
    for k, buf in enumerate(prev):
        aliases[len(args)] = k
        in_specs.append(pl.BlockSpec(memory_space=pl.ANY))
        args.append(buf)

    out_specs = [pl.BlockSpec((tm, tn), lambda i, j: (i + t0, j))]
    out_shape = [jax.ShapeDtypeStruct((m, n_out), out_dtype)]
    if has_extra:
        out_specs.append(pl.BlockSpec((tm, nx), lambda i, j: (i + t0, 0)))
        out_shape.append(jax.ShapeDtypeStruct((m, nx), F32))
    n_main = len(out_shape)
    if has_side:
        out_specs.append(pl.BlockSpec((ns, tn), lambda i, j: (0, jnp.where(i == 0, j, nj - 1))))
        out_shape.append(jax.ShapeDtypeStruct((ns, n_out), out_dtype))
        if has_extra:
            out_specs.append(pl.BlockSpec((ns, nx), lambda i, j: (0, 0)))
            out_shape.append(jax.ShapeDtypeStruct((ns, nx), F32))
    n_side = len(out_shape) - n_main
    if emit_bf16:
        assert single_row_tile, "weight copies are written once per column tile"
        for _ in ws:
            out_specs.append(w_spec(0))
            out_shape.append(jax.ShapeDtypeStruct((n_out, d) if trans_w else (d, n_out), BF16))
    out_specs += st_out_specs
    out_shape += st_out_shape
    outs = pl.pallas_call(
        functools.partial(_proj_kernel, n_w=len(ws), swiglu=swiglu, trans_w=trans_w,
                          has_extra=has_extra, has_side=has_side, emit_bf16=emit_bf16,
                          streams=tuple(streams), n_prev=len(prev), nj=nj,
                          n_steps=(t1 - t0) * nj),
        grid=(t1 - t0, nj),
        in_specs=in_specs,
        out_specs=out_specs,
        out_shape=out_shape,
        input_output_aliases=aliases,
        compiler_params=_params(("arbitrary" if streams or has_side else "parallel", "arbitrary")),
        name="proj_swiglu" if swiglu else "proj",
    )(*args)
    n_wb = len(ws) if emit_bf16 else 0
    main, rest = outs[:n_main], outs[n_main:]
    side_outs, rest = rest[:n_side], rest[n_side:]
    wb, rest = rest[:n_wb], rest[n_wb:]
    return main, side_outs, wb, _split_stream_outs(streams, rest)


def _up_kernel(*refs, streams, n_rows, n_steps):
    n_in = sum(len(st.ins) for st in streams)
    n_out = sum(len(st.outs) for st in streams)
    x_ref, xs_ref, wg_ref, wu_ref = refs[:4]
    stream_ins = refs[4:4 + n_in]
    o_ref, os_ref = refs[4 + n_in:6 + n_in]
    stream_outs = refs[6 + n_in:6 + n_in + n_out]
    wg_scr, wu_scr = refs[6 + n_in + n_out:]

    def swiglu(h, out_ref):
        g = _bdot(h, wg_scr[...])
        u = _bdot(h, wu_scr[...])
        out_ref[...] = (_silu(g) * u).astype(out_ref.dtype)

    @pl.when(pl.program_id(1) == 0)
    def _():
        wg_scr[...] = wg_ref[...].astype(BF16)
        wu_scr[...] = wu_ref[...].astype(BF16)
        swiglu(xs_ref[...], os_ref)

    swiglu(x_ref[...], o_ref)
    _run_streams(streams, stream_ins, stream_outs, n_rows, n_steps)


def _up_proj(x, x_s, w_up, *, tm, tn, streams=()):
    m, d = x.shape
    ns = x_s.shape[0]
    f = w_up.shape[1] // 2
    nj, n_rows = f // tn, m // tm
    assert all(st.start + st.steps <= nj * n_rows for st in streams)
    st_in_specs, st_args, st_out_specs, st_out_shape = _stream_io(streams, n_rows)
    w_spec = lambda off: pl.BlockSpec((d, tn), lambda j, i: (0, j + off))
    out, out_s, *rest = pl.pallas_call(
        functools.partial(_up_kernel, streams=tuple(streams), n_rows=n_rows,
                          n_steps=nj * n_rows),
        grid=(nj, n_rows),
        in_specs=[pl.BlockSpec((tm, d), lambda j, i: (i, 0)),
                  pl.BlockSpec((ns, d), lambda j, i: (0, 0), pipeline_mode=pl.Buffered(1)),
                  w_spec(0), w_spec(nj)] + st_in_specs,
        out_specs=[pl.BlockSpec((tm, tn), lambda j, i: (i, j)),
                   pl.BlockSpec((ns, tn), lambda j, i: (0, j))] + st_out_specs,
        out_shape=[jax.ShapeDtypeStruct((m, f), BF16),
                   jax.ShapeDtypeStruct((ns, f), BF16)] + st_out_shape,
        scratch_shapes=[pltpu.VMEM((d, tn), BF16), pltpu.VMEM((d, tn), BF16)],
        compiler_params=_params(("arbitrary", "arbitrary")),
        name="up_proj",
    )(x, x_s, w_up, w_up, *st_args)
    return out, out_s, _split_stream_outs(streams, rest)


def _resid_kernel(*refs, n_lhs, factor, emit_x):
    it = iter(refs)
    lhs_refs = [next(it) for _ in range(n_lhs)]
    lhs_s_refs = [next(it) for _ in range(n_lhs)]
    w_refs = [next(it) for _ in range(n_lhs)]
    x_ref, gate_ref, gain_ref, sh_ref, sc_ref = (next(it) for _ in range(5))
    xs_ref, gate_s_ref, sh_s_ref, sc_s_ref = (next(it) for _ in range(4))
    n_out = 2 if emit_x else 1
    outs = [next(it) for _ in range(n_out)]
    outs_s = [next(it) for _ in range(n_out)]

    def update(lhs, x_in, gate, sh, sc, out_refs):
        acc = _bdot(lhs[0][...], w_refs[0][...])
        for l_ref, w_ref in zip(lhs[1:], w_refs[1:]):
            acc = acc + _bdot(l_ref[...], w_ref[...])
        x_new = x_in[...] + (factor * gate[...]) * acc
        if emit_x:
            out_refs[0][...] = x_new
        h_ref = out_refs[-1]
        h_ref[...] = _norm_modulate(x_new, gain_ref[...], sh[...], sc[...]).astype(h_ref.dtype)

    update(lhs_refs, x_ref, gate_ref, sh_ref, sc_ref, outs)

    @pl.when(pl.program_id(0) == 0)
    def _():
        update(lhs_s_refs, xs_ref, gate_s_ref, sh_s_ref, sc_s_ref, outs_s)


def _resid(lhs_list, lhs_s_list, ws, x, x_s, mod, mod_s, gate_chunk, gain_next, mod_next,
           mod_next_s, shift_chunk_next, *, factor, tm, emit_x, h_dtype):
    m, d = x.shape
    ns = x_s.shape[0]
    groups = mod.shape[0]
    tiles_per_group = (m // tm) // groups
    kp = lhs_list[0].shape[1]
    once = dict(pipeline_mode=pl.Buffered(1))

    def mod_spec(chunk):
        return pl.BlockSpec((None, 1, d), lambda i: (i // tiles_per_group, 0, chunk))

    def mod_s_spec(chunk):
        return pl.BlockSpec((None, ns, d), lambda i: (0, 0, chunk), **once)

    in_specs = [pl.BlockSpec((tm, kp), lambda i: (i, 0)) for _ in lhs_list]
    in_specs += [pl.BlockSpec((ns, kp), lambda i: (0, 0), **once) for _ in lhs_s_list]
    in_specs += [pl.BlockSpec((kp, d), lambda i, k=k: (k, 0), **once) for _, k in ws]
    in_specs += [pl.BlockSpec((tm, d), lambda i: (i, 0)), mod_spec(gate_chunk),
                 pl.BlockSpec((1, d), lambda i: (0, 0)),
                 mod_spec(shift_chunk_next), mod_spec(shift_chunk_next + 1),
                 pl.BlockSpec((ns, d), lambda i: (0, 0), **once), mod_s_spec(gate_chunk),
                 mod_s_spec(shift_chunk_next), mod_s_spec(shift_chunk_next + 1)]
    row = pl.BlockSpec((tm, d), lambda i: (i, 0))
    row_s = pl.BlockSpec((ns, d), lambda i: (0, 0))
    dtypes = ([F32] if emit_x else []) + [h_dtype]
    out_specs = [row for _ in dtypes] + [row_s for _ in dtypes]
    out_shape = ([jax.ShapeDtypeStruct((m, d), t) for t in dtypes]
                 + [jax.ShapeDtypeStruct((ns, d), t) for t in dtypes])
    outs = pl.pallas_call(
        functools.partial(_resid_kernel, n_lhs=len(lhs_list), factor=factor, emit_x=emit_x),
        grid=(m // tm,),
        in_specs=in_specs,
        out_specs=out_specs,
        out_shape=out_shape,
        compiler_params=_params(("arbitrary",)),
        name="resid",
    )(*lhs_list, *lhs_s_list, *[w for w, _ in ws], x, mod, gain_next.reshape(1, d), mod_next,
      mod_next, x_s, mod_s, mod_next_s, mod_next_s)
    return outs[:len(dtypes)], outs[len(dtypes):]


def _lru_gates(xc, wg_ref, ba, bi, sp):
    a_parts, b_parts = [], []
    for g in range(W_LRU // LRU_GATE_GROUP):
        cols = slice(g * LRU_GATE_GROUP, (g + 1) * LRU_GATE_GROUP)
        xg = xc[:, cols]
        ri = _bdot(xg.astype(BF16), wg_ref[g].astype(BF16))
        r = _sigmoid(ri[:, :LRU_GATE_GROUP] + ba[:, cols])
        i = _sigmoid(ri[:, LRU_GATE_GROUP:] + bi[:, cols])
        log_a = (-LRU_C * r) * sp[:, cols]
        a = jnp.exp(log_a)
        a_parts.append(a)
        v = 1.0 - a * a
        root = jnp.where(v > 0.0, v * lax.rsqrt(v), 0.0)
        b_parts.append(root * (i * xg))
    return jnp.concatenate(a_parts, axis=1), jnp.concatenate(b_parts, axis=1)


def _causal_conv(prev8, x, w_ref, b_ref):
    rows, width = x.shape
    rid = lax.broadcasted_iota(jnp.int32, (SUBLANES, width), 0)
    shifts = (1, 2, 3)
    taps = [w_ref[k:k + 1, :] for k in range(CONV_W)]
    bias = b_ref[...]
    prev_rot = [pltpu.roll(prev8, k, 0) for k in shifts]
    out = []
    for r in range(rows // SUBLANES):
        cur = x[r * SUBLANES:(r + 1) * SUBLANES, :]
        cur_rot = [pltpu.roll(cur, k, 0) for k in shifts]
        s1, s2, s3 = [jnp.where(rid < k, p, c) for k, p, c in zip(shifts, prev_rot, cur_rot)]
        out.append(bias + taps[0] * s3 + taps[1] * s2 + taps[2] * s1 + taps[3] * cur)
        prev_rot = cur_rot
    return jnp.concatenate(out, axis=0)


def _lru_prompt_kernel(*refs, streams, nt, n_steps):
    n_in = sum(len(st.ins) for st in streams)
    n_out = sum(len(st.outs) for st in streams)
    xl_ref, gl_ref, cw_ref, cb_ref, wg_ref, ba_ref, bi_ref, lam_ref = refs[:8]
    stream_ins = refs[8:8 + n_in]
    o_ref, hT_ref = refs[8 + n_in:10 + n_in]
    stream_outs = refs[10 + n_in:10 + n_in + n_out]
    xbuf, a_scr, b_scr, hcar = refs[10 + n_in + n_out:]
    t = pl.program_id(1)
    tt = xl_ref.shape[0]

    @pl.when(t == 0)
    def _():
        xbuf[...] = jnp.zeros_like(xbuf)
        hcar[...] = jnp.zeros_like(hcar)

    _run_streams(streams, stream_ins, stream_outs, nt, n_steps)

    x = xl_ref[...]
    xc = _causal_conv(xbuf[...], x, cw_ref, cb_ref)
    xbuf[...] = x[tt - SUBLANES:, :]

    sp = _softplus(-lam_ref[...])
    a, bt = _lru_gates(xc, wg_ref, ba_ref[...], bi_ref[...], sp)
    a_scr[...] = a
    b_scr[...] = bt

    rid = lax.broadcasted_iota(jnp.int32, (SUBLANES, W_LRU), 0)

    def scan8(a8, b8, h_in):
        for s in (1, 2, 4):
            a_sh = pltpu.roll(a8, s, 0)
            b_sh = pltpu.roll(b8, s, 0)
            m = rid >= s
            b8 = jnp.where(m, a8 * b_sh + b8, b8)
            a8 = jnp.where(m, a8 * a_sh, a8)
        h8 = a8 * h_in + b8
        return h8, jnp.broadcast_to(h8[SUBLANES - 1:SUBLANES, :], (SUBLANES, W_LRU))

    def body(g, h_in):
        r0 = pl.multiple_of(g * SCAN_ROWS, SCAN_ROWS)
        lo = pl.ds(r0, SUBLANES)
        hi = pl.ds(r0 + SUBLANES, SUBLANES)
        h_lo, h_mid = scan8(a_scr[lo, :], b_scr[lo, :], h_in)
        h_hi, h_out = scan8(a_scr[hi, :], b_scr[hi, :], h_mid)
        rows = pl.ds(r0, SCAN_ROWS)
        h16 = jnp.concatenate([h_lo, h_hi], axis=0)
        o_ref[rows, :] = (h16 * _gelu_tanh(gl_ref[rows, :])).astype(o_ref.dtype)
        return h_out

    h_last = lax.fori_loop(0, tt // SCAN_ROWS, body, hcar[...])
    hcar[...] = h_last

    @pl.when(t == pl.num_programs(1) - 1)
    def _():
        hT_ref[...] = h_last[0:1, :]


def _lru_prompt(proj, batch, seq, cw, cb, wg, ba, bi, lam, streams=()):
    tt = LRU_TIME_TILE
    nt = seq // tt
    assert all(st.start + st.steps <= batch * nt for st in streams)
    st_in_specs, st_args, st_out_specs, st_out_shape = _stream_io(streams, nt)
    row = lambda v: v.reshape(1, W_LRU)
    full = lambda shape: pl.BlockSpec(shape, lambda b, t: (0,) * len(shape))
    out, h_t, *rest = pl.pallas_call(
        functools.partial(_lru_prompt_kernel, streams=tuple(streams), nt=nt, n_steps=batch * nt),
        grid=(batch, nt),
        in_specs=[pl.BlockSpec((tt, W_LRU), lambda b, t: (b * nt + t, 0)),
                  pl.BlockSpec((tt, W_LRU), lambda b, t: (b * nt + t, 1)),
                  full((CONV_W, W_LRU)), full((1, W_LRU)), full(wg.shape),
                  full((1, W_LRU)), full((1, W_LRU)), full((1, W_LRU))] + st_in_specs,
        out_specs=[pl.BlockSpec((tt, W_LRU), lambda b, t: (b * nt + t, 0)),
                   pl.BlockSpec((None, 1, W_LRU), lambda b, t: (b, 0, 0))] + st_out_specs,
        out_shape=[jax.ShapeDtypeStruct((batch * seq, W_LRU), BF16),
                   jax.ShapeDtypeStruct((batch, 1, W_LRU), F32)] + st_out_shape,
        scratch_shapes=[pltpu.VMEM((SUBLANES, W_LRU), F32),
                        pltpu.VMEM((tt, W_LRU), F32),
                        pltpu.VMEM((tt, W_LRU), F32),
                        pltpu.VMEM((SUBLANES, W_LRU), F32)],
        compiler_params=_params(("arbitrary" if streams else "parallel", "arbitrary")),
        name="lru_prompt",
    )(proj, proj, cw, row(cb), wg, row(ba), row(bi), row(lam), *st_args)
    return out, h_t.reshape(batch, W_LRU), _split_stream_outs(streams, rest)


def _ssd_chunk(z_ref, xbc_ref, dt_ref, cw_ref, cb_ref, dtb_ref, alog_ref, dexp_ref, ng_ref, y_ref,
               xbuf, st_scr, y_scr, m_scr, xbd_scr):
    lc = SSD_CHUNK
    x = xbc_ref[...]
    act = _silu(_causal_conv(xbuf[...], x, cw_ref, cb_ref))
    xbuf[...] = x[lc - SUBLANES:, :]
    xs = act[:, :W_SSD]
    bm = act[:, W_SSD:W_SSD + SSD_GROUPS * SSD_STATE]
    cm = act[:, W_SSD + SSD_GROUPS * SSD_STATE:]

    dt = _softplus(dt_ref[...] + dtb_ref[...])
    d_a = dt * (-jnp.exp(alog_ref[...]))
    row_i = lax.broadcasted_iota(jnp.int32, (lc, lc), 0)
    col_i = lax.broadcasted_iota(jnp.int32, (lc, lc), 1)
    causal = row_i >= col_i
    tril = jnp.where(causal, 1.0, 0.0).astype(F32)
    cs = jnp.dot(tril, d_a, preferred_element_type=F32, precision=lax.Precision.HIGHEST)
    cs_t = cs.T
    dt_t = dt.T
    cs_last = cs[lc - 1:lc, :]

    def per_head_lanes(v):
        rows = v.shape[0]
        return jnp.concatenate(
            [jnp.broadcast_to(v[:, h:h + 1], (rows, SSD_HEAD_DIM)) for h in range(SSD_HEADS)],
            axis=1)

    w_exp = per_head_lanes(jnp.exp(cs_last - cs) * dt)
    ecs_exp = per_head_lanes(jnp.exp(cs))
    cd_exp = per_head_lanes(jnp.exp(cs_last))
    gw = SSD_HPG * SSD_HEAD_DIM
    low_half = col_i < SSD_HEAD_DIM

    for g in range(SSD_GROUPS):
        ncols = slice(g * SSD_STATE, (g + 1) * SSD_STATE)
        gcols = slice(g * gw, (g + 1) * gw)
        b_g = bm[:, ncols].astype(BF16)
        c_g = cm[:, ncols].astype(BF16)
        cb_mat = lax.dot_general(c_g, b_g, (((1,), (1,)), ((), ())),
                                 preferred_element_type=F32)
        for e in range(SSD_HPG):
            h = g * SSD_HPG + e
            cs_col = jnp.broadcast_to(cs[:, h:h + 1], (lc, lc))
            l_mat = jnp.exp(jnp.where(causal, cs_col - cs_t[h:h + 1, :], -jnp.inf))
            m_scr[g, :, e * lc:(e + 1) * lc] = (cb_mat * l_mat * dt_t[h:h + 1, :]).astype(BF16)
        for q in range(SSD_HPG // 2):
            lanes = slice(q * LANES, (q + 1) * LANES)
            slab = xs[:, g * gw + q * LANES:g * gw + (q + 1) * LANES]
            xbd_scr[g, (2 * q) * lc:(2 * q + 1) * lc, lanes] = jnp.where(
                low_half, slab, 0.0).astype(BF16)
            xbd_scr[g, (2 * q + 1) * lc:(2 * q + 2) * lc, lanes] = jnp.where(
                low_half, 0.0, slab).astype(BF16)
        st_g = st_scr[:, gcols]
        y_off = _bdot(c_g, st_g.astype(BF16)) * ecs_exp[:, gcols]
        y_scr[:, gcols] = (_bdot(m_scr[g], xbd_scr[g]) + y_off
                           + dexp_ref[:, gcols] * xs[:, gcols])
        xw = (xs[:, gcols] * w_exp[:, gcols]).astype(BF16)
        st_scr[:, gcols] = cd_exp[:, gcols] * st_g + lax.dot_general(
            b_g, xw, (((0,), (0,)), ((), ())), preferred_element_type=F32)

    yg = y_scr[...] * _silu(z_ref[...])
    ms = jnp.mean(yg * yg, axis=-1, keepdims=True)
    y_ref[...] = (yg * lax.rsqrt(ms + EPS) * ng_ref[...]).astype(y_ref.dtype)


def _ssd_prompt_kernel(*refs, streams, nc, n_steps):
    n_in = sum(len(st.ins) for st in streams)
    n_out = sum(len(st.outs) for st in streams)
    z_ref, xbc_ref, dt_ref, cw_ref, cb_ref, dtb_ref, alog_ref, dexp_ref, ng_ref = refs[:9]
    stream_ins = refs[9:9 + n_in]
    y_ref, st_ref = refs[9 + n_in:11 + n_in]
    stream_outs = refs[11 + n_in:11 + n_in + n_out]
    xbuf, st_scr, y_scr, m_scr, xbd_scr = refs[11 + n_in + n_out:]
    c = pl.program_id(1)

    @pl.when(c == 0)
    def _():
        xbuf[...] = jnp.zeros_like(xbuf)
        st_scr[...] = jnp.zeros_like(st_scr)
        xbd_scr[...] = jnp.zeros_like(xbd_scr)

    _run_streams(streams, stream_ins, stream_outs, nc, n_steps)

    for cc in range(SSD_CHUNKS_PER_STEP):
        rows = pl.ds(cc * SSD_CHUNK, SSD_CHUNK)
        _ssd_chunk(z_ref.at[rows], xbc_ref.at[rows], dt_ref.at[rows], cw_ref, cb_ref, dtb_ref,
                   alog_ref, dexp_ref, ng_ref, y_ref.at[rows], xbuf, st_scr, y_scr.at[cc],
                   m_scr.at[cc], xbd_scr.at[cc])

    @pl.when(c == pl.num_programs(1) - 1)
    def _():
        st_ref[...] = st_scr[...].T


def _ssd_prompt(proj, dt_raw, batch, seq, cw, cb, dtb, alog, dexp, ng, streams=()):
    lc = SSD_CHUNK
    cps = SSD_CHUNKS_PER_STEP
    tl = cps * lc
    nc = seq // tl
    assert all(st.start + st.steps <= batch * nc for st in streams)
    st_in_specs, st_args, st_out_specs, st_out_shape = _stream_io(streams, nc)
    full = lambda shape: pl.BlockSpec(shape, lambda b, c: (0,) * len(shape))
    z_blk = (2 * W_LRU) // W_SSD
    xbc_blk = (2 * W_LRU + W_SSD) // SSD_CONV_DIM
    y, st, *rest = pl.pallas_call(
        functools.partial(_ssd_prompt_kernel, streams=tuple(streams), nc=nc, n_steps=batch * nc),
        grid=(batch, nc),
        in_specs=[pl.BlockSpec((tl, W_SSD), lambda b, c: (b * nc + c, z_blk)),
                  pl.BlockSpec((tl, SSD_CONV_DIM), lambda b, c: (b * nc + c, xbc_blk)),
                  pl.BlockSpec((tl, LANES), lambda b, c: (b * nc + c, 0)),
                  full((CONV_W, SSD_CONV_DIM)), full((1, SSD_CONV_DIM)),
                  full((1, LANES)), full((1, LANES)), full((1, W_SSD)), full((1, W_SSD))]
        + st_in_specs,
        out_specs=[pl.BlockSpec((tl, W_SSD), lambda b, c: (b * nc + c, 0)),
                   pl.BlockSpec((None, W_SSD, SSD_STATE), lambda b, c: (b, 0, 0))] + st_out_specs,
        out_shape=[jax.ShapeDtypeStruct((batch * seq, W_SSD), BF16),
                   jax.ShapeDtypeStruct((batch, W_SSD, SSD_STATE), F32)] + st_out_shape,
        scratch_shapes=[pltpu.VMEM((SUBLANES, SSD_CONV_DIM), F32),
                        pltpu.VMEM((SSD_STATE, W_SSD), F32),
                        pltpu.VMEM((cps, lc, W_SSD), F32),
                        pltpu.VMEM((cps, SSD_GROUPS, lc, SSD_HPG * lc), BF16),
                        pltpu.VMEM((cps, SSD_GROUPS, SSD_HPG * lc, SSD_HPG * SSD_HEAD_DIM), BF16)],
        compiler_params=_params(("arbitrary" if streams else "parallel", "arbitrary")),
        name="ssd_prompt",
    )(proj, proj, dt_raw, cw, cb, dtb, alog, dexp, ng, *st_args)
    return (y, st.reshape(batch, SSD_HEADS, SSD_HEAD_DIM, SSD_STATE),
            _split_stream_outs(streams, rest))


def _sample_pre_kernel(proj_ref, dt_ref, h0_ref, lconv_ref, sconv_ref,
                       lcw_ref, lcb_ref, wg_ref, ba_ref, bi_ref, lam_ref,
                       scw_ref, scb_ref, dtb_ref, alog_ref,
                       outl_ref, hnew_ref, lconv_new_ref, sconv_new_ref,
                       xs_ref, xdt_ref, bc_ref, dec_ref):
    nb = proj_ref.shape[0]
    xl = proj_ref[:, 0:W_LRU]
    gl = proj_ref[:, W_LRU:2 * W_LRU]
    xbc = proj_ref[:, 2 * W_LRU + W_SSD:IN_MAIN]

    def conv1(state_ref, width, x_new, w_ref, b_ref):
        y = b_ref[...] + w_ref[0:1, :] * state_ref[:, 0:width]
        y = y + w_ref[1:2, :] * state_ref[:, width:2 * width]
        y = y + w_ref[2:3, :] * state_ref[:, 2 * width:3 * width]
        return y + w_ref[3:4, :] * x_new

    xc = conv1(lconv_ref, W_LRU, xl, lcw_ref, lcb_ref)
    a, bt = _lru_gates(xc, wg_ref, ba_ref[...], bi_ref[...], _softplus(-lam_ref[...]))
    h_new = a * h0_ref[...] + bt
    hnew_ref[...] = h_new
    outl_ref[...] = (h_new * _gelu_tanh(gl)).astype(outl_ref.dtype)
    lconv_new_ref[:, 0:2 * W_LRU] = lconv_ref[:, W_LRU:3 * W_LRU]
    lconv_new_ref[:, 2 * W_LRU:3 * W_LRU] = xl

    act = _silu(conv1(sconv_ref, SSD_CONV_DIM, xbc, scw_ref, scb_ref))
    sconv_new_ref[:, 0:2 * SSD_CONV_DIM] = sconv_ref[:, SSD_CONV_DIM:3 * SSD_CONV_DIM]
    sconv_new_ref[:, 2 * SSD_CONV_DIM:3 * SSD_CONV_DIM] = xbc
    xs = act[:, :W_SSD]
    xs_ref[...] = xs
    bc_ref[...] = act[:, W_SSD:]
    dt = _softplus(dt_ref[...] + dtb_ref[...])
    dec = jnp.exp(dt * (-jnp.exp(alog_ref[...])))
    for h in range(SSD_HEADS):
        pcols = slice(h * SSD_HEAD_DIM, (h + 1) * SSD_HEAD_DIM)
        xdt_ref[:, pcols] = xs[:, pcols] * jnp.broadcast_to(dt[:, h:h + 1], (nb, SSD_HEAD_DIM))
        dec_ref[h] = jnp.broadcast_to(dec[:, h:h + 1], (nb, SSD_STATE))


def _sample_pre(proj, dt_raw, h0, lconv, sconv, p):
    nb = proj.shape[0]
    out_shape = [jax.ShapeDtypeStruct((nb, W_LRU), BF16),
                 jax.ShapeDtypeStruct((nb, W_LRU), F32),
                 jax.ShapeDtypeStruct((nb, 3 * W_LRU), F32),
                 jax.ShapeDtypeStruct((nb, 3 * SSD_CONV_DIM), F32),
                 jax.ShapeDtypeStruct((nb, W_SSD), F32),
                 jax.ShapeDtypeStruct((nb, W_SSD), F32),
                 jax.ShapeDtypeStruct((nb, 2 * SSD_GROUPS * SSD_STATE), F32),
                 jax.ShapeDtypeStruct((SSD_HEADS, nb, SSD_STATE), F32)]
    return pl.pallas_call(
        _sample_pre_kernel,
        out_shape=out_shape,
        compiler_params=pltpu.CompilerParams(vmem_limit_bytes=VMEM_LIMIT_BYTES),
        name="sample_pre",
    )(proj, dt_raw, h0, lconv, sconv,
      p["lru_cw"], p["lru_cb"], p["lru_wg"], p["lru_ba"], p["lru_bi"], p["lru_lam"],
      p["ssd_cw"], p["ssd_cb"], p["ssd_dtb"], p["ssd_alog"])


def _sample_state_kernel(s_ref, xdt_ref, bc_ref, dec_ref, o_ref, y_ref):
    bb = s_ref.shape[0]
    half = SSD_HPG * SSD_HEAD_DIM
    rid = lax.broadcasted_iota(jnp.int32, (bb, W_SSD), 0)
    xdt = xdt_ref[...]
    bcb = bc_ref[...].astype(BF16)
    for k in range(bb):
        xk = jnp.where(rid == k, xdt, 0.0).astype(BF16)
        for g in range(SSD_GROUPS):
            rows = slice(g * half, (g + 1) * half)
            b_g = bcb[:, g * SSD_STATE:(g + 1) * SSD_STATE]
            c_g = bcb[:, (SSD_GROUPS + g) * SSD_STATE:(SSD_GROUPS + g + 1) * SSD_STATE]
            outer = lax.dot_general(xk[:, rows], b_g, (((0,), (0,)), ((), ())),
                                    preferred_element_type=F32)
            dec = jnp.concatenate(
                [jnp.broadcast_to(dec_ref[g * SSD_HPG + e, k:k + 1, :], (SSD_HEAD_DIM, SSD_STATE))
                 for e in range(SSD_HPG)], axis=0)
            s_new = dec * s_ref[k, rows, :] + outer
            o_ref[k, rows, :] = s_new
            yk = lax.dot_general(c_g, s_new.astype(BF16), (((1,), (1,)), ((), ())),
                                 preferred_element_type=F32)
            y_ref[k:k + 1, rows] = yk[k:k + 1, :]


def _state_stream(ssm, xdt, bc, dec, bb=8, start=0):
    nb = ssm.shape[0]
    state_block = (bb, W_SSD, SSD_STATE)
    return _Stream(
        [(ssm, state_block, lambda k: (k, 0, 0)),
         (xdt, (bb, W_SSD), lambda k: (k, 0)),
         (bc, (bb, 2 * SSD_GROUPS * SSD_STATE), lambda k: (k, 0)),
         (dec, (SSD_HEADS, bb, SSD_STATE), lambda k: (0, k, 0))],
        [(jax.ShapeDtypeStruct(ssm.shape, F32), state_block, lambda k: (k, 0, 0)),
         (jax.ShapeDtypeStruct((nb, W_SSD), F32), (bb, W_SSD), lambda k: (k, 0))],
        _sample_state_kernel, start, nb // bb)


def _sample_post_kernel(y_ref, xs_ref, proj_ref, dexp_ref, ng_ref, o_ref):
    z = proj_ref[:, 2 * W_LRU:2 * W_LRU + W_SSD]
    yg = (y_ref[...] + dexp_ref[...] * xs_ref[...]) * _silu(z)
    ms = jnp.mean(yg * yg, axis=-1, keepdims=True)
    o_ref[...] = (yg * lax.rsqrt(ms + EPS) * ng_ref[...]).astype(o_ref.dtype)


def _sample_post(y_raw, xs, proj, dexp, ng):
    return pl.pallas_call(
        _sample_post_kernel,
        out_shape=jax.ShapeDtypeStruct(y_raw.shape, BF16),
        compiler_params=pltpu.CompilerParams(vmem_limit_bytes=VMEM_LIMIT_BYTES),
        name="sample_post",
    )(y_raw, xs, proj, dexp, ng)


def _block_diag_groups(w):
    per = LRU_GATE_GROUP // LRU_BLOCK
    w4 = w.reshape(LRU_HEADS // per, per, LRU_BLOCK, LRU_BLOCK)
    bd = jnp.einsum("ghij,hk->ghikj", w4, jnp.eye(per, dtype=w.dtype))
    return bd.reshape(LRU_HEADS // per, LRU_GATE_GROUP, LRU_GATE_GROUP)


def _pad_lanes(v):
    v = v.reshape(1, -1)
    return jnp.pad(v, ((0, 0), (0, LANES - v.shape[1])))


def kernel(x_prompt, x_sample, c_prompt, c_sample, state_lru_h, state_lru_conv, state_ssm, state_ssd_conv, w_ada, b_ada, g_ffn1, w_up1, w_down1, g_mix, w_in, lru_conv_w, lru_conv_b, lru_wa, lru_ba, lru_wi, lru_bi, lru_lambda, ssd_conv_w, ssd_conv_b, ssd_dt_bias, ssd_A_log, ssd_D, ssd_norm_g, w_out, g_ffn2, w_up2, w_down2, w_ada_f, b_ada_f, g_final):
    bp, seq, d = x_prompt.shape
    bs = x_sample.shape[0]
    depth = w_ada.shape[0]
    assert depth == 1 and x_sample.shape[1] == 1 and d == D_MODEL

    pad_rows = (-(bs + bp)) % (2 * SUBLANES)
    c_rows = bs + bp + pad_rows
    c_all = jnp.concatenate([c_sample, c_prompt, jnp.zeros((pad_rows, d), F32)], axis=0)

    def split_rows(mod_all):
        width = mod_all.shape[1]
        return mod_all[bs:bs + bp].reshape(bp, 1, width), mod_all.reshape(1, c_rows, width)

    w_in_t = jnp.swapaxes(w_in[0], 0, 1)
    w_dt_t = jnp.pad(w_in_t[IN_MAIN:], ((0, LANES - SSD_HEADS), (0, 0)))
    p = {
        "lru_cw": lru_conv_w[0], "lru_cb": lru_conv_b[0].reshape(1, W_LRU),
        "lru_wg": jnp.concatenate([_block_diag_groups(lru_wa[0]), _block_diag_groups(lru_wi[0])],
                                  axis=-1),
        "lru_ba": lru_ba[0].reshape(1, W_LRU), "lru_bi": lru_bi[0].reshape(1, W_LRU),
        "lru_lam": lru_lambda[0].reshape(1, W_LRU),
        "ssd_cw": ssd_conv_w[0], "ssd_cb": ssd_conv_b[0].reshape(1, SSD_CONV_DIM),
        "ssd_dtb": _pad_lanes(ssd_dt_bias[0]), "ssd_alog": _pad_lanes(ssd_A_log[0]),
        "ssd_dexp": jnp.repeat(ssd_D[0], SSD_HEAD_DIM).reshape(1, W_SSD),
        "ssd_ng": ssd_norm_g[0].reshape(1, W_SSD),
    }

    xp = x_prompt.reshape(bp * seq, d)
    xs = x_sample.reshape(bs, d)
    tm = ROW_TILE

    mod_a_all, silu_c = _ada(c_all, w_ada[0], b_ada[0], 2 * d)
    mod_a_p, mod_a_s = split_rows(mod_a_all)

    hp, hs = _norm_rows(xp, xs, g_ffn1[0], mod_a_p, mod_a_s, 0, tm=tm)
    hmid, hmid_s, ((w_down_b,), (mod_b_all,), (w_in_b,)) = _up_proj(
        hp, hs, w_up1[0], tm=tm, tn=UP_COL_TILE,
        streams=[_cast_stream(w_down1[0], CAST_CHUNKS),
                 _ada_stream(silu_c, w_ada[0], b_ada[0], 2 * d, 3 * d, ADA_STREAM_TILE),
                 _cast_stream(w_in_t, CAST_CHUNKS, start=CAST_CHUNKS, rows=IN_MAIN)])
    mod_b_p, mod_b_s = split_rows(mod_b_all)
    (xp, hp), (xs, hs) = _resid([hmid], [hmid_s], [(w_down_b, 0)], xp, xs, mod_b_p, mod_b_s, 0,
                                g_mix[0], mod_b_p, mod_b_s, 1, factor=0.5, tm=RESID_ROWS_FFN,
                                emit_x=True,
                                h_dtype=BF16)

    (proj, dt_raw), (proj_s, dt_raw_s), _, ((w_out_b,),) = _proj(
        hp, [(w_in_b, 0)], n_out=IN_MAIN, tm=tm, tn=IN_COL_TILE, swiglu=False, out_dtype=F32,
        row_tiles=(0, bp * seq // tm), side=hs, trans_w=True, w_extra=w_dt_t,
        streams=[_cast_stream(w_out[0], CAST_CHUNKS_OUT)])
    lconv = state_lru_conv[0].reshape(bs, (CONV_W - 1) * W_LRU)
    sconv = state_ssd_conv[0].reshape(bs, (CONV_W - 1) * SSD_CONV_DIM)
    out_l_s, lru_h_s, lconv_new, sconv_new, xs_act, xdt, bc, dec = _sample_pre(
        proj_s, dt_raw_s, state_lru_h[0], lconv, sconv, p)
    out_l, lru_h_p, ((ssm_s, y_raw), (modf_all,)) = _lru_prompt(
        proj, bp, seq, p["lru_cw"], p["lru_cb"], p["lru_wg"], p["lru_ba"], p["lru_bi"],
        p["lru_lam"],
        streams=[_state_stream(state_ssm[0].reshape(bs, W_SSD, SSD_STATE), xdt, bc, dec),
                 _ada_stream(silu_c, w_ada_f, b_ada_f, 0, 2 * d, 2 * d // (bp * seq // LRU_TIME_TILE))])
    y_ssd_s = _sample_post(y_raw, xs_act, proj_s, p["ssd_dexp"], p["ssd_ng"])
    y_ssd, ssm_p, ((mod_c_all,),) = _ssd_prompt(
        proj, dt_raw, bp, seq, p["ssd_cw"], p["ssd_cb"], p["ssd_dtb"], p["ssd_alog"],
        p["ssd_dexp"], p["ssd_ng"],
        streams=[_ada_stream(silu_c, w_ada[0], b_ada[0], 5 * d, (N_MOD - 5) * d,
                             (N_MOD - 5) * d // (bp * seq // (SSD_CHUNK * SSD_CHUNKS_PER_STEP)))])
    mod_c_p, mod_c_s = split_rows(mod_c_all)
    modf_p, modf_s = split_rows(modf_all)
    proj3 = proj.reshape(bp, seq, IN_MAIN)
    lru_buf_p = proj3[:, seq - (CONV_W - 1):, :W_LRU]
    ssd_buf_p = proj3[:, seq - (CONV_W - 1):, 2 * W_LRU + W_SSD:]

    (xp, hp), (xs, hs) = _resid([out_l, y_ssd], [out_l_s, y_ssd_s], [(w_out_b, 0), (w_out_b, 1)],
                                xp, xs, mod_c_p, mod_c_s, 0, g_ffn2[0], mod_c_p, mod_c_s, 1,
                                factor=1.0, tm=RESID_ROWS_MIX, emit_x=True, h_dtype=BF16)

    hmid, hmid_s, ((w_down_b,),) = _up_proj(
        hp, hs, w_up2[0], tm=tm, tn=UP_COL_TILE, streams=[_cast_stream(w_down2[0], CAST_CHUNKS)])
    (yp,), (ys,) = _resid([hmid], [hmid_s], [(w_down_b, 0)], xp, xs, mod_c_p, mod_c_s, 3, g_final,
                          modf_p, modf_s, 0, factor=0.5, tm=RESID_ROWS_FFN, emit_x=False,
                          h_dtype=F32)

    stack = lambda v: v[None]
    return (yp.reshape(bp, seq, d), ys.reshape(bs, 1, d),
            stack(lru_h_p), stack(lru_buf_p), stack(ssm_p), stack(ssd_buf_p),
            stack(lru_h_s), stack(lconv_new.reshape(bs, CONV_W - 1, W_LRU)),
            stack(ssm_s.reshape(bs, SSD_HEADS, SSD_HEAD_DIM, SSD_STATE)),
            stack(sconv_new.reshape(bs, CONV_W - 1, SSD_CONV_DIM)))
```

```python
import functools
from typing import Callable, NamedTuple

import jax
import jax.numpy as jnp
from jax import lax
from jax.experimental import pallas as pl
from jax.experimental.pallas import tpu as pltpu

F32 = jnp.float32
BF16 = jnp.bfloat16

D_MODEL = 2048
D_FF = 5632
W_LRU = 1024
W_SSD = 1024
LRU_HEADS = 16
LRU_BLOCK = 64
LRU_C = 8.0
SSD_HEADS = 16
SSD_HEAD_DIM = 64
SSD_GROUPS = 2
SSD_HPG = 8
SSD_STATE = 128
SSD_CHUNK = 128
SSD_CHUNKS_PER_STEP = 2
CONV_W = 4
SSD_CONV_DIM = W_SSD + 2 * SSD_GROUPS * SSD_STATE
IN_MAIN = 2 * W_LRU + W_SSD + SSD_CONV_DIM
N_MOD = 9
EPS = 1e-6

LANES = 128
SUBLANES = 8
VMEM_LIMIT_BYTES = 56 * 1024 * 1024

LRU_GATE_GROUP = 256
LRU_TIME_TILE = 512
SCAN_ROWS = 2 * SUBLANES

ROW_TILE = 1024
UP_COL_TILE = 512
IN_COL_TILE = IN_MAIN // 3
RESID_ROWS_FFN = 256
RESID_ROWS_MIX = 512
CAST_CHUNKS = 32
CAST_CHUNKS_OUT = 16
ADA_STREAM_TILE = 256


def _sigmoid(v):
    return 0.5 * (jnp.tanh(0.5 * v) + 1.0)


def _silu(v):
    return v * _sigmoid(v)


def _softplus(v):
    return jnp.maximum(v, 0.0) + jnp.log1p(jnp.exp(-jnp.abs(v)))


def _gelu_tanh(v):
    return 0.5 * v * (1.0 + jnp.tanh(0.7978845608028654 * (v + 0.044715 * (v * v * v))))


def _bdot(a, b):
    return jnp.dot(a, b, preferred_element_type=F32)


def _params(sem):
    return pltpu.CompilerParams(dimension_semantics=sem, vmem_limit_bytes=VMEM_LIMIT_BYTES)


def _ada_mm_kernel(s_ref, w_ref, b_ref, o_ref):
    o_ref[...] = _bdot(s_ref[...], w_ref[...].astype(BF16)) + b_ref[...]


def _ada_kernel(c_ref, w_ref, b_ref, o_ref, s_ref):
    s_ref[...] = _silu(c_ref[...]).astype(BF16)
    _ada_mm_kernel(s_ref, w_ref, b_ref, o_ref)


def _ada(c, w, b, cols, tn=1024):
    m, k = c.shape
    n = cols
    return pl.pallas_call(
        _ada_kernel,
        grid=(n // tn,),
        in_specs=[pl.BlockSpec((m, k), lambda j: (0, 0)),
                  pl.BlockSpec((k, tn), lambda j: (0, j)),
                  pl.BlockSpec((1, tn), lambda j: (0, j))],
        out_specs=[pl.BlockSpec((m, tn), lambda j: (0, j)), pl.BlockSpec((m, k), lambda j: (0, 0))],
        out_shape=[jax.ShapeDtypeStruct((m, n), F32), jax.ShapeDtypeStruct((m, k), BF16)],
        compiler_params=_params(("arbitrary",)),
        name="ada_proj",
    )(c, w, b.reshape(1, -1))


def _norm_modulate(x, gain, shift, scale):
    ms = jnp.mean(x * x, axis=-1, keepdims=True)
    y = x * lax.rsqrt(ms + EPS) * gain
    return y * (1.0 + scale) + shift


def _norm_rows_kernel(x_ref, gain_ref, sh_ref, sc_ref, xs_ref, shs_ref, scs_ref, o_ref, os_ref):
    o_ref[...] = _norm_modulate(x_ref[...], gain_ref[...], sh_ref[...],
                                sc_ref[...]).astype(o_ref.dtype)

    @pl.when(pl.program_id(0) == 0)
    def _():
        os_ref[...] = _norm_modulate(xs_ref[...], gain_ref[...], shs_ref[...],
                                     scs_ref[...]).astype(os_ref.dtype)


def _norm_rows(x, x_s, gain, mod, mod_s, shift_chunk, *, tm):
    m, d = x.shape
    ns = x_s.shape[0]
    tiles_per_group = (m // tm) // mod.shape[0]
    once = dict(pipeline_mode=pl.Buffered(1))
    mod_spec = lambda c: pl.BlockSpec((None, 1, d), lambda i: (i // tiles_per_group, 0, c))
    mod_s_spec = lambda c: pl.BlockSpec((None, ns, d), lambda i: (0, 0, c), **once)
    return pl.pallas_call(
        _norm_rows_kernel,
        grid=(m // tm,),
        in_specs=[pl.BlockSpec((tm, d), lambda i: (i, 0)), pl.BlockSpec((1, d), lambda i: (0, 0)),
                  mod_spec(shift_chunk), mod_spec(shift_chunk + 1),
                  pl.BlockSpec((ns, d), lambda i: (0, 0), **once),
                  mod_s_spec(shift_chunk), mod_s_spec(shift_chunk + 1)],
        out_specs=[pl.BlockSpec((tm, d), lambda i: (i, 0)), pl.BlockSpec((ns, d), lambda i: (0, 0))],
        out_shape=[jax.ShapeDtypeStruct((m, d), BF16), jax.ShapeDtypeStruct((ns, d), BF16)],
        compiler_params=_params(("arbitrary",)),
        name="norm_rows",
    )(x, gain.reshape(1, d), mod, mod, x_s, mod_s, mod_s)


class _Stream(NamedTuple):
    ins: list
    outs: list
    body: Callable
    start: int
    steps: int


def _stream_io(streams, n_inner):
    in_specs, args, out_specs, out_shape = [], [], [], []

    def spec(st, block, index_fn):
        return pl.BlockSpec(
            block, lambda i, j: index_fn(jnp.clip(i * n_inner + j - st.start, 0, st.steps - 1)))

    for st in streams:
        for arr, block, index_fn in st.ins:
            in_specs.append(spec(st, block, index_fn))
            args.append(arr)
        for shape, block, index_fn in st.outs:
            out_specs.append(spec(st, block, index_fn))
            out_shape.append(shape)
    return in_specs, args, out_specs, out_shape


def _run_streams(streams, in_refs, out_refs, n_inner, n_steps):
    step = pl.program_id(0) * n_inner + pl.program_id(1)
    in_refs, out_refs = iter(in_refs), iter(out_refs)
    for st in streams:
        ins = [next(in_refs) for _ in st.ins]
        outs = [next(out_refs) for _ in st.outs]
        if st.start == 0 and st.steps == n_steps:
            st.body(*ins, *outs)
            continue

        @pl.when((step >= st.start) & (step < st.start + st.steps))
        def _(st=st, ins=ins, outs=outs):
            st.body(*ins, *outs)


def _split_stream_outs(streams, flat):
    flat = list(flat)
    return [[flat.pop(0) for _ in st.outs] for st in streams]


def _cast_body(src_ref, dst_ref):
    dst_ref[...] = src_ref[...].astype(dst_ref.dtype)


def _cast_stream(w, chunks, start=0, rows=None):
    rows = w.shape[0] if rows is None else rows
    block = (rows // chunks, w.shape[1])
    index = lambda k: (k, 0)
    return _Stream([(w, block, index)],
                   [(jax.ShapeDtypeStruct((rows, w.shape[1]), BF16), block, index)],
                   _cast_body, start, chunks)


def _ada_stream(sc, w, b, col0, cols, tn, start=0):
    m, k = sc.shape
    t0 = col0 // tn
    return _Stream(
        [(sc, (m, k), lambda s: (0, 0)), (w, (k, tn), lambda s: (0, s + t0)),
         (b.reshape(1, -1), (1, tn), lambda s: (0, s + t0))],
        [(jax.ShapeDtypeStruct((m, cols), F32), (m, tn), lambda s: (0, s))],
        _ada_mm_kernel, start, cols // tn)


def _wdot(h, w, trans_w):
    if trans_w:
        return lax.dot_general(h, w, (((1,), (1,)), ((), ())), preferred_element_type=F32)
    return _bdot(h, w)


def _proj_kernel(*refs, n_w, swiglu, trans_w, has_extra, has_side, emit_bf16, streams, n_prev,
                 nj, n_steps):
    it = iter(refs)
    x_ref = next(it)
    xs_ref = next(it) if has_side else None
    w_refs = [next(it) for _ in range(n_w)]
    wx_ref = next(it) if has_extra else None
    stream_ins = [next(it) for st in streams for _ in st.ins]
    for _ in range(n_prev):
        next(it)
    o_ref = next(it)
    ox_ref = next(it) if has_extra else None
    os_ref = next(it) if has_side else None
    osx_ref = next(it) if has_side and has_extra else None
    wo_refs = [next(it) for _ in range(n_w)] if emit_bf16 else []
    stream_outs = [next(it) for st in streams for _ in st.outs]

    j = pl.program_id(1)
    first_tile = pl.program_id(0) == 0

    if has_extra:
        @pl.when(j == 0)
        def _():
            ox_ref[...] = _wdot(x_ref[...], wx_ref[...].astype(BF16), trans_w)

        if has_side:
            @pl.when((j == 0) & first_tile)
            def _():
                osx_ref[...] = _wdot(xs_ref[...], wx_ref[...].astype(BF16), trans_w)

    wbs = [w_ref[...].astype(BF16) for w_ref in w_refs]
    for wo_ref, wb in zip(wo_refs, wbs):
        wo_ref[...] = wb

    def project(h, out_ref):
        if swiglu:
            g = _wdot(h, wbs[0], trans_w)
            u = _wdot(h, wbs[1], trans_w)
            out_ref[...] = (_silu(g) * u).astype(out_ref.dtype)
        else:
            out_ref[...] = _wdot(h, wbs[0], trans_w).astype(out_ref.dtype)

    project(x_ref[...], o_ref)
    if has_side:
        @pl.when(first_tile)
        def _():
            project(xs_ref[...], os_ref)

    _run_streams(streams, stream_ins, stream_outs, nj, n_steps)


def _proj(x, ws, *, n_out, tm, tn, swiglu, out_dtype, row_tiles, side=None, trans_w=False,
          w_extra=None, emit_bf16=False, streams=(), prev=None):
    m, d = x.shape
    t0, t1 = row_tiles
    nj = n_out // tn
    has_side = side is not None
    has_extra = w_extra is not None
    single_row_tile = t1 - t0 == 1
    once = dict(pipeline_mode=pl.Buffered(1))

    def w_spec(off):
        if trans_w:
            return pl.BlockSpec((tn, d), lambda i, j: (j + off, 0))
        return pl.BlockSpec((d, tn), lambda i, j: (0, j + off))

    x_mode = once if single_row_tile else {}
    in_specs = [pl.BlockSpec((tm, d), lambda i, j: (i + t0, 0), **x_mode)]
    args = [x]
    if has_side:
        ns = side.shape[0]
        in_specs.append(pl.BlockSpec((ns, d), lambda i, j: (0, 0), **once))
        args.append(side)
    in_specs += [w_spec(off) for _, off in ws]
    args += [w for w, _ in ws]
    if has_extra:
        nx = w_extra.shape[0] if trans_w else w_extra.shape[1]
        in_specs.append(pl.BlockSpec(w_extra.shape, lambda i, j: (0, 0)))
        args.append(w_extra)
    assert all(st.start + st.steps <= (t1 - t0) * nj for st in streams)
    st_in_specs, st_args, st_out_specs, st_out_shape = _stream_io(streams, nj)
    in_specs += st_in_specs
    args += st_args
    prev = list(prev or [])
    aliases = {}
    for k, buf in enumerate(prev):
        aliases[len(args)] = k
        in_specs.append(pl.BlockSpec(memory_space=pl.ANY))
        args.append(buf)

    out_specs = [pl.BlockSpec((tm, tn), lambda i, j: (i + t0, j))]
    out_shape = [jax.ShapeDtypeStruct((m, n_out), out_dtype)]
    if has_extra:
        out_specs.append(pl.BlockSpec((tm, nx), lambda i, j: (i + t0, 0)))
        out_shape.append(jax.ShapeDtypeStruct((m, nx), F32))
    n_main = len(out_shape)
    if has_side:
        out_specs.append(pl.BlockSpec((ns, tn), lambda i, j: (0, jnp.where(i == 0, j, nj - 1))))
        out_shape.append(jax.ShapeDtypeStruct((ns, n_out), out_dtype))
        if has_extra:
            out_specs.append(pl.BlockSpec((ns, nx), lambda i, j: (0, 0)))
            out_shape.append(jax.ShapeDtypeStruct((ns, nx), F32))
    n_side = len(out_shape) - n_main
    if emit_bf16:
        assert single_row_tile, "weight copies are written once per column tile"
        for _ in ws:
            out_specs.append(w_spec(0))
            out_shape.append(jax.ShapeDtypeStruct((n_out, d) if trans_w else (d, n_out), BF16))
    out_specs += st_out_specs
    out_shape += st_out_shape
    outs = pl.pallas_call(
        functools.partial(_proj_kernel, n_w=len(ws), swiglu=swiglu, trans_w=trans_w,
                          has_extra=has_extra, has_side=has_side, emit_bf16=emit_bf16,
                          streams=tuple(streams), n_prev=len(prev), nj=nj,
                          n_steps=(t1 - t0) * nj),
        grid=(t1 - t0, nj),
        in_specs=in_specs,
        out_specs=out_specs,
        out_shape=out_shape,
        input_output_aliases=aliases,
        compiler_params=_params(("arbitrary" if streams or has_side else "parallel", "arbitrary")),
        name="proj_swiglu" if swiglu else "proj",
    )(*args)
    n_wb = len(ws) if emit_bf16 else 0
    main, rest = outs[:n_main], outs[n_main:]
    side_outs, rest = rest[:n_side], rest[n_side:]
    wb, rest = rest[:n_wb], rest[n_wb:]
    return main, side_outs, wb, _split_stream_outs(streams, rest)


def _proj_first_tile(x, ws_f32, *, tm, tn, side, **kw):
    main, side_outs, wb, _ = _proj(x, ws_f32, tm=tm, tn=tn, row_tiles=(0, 1), side=side,
                                   emit_bf16=True, **kw)
    return main, side_outs, wb


def _proj_other_tiles(x, wb, prev, *, tm, tn, streams, **kw):
    main, _, _, stream_outs = _proj(x, [(w, 0) for w in wb], tm=tm, tn=tn,
                                    row_tiles=(1, x.shape[0] // tm), prev=prev, streams=streams,
                                    **kw)
    return main, stream_outs


def _resid_kernel(*refs, n_lhs, factor, emit_x):
    it = iter(refs)
    lhs_refs = [next(it) for _ in range(n_lhs)]
    lhs_s_refs = [next(it) for _ in range(n_lhs)]
    w_refs = [next(it) for _ in range(n_lhs)]
    x_ref, gate_ref, gain_ref, sh_ref, sc_ref = (next(it) for _ in range(5))
    xs_ref, gate_s_ref, sh_s_ref, sc_s_ref = (next(it) for _ in range(4))
    n_out = 2 if emit_x else 1
    outs = [next(it) for _ in range(n_out)]
    outs_s = [next(it) for _ in range(n_out)]

    def update(lhs, x_in, gate, sh, sc, out_refs):
        acc = _bdot(lhs[0][...], w_refs[0][...])
        for l_ref, w_ref in zip(lhs[1:], w_refs[1:]):
            acc = acc + _bdot(l_ref[...], w_ref[...])
        x_new = x_in[...] + (factor * gate[...]) * acc
        if emit_x:
            out_refs[0][...] = x_new
        h_ref = out_refs[-1]
        h_ref[...] = _norm_modulate(x_new, gain_ref[...], sh[...], sc[...]).astype(h_ref.dtype)

    update(lhs_refs, x_ref, gate_ref, sh_ref, sc_ref, outs)

    @pl.when(pl.program_id(0) == 0)
    def _():
        update(lhs_s_refs, xs_ref, gate_s_ref, sh_s_ref, sc_s_ref, outs_s)


def _resid(lhs_list, lhs_s_list, ws, x, x_s, mod, mod_s, gate_chunk, gain_next, mod_next,
           mod_next_s, shift_chunk_next, *, factor, tm, emit_x, h_dtype):
    m, d = x.shape
    ns = x_s.shape[0]
    groups = mod.shape[0]
    tiles_per_group = (m // tm) // groups
    kp = lhs_list[0].shape[1]
    once = dict(pipeline_mode=pl.Buffered(1))

    def mod_spec(chunk):
        return pl.BlockSpec((None, 1, d), lambda i: (i // tiles_per_group, 0, chunk))

    def mod_s_spec(chunk):
        return pl.BlockSpec((None, ns, d), lambda i: (0, 0, chunk), **once)

    in_specs = [pl.BlockSpec((tm, kp), lambda i: (i, 0)) for _ in lhs_list]
    in_specs += [pl.BlockSpec((ns, kp), lambda i: (0, 0), **once) for _ in lhs_s_list]
    in_specs += [pl.BlockSpec((kp, d), lambda i, k=k: (k, 0), **once) for _, k in ws]
    in_specs += [pl.BlockSpec((tm, d), lambda i: (i, 0)), mod_spec(gate_chunk),
                 pl.BlockSpec((1, d), lambda i: (0, 0)),
                 mod_spec(shift_chunk_next), mod_spec(shift_chunk_next + 1),
                 pl.BlockSpec((ns, d), lambda i: (0, 0), **once), mod_s_spec(gate_chunk),
                 mod_s_spec(shift_chunk_next), mod_s_spec(shift_chunk_next + 1)]
    row = pl.BlockSpec((tm, d), lambda i: (i, 0))
    row_s = pl.BlockSpec((ns, d), lambda i: (0, 0))
    dtypes = ([F32] if emit_x else []) + [h_dtype]
    out_specs = [row for _ in dtypes] + [row_s for _ in dtypes]
    out_shape = ([jax.ShapeDtypeStruct((m, d), t) for t in dtypes]
                 + [jax.ShapeDtypeStruct((ns, d), t) for t in dtypes])
    outs = pl.pallas_call(
        functools.partial(_resid_kernel, n_lhs=len(lhs_list), factor=factor, emit_x=emit_x),
        grid=(m // tm,),
        in_specs=in_specs,
        out_specs=out_specs,
        out_shape=out_shape,
        compiler_params=_params(("arbitrary",)),
        name="resid",
    )(*lhs_list, *lhs_s_list, *[w for w, _ in ws], x, mod, gain_next.reshape(1, d), mod_next,
      mod_next, x_s, mod_s, mod_next_s, mod_next_s)
    return outs[:len(dtypes)], outs[len(dtypes):]


def _lru_gates(xc, wg_ref, ba, bi, sp):
    a_parts, b_parts = [], []
    for g in range(W_LRU // LRU_GATE_GROUP):
        cols = slice(g * LRU_GATE_GROUP, (g + 1) * LRU_GATE_GROUP)
        xg = xc[:, cols]
        ri = _bdot(xg.astype(BF16), wg_ref[g].astype(BF16))
        r = _sigmoid(ri[:, :LRU_GATE_GROUP] + ba[:, cols])
        i = _sigmoid(ri[:, LRU_GATE_GROUP:] + bi[:, cols])
        log_a = (-LRU_C * r) * sp[:, cols]
        a = jnp.exp(log_a)
        a_parts.append(a)
        v = 1.0 - a * a
        root = jnp.where(v > 0.0, v * lax.rsqrt(v), 0.0)
        b_parts.append(root * (i * xg))
    return jnp.concatenate(a_parts, axis=1), jnp.concatenate(b_parts, axis=1)


def _causal_conv(prev8, x, w_ref, b_ref):
    rows, width = x.shape
    rid = lax.broadcasted_iota(jnp.int32, (SUBLANES, width), 0)
    shifts = (1, 2, 3)
    taps = [w_ref[k:k + 1, :] for k in range(CONV_W)]
    bias = b_ref[...]
    prev_rot = [pltpu.roll(prev8, k, 0) for k in shifts]
    out = []
    for r in range(rows // SUBLANES):
        cur = x[r * SUBLANES:(r + 1) * SUBLANES, :]
        cur_rot = [pltpu.roll(cur, k, 0) for k in shifts]
        s1, s2, s3 = [jnp.where(rid < k, p, c) for k, p, c in zip(shifts, prev_rot, cur_rot)]
        out.append(bias + taps[0] * s3 + taps[1] * s2 + taps[2] * s1 + taps[3] * cur)
        prev_rot = cur_rot
    return jnp.concatenate(out, axis=0)


def _lru_prompt_kernel(*refs, streams, nt, n_steps):
    n_in = sum(len(st.ins) for st in streams)
    n_out = sum(len(st.outs) for st in streams)
    xl_ref, gl_ref, cw_ref, cb_ref, wg_ref, ba_ref, bi_ref, lam_ref = refs[:8]
    stream_ins = refs[8:8 + n_in]
    o_ref, hT_ref = refs[8 + n_in:10 + n_in]
    stream_outs = refs[10 + n_in:10 + n_in + n_out]
    xbuf, a_scr, b_scr, hcar = refs[10 + n_in + n_out:]
    t = pl.program_id(1)
    tt = xl_ref.shape[0]

    @pl.when(t == 0)
    def _():
        xbuf[...] = jnp.zeros_like(xbuf)
        hcar[...] = jnp.zeros_like(hcar)

    _run_streams(streams, stream_ins, stream_outs, nt, n_steps)

    x = xl_ref[...]
    xc = _causal_conv(xbuf[...], x, cw_ref, cb_ref)
    xbuf[...] = x[tt - SUBLANES:, :]

    sp = _softplus(-lam_ref[...])
    a, bt = _lru_gates(xc, wg_ref, ba_ref[...], bi_ref[...], sp)
    a_scr[...] = a
    b_scr[...] = bt

    rid = lax.broadcasted_iota(jnp.int32, (SUBLANES, W_LRU), 0)

    def scan8(a8, b8, h_in):
        for s in (1, 2, 4):
            a_sh = pltpu.roll(a8, s, 0)
            b_sh = pltpu.roll(b8, s, 0)
            m = rid >= s
            b8 = jnp.where(m, a8 * b_sh + b8, b8)
            a8 = jnp.where(m, a8 * a_sh, a8)
        h8 = a8 * h_in + b8
        return h8, jnp.broadcast_to(h8[SUBLANES - 1:SUBLANES, :], (SUBLANES, W_LRU))

    def body(g, h_in):
        r0 = pl.multiple_of(g * SCAN_ROWS, SCAN_ROWS)
        lo = pl.ds(r0, SUBLANES)
        hi = pl.ds(r0 + SUBLANES, SUBLANES)
        h_lo, h_mid = scan8(a_scr[lo, :], b_scr[lo, :], h_in)
        h_hi, h_out = scan8(a_scr[hi, :], b_scr[hi, :], h_mid)
        rows = pl.ds(r0, SCAN_ROWS)
        h16 = jnp.concatenate([h_lo, h_hi], axis=0)
        o_ref[rows, :] = (h16 * _gelu_tanh(gl_ref[rows, :])).astype(o_ref.dtype)
        return h_out

    h_last = lax.fori_loop(0, tt // SCAN_ROWS, body, hcar[...], unroll=4)
    hcar[...] = h_last

    @pl.when(t == pl.num_programs(1) - 1)
    def _():
        hT_ref[...] = h_last[0:1, :]


def _lru_prompt(proj, batch, seq, cw, cb, wg, ba, bi, lam, streams=()):
    tt = LRU_TIME_TILE
    nt = seq // tt
    assert all(st.start + st.steps <= batch * nt for st in streams)
    st_in_specs, st_args, st_out_specs, st_out_shape = _stream_io(streams, nt)
    row = lambda v: v.reshape(1, W_LRU)
    full = lambda shape: pl.BlockSpec(shape, lambda b, t: (0,) * len(shape))
    out, h_t, *rest = pl.pallas_call(
        functools.partial(_lru_prompt_kernel, streams=tuple(streams), nt=nt, n_steps=batch * nt),
        grid=(batch, nt),
        in_specs=[pl.BlockSpec((tt, W_LRU), lambda b, t: (b * nt + t, 0)),
                  pl.BlockSpec((tt, W_LRU), lambda b, t: (b * nt + t, 1)),
                  full((CONV_W, W_LRU)), full((1, W_LRU)), full(wg.shape),
                  full((1, W_LRU)), full((1, W_LRU)), full((1, W_LRU))] + st_in_specs,
        out_specs=[pl.BlockSpec((tt, W_LRU), lambda b, t: (b * nt + t, 0)),
                   pl.BlockSpec((None, 1, W_LRU), lambda b, t: (b, 0, 0))] + st_out_specs,
        out_shape=[jax.ShapeDtypeStruct((batch * seq, W_LRU), BF16),
                   jax.ShapeDtypeStruct((batch, 1, W_LRU), F32)] + st_out_shape,
        scratch_shapes=[pltpu.VMEM((SUBLANES, W_LRU), F32),
                        pltpu.VMEM((tt, W_LRU), F32),
                        pltpu.VMEM((tt, W_LRU), F32),
                        pltpu.VMEM((SUBLANES, W_LRU), F32)],
        compiler_params=_params(("arbitrary" if streams else "parallel", "arbitrary")),
        name="lru_prompt",
    )(proj, proj, cw, row(cb), wg, row(ba), row(bi), row(lam), *st_args)
    return out, h_t.reshape(batch, W_LRU), _split_stream_outs(streams, rest)


def _ssd_chunk(z_ref, xbc_ref, dt_ref, cw_ref, cb_ref, dtb_ref, alog_ref, dexp_ref, ng_ref, y_ref,
               xbuf, st_scr, y_scr, m_scr, xbd_scr):
    lc = SSD_CHUNK
    x = xbc_ref[...]
    act = _silu(_causal_conv(xbuf[...], x, cw_ref, cb_ref))
    xbuf[...] = x[lc - SUBLANES:, :]
    xs = act[:, :W_SSD]
    bm = act[:, W_SSD:W_SSD + SSD_GROUPS * SSD_STATE]
    cm = act[:, W_SSD + SSD_GROUPS * SSD_STATE:]

    dt = _softplus(dt_ref[...] + dtb_ref[...])
    d_a = dt * (-jnp.exp(alog_ref[...]))
    row_i = lax.broadcasted_iota(jnp.int32, (lc, lc), 0)
    col_i = lax.broadcasted_iota(jnp.int32, (lc, lc), 1)
    causal = row_i >= col_i
    tril = jnp.where(causal, 1.0, 0.0).astype(F32)
    cs = jnp.dot(tril, d_a, preferred_element_type=F32, precision=lax.Precision.HIGHEST)
    cs_t = cs.T
    dt_t = dt.T
    cs_last = cs[lc - 1:lc, :]

    def per_head_lanes(v):
        rows = v.shape[0]
        return jnp.concatenate(
            [jnp.broadcast_to(v[:, h:h + 1], (rows, SSD_HEAD_DIM)) for h in range(SSD_HEADS)],
            axis=1)

    w_exp = per_head_lanes(jnp.exp(cs_last - cs) * dt)
    ecs_exp = per_head_lanes(jnp.exp(cs))
    cd_exp = per_head_lanes(jnp.exp(cs_last))
    gw = SSD_HPG * SSD_HEAD_DIM
    low_half = col_i < SSD_HEAD_DIM

    for g in range(SSD_GROUPS):
        ncols = slice(g * SSD_STATE, (g + 1) * SSD_STATE)
        gcols = slice(g * gw, (g + 1) * gw)
        b_g = bm[:, ncols].astype(BF16)
        c_g = cm[:, ncols].astype(BF16)
        cb_mat = lax.dot_general(c_g, b_g, (((1,), (1,)), ((), ())),
                                 preferred_element_type=F32)
        for e in range(SSD_HPG):
            h = g * SSD_HPG + e
            cs_col = jnp.broadcast_to(cs[:, h:h + 1], (lc, lc))
            l_mat = jnp.exp(jnp.where(causal, cs_col - cs_t[h:h + 1, :], -jnp.inf))
            m_scr[g, :, e * lc:(e + 1) * lc] = (cb_mat * l_mat * dt_t[h:h + 1, :]).astype(BF16)
        for q in range(SSD_HPG // 2):
            lanes = slice(q * LANES, (q + 1) * LANES)
            slab = xs[:, g * gw + q * LANES:g * gw + (q + 1) * LANES]
            xbd_scr[g, (2 * q) * lc:(2 * q + 1) * lc, lanes] = jnp.where(
                low_half, slab, 0.0).astype(BF16)
            xbd_scr[g, (2 * q + 1) * lc:(2 * q + 2) * lc, lanes] = jnp.where(
                low_half, 0.0, slab).astype(BF16)
        st_g = st_scr[:, gcols]
        y_off = _bdot(c_g, st_g.astype(BF16)) * ecs_exp[:, gcols]
        y_scr[:, gcols] = (_bdot(m_scr[g], xbd_scr[g]) + y_off
                           + dexp_ref[:, gcols] * xs[:, gcols])
        xw = (xs[:, gcols] * w_exp[:, gcols]).astype(BF16)
        st_scr[:, gcols] = cd_exp[:, gcols] * st_g + lax.dot_general(
            b_g, xw, (((0,), (0,)), ((), ())), preferred_element_type=F32)

    yg = y_scr[...] * _silu(z_ref[...])
    ms = jnp.mean(yg * yg, axis=-1, keepdims=True)
    y_ref[...] = (yg * lax.rsqrt(ms + EPS) * ng_ref[...]).astype(y_ref.dtype)


def _ssd_prompt_kernel(*refs, streams, nc, n_steps):
    n_in = sum(len(st.ins) for st in streams)
    n_out = sum(len(st.outs) for st in streams)
    z_ref, xbc_ref, dt_ref, cw_ref, cb_ref, dtb_ref, alog_ref, dexp_ref, ng_ref = refs[:9]
    stream_ins = refs[9:9 + n_in]
    y_ref, st_ref = refs[9 + n_in:11 + n_in]
    stream_outs = refs[11 + n_in:11 + n_in + n_out]
    xbuf, st_scr, y_scr, m_scr, xbd_scr = refs[11 + n_in + n_out:]
    c = pl.program_id(1)

    @pl.when(c == 0)
    def _():
        xbuf[...] = jnp.zeros_like(xbuf)
        st_scr[...] = jnp.zeros_like(st_scr)
        xbd_scr[...] = jnp.zeros_like(xbd_scr)

    _run_streams(streams, stream_ins, stream_outs, nc, n_steps)

    for cc in range(SSD_CHUNKS_PER_STEP):
        rows = pl.ds(cc * SSD_CHUNK, SSD_CHUNK)
        _ssd_chunk(z_ref.at[rows], xbc_ref.at[rows], dt_ref.at[rows], cw_ref, cb_ref, dtb_ref,
                   alog_ref, dexp_ref, ng_ref, y_ref.at[rows], xbuf, st_scr, y_scr.at[cc],
                   m_scr.at[cc], xbd_scr.at[cc])

    @pl.when(c == pl.num_programs(1) - 1)
    def _():
        st_ref[...] = st_scr[...].T


def _ssd_prompt(proj, dt_raw, batch, seq, cw, cb, dtb, alog, dexp, ng, streams=()):
    lc = SSD_CHUNK
    cps = SSD_CHUNKS_PER_STEP
    tl = cps * lc
    nc = seq // tl
    assert all(st.start + st.steps <= batch * nc for st in streams)
    st_in_specs, st_args, st_out_specs, st_out_shape = _stream_io(streams, nc)
    full = lambda shape: pl.BlockSpec(shape, lambda b, c: (0,) * len(shape))
    z_blk = (2 * W_LRU) // W_SSD
    xbc_blk = (2 * W_LRU + W_SSD) // SSD_CONV_DIM
    y, st, *rest = pl.pallas_call(
        functools.partial(_ssd_prompt_kernel, streams=tuple(streams), nc=nc, n_steps=batch * nc),
        grid=(batch, nc),
        in_specs=[pl.BlockSpec((tl, W_SSD), lambda b, c: (b * nc + c, z_blk)),
                  pl.BlockSpec((tl, SSD_CONV_DIM), lambda b, c: (b * nc + c, xbc_blk)),
                  pl.BlockSpec((tl, LANES), lambda b, c: (b * nc + c, 0)),
                  full((CONV_W, SSD_CONV_DIM)), full((1, SSD_CONV_DIM)),
                  full((1, LANES)), full((1, LANES)), full((1, W_SSD)), full((1, W_SSD))]
        + st_in_specs,
        out_specs=[pl.BlockSpec((tl, W_SSD), lambda b, c: (b * nc + c, 0)),
                   pl.BlockSpec((None, W_SSD, SSD_STATE), lambda b, c: (b, 0, 0))] + st_out_specs,
        out_shape=[jax.ShapeDtypeStruct((batch * seq, W_SSD), BF16),
                   jax.ShapeDtypeStruct((batch, W_SSD, SSD_STATE), F32)] + st_out_shape,
        scratch_shapes=[pltpu.VMEM((SUBLANES, SSD_CONV_DIM), F32),
                        pltpu.VMEM((SSD_STATE, W_SSD), F32),
                        pltpu.VMEM((cps, lc, W_SSD), F32),
                        pltpu.VMEM((cps, SSD_GROUPS, lc, SSD_HPG * lc), BF16),
                        pltpu.VMEM((cps, SSD_GROUPS, SSD_HPG * lc, SSD_HPG * SSD_HEAD_DIM), BF16)],
        compiler_params=_params(("arbitrary" if streams else "parallel", "arbitrary")),
        name="ssd_prompt",
    )(proj, proj, dt_raw, cw, cb, dtb, alog, dexp, ng, *st_args)
    return (y, st.reshape(batch, SSD_HEADS, SSD_HEAD_DIM, SSD_STATE),
            _split_stream_outs(streams, rest))


def _sample_pre_kernel(proj_ref, dt_ref, h0_ref, lconv_ref, sconv_ref,
                       lcw_ref, lcb_ref, wg_ref, ba_ref, bi_ref, lam_ref,
                       scw_ref, scb_ref, dtb_ref, alog_ref,
                       outl_ref, hnew_ref, lconv_new_ref, sconv_new_ref,
                       xs_ref, xdt_ref, bc_ref, dec_ref):
    nb = proj_ref.shape[0]
    xl = proj_ref[:, 0:W_LRU]
    gl = proj_ref[:, W_LRU:2 * W_LRU]
    xbc = proj_ref[:, 2 * W_LRU + W_SSD:IN_MAIN]

    def conv1(state_ref, width, x_new, w_ref, b_ref):
        y = b_ref[...] + w_ref[0:1, :] * state_ref[:, 0:width]
        y = y + w_ref[1:2, :] * state_ref[:, width:2 * width]
        y = y + w_ref[2:3, :] * state_ref[:, 2 * width:3 * width]
        return y + w_ref[3:4, :] * x_new

    xc = conv1(lconv_ref, W_LRU, xl, lcw_ref, lcb_ref)
    a, bt = _lru_gates(xc, wg_ref, ba_ref[...], bi_ref[...], _softplus(-lam_ref[...]))
    h_new = a * h0_ref[...] + bt
    hnew_ref[...] = h_new
    outl_ref[...] = (h_new * _gelu_tanh(gl)).astype(outl_ref.dtype)
    lconv_new_ref[:, 0:2 * W_LRU] = lconv_ref[:, W_LRU:3 * W_LRU]
    lconv_new_ref[:, 2 * W_LRU:3 * W_LRU] = xl

    act = _silu(conv1(sconv_ref, SSD_CONV_DIM, xbc, scw_ref, scb_ref))
    sconv_new_ref[:, 0:2 * SSD_CONV_DIM] = sconv_ref[:, SSD_CONV_DIM:3 * SSD_CONV_DIM]
    sconv_new_ref[:, 2 * SSD_CONV_DIM:3 * SSD_CONV_DIM] = xbc
    xs = act[:, :W_SSD]
    xs_ref[...] = xs
    bc_ref[...] = act[:, W_SSD:]
    dt = _softplus(dt_ref[...] + dtb_ref[...])
    dec = jnp.exp(dt * (-jnp.exp(alog_ref[...])))
    for h in range(SSD_HEADS):
        pcols = slice(h * SSD_HEAD_DIM, (h + 1) * SSD_HEAD_DIM)
        xdt_ref[:, pcols] = xs[:, pcols] * jnp.broadcast_to(dt[:, h:h + 1], (nb, SSD_HEAD_DIM))
        dec_ref[h] = jnp.broadcast_to(dec[:, h:h + 1], (nb, SSD_STATE))


def _sample_pre(proj, dt_raw, h0, lconv, sconv, p):
    nb = proj.shape[0]
    out_shape = [jax.ShapeDtypeStruct((nb, W_LRU), BF16),
                 jax.ShapeDtypeStruct((nb, W_LRU), F32),
                 jax.ShapeDtypeStruct((nb, 3 * W_LRU), F32),
                 jax.ShapeDtypeStruct((nb, 3 * SSD_CONV_DIM), F32),
                 jax.ShapeDtypeStruct((nb, W_SSD), F32),
                 jax.ShapeDtypeStruct((nb, W_SSD), F32),
                 jax.ShapeDtypeStruct((nb, 2 * SSD_GROUPS * SSD_STATE), F32),
                 jax.ShapeDtypeStruct((SSD_HEADS, nb, SSD_STATE), F32)]
    return pl.pallas_call(
        _sample_pre_kernel,
        out_shape=out_shape,
        compiler_params=pltpu.CompilerParams(vmem_limit_bytes=VMEM_LIMIT_BYTES),
        name="sample_pre",
    )(proj, dt_raw, h0, lconv, sconv,
      p["lru_cw"], p["lru_cb"], p["lru_wg"], p["lru_ba"], p["lru_bi"], p["lru_lam"],
      p["ssd_cw"], p["ssd_cb"], p["ssd_dtb"], p["ssd_alog"])


def _sample_state_kernel(s_ref, xdt_ref, bc_ref, dec_ref, o_ref, y_ref):
    bb = s_ref.shape[0]
    half = SSD_HPG * SSD_HEAD_DIM
    rid = lax.broadcasted_iota(jnp.int32, (bb, W_SSD), 0)
    xdt = xdt_ref[...]
    bcb = bc_ref[...].astype(BF16)
    for k in range(bb):
        xk = jnp.where(rid == k, xdt, 0.0).astype(BF16)
        for g in range(SSD_GROUPS):
            rows = slice(g * half, (g + 1) * half)
            b_g = bcb[:, g * SSD_STATE:(g + 1) * SSD_STATE]
            c_g = bcb[:, (SSD_GROUPS + g) * SSD_STATE:(SSD_GROUPS + g + 1) * SSD_STATE]
            outer = lax.dot_general(xk[:, rows], b_g, (((0,), (0,)), ((), ())),
                                    preferred_element_type=F32)
            dec = jnp.concatenate(
                [jnp.broadcast_to(dec_ref[g * SSD_HPG + e, k:k + 1, :], (SSD_HEAD_DIM, SSD_STATE))
                 for e in range(SSD_HPG)], axis=0)
            s_new = dec * s_ref[k, rows, :] + outer
            o_ref[k, rows, :] = s_new
            yk = lax.dot_general(c_g, s_new.astype(BF16), (((1,), (1,)), ((), ())),
                                 preferred_element_type=F32)
            y_ref[k:k + 1, rows] = yk[k:k + 1, :]


def _state_stream(ssm, xdt, bc, dec, bb=8, start=0):
    nb = ssm.shape[0]
    state_block = (bb, W_SSD, SSD_STATE)
    return _Stream(
        [(ssm, state_block, lambda k: (k, 0, 0)),
         (xdt, (bb, W_SSD), lambda k: (k, 0)),
         (bc, (bb, 2 * SSD_GROUPS * SSD_STATE), lambda k: (k, 0)),
         (dec, (SSD_HEADS, bb, SSD_STATE), lambda k: (0, k, 0))],
        [(jax.ShapeDtypeStruct(ssm.shape, F32), state_block, lambda k: (k, 0, 0)),
         (jax.ShapeDtypeStruct((nb, W_SSD), F32), (bb, W_SSD), lambda k: (k, 0))],
        _sample_state_kernel, start, nb // bb)


def _sample_post_kernel(y_ref, xs_ref, proj_ref, dexp_ref, ng_ref, o_ref):
    z = proj_ref[:, 2 * W_LRU:2 * W_LRU + W_SSD]
    yg = (y_ref[...] + dexp_ref[...] * xs_ref[...]) * _silu(z)
    ms = jnp.mean(yg * yg, axis=-1, keepdims=True)
    o_ref[...] = (yg * lax.rsqrt(ms + EPS) * ng_ref[...]).astype(o_ref.dtype)


def _sample_post(y_raw, xs, proj, dexp, ng):
    return pl.pallas_call(
        _sample_post_kernel,
        out_shape=jax.ShapeDtypeStruct(y_raw.shape, BF16),
        compiler_params=pltpu.CompilerParams(vmem_limit_bytes=VMEM_LIMIT_BYTES),
        name="sample_post",
    )(y_raw, xs, proj, dexp, ng)


def _block_diag_groups(w):
    per = LRU_GATE_GROUP // LRU_BLOCK
    w4 = w.reshape(LRU_HEADS // per, per, LRU_BLOCK, LRU_BLOCK)
    bd = jnp.einsum("ghij,hk->ghikj", w4, jnp.eye(per, dtype=w.dtype))
    return bd.reshape(LRU_HEADS // per, LRU_GATE_GROUP, LRU_GATE_GROUP)


def _pad_lanes(v):
    v = v.reshape(1, -1)
    return jnp.pad(v, ((0, 0), (0, LANES - v.shape[1])))


def kernel(x_prompt, x_sample, c_prompt, c_sample, state_lru_h, state_lru_conv, state_ssm, state_ssd_conv, w_ada, b_ada, g_ffn1, w_up1, w_down1, g_mix, w_in, lru_conv_w, lru_conv_b, lru_wa, lru_ba, lru_wi, lru_bi, lru_lambda, ssd_conv_w, ssd_conv_b, ssd_dt_bias, ssd_A_log, ssd_D, ssd_norm_g, w_out, g_ffn2, w_up2, w_down2, w_ada_f, b_ada_f, g_final):
    bp, seq, d = x_prompt.shape
    bs = x_sample.shape[0]
    depth = w_ada.shape[0]
    assert depth == 1 and x_sample.shape[1] == 1 and d == D_MODEL

    pad_rows = (-(bs + bp)) % (2 * SUBLANES)
    c_rows = bs + bp + pad_rows
    c_all = jnp.concatenate([c_sample, c_prompt, jnp.zeros((pad_rows, d), F32)], axis=0)

    def split_rows(mod_all):
        width = mod_all.shape[1]
        return mod_all[bs:bs + bp].reshape(bp, 1, width), mod_all.reshape(1, c_rows, width)

    w_in_t = jnp.swapaxes(w_in[0], 0, 1)
    w_dt_t = jnp.pad(w_in_t[IN_MAIN:], ((0, LANES - SSD_HEADS), (0, 0)))
    up_blocks = D_FF // UP_COL_TILE
    p = {
        "lru_cw": lru_conv_w[0], "lru_cb": lru_conv_b[0].reshape(1, W_LRU),
        "lru_wg": jnp.concatenate([_block_diag_groups(lru_wa[0]), _block_diag_groups(lru_wi[0])],
                                  axis=-1),
        "lru_ba": lru_ba[0].reshape(1, W_LRU), "lru_bi": lru_bi[0].reshape(1, W_LRU),
        "lru_lam": lru_lambda[0].reshape(1, W_LRU),
        "ssd_cw": ssd_conv_w[0], "ssd_cb": ssd_conv_b[0].reshape(1, SSD_CONV_DIM),
        "ssd_dtb": _pad_lanes(ssd_dt_bias[0]), "ssd_alog": _pad_lanes(ssd_A_log[0]),
        "ssd_dexp": jnp.repeat(ssd_D[0], SSD_HEAD_DIM).reshape(1, W_SSD),
        "ssd_ng": ssd_norm_g[0].reshape(1, W_SSD),
    }

    xp = x_prompt.reshape(bp * seq, d)
    xs = x_sample.reshape(bs, d)
    tm = ROW_TILE
    up_kw = dict(n_out=D_FF, swiglu=True, out_dtype=BF16)

    mod_a_all, silu_c = _ada(c_all, w_ada[0], b_ada[0], 2 * d)
    mod_a_p, mod_a_s = split_rows(mod_a_all)

    hp, hs = _norm_rows(xp, xs, g_ffn1[0], mod_a_p, mod_a_s, 0, tm=tm)
    first, (hmid_s,), wb = _proj_first_tile(
        hp, [(w_up1[0], 0), (w_up1[0], up_blocks)], tm=tm, tn=UP_COL_TILE, side=hs, **up_kw)
    (hmid,), ((w_down_b,), (mod_b_all,), (w_in_b,)) = _proj_other_tiles(
        hp, wb, first, tm=tm, tn=UP_COL_TILE,
        streams=[_cast_stream(w_down1[0], CAST_CHUNKS),
                 _ada_stream(silu_c, w_ada[0], b_ada[0], 2 * d, 3 * d, ADA_STREAM_TILE),
                 _cast_stream(w_in_t, CAST_CHUNKS, start=CAST_CHUNKS, rows=IN_MAIN)],
        **up_kw)
    mod_b_p, mod_b_s = split_rows(mod_b_all)
    (xp, hp), (xs, hs) = _resid([hmid], [hmid_s], [(w_down_b, 0)], xp, xs, mod_b_p, mod_b_s, 0,
                                g_mix[0], mod_b_p, mod_b_s, 1, factor=0.5, tm=RESID_ROWS_FFN,
                                emit_x=True,
                                h_dtype=BF16)

    (proj, dt_raw), (proj_s, dt_raw_s), _, ((w_out_b,),) = _proj(
        hp, [(w_in_b, 0)], n_out=IN_MAIN, tm=tm, tn=IN_COL_TILE, swiglu=False, out_dtype=F32,
        row_tiles=(0, bp * seq // tm), side=hs, trans_w=True, w_extra=w_dt_t,
        streams=[_cast_stream(w_out[0], CAST_CHUNKS_OUT)])
    lconv = state_lru_conv[0].reshape(bs, (CONV_W - 1) * W_LRU)
    sconv = state_ssd_conv[0].reshape(bs, (CONV_W - 1) * SSD_CONV_DIM)
    out_l_s, lru_h_s, lconv_new, sconv_new, xs_act, xdt, bc, dec = _sample_pre(
        proj_s, dt_raw_s, state_lru_h[0], lconv, sconv, p)
    out_l, lru_h_p, ((ssm_s, y_raw), (modf_all,)) = _lru_prompt(
        proj, bp, seq, p["lru_cw"], p["lru_cb"], p["lru_wg"], p["lru_ba"], p["lru_bi"],
        p["lru_lam"],
        streams=[_state_stream(state_ssm[0].reshape(bs, W_SSD, SSD_STATE), xdt, bc, dec),
                 _ada_stream(silu_c, w_ada_f, b_ada_f, 0, 2 * d, 2 * d // (bp * seq // LRU_TIME_TILE))])
    y_ssd_s = _sample_post(y_raw, xs_act, proj_s, p["ssd_dexp"], p["ssd_ng"])
    y_ssd, ssm_p, ((mod_c_all,),) = _ssd_prompt(
        proj, dt_raw, bp, seq, p["ssd_cw"], p["ssd_cb"], p["ssd_dtb"], p["ssd_alog"],
        p["ssd_dexp"], p["ssd_ng"],
        streams=[_ada_stream(silu_c, w_ada[0], b_ada[0], 5 * d, (N_MOD - 5) * d,
                             (N_MOD - 5) * d // (bp * seq // (SSD_CHUNK * SSD_CHUNKS_PER_STEP)))])
    mod_c_p, mod_c_s = split_rows(mod_c_all)
    modf_p, modf_s = split_rows(modf_all)
    proj3 = proj.reshape(bp, seq, IN_MAIN)
    lru_buf_p = proj3[:, seq - (CONV_W - 1):, :W_LRU]
    ssd_buf_p = proj3[:, seq - (CONV_W - 1):, 2 * W_LRU + W_SSD:]

    (xp, hp), (xs, hs) = _resid([out_l, y_ssd], [out_l_s, y_ssd_s], [(w_out_b, 0), (w_out_b, 1)],
                                xp, xs, mod_c_p, mod_c_s, 0, g_ffn2[0], mod_c_p, mod_c_s, 1,
                                factor=1.0, tm=RESID_ROWS_MIX, emit_x=True, h_dtype=BF16)

    first, (hmid_s,), wb = _proj_first_tile(
        hp, [(w_up2[0], 0), (w_up2[0], up_blocks)], tm=tm, tn=UP_COL_TILE, side=hs, **up_kw)
    (hmid,), ((w_down_b,),) = _proj_other_tiles(
        hp, wb, first, tm=tm, tn=UP_COL_TILE, streams=[_cast_stream(w_down2[0], CAST_CHUNKS)],
        **up_kw)
    (yp,), (ys,) = _resid([hmid], [hmid_s], [(w_down_b, 0)], xp, xs, mod_c_p, mod_c_s, 3, g_final,
                          modf_p, modf_s, 0, factor=0.5, tm=RESID_ROWS_FFN, emit_x=False,
                          h_dtype=F32)

    stack = lambda v: v[None]
    return (yp.reshape(bp, seq, d), ys.reshape(bs, 1, d),
            stack(lru_h_p), stack(lru_buf_p), stack(ssm_p), stack(ssd_buf_p),
            stack(lru_h_s), stack(lconv_new.reshape(bs, CONV_W - 1, W_LRU)),
            stack(ssm_s.reshape(bs, SSD_HEADS, SSD_HEAD_DIM, SSD_STATE)),
            stack(sconv_new.reshape(bs, CONV_W - 1, SSD_CONV_DIM)))
```

```python
import functools
from typing import Callable, NamedTuple

import jax
import jax.numpy as jnp
from jax import lax
from jax.experimental import pallas as pl
from jax.experimental.pallas import tpu as pltpu

F32 = jnp.float32
BF16 = jnp.bfloat16

D_MODEL = 2048
D_FF = 5632
W_LRU = 1024
W_SSD = 1024
LRU_HEADS = 16
LRU_BLOCK = 64
LRU_C = 8.0
SSD_HEADS = 16
SSD_HEAD_DIM = 64
SSD_GROUPS = 2
SSD_HPG = 8
SSD_STATE = 128
SSD_CHUNK = 128
SSD_CHUNKS_PER_STEP = 2
CONV_W = 4
SSD_CONV_DIM = W_SSD + 2 * SSD_GROUPS * SSD_STATE
IN_MAIN = 2 * W_LRU + W_SSD + SSD_CONV_DIM
N_MOD = 9
EPS = 1e-6

LANES = 128
SUBLANES = 8
VMEM_LIMIT_BYTES = 56 * 1024 * 1024

LRU_GATE_GROUP = 256
LRU_TIME_TILE = 512
SCAN_ROWS = 2 * SUBLANES

ROW_TILE = 1024
UP_COL_TILE = 512
IN_COL_TILE = IN_MAIN // 3
RESID_ROWS_FFN = 256
RESID_ROWS_MIX = 512
CAST_CHUNKS = 32
CAST_CHUNKS_OUT = 16
ADA_STREAM_TILE = 256


def _sigmoid(v):
    return 0.5 * (jnp.tanh(0.5 * v) + 1.0)


def _silu(v):
    return v * _sigmoid(v)


def _softplus(v):
    return jnp.maximum(v, 0.0) + jnp.log1p(jnp.exp(-jnp.abs(v)))


def _gelu_tanh(v):
    return 0.5 * v * (1.0 + jnp.tanh(0.7978845608028654 * (v + 0.044715 * (v * v * v))))


def _bdot(a, b):
    return jnp.dot(a, b, preferred_element_type=F32)


def _params(sem):
    return pltpu.CompilerParams(dimension_semantics=sem, vmem_limit_bytes=VMEM_LIMIT_BYTES)


def _ada_mm_kernel(s_ref, w_ref, b_ref, o_ref):
    o_ref[...] = _bdot(s_ref[...], w_ref[...].astype(BF16)) + b_ref[...]


def _ada_kernel(c_ref, w_ref, b_ref, o_ref, s_ref):
    s_ref[...] = _silu(c_ref[...]).astype(BF16)
    _ada_mm_kernel(s_ref, w_ref, b_ref, o_ref)


def _ada(c, w, b, cols, tn=2048):
    m, k = c.shape
    n = cols
    return pl.pallas_call(
        _ada_kernel,
        grid=(n // tn,),
        in_specs=[pl.BlockSpec((m, k), lambda j: (0, 0)),
                  pl.BlockSpec((k, tn), lambda j: (0, j)),
                  pl.BlockSpec((1, tn), lambda j: (0, j))],
        out_specs=[pl.BlockSpec((m, tn), lambda j: (0, j)), pl.BlockSpec((m, k), lambda j: (0, 0))],
        out_shape=[jax.ShapeDtypeStruct((m, n), F32), jax.ShapeDtypeStruct((m, k), BF16)],
        compiler_params=_params(("arbitrary",)),
        name="ada_proj",
    )(c, w, b.reshape(1, -1))


def _norm_modulate(x, gain, shift, scale):
    ms = jnp.mean(x * x, axis=-1, keepdims=True)
    y = x * lax.rsqrt(ms + EPS) * gain
    return y * (1.0 + scale) + shift


def _norm_rows_kernel(x_ref, gain_ref, sh_ref, sc_ref, xs_ref, shs_ref, scs_ref, o_ref, os_ref):
    o_ref[...] = _norm_modulate(x_ref[...], gain_ref[...], sh_ref[...],
                                sc_ref[...]).astype(o_ref.dtype)

    @pl.when(pl.program_id(0) == 0)
    def _():
        os_ref[...] = _norm_modulate(xs_ref[...], gain_ref[...], shs_ref[...],
                                     scs_ref[...]).astype(os_ref.dtype)


def _norm_rows(x, x_s, gain, mod, mod_s, shift_chunk, *, tm):
    m, d = x.shape
    ns = x_s.shape[0]
    tiles_per_group = (m // tm) // mod.shape[0]
    once = dict(pipeline_mode=pl.Buffered(1))
    mod_spec = lambda c: pl.BlockSpec((None, 1, d), lambda i: (i // tiles_per_group, 0, c))
    mod_s_spec = lambda c: pl.BlockSpec((None, ns, d), lambda i: (0, 0, c), **once)
    return pl.pallas_call(
        _norm_rows_kernel,
        grid=(m // tm,),
        in_specs=[pl.BlockSpec((tm, d), lambda i: (i, 0)), pl.BlockSpec((1, d), lambda i: (0, 0)),
                  mod_spec(shift_chunk), mod_spec(shift_chunk + 1),
                  pl.BlockSpec((ns, d), lambda i: (0, 0), **once),
                  mod_s_spec(shift_chunk), mod_s_spec(shift_chunk + 1)],
        out_specs=[pl.BlockSpec((tm, d), lambda i: (i, 0)), pl.BlockSpec((ns, d), lambda i: (0, 0))],
        out_shape=[jax.ShapeDtypeStruct((m, d), BF16), jax.ShapeDtypeStruct((ns, d), BF16)],
        compiler_params=_params(("arbitrary",)),
        name="norm_rows",
    )(x, gain.reshape(1, d), mod, mod, x_s, mod_s, mod_s)


class _Stream(NamedTuple):
    ins: list
    outs: list
    body: Callable
    start: int
    steps: int


def _stream_io(streams, n_inner):
    in_specs, args, out_specs, out_shape = [], [], [], []

    def spec(st, block, index_fn):
        return pl.BlockSpec(
            block, lambda i, j: index_fn(jnp.clip(i * n_inner + j - st.start, 0, st.steps - 1)))

    for st in streams:
        for arr, block, index_fn in st.ins:
            in_specs.append(spec(st, block, index_fn))
            args.append(arr)
        for shape, block, index_fn in st.outs:
            out_specs.append(spec(st, block, index_fn))
            out_shape.append(shape)
    return in_specs, args, out_specs, out_shape


def _run_streams(streams, in_refs, out_refs, n_inner, n_steps):
    step = pl.program_id(0) * n_inner + pl.program_id(1)
    in_refs, out_refs = iter(in_refs), iter(out_refs)
    for st in streams:
        ins = [next(in_refs) for _ in st.ins]
        outs = [next(out_refs) for _ in st.outs]
        if st.start == 0 and st.steps == n_steps:
            st.body(*ins, *outs)
            continue

        @pl.when((step >= st.start) & (step < st.start + st.steps))
        def _(st=st, ins=ins, outs=outs):
            st.body(*ins, *outs)


def _split_stream_outs(streams, flat):
    flat = list(flat)
    return [[flat.pop(0) for _ in st.outs] for st in streams]


def _cast_body(src_ref, dst_ref):
    dst_ref[...] = src_ref[...].astype(dst_ref.dtype)


def _cast_stream(w, chunks, start=0, rows=None):
    rows = w.shape[0] if rows is None else rows
    block = (rows // chunks, w.shape[1])
    index = lambda k: (k, 0)
    return _Stream([(w, block, index)],
                   [(jax.ShapeDtypeStruct((rows, w.shape[1]), BF16), block, index)],
                   _cast_body, start, chunks)


def _ada_stream(sc, w, b, col0, cols, tn, start=0):
    m, k = sc.shape
    t0 = col0 // tn
    return _Stream(
        [(sc, (m, k), lambda s: (0, 0)), (w, (k, tn), lambda s: (0, s + t0)),
         (b.reshape(1, -1), (1, tn), lambda s: (0, s + t0))],
        [(jax.ShapeDtypeStruct((m, cols), F32), (m, tn), lambda s: (0, s))],
        _ada_mm_kernel, start, cols // tn)


def _wdot(h, w, trans_w):
    if trans_w:
        return lax.dot_general(h, w, (((1,), (1,)), ((), ())), preferred_element_type=F32)
    return _bdot(h, w)


def _proj_kernel(*refs, n_w, swiglu, trans_w, has_extra, has_side, emit_bf16, streams, n_prev,
                 nj, n_steps):
    it = iter(refs)
    x_ref = next(it)
    xs_ref = next(it) if has_side else None
    w_refs = [next(it) for _ in range(n_w)]
    wx_ref = next(it) if has_extra else None
    stream_ins = [next(it) for st in streams for _ in st.ins]
    for _ in range(n_prev):
        next(it)
    o_ref = next(it)
    ox_ref = next(it) if has_extra else None
    os_ref = next(it) if has_side else None
    osx_ref = next(it) if has_side and has_extra else None
    wo_refs = [next(it) for _ in range(n_w)] if emit_bf16 else []
    stream_outs = [next(it) for st in streams for _ in st.outs]

    j = pl.program_id(1)
    first_tile = pl.program_id(0) == 0

    if has_extra:
        @pl.when(j == 0)
        def _():
            ox_ref[...] = _wdot(x_ref[...], wx_ref[...].astype(BF16), trans_w)

        if has_side:
            @pl.when((j == 0) & first_tile)
            def _():
                osx_ref[...] = _wdot(xs_ref[...], wx_ref[...].astype(BF16), trans_w)

    wbs = [w_ref[...].astype(BF16) for w_ref in w_refs]
    for wo_ref, wb in zip(wo_refs, wbs):
        wo_ref[...] = wb

    def project(h, out_ref):
        if swiglu:
            g = _wdot(h, wbs[0], trans_w)
            u = _wdot(h, wbs[1], trans_w)
            out_ref[...] = (_silu(g) * u).astype(out_ref.dtype)
        else:
            out_ref[...] = _wdot(h, wbs[0], trans_w).astype(out_ref.dtype)

    project(x_ref[...], o_ref)
    if has_side:
        @pl.when(first_tile)
        def _():
            project(xs_ref[...], os_ref)

    _run_streams(streams, stream_ins, stream_outs, nj, n_steps)


def _proj(x, ws, *, n_out, tm, tn, swiglu, out_dtype, row_tiles, side=None, trans_w=False,
          w_extra=None, emit_bf16=False, streams=(), prev=None):
    m, d = x.shape
    t0, t1 = row_tiles
    nj = n_out // tn
    has_side = side is not None
    has_extra = w_extra is not None
    single_row_tile = t1 - t0 == 1
    once = dict(pipeline_mode=pl.Buffered(1))

    def w_spec(off):
        if trans_w:
            return pl.BlockSpec((tn, d), lambda i, j: (j + off, 0))
        return pl.BlockSpec((d, tn), lambda i, j: (0, j + off))

    x_mode = once if single_row_tile else {}
    in_specs = [pl.BlockSpec((tm, d), lambda i, j: (i + t0, 0), **x_mode)]
    args = [x]
    if has_side:
        ns = side.shape[0]
        in_specs.append(pl.BlockSpec((ns, d), lambda i, j: (0, 0), **once))
        args.append(side)
    in_specs += [w_spec(off) for _, off in ws]
    args += [w for w, _ in ws]
    if has_extra:
        nx = w_extra.shape[0] if trans_w else w_extra.shape[1]
        in_specs.append(pl.BlockSpec(w_extra.shape, lambda i, j: (0, 0)))
        args.append(w_extra)
    assert all(st.start + st.steps <= (t1 - t0) * nj for st in streams)
    st_in_specs, st_args, st_out_specs, st_out_shape = _stream_io(streams, nj)
    in_specs += st_in_specs
    args += st_args
    prev = list(prev or [])
    aliases = {}
    for k, buf in enumerate(prev):
        aliases[len(args)] = k
        in_specs.append(pl.BlockSpec(memory_space=pl.ANY))
        args.append(buf)

    out_specs = [pl.BlockSpec((tm, tn), lambda i, j: (i + t0, j))]
    out_shape = [jax.ShapeDtypeStruct((m, n_out), out_dtype)]
    if has_extra:
        out_specs.append(pl.BlockSpec((tm, nx), lambda i, j: (i + t0, 0)))
        out_shape.append(jax.ShapeDtypeStruct((m, nx), F32))
    n_main = len(out_shape)
    if has_side:
        out_specs.append(pl.BlockSpec((ns, tn), lambda i, j: (0, jnp.where(i == 0, j, nj - 1))))
        out_shape.append(jax.ShapeDtypeStruct((ns, n_out), out_dtype))
        if has_extra:
            out_specs.append(pl.BlockSpec((ns, nx), lambda i, j: (0, 0)))
            out_shape.append(jax.ShapeDtypeStruct((ns, nx), F32))
    n_side = len(out_shape) - n_main
    if emit_bf16:
        assert single_row_tile, "weight copies are written once per column tile"
        for _ in ws:
            out_specs.append(w_spec(0))
            out_shape.append(jax.ShapeDtypeStruct((n_out, d) if trans_w else (d, n_out), BF16))
    out_specs += st_out_specs
    out_shape += st_out_shape
    outs = pl.pallas_call(
        functools.partial(_proj_kernel, n_w=len(ws), swiglu=swiglu, trans_w=trans_w,
                          has_extra=has_extra, has_side=has_side, emit_bf16=emit_bf16,
                          streams=tuple(streams), n_prev=len(prev), nj=nj,
                          n_steps=(t1 - t0) * nj),
        grid=(t1 - t0, nj),
        in_specs=in_specs,
        out_specs=out_specs,
        out_shape=out_shape,
        input_output_aliases=aliases,
        compiler_params=_params(("arbitrary" if streams or has_side else "parallel", "arbitrary")),
        name="proj_swiglu" if swiglu else "proj",
    )(*args)
    n_wb = len(ws) if emit_bf16 else 0
    main, rest = outs[:n_main], outs[n_main:]
    side_outs, rest = rest[:n_side], rest[n_side:]
    wb, rest = rest[:n_wb], rest[n_wb:]
    return main, side_outs, wb, _split_stream_outs(streams, rest)


def _proj_first_tile(x, ws_f32, *, tm, tn, side, **kw):
    main, side_outs, wb, _ = _proj(x, ws_f32, tm=tm, tn=tn, row_tiles=(0, 1), side=side,
                                   emit_bf16=True, **kw)
    return main, side_outs, wb


def _proj_other_tiles(x, wb, prev, *, tm, tn, streams, **kw):
    main, _, _, stream_outs = _proj(x, [(w, 0) for w in wb], tm=tm, tn=tn,
                                    row_tiles=(1, x.shape[0] // tm), prev=prev, streams=streams,
                                    **kw)
    return main, stream_outs


def _resid_kernel(*refs, n_lhs, factor, emit_x):
    it = iter(refs)
    lhs_refs = [next(it) for _ in range(n_lhs)]
    lhs_s_refs = [next(it) for _ in range(n_lhs)]
    w_refs = [next(it) for _ in range(n_lhs)]
    x_ref, gate_ref, gain_ref, sh_ref, sc_ref = (next(it) for _ in range(5))
    xs_ref, gate_s_ref, sh_s_ref, sc_s_ref = (next(it) for _ in range(4))
    n_out = 2 if emit_x else 1
    outs = [next(it) for _ in range(n_out)]
    outs_s = [next(it) for _ in range(n_out)]

    def update(lhs, x_in, gate, sh, sc, out_refs):
        acc = _bdot(lhs[0][...], w_refs[0][...])
        for l_ref, w_ref in zip(lhs[1:], w_refs[1:]):
            acc = acc + _bdot(l_ref[...], w_ref[...])
        x_new = x_in[...] + (factor * gate[...]) * acc
        if emit_x:
            out_refs[0][...] = x_new
        h_ref = out_refs[-1]
        h_ref[...] = _norm_modulate(x_new, gain_ref[...], sh[...], sc[...]).astype(h_ref.dtype)

    update(lhs_refs, x_ref, gate_ref, sh_ref, sc_ref, outs)

    @pl.when(pl.program_id(0) == 0)
    def _():
        update(lhs_s_refs, xs_ref, gate_s_ref, sh_s_ref, sc_s_ref, outs_s)


def _resid(lhs_list, lhs_s_list, ws, x, x_s, mod, mod_s, gate_chunk, gain_next, mod_next,
           mod_next_s, shift_chunk_next, *, factor, tm, emit_x, h_dtype):
    m, d = x.shape
    ns = x_s.shape[0]
    groups = mod.shape[0]
    tiles_per_group = (m // tm) // groups
    kp = lhs_list[0].shape[1]
    once = dict(pipeline_mode=pl.Buffered(1))

    def mod_spec(chunk):
        return pl.BlockSpec((None, 1, d), lambda i: (i // tiles_per_group, 0, chunk))

    def mod_s_spec(chunk):
        return pl.BlockSpec((None, ns, d), lambda i: (0, 0, chunk), **once)

    in_specs = [pl.BlockSpec((tm, kp), lambda i: (i, 0)) for _ in lhs_list]
    in_specs += [pl.BlockSpec((ns, kp), lambda i: (0, 0), **once) for _ in lhs_s_list]
    in_specs += [pl.BlockSpec((kp, d), lambda i, k=k: (k, 0), **once) for _, k in ws]
    in_specs += [pl.BlockSpec((tm, d), lambda i: (i, 0)), mod_spec(gate_chunk),
                 pl.BlockSpec((1, d), lambda i: (0, 0)),
                 mod_spec(shift_chunk_next), mod_spec(shift_chunk_next + 1),
                 pl.BlockSpec((ns, d), lambda i: (0, 0), **once), mod_s_spec(gate_chunk),
                 mod_s_spec(shift_chunk_next), mod_s_spec(shift_chunk_next + 1)]
    row = pl.BlockSpec((tm, d), lambda i: (i, 0))
    row_s = pl.BlockSpec((ns, d), lambda i: (0, 0))
    dtypes = ([F32] if emit_x else []) + [h_dtype]
    out_specs = [row for _ in dtypes] + [row_s for _ in dtypes]
    out_shape = ([jax.ShapeDtypeStruct((m, d), t) for t in dtypes]
                 + [jax.ShapeDtypeStruct((ns, d), t) for t in dtypes])
    outs = pl.pallas_call(
        functools.partial(_resid_kernel, n_lhs=len(lhs_list), factor=factor, emit_x=emit_x),
        grid=(m // tm,),
        in_specs=in_specs,
        out_specs=out_specs,
        out_shape=out_shape,
        compiler_params=_params(("arbitrary",)),
        name="resid",
    )(*lhs_list, *lhs_s_list, *[w for w, _ in ws], x, mod, gain_next.reshape(1, d), mod_next,
      mod_next, x_s, mod_s, mod_next_s, mod_next_s)
    return outs[:len(dtypes)], outs[len(dtypes):]


def _lru_gates(xc, wg_ref, ba, bi, sp):
    a_parts, b_parts = [], []
    for g in range(W_LRU // LRU_GATE_GROUP):
        cols = slice(g * LRU_GATE_GROUP, (g + 1) * LRU_GATE_GROUP)
        xg = xc[:, cols]
        ri = _bdot(xg.astype(BF16), wg_ref[g].astype(BF16))
        r = _sigmoid(ri[:, :LRU_GATE_GROUP] + ba[:, cols])
        i = _sigmoid(ri[:, LRU_GATE_GROUP:] + bi[:, cols])
        log_a = (-LRU_C * r) * sp[:, cols]
        a = jnp.exp(log_a)
        a_parts.append(a)
        v = 1.0 - a * a
        root = jnp.where(v > 0.0, v * lax.rsqrt(v), 0.0)
        b_parts.append(root * (i * xg))
    return jnp.concatenate(a_parts, axis=1), jnp.concatenate(b_parts, axis=1)


def _causal_conv(prev8, x, w_ref, b_ref):
    rows, width = x.shape
    rid = lax.broadcasted_iota(jnp.int32, (SUBLANES, width), 0)
    shifts = (1, 2, 3)
    taps = [w_ref[k:k + 1, :] for k in range(CONV_W)]
    bias = b_ref[...]
    prev_rot = [pltpu.roll(prev8, k, 0) for k in shifts]
    out = []
    for r in range(rows // SUBLANES):
        cur = x[r * SUBLANES:(r + 1) * SUBLANES, :]
        cur_rot = [pltpu.roll(cur, k, 0) for k in shifts]
        s1, s2, s3 = [jnp.where(rid < k, p, c) for k, p, c in zip(shifts, prev_rot, cur_rot)]
        out.append(bias + taps[0] * s3 + taps[1] * s2 + taps[2] * s1 + taps[3] * cur)
        prev_rot = cur_rot
    return jnp.concatenate(out, axis=0)


def _lru_prompt_kernel(*refs, streams, nt, n_steps):
    n_in = sum(len(st.ins) for st in streams)
    n_out = sum(len(st.outs) for st in streams)
    xl_ref, gl_ref, cw_ref, cb_ref, wg_ref, ba_ref, bi_ref, lam_ref = refs[:8]
    stream_ins = refs[8:8 + n_in]
    o_ref, hT_ref = refs[8 + n_in:10 + n_in]
    stream_outs = refs[10 + n_in:10 + n_in + n_out]
    xbuf, a_scr, b_scr, hcar = refs[10 + n_in + n_out:]
    t = pl.program_id(1)
    tt = xl_ref.shape[0]

    @pl.when(t == 0)
    def _():
        xbuf[...] = jnp.zeros_like(xbuf)
        hcar[...] = jnp.zeros_like(hcar)

    _run_streams(streams, stream_ins, stream_outs, nt, n_steps)

    x = xl_ref[...]
    xc = _causal_conv(xbuf[...], x, cw_ref, cb_ref)
    xbuf[...] = x[tt - SUBLANES:, :]

    sp = _softplus(-lam_ref[...])
    a, bt = _lru_gates(xc, wg_ref, ba_ref[...], bi_ref[...], sp)
    a_scr[...] = a
    b_scr[...] = bt

    rid = lax.broadcasted_iota(jnp.int32, (SUBLANES, W_LRU), 0)

    def scan8(a8, b8, h_in):
        for s in (1, 2, 4):
            a_sh = pltpu.roll(a8, s, 0)
            b_sh = pltpu.roll(b8, s, 0)
            m = rid >= s
            b8 = jnp.where(m, a8 * b_sh + b8, b8)
            a8 = jnp.where(m, a8 * a_sh, a8)
        h8 = a8 * h_in + b8
        return h8, jnp.broadcast_to(h8[SUBLANES - 1:SUBLANES, :], (SUBLANES, W_LRU))

    def body(g, h_in):
        r0 = pl.multiple_of(g * SCAN_ROWS, SCAN_ROWS)
        lo = pl.ds(r0, SUBLANES)
        hi = pl.ds(r0 + SUBLANES, SUBLANES)
        h_lo, h_mid = scan8(a_scr[lo, :], b_scr[lo, :], h_in)
        h_hi, h_out = scan8(a_scr[hi, :], b_scr[hi, :], h_mid)
        rows = pl.ds(r0, SCAN_ROWS)
        h16 = jnp.concatenate([h_lo, h_hi], axis=0)
        o_ref[rows, :] = (h16 * _gelu_tanh(gl_ref[rows, :])).astype(o_ref.dtype)
        return h_out

    h_last = lax.fori_loop(0, tt // SCAN_ROWS, body, hcar[...], unroll=4)
    hcar[...] = h_last

    @pl.when(t == pl.num_programs(1) - 1)
    def _():
        hT_ref[...] = h_last[0:1, :]


def _lru_prompt(proj, batch, seq, cw, cb, wg, ba, bi, lam, streams=()):
    tt = LRU_TIME_TILE
    nt = seq // tt
    assert all(st.start + st.steps <= batch * nt for st in streams)
    st_in_specs, st_args, st_out_specs, st_out_shape = _stream_io(streams, nt)
    row = lambda v: v.reshape(1, W_LRU)
    full = lambda shape: pl.BlockSpec(shape, lambda b, t: (0,) * len(shape))
    out, h_t, *rest = pl.pallas_call(
        functools.partial(_lru_prompt_kernel, streams=tuple(streams), nt=nt, n_steps=batch * nt),
        grid=(batch, nt),
        in_specs=[pl.BlockSpec((tt, W_LRU), lambda b, t: (b * nt + t, 0)),
                  pl.BlockSpec((tt, W_LRU), lambda b, t: (b * nt + t, 1)),
                  full((CONV_W, W_LRU)), full((1, W_LRU)), full(wg.shape),
                  full((1, W_LRU)), full((1, W_LRU)), full((1, W_LRU))] + st_in_specs,
        out_specs=[pl.BlockSpec((tt, W_LRU), lambda b, t: (b * nt + t, 0)),
                   pl.BlockSpec((None, 1, W_LRU), lambda b, t: (b, 0, 0))] + st_out_specs,
        out_shape=[jax.ShapeDtypeStruct((batch * seq, W_LRU), BF16),
                   jax.ShapeDtypeStruct((batch, 1, W_LRU), F32)] + st_out_shape,
        scratch_shapes=[pltpu.VMEM((SUBLANES, W_LRU), F32),
                        pltpu.VMEM((tt, W_LRU), F32),
                        pltpu.VMEM((tt, W_LRU), F32),
                        pltpu.VMEM((SUBLANES, W_LRU), F32)],
        compiler_params=_params(("arbitrary" if streams else "parallel", "arbitrary")),
        name="lru_prompt",
    )(proj, proj, cw, row(cb), wg, row(ba), row(bi), row(lam), *st_args)
    return out, h_t.reshape(batch, W_LRU), _split_stream_outs(streams, rest)


def _ssd_chunk(z_ref, xbc_ref, dt_ref, cw_ref, cb_ref, dtb_ref, alog_ref, dexp_ref, ng_ref, y_ref,
               xbuf, st_scr, y_scr, m_scr, xbd_scr):
    lc = SSD_CHUNK
    x = xbc_ref[...]
    act = _silu(_causal_conv(xbuf[...], x, cw_ref, cb_ref))
    xbuf[...] = x[lc - SUBLANES:, :]
    xs = act[:, :W_SSD]
    bm = act[:, W_SSD:W_SSD + SSD_GROUPS * SSD_STATE]
    cm = act[:, W_SSD + SSD_GROUPS * SSD_STATE:]

    dt = _softplus(dt_ref[...] + dtb_ref[...])
    d_a = dt * (-jnp.exp(alog_ref[...]))
    row_i = lax.broadcasted_iota(jnp.int32, (lc, lc), 0)
    col_i = lax.broadcasted_iota(jnp.int32, (lc, lc), 1)
    causal = row_i >= col_i
    tril = jnp.where(causal, 1.0, 0.0).astype(F32)
    cs = jnp.dot(tril, d_a, preferred_element_type=F32, precision=lax.Precision.HIGHEST)
    cs_t = cs.T
    dt_t = dt.T
    cs_last = cs[lc - 1:lc, :]

    def per_head_lanes(v):
        rows = v.shape[0]
        return jnp.concatenate(
            [jnp.broadcast_to(v[:, h:h + 1], (rows, SSD_HEAD_DIM)) for h in range(SSD_HEADS)],
            axis=1)

    w_exp = per_head_lanes(jnp.exp(cs_last - cs) * dt)
    ecs_exp = per_head_lanes(jnp.exp(cs))
    cd_exp = per_head_lanes(jnp.exp(cs_last))
    gw = SSD_HPG * SSD_HEAD_DIM
    low_half = col_i < SSD_HEAD_DIM

    for g in range(SSD_GROUPS):
        ncols = slice(g * SSD_STATE, (g + 1) * SSD_STATE)
        gcols = slice(g * gw, (g + 1) * gw)
        b_g = bm[:, ncols].astype(BF16)
        c_g = cm[:, ncols].astype(BF16)
        cb_mat = lax.dot_general(c_g, b_g, (((1,), (1,)), ((), ())),
                                 preferred_element_type=F32)
        for e in range(SSD_HPG):
            h = g * SSD_HPG + e
            cs_col = jnp.broadcast_to(cs[:, h:h + 1], (lc, lc))
            l_mat = jnp.exp(jnp.where(causal, cs_col - cs_t[h:h + 1, :], -jnp.inf))
            m_scr[g, :, e * lc:(e + 1) * lc] = (cb_mat * l_mat * dt_t[h:h + 1, :]).astype(BF16)
        for q in range(SSD_HPG // 2):
            lanes = slice(q * LANES, (q + 1) * LANES)
            slab = xs[:, g * gw + q * LANES:g * gw + (q + 1) * LANES]
            xbd_scr[g, (2 * q) * lc:(2 * q + 1) * lc, lanes] = jnp.where(
                low_half, slab, 0.0).astype(BF16)
            xbd_scr[g, (2 * q + 1) * lc:(2 * q + 2) * lc, lanes] = jnp.where(
                low_half, 0.0, slab).astype(BF16)
        st_g = st_scr[:, gcols]
        y_off = _bdot(c_g, st_g.astype(BF16)) * ecs_exp[:, gcols]
        y_scr[:, gcols] = (_bdot(m_scr[g], xbd_scr[g]) + y_off
                           + dexp_ref[:, gcols] * xs[:, gcols])
        xw = (xs[:, gcols] * w_exp[:, gcols]).astype(BF16)
        st_scr[:, gcols] = cd_exp[:, gcols] * st_g + lax.dot_general(
            b_g, xw, (((0,), (0,)), ((), ())), preferred_element_type=F32)

    yg = y_scr[...] * _silu(z_ref[...])
    ms = jnp.mean(yg * yg, axis=-1, keepdims=True)
    y_ref[...] = (yg * lax.rsqrt(ms + EPS) * ng_ref[...]).astype(y_ref.dtype)


def _ssd_prompt_kernel(*refs, streams, nc, n_steps):
    n_in = sum(len(st.ins) for st in streams)
    n_out = sum(len(st.outs) for st in streams)
    z_ref, xbc_ref, dt_ref, cw_ref, cb_ref, dtb_ref, alog_ref, dexp_ref, ng_ref = refs[:9]
    stream_ins = refs[9:9 + n_in]
    y_ref, st_ref = refs[9 + n_in:11 + n_in]
    stream_outs = refs[11 + n_in:11 + n_in + n_out]
    xbuf, st_scr, y_scr, m_scr, xbd_scr = refs[11 + n_in + n_out:]
    c = pl.program_id(1)

    @pl.when(c == 0)
    def _():
        xbuf[...] = jnp.zeros_like(xbuf)
        st_scr[...] = jnp.zeros_like(st_scr)
        xbd_scr[...] = jnp.zeros_like(xbd_scr)

    _run_streams(streams, stream_ins, stream_outs, nc, n_steps)

    for cc in range(SSD_CHUNKS_PER_STEP):
        rows = pl.ds(cc * SSD_CHUNK, SSD_CHUNK)
        _ssd_chunk(z_ref.at[rows], xbc_ref.at[rows], dt_ref.at[rows], cw_ref, cb_ref, dtb_ref,
                   alog_ref, dexp_ref, ng_ref, y_ref.at[rows], xbuf, st_scr, y_scr.at[cc],
                   m_scr.at[cc], xbd_scr.at[cc])

    @pl.when(c == pl.num_programs(1) - 1)
    def _():
        st_ref[...] = st_scr[...].T


def _ssd_prompt(proj, dt_raw, batch, seq, cw, cb, dtb, alog, dexp, ng, streams=()):
    lc = SSD_CHUNK
    cps = SSD_CHUNKS_PER_STEP
    tl = cps * lc
    nc = seq // tl
    assert all(st.start + st.steps <= batch * nc for st in streams)
    st_in_specs, st_args, st_out_specs, st_out_shape = _stream_io(streams, nc)
    full = lambda shape: pl.BlockSpec(shape, lambda b, c: (0,) * len(shape))
    z_blk = (2 * W_LRU) // W_SSD
    xbc_blk = (2 * W_LRU + W_SSD) // SSD_CONV_DIM
    y, st, *rest = pl.pallas_call(
        functools.partial(_ssd_prompt_kernel, streams=tuple(streams), nc=nc, n_steps=batch * nc),
        grid=(batch, nc),
        in_specs=[pl.BlockSpec((tl, W_SSD), lambda b, c: (b * nc + c, z_blk)),
                  pl.BlockSpec((tl, SSD_CONV_DIM), lambda b, c: (b * nc + c, xbc_blk)),
                  pl.BlockSpec((tl, LANES), lambda b, c: (b * nc + c, 0)),
                  full((CONV_W, SSD_CONV_DIM)), full((1, SSD_CONV_DIM)),
                  full((1, LANES)), full((1, LANES)), full((1, W_SSD)), full((1, W_SSD))]
        + st_in_specs,
        out_specs=[pl.BlockSpec((tl, W_SSD), lambda b, c: (b * nc + c, 0)),
                   pl.BlockSpec((None, W_SSD, SSD_STATE), lambda b, c: (b, 0, 0))] + st_out_specs,
        out_shape=[jax.ShapeDtypeStruct((batch * seq, W_SSD), BF16),
                   jax.ShapeDtypeStruct((batch, W_SSD, SSD_STATE), F32)] + st_out_shape,
        scratch_shapes=[pltpu.VMEM((SUBLANES, SSD_CONV_DIM), F32),
                        pltpu.VMEM((SSD_STATE, W_SSD), F32),
                        pltpu.VMEM((cps, lc, W_SSD), F32),
                        pltpu.VMEM((cps, SSD_GROUPS, lc, SSD_HPG * lc), BF16),
                        pltpu.VMEM((cps, SSD_GROUPS, SSD_HPG * lc, SSD_HPG * SSD_HEAD_DIM), BF16)],
        compiler_params=_params(("arbitrary" if streams else "parallel", "arbitrary")),
        name="ssd_prompt",
    )(proj, proj, dt_raw, cw, cb, dtb, alog, dexp, ng, *st_args)
    return (y, st.reshape(batch, SSD_HEADS, SSD_HEAD_DIM, SSD_STATE),
            _split_stream_outs(streams, rest))


def _sample_pre_kernel(proj_ref, dt_ref, h0_ref, lconv_ref, sconv_ref,
                       lcw_ref, lcb_ref, wg_ref, ba_ref, bi_ref, lam_ref,
                       scw_ref, scb_ref, dtb_ref, alog_ref,
                       outl_ref, hnew_ref, lconv_new_ref, sconv_new_ref,
                       xs_ref, xdt_ref, bc_ref, dec_ref):
    nb = proj_ref.shape[0]
    xl = proj_ref[:, 0:W_LRU]
    gl = proj_ref[:, W_LRU:2 * W_LRU]
    xbc = proj_ref[:, 2 * W_LRU + W_SSD:IN_MAIN]

    def conv1(state_ref, width, x_new, w_ref, b_ref):
        y = b_ref[...] + w_ref[0:1, :] * state_ref[:, 0:width]
        y = y + w_ref[1:2, :] * state_ref[:, width:2 * width]
        y = y + w_ref[2:3, :] * state_ref[:, 2 * width:3 * width]
        return y + w_ref[3:4, :] * x_new

    xc = conv1(lconv_ref, W_LRU, xl, lcw_ref, lcb_ref)
    a, bt = _lru_gates(xc, wg_ref, ba_ref[...], bi_ref[...], _softplus(-lam_ref[...]))
    h_new = a * h0_ref[...] + bt
    hnew_ref[...] = h_new
    outl_ref[...] = (h_new * _gelu_tanh(gl)).astype(outl_ref.dtype)
    lconv_new_ref[:, 0:2 * W_LRU] = lconv_ref[:, W_LRU:3 * W_LRU]
    lconv_new_ref[:, 2 * W_LRU:3 * W_LRU] = xl

    act = _silu(conv1(sconv_ref, SSD_CONV_DIM, xbc, scw_ref, scb_ref))
    sconv_new_ref[:, 0:2 * SSD_CONV_DIM] = sconv_ref[:, SSD_CONV_DIM:3 * SSD_CONV_DIM]
    sconv_new_ref[:, 2 * SSD_CONV_DIM:3 * SSD_CONV_DIM] = xbc
    xs = act[:, :W_SSD]
    xs_ref[...] = xs
    bc_ref[...] = act[:, W_SSD:]
    dt = _softplus(dt_ref[...] + dtb_ref[...])
    dec = jnp.exp(dt * (-jnp.exp(alog_ref[...])))
    for h in range(SSD_HEADS):
        pcols = slice(h * SSD_HEAD_DIM, (h + 1) * SSD_HEAD_DIM)
        xdt_ref[:, pcols] = xs[:, pcols] * jnp.broadcast_to(dt[:, h:h + 1], (nb, SSD_HEAD_DIM))
        dec_ref[h] = jnp.broadcast_to(dec[:, h:h + 1], (nb, SSD_STATE))


def _sample_pre(proj, dt_raw, h0, lconv, sconv, p):
    nb = proj.shape[0]
    out_shape = [jax.ShapeDtypeStruct((nb, W_LRU), BF16),
                 jax.ShapeDtypeStruct((nb, W_LRU), F32),
                 jax.ShapeDtypeStruct((nb, 3 * W_LRU), F32),
                 jax.ShapeDtypeStruct((nb, 3 * SSD_CONV_DIM), F32),
                 jax.ShapeDtypeStruct((nb, W_SSD), F32),
                 jax.ShapeDtypeStruct((nb, W_SSD), F32),
                 jax.ShapeDtypeStruct((nb, 2 * SSD_GROUPS * SSD_STATE), F32),
                 jax.ShapeDtypeStruct((SSD_HEADS, nb, SSD_STATE), F32)]
    return pl.pallas_call(
        _sample_pre_kernel,
        out_shape=out_shape,
        compiler_params=pltpu.CompilerParams(vmem_limit_bytes=VMEM_LIMIT_BYTES),
        name="sample_pre",
    )(proj, dt_raw, h0, lconv, sconv,
      p["lru_cw"], p["lru_cb"], p["lru_wg"], p["lru_ba"], p["lru_bi"], p["lru_lam"],
      p["ssd_cw"], p["ssd_cb"], p["ssd_dtb"], p["ssd_alog"])


def _sample_state_kernel(s_ref, xdt_ref, bc_ref, dec_ref, o_ref, y_ref):
    bb = s_ref.shape[0]
    half = SSD_HPG * SSD_HEAD_DIM
    rid = lax.broadcasted_iota(jnp.int32, (bb, W_SSD), 0)
    xdt = xdt_ref[...]
    bcb = bc_ref[...].astype(BF16)
    for k in range(bb):
        xk = jnp.where(rid == k, xdt, 0.0).astype(BF16)
        for g in range(SSD_GROUPS):
            rows = slice(g * half, (g + 1) * half)
            b_g = bcb[:, g * SSD_STATE:(g + 1) * SSD_STATE]
            c_g = bcb[:, (SSD_GROUPS + g) * SSD_STATE:(SSD_GROUPS + g + 1) * SSD_STATE]
            outer = lax.dot_general(xk[:, rows], b_g, (((0,), (0,)), ((), ())),
                                    preferred_element_type=F32)
            dec = jnp.concatenate(
                [jnp.broadcast_to(dec_ref[g * SSD_HPG + e, k:k + 1, :], (SSD_HEAD_DIM, SSD_STATE))
                 for e in range(SSD_HPG)], axis=0)
            s_new = dec * s_ref[k, rows, :] + outer
            o_ref[k, rows, :] = s_new
            yk = lax.dot_general(c_g, s_new.astype(BF16), (((1,), (1,)), ((), ())),
                                 preferred_element_type=F32)
            y_ref[k:k + 1, rows] = yk[k:k + 1, :]


def _state_stream(ssm, xdt, bc, dec, bb=8, start=0):
    nb = ssm.shape[0]
    state_block = (bb, W_SSD, SSD_STATE)
    return _Stream(
        [(ssm, state_block, lambda k: (k, 0, 0)),
         (xdt, (bb, W_SSD), lambda k: (k, 0)),
         (bc, (bb, 2 * SSD_GROUPS * SSD_STATE), lambda k: (k, 0)),
         (dec, (SSD_HEADS, bb, SSD_STATE), lambda k: (0, k, 0))],
        [(jax.ShapeDtypeStruct(ssm.shape, F32), state_block, lambda k: (k, 0, 0)),
         (jax.ShapeDtypeStruct((nb, W_SSD), F32), (bb, W_SSD), lambda k: (k, 0))],
        _sample_state_kernel, start, nb // bb)


def _sample_post_kernel(y_ref, xs_ref, proj_ref, dexp_ref, ng_ref, o_ref):
    z = proj_ref[:, 2 * W_LRU:2 * W_LRU + W_SSD]
    yg = (y_ref[...] + dexp_ref[...] * xs_ref[...]) * _silu(z)
    ms = jnp.mean(yg * yg, axis=-1, keepdims=True)
    o_ref[...] = (yg * lax.rsqrt(ms + EPS) * ng_ref[...]).astype(o_ref.dtype)


def _sample_post(y_raw, xs, proj, dexp, ng):
    return pl.pallas_call(
        _sample_post_kernel,
        out_shape=jax.ShapeDtypeStruct(y_raw.shape, BF16),
        compiler_params=pltpu.CompilerParams(vmem_limit_bytes=VMEM_LIMIT_BYTES),
        name="sample_post",
    )(y_raw, xs, proj, dexp, ng)


def _block_diag_groups(w):
    per = LRU_GATE_GROUP // LRU_BLOCK
    w4 = w.reshape(LRU_HEADS // per, per, LRU_BLOCK, LRU_BLOCK)
    bd = jnp.einsum("ghij,hk->ghikj", w4, jnp.eye(per, dtype=w.dtype))
    return bd.reshape(LRU_HEADS // per, LRU_GATE_GROUP, LRU_GATE_GROUP)


def _pad_lanes(v):
    v = v.reshape(1, -1)
    return jnp.pad(v, ((0, 0), (0, LANES - v.shape[1])))


def kernel(x_prompt, x_sample, c_prompt, c_sample, state_lru_h, state_lru_conv, state_ssm, state_ssd_conv, w_ada, b_ada, g_ffn1, w_up1, w_down1, g_mix, w_in, lru_conv_w, lru_conv_b, lru_wa, lru_ba, lru_wi, lru_bi, lru_lambda, ssd_conv_w, ssd_conv_b, ssd_dt_bias, ssd_A_log, ssd_D, ssd_norm_g, w_out, g_ffn2, w_up2, w_down2, w_ada_f, b_ada_f, g_final):
    bp, seq, d = x_prompt.shape
    bs = x_sample.shape[0]
    depth = w_ada.shape[0]
    assert depth == 1 and x_sample.shape[1] == 1 and d == D_MODEL

    pad_rows = (-(bs + bp)) % (2 * SUBLANES)
    c_rows = bs + bp + pad_rows
    c_all = jnp.concatenate([c_sample, c_prompt, jnp.zeros((pad_rows, d), F32)], axis=0)

    def split_rows(mod_all):
        width = mod_all.shape[1]
        return mod_all[bs:bs + bp].reshape(bp, 1, width), mod_all.reshape(1, c_rows, width)

    w_in_t = jnp.swapaxes(w_in[0], 0, 1)
    w_dt_t = jnp.pad(w_in_t[IN_MAIN:], ((0, LANES - SSD_HEADS), (0, 0)))
    up_blocks = D_FF // UP_COL_TILE
    p = {
        "lru_cw": lru_conv_w[0], "lru_cb": lru_conv_b[0].reshape(1, W_LRU),
        "lru_wg": jnp.concatenate([_block_diag_groups(lru_wa[0]), _block_diag_groups(lru_wi[0])],
                                  axis=-1),
        "lru_ba": lru_ba[0].reshape(1, W_LRU), "lru_bi": lru_bi[0].reshape(1, W_LRU),
        "lru_lam": lru_lambda[0].reshape(1, W_LRU),
        "ssd_cw": ssd_conv_w[0], "ssd_cb": ssd_conv_b[0].reshape(1, SSD_CONV_DIM),
        "ssd_dtb": _pad_lanes(ssd_dt_bias[0]), "ssd_alog": _pad_lanes(ssd_A_log[0]),
        "ssd_dexp": jnp.repeat(ssd_D[0], SSD_HEAD_DIM).reshape(1, W_SSD),
        "ssd_ng": ssd_norm_g[0].reshape(1, W_SSD),
    }

    xp = x_prompt.reshape(bp * seq, d)
    xs = x_sample.reshape(bs, d)
    tm = ROW_TILE
    up_kw = dict(n_out=D_FF, swiglu=True, out_dtype=BF16)

    mod_a_all, silu_c = _ada(c_all, w_ada[0], b_ada[0], 2 * d)
    mod_a_p, mod_a_s = split_rows(mod_a_all)

    hp, hs = _norm_rows(xp, xs, g_ffn1[0], mod_a_p, mod_a_s, 0, tm=tm)
    first, (hmid_s,), wb = _proj_first_tile(
        hp, [(w_up1[0], 0), (w_up1[0], up_blocks)], tm=tm, tn=UP_COL_TILE, side=hs, **up_kw)
    (hmid,), ((w_down_b,), (mod_b_all,), (w_in_b,)) = _proj_other_tiles(
        hp, wb, first, tm=tm, tn=UP_COL_TILE,
        streams=[_cast_stream(w_down1[0], CAST_CHUNKS),
                 _ada_stream(silu_c, w_ada[0], b_ada[0], 2 * d, 3 * d, ADA_STREAM_TILE),
                 _cast_stream(w_in_t, CAST_CHUNKS, start=CAST_CHUNKS, rows=IN_MAIN)],
        **up_kw)
    mod_b_p, mod_b_s = split_rows(mod_b_all)
    (xp, hp), (xs, hs) = _resid([hmid], [hmid_s], [(w_down_b, 0)], xp, xs, mod_b_p, mod_b_s, 0,
                                g_mix[0], mod_b_p, mod_b_s, 1, factor=0.5, tm=RESID_ROWS_FFN,
                                emit_x=True,
                                h_dtype=BF16)

    (proj, dt_raw), (proj_s, dt_raw_s), _, ((w_out_b,),) = _proj(
        hp, [(w_in_b, 0)], n_out=IN_MAIN, tm=tm, tn=IN_COL_TILE, swiglu=False, out_dtype=F32,
        row_tiles=(0, bp * seq // tm), side=hs, trans_w=True, w_extra=w_dt_t,
        streams=[_cast_stream(w_out[0], CAST_CHUNKS_OUT)])
    lconv = state_lru_conv[0].reshape(bs, (CONV_W - 1) * W_LRU)
    sconv = state_ssd_conv[0].reshape(bs, (CONV_W - 1) * SSD_CONV_DIM)
    out_l_s, lru_h_s, lconv_new, sconv_new, xs_act, xdt, bc, dec = _sample_pre(
        proj_s, dt_raw_s, state_lru_h[0], lconv, sconv, p)
    out_l, lru_h_p, ((ssm_s, y_raw), (modf_all,)) = _lru_prompt(
        proj, bp, seq, p["lru_cw"], p["lru_cb"], p["lru_wg"], p["lru_ba"], p["lru_bi"],
        p["lru_lam"],
        streams=[_state_stream(state_ssm[0].reshape(bs, W_SSD, SSD_STATE), xdt, bc, dec),
                 _ada_stream(silu_c, w_ada_f, b_ada_f, 0, 2 * d, 2 * d // (bp * seq // LRU_TIME_TILE))])
    y_ssd_s = _sample_post(y_raw, xs_act, proj_s, p["ssd_dexp"], p["ssd_ng"])
    y_ssd, ssm_p, ((mod_c_all,),) = _ssd_prompt(
        proj, dt_raw, bp, seq, p["ssd_cw"], p["ssd_cb"], p["ssd_dtb"], p["ssd_alog"],
        p["ssd_dexp"], p["ssd_ng"],
        streams=[_ada_stream(silu_c, w_ada[0], b_ada[0], 5 * d, (N_MOD - 5) * d,
                             (N_MOD - 5) * d // (bp * seq // (SSD_CHUNK * SSD_CHUNKS_PER_STEP)))])
    mod_c_p, mod_c_s = split_rows(mod_c_all)
    modf_p, modf_s = split_rows(modf_all)
    proj3 = proj.reshape(bp, seq, IN_MAIN)
    lru_buf_p = proj3[:, seq - (CONV_W - 1):, :W_LRU]
    ssd_buf_p = proj3[:, seq - (CONV_W - 1):, 2 * W_LRU + W_SSD:]

    (xp, hp), (xs, hs) = _resid([out_l, y_ssd], [out_l_s, y_ssd_s], [(w_out_b, 0), (w_out_b, 1)],
                                xp, xs, mod_c_p, mod_c_s, 0, g_ffn2[0], mod_c_p, mod_c_s, 1,
                                factor=1.0, tm=RESID_ROWS_MIX, emit_x=True, h_dtype=BF16)

    first, (hmid_s,), wb = _proj_first_tile(
        hp, [(w_up2[0], 0), (w_up2[0], up_blocks)], tm=tm, tn=UP_COL_TILE, side=hs, **up_kw)
    (hmid,), ((w_down_b,),) = _proj_other_tiles(
        hp, wb, first, tm=tm, tn=UP_COL_TILE, streams=[_cast_stream(w_down2[0], CAST_CHUNKS)],
        **up_kw)
    (yp,), (ys,) = _resid([hmid], [hmid_s], [(w_down_b, 0)], xp, xs, mod_c_p, mod_c_s, 3, g_final,
                          modf_p, modf_s, 0, factor=0.5, tm=RESID_ROWS_FFN, emit_x=False,
                          h_dtype=F32)

    stack = lambda v: v[None]
    return (yp.reshape(bp, seq, d), ys.reshape(bs, 1, d),
            stack(lru_h_p), stack(lru_buf_p), stack(ssm_p), stack(ssd_buf_p),
            stack(lru_h_s), stack(lconv_new.reshape(bs, CONV_W - 1, W_LRU)),
            stack(ssm_s.reshape(bs, SSD_HEADS, SSD_HEAD_DIM, SSD_STATE)),
            stack(sconv_new.reshape(bs, CONV_W - 1, SSD_CONV_DIM)))
```

```python
import functools
from typing import Callable, NamedTuple

import jax
import jax.numpy as jnp
from jax import lax
from jax.experimental import pallas as pl
from jax.experimental.pallas import tpu as pltpu

F32 = jnp.float32
BF16 = jnp.bfloat16

D_MODEL = 2048
D_FF = 5632
W_LRU = 1024
W_SSD = 1024
LRU_HEADS = 16
LRU_BLOCK = 64
LRU_C = 8.0
SSD_HEADS = 16
SSD_HEAD_DIM = 64
SSD_GROUPS = 2
SSD_HPG = 8
SSD_STATE = 128
SSD_CHUNK = 128
SSD_CHUNKS_PER_STEP = 2
CONV_W = 4
SSD_CONV_DIM = W_SSD + 2 * SSD_GROUPS * SSD_STATE
IN_MAIN = 2 * W_LRU + W_SSD + SSD_CONV_DIM
N_MOD = 9
EPS = 1e-6

LANES = 128
SUBLANES = 8
VMEM_LIMIT_BYTES = 56 * 1024 * 1024

LRU_GATE_GROUP = 256
LRU_TIME_TILE = 512
SCAN_ROWS = 2 * SUBLANES

ROW_TILE = 1024
UP_COL_TILE = 512
IN_COL_TILE = IN_MAIN // 3
RESID_ROWS_FFN = 256
RESID_ROWS_MIX = 512
CAST_CHUNKS = 32
CAST_CHUNKS_OUT = 16
ADA_STREAM_TILE = 256


def _sigmoid(v):
    return 0.5 * (jnp.tanh(0.5 * v) + 1.0)


def _silu(v):
    return v * _sigmoid(v)


def _softplus(v):
    return jnp.maximum(v, 0.0) + jnp.log1p(jnp.exp(-jnp.abs(v)))


def _gelu_tanh(v):
    return 0.5 * v * (1.0 + jnp.tanh(0.7978845608028654 * (v + 0.044715 * (v * v * v))))


def _bdot(a, b):
    return jnp.dot(a, b, preferred_element_type=F32)


def _params(sem):
    return pltpu.CompilerParams(dimension_semantics=sem, vmem_limit_bytes=VMEM_LIMIT_BYTES)


def _ada_mm_kernel(s_ref, w_ref, b_ref, o_ref):
    o_ref[...] = _bdot(s_ref[...], w_ref[...].astype(BF16)) + b_ref[...]


def _ada_kernel(c_ref, w_ref, b_ref, o_ref, s_ref):
    s_ref[...] = _silu(c_ref[...]).astype(BF16)
    _ada_mm_kernel(s_ref, w_ref, b_ref, o_ref)


def _ada(c, w, b, cols, tn=1024):
    m, k = c.shape
    n = cols
    return pl.pallas_call(
        _ada_kernel,
        grid=(n // tn,),
        in_specs=[pl.BlockSpec((m, k), lambda j: (0, 0)),
                  pl.BlockSpec((k, tn), lambda j: (0, j)),
                  pl.BlockSpec((1, tn), lambda j: (0, j))],
        out_specs=[pl.BlockSpec((m, tn), lambda j: (0, j)), pl.BlockSpec((m, k), lambda j: (0, 0))],
        out_shape=[jax.ShapeDtypeStruct((m, n), F32), jax.ShapeDtypeStruct((m, k), BF16)],
        compiler_params=_params(("arbitrary",)),
        name="ada_proj",
    )(c, w, b.reshape(1, -1))


def _norm_modulate(x, gain, shift, scale):
    ms = jnp.mean(x * x, axis=-1, keepdims=True)
    y = x * lax.rsqrt(ms + EPS) * gain
    return y * (1.0 + scale) + shift


def _norm_rows_kernel(x_ref, gain_ref, sh_ref, sc_ref, xs_ref, shs_ref, scs_ref, o_ref, os_ref):
    o_ref[...] = _norm_modulate(x_ref[...], gain_ref[...], sh_ref[...],
                                sc_ref[...]).astype(o_ref.dtype)

    @pl.when(pl.program_id(0) == 0)
    def _():
        os_ref[...] = _norm_modulate(xs_ref[...], gain_ref[...], shs_ref[...],
                                     scs_ref[...]).astype(os_ref.dtype)


def _norm_rows(x, x_s, gain, mod, mod_s, shift_chunk, *, tm):
    m, d = x.shape
    ns = x_s.shape[0]
    tiles_per_group = (m // tm) // mod.shape[0]
    once = dict(pipeline_mode=pl.Buffered(1))
    mod_spec = lambda c: pl.BlockSpec((None, 1, d), lambda i: (i // tiles_per_group, 0, c))
    mod_s_spec = lambda c: pl.BlockSpec((None, ns, d), lambda i: (0, 0, c), **once)
    return pl.pallas_call(
        _norm_rows_kernel,
        grid=(m // tm,),
        in_specs=[pl.BlockSpec((tm, d), lambda i: (i, 0)), pl.BlockSpec((1, d), lambda i: (0, 0)),
                  mod_spec(shift_chunk), mod_spec(shift_chunk + 1),
                  pl.BlockSpec((ns, d), lambda i: (0, 0), **once),
                  mod_s_spec(shift_chunk), mod_s_spec(shift_chunk + 1)],
        out_specs=[pl.BlockSpec((tm, d), lambda i: (i, 0)), pl.BlockSpec((ns, d), lambda i: (0, 0))],
        out_shape=[jax.ShapeDtypeStruct((m, d), BF16), jax.ShapeDtypeStruct((ns, d), BF16)],
        compiler_params=_params(("arbitrary",)),
        name="norm_rows",
    )(x, gain.reshape(1, d), mod, mod, x_s, mod_s, mod_s)


class _Stream(NamedTuple):
    ins: list
    outs: list
    body: Callable
    start: int
    steps: int


def _stream_io(streams, n_inner):
    in_specs, args, out_specs, out_shape = [], [], [], []

    def spec(st, block, index_fn):
        return pl.BlockSpec(
            block, lambda i, j: index_fn(jnp.clip(i * n_inner + j - st.start, 0, st.steps - 1)))

    for st in streams:
        for arr, block, index_fn in st.ins:
            in_specs.append(spec(st, block, index_fn))
            args.append(arr)
        for shape, block, index_fn in st.outs:
            out_specs.append(spec(st, block, index_fn))
            out_shape.append(shape)
    return in_specs, args, out_specs, out_shape


def _run_streams(streams, in_refs, out_refs, n_inner, n_steps):
    step = pl.program_id(0) * n_inner + pl.program_id(1)
    in_refs, out_refs = iter(in_refs), iter(out_refs)
    for st in streams:
        ins = [next(in_refs) for _ in st.ins]
        outs = [next(out_refs) for _ in st.outs]
        if st.start == 0 and st.steps == n_steps:
            st.body(*ins, *outs)
            continue

        @pl.when((step >= st.start) & (step < st.start + st.steps))
        def _(st=st, ins=ins, outs=outs):
            st.body(*ins, *outs)


def _split_stream_outs(streams, flat):
    flat = list(flat)
    return [[flat.pop(0) for _ in st.outs] for st in streams]


def _cast_body(src_ref, dst_ref):
    dst_ref[...] = src_ref[...].astype(dst_ref.dtype)


def _cast_stream(w, chunks, start=0, rows=None):
    rows = w.shape[0] if rows is None else rows
    block = (rows // chunks, w.shape[1])
    index = lambda k: (k, 0)
    return _Stream([(w, block, index)],
                   [(jax.ShapeDtypeStruct((rows, w.shape[1]), BF16), block, index)],
                   _cast_body, start, chunks)


def _ada_stream(sc, w, b, col0, cols, tn, start=0):
    m, k = sc.shape
    t0 = col0 // tn
    return _Stream(
        [(sc, (m, k), lambda s: (0, 0)), (w, (k, tn), lambda s: (0, s + t0)),
         (b.reshape(1, -1), (1, tn), lambda s: (0, s + t0))],
        [(jax.ShapeDtypeStruct((m, cols), F32), (m, tn), lambda s: (0, s))],
        _ada_mm_kernel, start, cols // tn)


def _wdot(h, w, trans_w):
    if trans_w:
        return lax.dot_general(h, w, (((1,), (1,)), ((), ())), preferred_element_type=F32)
    return _bdot(h, w)


def _proj_kernel(*refs, n_w, swiglu, trans_w, has_extra, has_side, emit_bf16, streams, n_prev,
                 nj, n_steps):
    it = iter(refs)
    x_ref = next(it)
    xs_ref = next(it) if has_side else None
    w_refs = [next(it) for _ in range(n_w)]
    wx_ref = next(it) if has_extra else None
    stream_ins = [next(it) for st in streams for _ in st.ins]
    for _ in range(n_prev):
        next(it)
    o_ref = next(it)
    ox_ref = next(it) if has_extra else None
    os_ref = next(it) if has_side else None
    osx_ref = next(it) if has_side and has_extra else None
    wo_refs = [next(it) for _ in range(n_w)] if emit_bf16 else []
    stream_outs = [next(it) for st in streams for _ in st.outs]

    j = pl.program_id(1)
    first_tile = pl.program_id(0) == 0

    if has_extra:
        @pl.when(j == 0)
        def _():
            ox_ref[...] = _wdot(x_ref[...], wx_ref[...].astype(BF16), trans_w)

        if has_side:
            @pl.when((j == 0) & first_tile)
            def _():
                osx_ref[...] = _wdot(xs_ref[...], wx_ref[...].astype(BF16), trans_w)

    wbs = [w_ref[...].astype(BF16) for w_ref in w_refs]
    for wo_ref, wb in zip(wo_refs, wbs):
        wo_ref[...] = wb

    def project(h, out_ref):
        if swiglu:
            g = _wdot(h, wbs[0], trans_w)
            u = _wdot(h, wbs[1], trans_w)
            out_ref[...] = (_silu(g) * u).astype(out_ref.dtype)
        else:
            out_ref[...] = _wdot(h, wbs[0], trans_w).astype(out_ref.dtype)

    project(x_ref[...], o_ref)
    if has_side:
        @pl.when(first_tile)
        def _():
            project(xs_ref[...], os_ref)

    _run_streams(streams, stream_ins, stream_outs, nj, n_steps)


def _proj(x, ws, *, n_out, tm, tn, swiglu, out_dtype, row_tiles, side=None, trans_w=False,
          w_extra=None, emit_bf16=False, streams=(), prev=None):
    m, d = x.shape
    t0, t1 = row_tiles
    nj = n_out // tn
    has_side = side is not None
    has_extra = w_extra is not None
    single_row_tile = t1 - t0 == 1
    once = dict(pipeline_mode=pl.Buffered(1))

    def w_spec(off):
        if trans_w:
            return pl.BlockSpec((tn, d), lambda i, j: (j + off, 0))
        return pl.BlockSpec((d, tn), lambda i, j: (0, j + off))

    x_mode = once if single_row_tile else {}
    in_specs = [pl.BlockSpec((tm, d), lambda i, j: (i + t0, 0), **x_mode)]
    args = [x]
    if has_side:
        ns = side.shape[0]
        in_specs.append(pl.BlockSpec((ns, d), lambda i, j: (0, 0), **once))
        args.append(side)
    in_specs += [w_spec(off) for _, off in ws]
    args += [w for w, _ in ws]
    if has_extra:
        nx = w_extra.shape[0] if trans_w else w_extra.shape[1]
        in_specs.append(pl.BlockSpec(w_extra.shape, lambda i, j: (0, 0)))
        args.append(w_extra)
    assert all(st.start + st.steps <= (t1 - t0) * nj for st in streams)
    st_in_specs, st_args, st_out_specs, st_out_shape = _stream_io(streams, nj)
    in_specs += st_in_specs
    args += st_args
    prev = list(prev or [])
    aliases = {}
    for k, buf in enumerate(prev):
        aliases[len(args)] = k
        in_specs.append(pl.BlockSpec(memory_space=pl.ANY))
        args.append(buf)

    out_specs = [pl.BlockSpec((tm, tn), lambda i, j: (i + t0, j))]
    out_shape = [jax.ShapeDtypeStruct((m, n_out), out_dtype)]
    if has_extra:
        out_specs.append(pl.BlockSpec((tm, nx), lambda i, j: (i + t0, 0)))
        out_shape.append(jax.ShapeDtypeStruct((m, nx), F32))
    n_main = len(out_shape)
    if has_side:
        out_specs.append(pl.BlockSpec((ns, tn), lambda i, j: (0, jnp.where(i == 0, j, nj - 1))))
        out_shape.append(jax.ShapeDtypeStruct((ns, n_out), out_dtype))
        if has_extra:
            out_specs.append(pl.BlockSpec((ns, nx), lambda i, j: (0, 0)))
            out_shape.append(jax.ShapeDtypeStruct((ns, nx), F32))
    n_side = len(out_shape) - n_main
    if emit_bf16:
        assert single_row_tile, "weight copies are written once per column tile"
        for _ in ws:
            out_specs.append(w_spec(0))
            out_shape.append(jax.ShapeDtypeStruct((n_out, d) if trans_w else (d, n_out), BF16))
    out_specs += st_out_specs
    out_shape += st_out_shape
    outs = pl.pallas_call(
        functools.partial(_proj_kernel, n_w=len(ws), swiglu=swiglu, trans_w=trans_w,
                          has_extra=has_extra, has_side=has_side, emit_bf16=emit_bf16,
                          streams=tuple(streams), n_prev=len(prev), nj=nj,
                          n_steps=(t1 - t0) * nj),
        grid=(t1 - t0, nj),
        in_specs=in_specs,
        out_specs=out_specs,
        out_shape=out_shape,
        input_output_aliases=aliases,
        compiler_params=_params(("arbitrary" if streams or has_side else "parallel", "arbitrary")),
        name="proj_swiglu" if swiglu else "proj",
    )(*args)
    n_wb = len(ws) if emit_bf16 else 0
    main, rest = outs[:n_main], outs[n_main:]
    side_outs, rest = rest[:n_side], rest[n_side:]
    wb, rest = rest[:n_wb], rest[n_wb:]
    return main, side_outs, wb, _split_stream_outs(streams, rest)


def _proj_first_tile(x, ws_f32, *, tm, tn, side, **kw):
    main, side_outs, wb, _ = _proj(x, ws_f32, tm=tm, tn=tn, row_tiles=(0, 1), side=side,
                                   emit_bf16=True, **kw)
    return main, side_outs, wb


def _proj_other_tiles(x, wb, prev, *, tm, tn, streams, **kw):
    main, _, _, stream_outs = _proj(x, [(w, 0) for w in wb], tm=tm, tn=tn,
                                    row_tiles=(1, x.shape[0] // tm), prev=prev, streams=streams,
                                    **kw)
    return main, stream_outs


def _resid_kernel(*refs, n_lhs, factor, emit_x):
    it = iter(refs)
    lhs_refs = [next(it) for _ in range(n_lhs)]
    lhs_s_refs = [next(it) for _ in range(n_lhs)]
    w_refs = [next(it) for _ in range(n_lhs)]
    x_ref, gate_ref, gain_ref, sh_ref, sc_ref = (next(it) for _ in range(5))
    xs_ref, gate_s_ref, sh_s_ref, sc_s_ref = (next(it) for _ in range(4))
    n_out = 2 if emit_x else 1
    outs = [next(it) for _ in range(n_out)]
    outs_s = [next(it) for _ in range(n_out)]

    def update(lhs, x_in, gate, sh, sc, out_refs):
        acc = _bdot(lhs[0][...], w_refs[0][...])
        for l_ref, w_ref in zip(lhs[1:], w_refs[1:]):
            acc = acc + _bdot(l_ref[...], w_ref[...])
        x_new = x_in[...] + (factor * gate[...]) * acc
        if emit_x:
            out_refs[0][...] = x_new
        h_ref = out_refs[-1]
        h_ref[...] = _norm_modulate(x_new, gain_ref[...], sh[...], sc[...]).astype(h_ref.dtype)

    update(lhs_refs, x_ref, gate_ref, sh_ref, sc_ref, outs)

    @pl.when(pl.program_id(0) == 0)
    def _():
        update(lhs_s_refs, xs_ref, gate_s_ref, sh_s_ref, sc_s_ref, outs_s)


def _resid(lhs_list, lhs_s_list, ws, x, x_s, mod, mod_s, gate_chunk, gain_next, mod_next,
           mod_next_s, shift_chunk_next, *, factor, tm, emit_x, h_dtype):
    m, d = x.shape
    ns = x_s.shape[0]
    groups = mod.shape[0]
    tiles_per_group = (m // tm) // groups
    kp = lhs_list[0].shape[1]
    once = dict(pipeline_mode=pl.Buffered(1))

    def mod_spec(chunk):
        return pl.BlockSpec((None, 1, d), lambda i: (i // tiles_per_group, 0, chunk))

    def mod_s_spec(chunk):
        return pl.BlockSpec((None, ns, d), lambda i: (0, 0, chunk), **once)

    in_specs = [pl.BlockSpec((tm, kp), lambda i: (i, 0)) for _ in lhs_list]
    in_specs += [pl.BlockSpec((ns, kp), lambda i: (0, 0), **once) for _ in lhs_s_list]
    in_specs += [pl.BlockSpec((kp, d), lambda i, k=k: (k, 0), **once) for _, k in ws]
    in_specs += [pl.BlockSpec((tm, d), lambda i: (i, 0)), mod_spec(gate_chunk),
                 pl.BlockSpec((1, d), lambda i: (0, 0)),
                 mod_spec(shift_chunk_next), mod_spec(shift_chunk_next + 1),
                 pl.BlockSpec((ns, d), lambda i: (0, 0), **once), mod_s_spec(gate_chunk),
                 mod_s_spec(shift_chunk_next), mod_s_spec(shift_chunk_next + 1)]
    row = pl.BlockSpec((tm, d), lambda i: (i, 0))
    row_s = pl.BlockSpec((ns, d), lambda i: (0, 0))
    dtypes = ([F32] if emit_x else []) + [h_dtype]
    out_specs = [row for _ in dtypes] + [row_s for _ in dtypes]
    out_shape = ([jax.ShapeDtypeStruct((m, d), t) for t in dtypes]
                 + [jax.ShapeDtypeStruct((ns, d), t) for t in dtypes])
    outs = pl.pallas_call(
        functools.partial(_resid_kernel, n_lhs=len(lhs_list), factor=factor, emit_x=emit_x),
        grid=(m // tm,),
        in_specs=in_specs,
        out_specs=out_specs,
        out_shape=out_shape,
        compiler_params=_params(("arbitrary",)),
        name="resid",
    )(*lhs_list, *lhs_s_list, *[w for w, _ in ws], x, mod, gain_next.reshape(1, d), mod_next,
      mod_next, x_s, mod_s, mod_next_s, mod_next_s)
    return outs[:len(dtypes)], outs[len(dtypes):]


def _lru_gates(xc, wg_ref, ba, bi, sp):
    a_parts, b_parts = [], []
    for g in range(W_LRU // LRU_GATE_GROUP):
        cols = slice(g * LRU_GATE_GROUP, (g + 1) * LRU_GATE_GROUP)
        xg = xc[:, cols]
        ri = _bdot(xg.astype(BF16), wg_ref[g].astype(BF16))
        r = _sigmoid(ri[:, :LRU_GATE_GROUP] + ba[:, cols])
        i = _sigmoid(ri[:, LRU_GATE_GROUP:] + bi[:, cols])
        log_a = (-LRU_C * r) * sp[:, cols]
        a = jnp.exp(log_a)
        a_parts.append(a)
        v = 1.0 - a * a
        root = jnp.where(v > 0.0, v * lax.rsqrt(v), 0.0)
        b_parts.append(root * (i * xg))
    return jnp.concatenate(a_parts, axis=1), jnp.concatenate(b_parts, axis=1)


def _causal_conv(prev8, x, w_ref, b_ref):
    rows, width = x.shape
    rid = lax.broadcasted_iota(jnp.int32, (SUBLANES, width), 0)
    shifts = (1, 2, 3)
    taps = [w_ref[k:k + 1, :] for k in range(CONV_W)]
    bias = b_ref[...]
    prev_rot = [pltpu.roll(prev8, k, 0) for k in shifts]
    out = []
    for r in range(rows // SUBLANES):
        cur = x[r * SUBLANES:(r + 1) * SUBLANES, :]
        cur_rot = [pltpu.roll(cur, k, 0) for k in shifts]
        s1, s2, s3 = [jnp.where(rid < k, p, c) for k, p, c in zip(shifts, prev_rot, cur_rot)]
        out.append(bias + taps[0] * s3 + taps[1] * s2 + taps[2] * s1 + taps[3] * cur)
        prev_rot = cur_rot
    return jnp.concatenate(out, axis=0)


def _lru_prompt_kernel(*refs, streams, nt, n_steps):
    n_in = sum(len(st.ins) for st in streams)
    n_out = sum(len(st.outs) for st in streams)
    xl_ref, gl_ref, cw_ref, cb_ref, wg_ref, ba_ref, bi_ref, lam_ref = refs[:8]
    stream_ins = refs[8:8 + n_in]
    o_ref, hT_ref = refs[8 + n_in:10 + n_in]
    stream_outs = refs[10 + n_in:10 + n_in + n_out]
    xbuf, a_scr, b_scr, hcar = refs[10 + n_in + n_out:]
    t = pl.program_id(1)
    tt = xl_ref.shape[0]

    @pl.when(t == 0)
    def _():
        xbuf[...] = jnp.zeros_like(xbuf)
        hcar[...] = jnp.zeros_like(hcar)

    _run_streams(streams, stream_ins, stream_outs, nt, n_steps)

    x = xl_ref[...]
    xc = _causal_conv(xbuf[...], x, cw_ref, cb_ref)
    xbuf[...] = x[tt - SUBLANES:, :]

    sp = _softplus(-lam_ref[...])
    a, bt = _lru_gates(xc, wg_ref, ba_ref[...], bi_ref[...], sp)
    a_scr[...] = a
    b_scr[...] = bt

    rid = lax.broadcasted_iota(jnp.int32, (SUBLANES, W_LRU), 0)

    def scan8(a8, b8, h_in):
        for s in (1, 2, 4):
            a_sh = pltpu.roll(a8, s, 0)
            b_sh = pltpu.roll(b8, s, 0)
            m = rid >= s
            b8 = jnp.where(m, a8 * b_sh + b8, b8)
            a8 = jnp.where(m, a8 * a_sh, a8)
        h8 = a8 * h_in + b8
        return h8, jnp.broadcast_to(h8[SUBLANES - 1:SUBLANES, :], (SUBLANES, W_LRU))

    def body(g, h_in):
        r0 = pl.multiple_of(g * SCAN_ROWS, SCAN_ROWS)
        lo = pl.ds(r0, SUBLANES)
        hi = pl.ds(r0 + SUBLANES, SUBLANES)
        h_lo, h_mid = scan8(a_scr[lo, :], b_scr[lo, :], h_in)
        h_hi, h_out = scan8(a_scr[hi, :], b_scr[hi, :], h_mid)
        rows = pl.ds(r0, SCAN_ROWS)
        h16 = jnp.concatenate([h_lo, h_hi], axis=0)
        o_ref[rows, :] = (h16 * _gelu_tanh(gl_ref[rows, :])).astype(o_ref.dtype)
        return h_out

    h_last = lax.fori_loop(0, tt // SCAN_ROWS, body, hcar[...], unroll=4)
    hcar[...] = h_last

    @pl.when(t == pl.num_programs(1) - 1)
    def _():
        hT_ref[...] = h_last[0:1, :]


def _lru_prompt(proj, batch, seq, cw, cb, wg, ba, bi, lam, streams=()):
    tt = LRU_TIME_TILE
    nt = seq // tt
    assert all(st.start + st.steps <= batch * nt for st in streams)
    st_in_specs, st_args, st_out_specs, st_out_shape = _stream_io(streams, nt)
    row = lambda v: v.reshape(1, W_LRU)
    full = lambda shape: pl.BlockSpec(shape, lambda b, t: (0,) * len(shape))
    out, h_t, *rest = pl.pallas_call(
        functools.partial(_lru_prompt_kernel, streams=tuple(streams), nt=nt, n_steps=batch * nt),
        grid=(batch, nt),
        in_specs=[pl.BlockSpec((tt, W_LRU), lambda b, t: (b * nt + t, 0)),
                  pl.BlockSpec((tt, W_LRU), lambda b, t: (b * nt + t, 1)),
                  full((CONV_W, W_LRU)), full((1, W_LRU)), full(wg.shape),
                  full((1, W_LRU)), full((1, W_LRU)), full((1, W_LRU))] + st_in_specs,
        out_specs=[pl.BlockSpec((tt, W_LRU), lambda b, t: (b * nt + t, 0)),
                   pl.BlockSpec((None, 1, W_LRU), lambda b, t: (b, 0, 0))] + st_out_specs,
        out_shape=[jax.ShapeDtypeStruct((batch * seq, W_LRU), BF16),
                   jax.ShapeDtypeStruct((batch, 1, W_LRU), F32)] + st_out_shape,
        scratch_shapes=[pltpu.VMEM((SUBLANES, W_LRU), F32),
                        pltpu.VMEM((tt, W_LRU), F32),
                        pltpu.VMEM((tt, W_LRU), F32),
                        pltpu.VMEM((SUBLANES, W_LRU), F32)],
        compiler_params=_params(("arbitrary" if streams else "parallel", "arbitrary")),
        name="lru_prompt",
    )(proj, proj, cw, row(cb), wg, row(ba), row(bi), row(lam), *st_args)
    return out, h_t.reshape(batch, W_LRU), _split_stream_outs(streams, rest)


def _ssd_chunk(z_ref, xbc_ref, dt_ref, cw_ref, cb_ref, dtb_ref, alog_ref, dexp_ref, ng_ref, y_ref,
               xbuf, st_scr, y_scr, m_scr, xbd_scr):
    lc = SSD_CHUNK
    x = xbc_ref[...]
    act = _silu(_causal_conv(xbuf[...], x, cw_ref, cb_ref))
    xbuf[...] = x[lc - SUBLANES:, :]
    xs = act[:, :W_SSD]
    bm = act[:, W_SSD:W_SSD + SSD_GROUPS * SSD_STATE]
    cm = act[:, W_SSD + SSD_GROUPS * SSD_STATE:]

    dt = _softplus(dt_ref[...] + dtb_ref[...])
    d_a = dt * (-jnp.exp(alog_ref[...]))
    row_i = lax.broadcasted_iota(jnp.int32, (lc, lc), 0)
    col_i = lax.broadcasted_iota(jnp.int32, (lc, lc), 1)
    causal = row_i >= col_i
    tril = jnp.where(causal, 1.0, 0.0).astype(F32)
    cs = jnp.dot(tril, d_a, preferred_element_type=F32, precision=lax.Precision.HIGHEST)
    cs_t = cs.T
    dt_t = dt.T
    cs_last = cs[lc - 1:lc, :]

    def per_head_lanes(v):
        rows = v.shape[0]
        return jnp.concatenate(
            [jnp.broadcast_to(v[:, h:h + 1], (rows, SSD_HEAD_DIM)) for h in range(SSD_HEADS)],
            axis=1)

    w_exp = per_head_lanes(jnp.exp(cs_last - cs) * dt)
    ecs_exp = per_head_lanes(jnp.exp(cs))
    cd_exp = per_head_lanes(jnp.exp(cs_last))
    gw = SSD_HPG * SSD_HEAD_DIM
    low_half = col_i < SSD_HEAD_DIM

    for g in range(SSD_GROUPS):
        ncols = slice(g * SSD_STATE, (g + 1) * SSD_STATE)
        gcols = slice(g * gw, (g + 1) * gw)
        b_g = bm[:, ncols].astype(BF16)
        c_g = cm[:, ncols].astype(BF16)
        cb_mat = lax.dot_general(c_g, b_g, (((1,), (1,)), ((), ())),
                                 preferred_element_type=F32)
        for e in range(SSD_HPG):
            h = g * SSD_HPG + e
            cs_col = jnp.broadcast_to(cs[:, h:h + 1], (lc, lc))
            l_mat = jnp.exp(jnp.where(causal, cs_col - cs_t[h:h + 1, :], -jnp.inf))
            m_scr[g, :, e * lc:(e + 1) * lc] = (cb_mat * l_mat * dt_t[h:h + 1, :]).astype(BF16)
        for q in range(SSD_HPG // 2):
            lanes = slice(q * LANES, (q + 1) * LANES)
            slab = xs[:, g * gw + q * LANES:g * gw + (q + 1) * LANES]
            xbd_scr[g, (2 * q) * lc:(2 * q + 1) * lc, lanes] = jnp.where(
                low_half, slab, 0.0).astype(BF16)
            xbd_scr[g, (2 * q + 1) * lc:(2 * q + 2) * lc, lanes] = jnp.where(
                low_half, 0.0, slab).astype(BF16)
        st_g = st_scr[:, gcols]
        y_off = _bdot(c_g, st_g.astype(BF16)) * ecs_exp[:, gcols]
        y_scr[:, gcols] = (_bdot(m_scr[g], xbd_scr[g]) + y_off
                           + dexp_ref[:, gcols] * xs[:, gcols])
        xw = (xs[:, gcols] * w_exp[:, gcols]).astype(BF16)
        st_scr[:, gcols] = cd_exp[:, gcols] * st_g + lax.dot_general(
            b_g, xw, (((0,), (0,)), ((), ())), preferred_element_type=F32)

    yg = y_scr[...] * _silu(z_ref[...])
    ms = jnp.mean(yg * yg, axis=-1, keepdims=True)
    y_ref[...] = (yg * lax.rsqrt(ms + EPS) * ng_ref[...]).astype(y_ref.dtype)


def _ssd_prompt_kernel(*refs, streams, nc, n_steps):
    n_in = sum(len(st.ins) for st in streams)
    n_out = sum(len(st.outs) for st in streams)
    z_ref, xbc_ref, dt_ref, cw_ref, cb_ref, dtb_ref, alog_ref, dexp_ref, ng_ref = refs[:9]
    stream_ins = refs[9:9 + n_in]
    y_ref, st_ref = refs[9 + n_in:11 + n_in]
    stream_outs = refs[11 + n_in:11 + n_in + n_out]
    xbuf, st_scr, y_scr, m_scr, xbd_scr = refs[11 + n_in + n_out:]
    c = pl.program_id(1)

    @pl.when(c == 0)
    def _():
        xbuf[...] = jnp.zeros_like(xbuf)
        st_scr[...] = jnp.zeros_like(st_scr)
        xbd_scr[...] = jnp.zeros_like(xbd_scr)

    _run_streams(streams, stream_ins, stream_outs, nc, n_steps)

    for cc in range(SSD_CHUNKS_PER_STEP):
        rows = pl.ds(cc * SSD_CHUNK, SSD_CHUNK)
        _ssd_chunk(z_ref.at[rows], xbc_ref.at[rows], dt_ref.at[rows], cw_ref, cb_ref, dtb_ref,
                   alog_ref, dexp_ref, ng_ref, y_ref.at[rows], xbuf, st_scr, y_scr.at[cc],
                   m_scr.at[cc], xbd_scr.at[cc])

    @pl.when(c == pl.num_programs(1) - 1)
    def _():
        st_ref[...] = st_scr[...].T


def _ssd_prompt(proj, dt_raw, batch, seq, cw, cb, dtb, alog, dexp, ng, streams=()):
    lc = SSD_CHUNK
    cps = SSD_CHUNKS_PER_STEP
    tl = cps * lc
    nc = seq // tl
    assert all(st.start + st.steps <= batch * nc for st in streams)
    st_in_specs, st_args, st_out_specs, st_out_shape = _stream_io(streams, nc)
    full = lambda shape: pl.BlockSpec(shape, lambda b, c: (0,) * len(shape))
    z_blk = (2 * W_LRU) // W_SSD
    xbc_blk = (2 * W_LRU + W_SSD) // SSD_CONV_DIM
    y, st, *rest = pl.pallas_call(
        functools.partial(_ssd_prompt_kernel, streams=tuple(streams), nc=nc, n_steps=batch * nc),
        grid=(batch, nc),
        in_specs=[pl.BlockSpec((tl, W_SSD), lambda b, c: (b * nc + c, z_blk)),
                  pl.BlockSpec((tl, SSD_CONV_DIM), lambda b, c: (b * nc + c, xbc_blk)),
                  pl.BlockSpec((tl, LANES), lambda b, c: (b * nc + c, 0)),
                  full((CONV_W, SSD_CONV_DIM)), full((1, SSD_CONV_DIM)),
                  full((1, LANES)), full((1, LANES)), full((1, W_SSD)), full((1, W_SSD))]
        + st_in_specs,
        out_specs=[pl.BlockSpec((tl, W_SSD), lambda b, c: (b * nc + c, 0)),
                   pl.BlockSpec((None, W_SSD, SSD_STATE), lambda b, c: (b, 0, 0))] + st_out_specs,
        out_shape=[jax.ShapeDtypeStruct((batch * seq, W_SSD), BF16),
                   jax.ShapeDtypeStruct((batch, W_SSD, SSD_STATE), F32)] + st_out_shape,
        scratch_shapes=[pltpu.VMEM((SUBLANES, SSD_CONV_DIM), F32),
                        pltpu.VMEM((SSD_STATE, W_SSD), F32),
                        pltpu.VMEM((cps, lc, W_SSD), F32),
                        pltpu.VMEM((cps, SSD_GROUPS, lc, SSD_HPG * lc), BF16),
                        pltpu.VMEM((cps, SSD_GROUPS, SSD_HPG * lc, SSD_HPG * SSD_HEAD_DIM), BF16)],
        compiler_params=_params(("arbitrary" if streams else "parallel", "arbitrary")),
        name="ssd_prompt",
    )(proj, proj, dt_raw, cw, cb, dtb, alog, dexp, ng, *st_args)
    return (y, st.reshape(batch, SSD_HEADS, SSD_HEAD_DIM, SSD_STATE),
            _split_stream_outs(streams, rest))


def _sample_pre_kernel(proj_ref, dt_ref, h0_ref, lconv_ref, sconv_ref,
                       lcw_ref, lcb_ref, wg_ref, ba_ref, bi_ref, lam_ref,
                       scw_ref, scb_ref, dtb_ref, alog_ref,
                       outl_ref, hnew_ref, lconv_new_ref, sconv_new_ref,
                       xs_ref, xdt_ref, bc_ref, dec_ref):
    nb = proj_ref.shape[0]
    xl = proj_ref[:, 0:W_LRU]
    gl = proj_ref[:, W_LRU:2 * W_LRU]
    xbc = proj_ref[:, 2 * W_LRU + W_SSD:IN_MAIN]

    def conv1(state_ref, width, x_new, w_ref, b_ref):
        y = b_ref[...] + w_ref[0:1, :] * state_ref[:, 0:width]
        y = y + w_ref[1:2, :] * state_ref[:, width:2 * width]
        y = y + w_ref[2:3, :] * state_ref[:, 2 * width:3 * width]
        return y + w_ref[3:4, :] * x_new

    xc = conv1(lconv_ref, W_LRU, xl, lcw_ref, lcb_ref)
    a, bt = _lru_gates(xc, wg_ref, ba_ref[...], bi_ref[...], _softplus(-lam_ref[...]))
    h_new = a * h0_ref[...] + bt
    hnew_ref[...] = h_new
    outl_ref[...] = (h_new * _gelu_tanh(gl)).astype(outl_ref.dtype)
    lconv_new_ref[:, 0:2 * W_LRU] = lconv_ref[:, W_LRU:3 * W_LRU]
    lconv_new_ref[:, 2 * W_LRU:3 * W_LRU] = xl

    act = _silu(conv1(sconv_ref, SSD_CONV_DIM, xbc, scw_ref, scb_ref))
    sconv_new_ref[:, 0:2 * SSD_CONV_DIM] = sconv_ref[:, SSD_CONV_DIM:3 * SSD_CONV_DIM]
    sconv_new_ref[:, 2 * SSD_CONV_DIM:3 * SSD_CONV_DIM] = xbc
    xs = act[:, :W_SSD]
    xs_ref[...] = xs
    bc_ref[...] = act[:, W_SSD:]
    dt = _softplus(dt_ref[...] + dtb_ref[...])
    dec = jnp.exp(dt * (-jnp.exp(alog_ref[...])))
    for h in range(SSD_HEADS):
        pcols = slice(h * SSD_HEAD_DIM, (h + 1) * SSD_HEAD_DIM)
        xdt_ref[:, pcols] = xs[:, pcols] * jnp.broadcast_to(dt[:, h:h + 1], (nb, SSD_HEAD_DIM))
        dec_ref[h] = jnp.broadcast_to(dec[:, h:h + 1], (nb, SSD_STATE))


def _sample_pre(proj, dt_raw, h0, lconv, sconv, p):
    nb = proj.shape[0]
    out_shape = [jax.ShapeDtypeStruct((nb, W_LRU), BF16),
                 jax.ShapeDtypeStruct((nb, W_LRU), F32),
                 jax.ShapeDtypeStruct((nb, 3 * W_LRU), F32),
                 jax.ShapeDtypeStruct((nb, 3 * SSD_CONV_DIM), F32),
                 jax.ShapeDtypeStruct((nb, W_SSD), F32),
                 jax.ShapeDtypeStruct((nb, W_SSD), F32),
                 jax.ShapeDtypeStruct((nb, 2 * SSD_GROUPS * SSD_STATE), F32),
                 jax.ShapeDtypeStruct((SSD_HEADS, nb, SSD_STATE), F32)]
    return pl.pallas_call(
        _sample_pre_kernel,
        out_shape=out_shape,
        compiler_params=pltpu.CompilerParams(vmem_limit_bytes=VMEM_LIMIT_BYTES),
        name="sample_pre",
    )(proj, dt_raw, h0, lconv, sconv,
      p["lru_cw"], p["lru_cb"], p["lru_wg"], p["lru_ba"], p["lru_bi"], p["lru_lam"],
      p["ssd_cw"], p["ssd_cb"], p["ssd_dtb"], p["ssd_alog"])


def _sample_state_kernel(s_ref, xdt_ref, bc_ref, dec_ref, o_ref, y_ref):
    bb = s_ref.shape[0]
    half = SSD_HPG * SSD_HEAD_DIM
    rid = lax.broadcasted_iota(jnp.int32, (bb, W_SSD), 0)
    xdt = xdt_ref[...]
    bcb = bc_ref[...].astype(BF16)
    for k in range(bb):
        xk = jnp.where(rid == k, xdt, 0.0).astype(BF16)
        for g in range(SSD_GROUPS):
            rows = slice(g * half, (g + 1) * half)
            b_g = bcb[:, g * SSD_STATE:(g + 1) * SSD_STATE]
            c_g = bcb[:, (SSD_GROUPS + g) * SSD_STATE:(SSD_GROUPS + g + 1) * SSD_STATE]
            outer = lax.dot_general(xk[:, rows], b_g, (((0,), (0,)), ((), ())),
                                    preferred_element_type=F32)
            dec = jnp.concatenate(
                [jnp.broadcast_to(dec_ref[g * SSD_HPG + e, k:k + 1, :], (SSD_HEAD_DIM, SSD_STATE))
                 for e in range(SSD_HPG)], axis=0)
            s_new = dec * s_ref[k, rows, :] + outer
            o_ref[k, rows, :] = s_new
            yk = lax.dot_general(c_g, s_new.astype(BF16), (((1,), (1,)), ((), ())),
                                 preferred_element_type=F32)
            y_ref[k:k + 1, rows] = yk[k:k + 1, :]


def _state_stream(ssm, xdt, bc, dec, bb=8, start=0):
    nb = ssm.shape[0]
    state_block = (bb, W_SSD, SSD_STATE)
    return _Stream(
        [(ssm, state_block, lambda k: (k, 0, 0)),
         (xdt, (bb, W_SSD), lambda k: (k, 0)),
         (bc, (bb, 2 * SSD_GROUPS * SSD_STATE), lambda k: (k, 0)),
         (dec, (SSD_HEADS, bb, SSD_STATE), lambda k: (0, k, 0))],
        [(jax.ShapeDtypeStruct(ssm.shape, F32), state_block, lambda k: (k, 0, 0)),
         (jax.ShapeDtypeStruct((nb, W_SSD), F32), (bb, W_SSD), lambda k: (k, 0))],
        _sample_state_kernel, start, nb // bb)


def _sample_post_kernel(y_ref, xs_ref, proj_ref, dexp_ref, ng_ref, o_ref):
    z = proj_ref[:, 2 * W_LRU:2 * W_LRU + W_SSD]
    yg = (y_ref[...] + dexp_ref[...] * xs_ref[...]) * _silu(z)
    ms = jnp.mean(yg * yg, axis=-1, keepdims=True)
    o_ref[...] = (yg * lax.rsqrt(ms + EPS) * ng_ref[...]).astype(o_ref.dtype)


def _sample_post(y_raw, xs, proj, dexp, ng):
    return pl.pallas_call(
        _sample_post_kernel,
        out_shape=jax.ShapeDtypeStruct(y_raw.shape, BF16),
        compiler_params=pltpu.CompilerParams(vmem_limit_bytes=VMEM_LIMIT_BYTES),
        name="sample_post",
    )(y_raw, xs, proj, dexp, ng)


def _block_diag_groups(w):
    per = LRU_GATE_GROUP // LRU_BLOCK
    w4 = w.reshape(LRU_HEADS // per, per, LRU_BLOCK, LRU_BLOCK)
    bd = jnp.einsum("ghij,hk->ghikj", w4, jnp.eye(per, dtype=w.dtype))
    return bd.reshape(LRU_HEADS // per, LRU_GATE_GROUP, LRU_GATE_GROUP)


def _pad_lanes(v):
    v = v.reshape(1, -1)
    return jnp.pad(v, ((0, 0), (0, LANES - v.shape[1])))


def kernel(x_prompt, x_sample, c_prompt, c_sample, state_lru_h, state_lru_conv, state_ssm, state_ssd_conv, w_ada, b_ada, g_ffn1, w_up1, w_down1, g_mix, w_in, lru_conv_w, lru_conv_b, lru_wa, lru_ba, lru_wi, lru_bi, lru_lambda, ssd_conv_w, ssd_conv_b, ssd_dt_bias, ssd_A_log, ssd_D, ssd_norm_g, w_out, g_ffn2, w_up2, w_down2, w_ada_f, b_ada_f, g_final):
    bp, seq, d = x_prompt.shape
    bs = x_sample.shape[0]
    depth = w_ada.shape[0]
    assert depth == 1 and x_sample.shape[1] == 1 and d == D_MODEL

    pad_rows = (-(bs + bp)) % (2 * SUBLANES)
    c_rows = bs + bp + pad_rows
    c_all = jnp.concatenate([c_sample, c_prompt, jnp.zeros((pad_rows, d), F32)], axis=0)

    def split_rows(mod_all):
        width = mod_all.shape[1]
        return mod_all[bs:bs + bp].reshape(bp, 1, width), mod_all.reshape(1, c_rows, width)

    w_in_t = jnp.swapaxes(w_in[0], 0, 1)
    w_dt_t = jnp.pad(w_in_t[IN_MAIN:], ((0, LANES - SSD_HEADS), (0, 0)))
    up_blocks = D_FF // UP_COL_TILE
    p = {
        "lru_cw": lru_conv_w[0], "lru_cb": lru_conv_b[0].reshape(1, W_LRU),
        "lru_wg": jnp.concatenate([_block_diag_groups(lru_wa[0]), _block_diag_groups(lru_wi[0])],
                                  axis=-1),
        "lru_ba": lru_ba[0].reshape(1, W_LRU), "lru_bi": lru_bi[0].reshape(1, W_LRU),
        "lru_lam": lru_lambda[0].reshape(1, W_LRU),
        "ssd_cw": ssd_conv_w[0], "ssd_cb": ssd_conv_b[0].reshape(1, SSD_CONV_DIM),
        "ssd_dtb": _pad_lanes(ssd_dt_bias[0]), "ssd_alog": _pad_lanes(ssd_A_log[0]),
        "ssd_dexp": jnp.repeat(ssd_D[0], SSD_HEAD_DIM).reshape(1, W_SSD),
        "ssd_ng": ssd_norm_g[0].reshape(1, W_SSD),
    }

    xp = x_prompt.reshape(bp * seq, d)
    xs = jnp.swapaxes(x_sample, 0, 1).reshape(bs, d)
    tm = ROW_TILE
    up_kw = dict(n_out=D_FF, swiglu=True, out_dtype=BF16)

    mod_a_all, silu_c = _ada(c_all, w_ada[0], b_ada[0], 2 * d)
    mod_a_p, mod_a_s = split_rows(mod_a_all)

    hp, hs = _norm_rows(xp, xs, g_ffn1[0], mod_a_p, mod_a_s, 0, tm=tm)
    first, (hmid_s,), wb = _proj_first_tile(
        hp, [(w_up1[0], 0), (w_up1[0], up_blocks)], tm=tm, tn=UP_COL_TILE, side=hs, **up_kw)
    (hmid,), ((w_down_b,), (mod_b_all,), (w_in_b,)) = _proj_other_tiles(
        hp, wb, first, tm=tm, tn=UP_COL_TILE,
        streams=[_cast_stream(w_down1[0], CAST_CHUNKS),
                 _ada_stream(silu_c, w_ada[0], b_ada[0], 2 * d, 3 * d, ADA_STREAM_TILE),
                 _cast_stream(w_in_t, CAST_CHUNKS, start=CAST_CHUNKS, rows=IN_MAIN)],
        **up_kw)
    mod_b_p, mod_b_s = split_rows(mod_b_all)
    (xp, hp), (xs, hs) = _resid([hmid], [hmid_s], [(w_down_b, 0)], xp, xs, mod_b_p, mod_b_s, 0,
                                g_mix[0], mod_b_p, mod_b_s, 1, factor=0.5, tm=RESID_ROWS_FFN,
                                emit_x=True,
                                h_dtype=BF16)

    (proj, dt_raw), (proj_s, dt_raw_s), _, ((w_out_b,),) = _proj(
        hp, [(w_in_b, 0)], n_out=IN_MAIN, tm=tm, tn=IN_COL_TILE, swiglu=False, out_dtype=F32,
        row_tiles=(0, bp * seq // tm), side=hs, trans_w=True, w_extra=w_dt_t,
        streams=[_cast_stream(w_out[0], CAST_CHUNKS_OUT)])
    lconv = state_lru_conv[0].reshape(bs, (CONV_W - 1) * W_LRU)
    sconv = state_ssd_conv[0].reshape(bs, (CONV_W - 1) * SSD_CONV_DIM)
    out_l_s, lru_h_s, lconv_new, sconv_new, xs_act, xdt, bc, dec = _sample_pre(
        proj_s, dt_raw_s, state_lru_h[0], lconv, sconv, p)
    out_l, lru_h_p, ((ssm_s, y_raw), (modf_all,)) = _lru_prompt(
        proj, bp, seq, p["lru_cw"], p["lru_cb"], p["lru_wg"], p["lru_ba"], p["lru_bi"],
        p["lru_lam"],
        streams=[_state_stream(state_ssm[0].reshape(bs, W_SSD, SSD_STATE), xdt, bc, dec),
                 _ada_stream(silu_c, w_ada_f, b_ada_f, 0, 2 * d, 2 * d // (bp * seq // LRU_TIME_TILE))])
    y_ssd_s = _sample_post(y_raw, xs_act, proj_s, p["ssd_dexp"], p["ssd_ng"])
    y_ssd, ssm_p, ((mod_c_all,),) = _ssd_prompt(
        proj, dt_raw, bp, seq, p["ssd_cw"], p["ssd_cb"], p["ssd_dtb"], p["ssd_alog"],
        p["ssd_dexp"], p["ssd_ng"],
        streams=[_ada_stream(silu_c, w_ada[0], b_ada[0], 5 * d, (N_MOD - 5) * d,
                             (N_MOD - 5) * d // (bp * seq // (SSD_CHUNK * SSD_CHUNKS_PER_STEP)))])
    mod_c_p, mod_c_s = split_rows(mod_c_all)
    modf_p, modf_s = split_rows(modf_all)
    proj3 = proj.reshape(bp, seq, IN_MAIN)
    lru_buf_p = proj3[:, seq - (CONV_W - 1):, :W_LRU]
    ssd_buf_p = proj3[:, seq - (CONV_W - 1):, 2 * W_LRU + W_SSD:]

    (xp, hp), (xs, hs) = _resid([out_l, y_ssd], [out_l_s, y_ssd_s], [(w_out_b, 0), (w_out_b, 1)],
                                xp, xs, mod_c_p, mod_c_s, 0, g_ffn2[0], mod_c_p, mod_c_s, 1,
                                factor=1.0, tm=RESID_ROWS_MIX, emit_x=True, h_dtype=BF16)

    first, (hmid_s,), wb = _proj_first_tile(
        hp, [(w_up2[0], 0), (w_up2[0], up_blocks)], tm=tm, tn=UP_COL_TILE, side=hs, **up_kw)
    (hmid,), ((w_down_b,),) = _proj_other_tiles(
        hp, wb, first, tm=tm, tn=UP_COL_TILE, streams=[_cast_stream(w_down2[0], CAST_CHUNKS)],
        **up_kw)
    (yp,), (ys,) = _resid([hmid], [hmid_s], [(w_down_b, 0)], xp, xs, mod_c_p, mod_c_s, 3, g_final,
                          modf_p, modf_s, 0, factor=0.5, tm=RESID_ROWS_FFN, emit_x=False,
                          h_dtype=F32)

    stack = lambda v: v[None]
    return (yp.reshape(bp, seq, d), jnp.swapaxes(ys.reshape(1, bs, d), 0, 1),
            stack(lru_h_p), stack(lru_buf_p), stack(ssm_p), stack(ssd_buf_p),
            stack(lru_h_s), stack(lconv_new.reshape(bs, CONV_W - 1, W_LRU)),
            stack(ssm_s.reshape(bs, SSD_HEADS, SSD_HEAD_DIM, SSD_STATE)),
            stack(sconv_new.reshape(bs, CONV_W - 1, SSD_CONV_DIM)))
```
